```python
import math
import jax, jax.numpy as jnp
from jax import lax
import numpy as np

D_MODEL = 2048
BATCH = 4
SEQ = 4096
DEPTH = 4

GRID_W = 64
EPS = 1e-6
ROPE_THETA = 10000.0
MIX_HALF = D_MODEL // 2
GLA_DV = 128
GLA_DK = GLA_DV // 2
GLA_HEADS = MIX_HALF // GLA_DV
GLA_LOWRANK = 16
GLA_TAU = 16.0
GLA_CHUNK = 64
NA_DH = 128
NA_HEADS = MIX_HALF // NA_DH
NA_KH = 8
NA_KW = 16
SSD_INNER = MIX_HALF
SSD_HEADDIM = 64
SSD_HEADS = SSD_INNER // SSD_HEADDIM
SSD_GROUPS = 2
SSD_STATE = 128
SSD_CONV = 5
SSD_CHUNK = 128
SSD_CONV_DIM = SSD_INNER + 2 * SSD_GROUPS * SSD_STATE
SWA_DH = 128
SWA_HEADS = MIX_HALF // SWA_DH
SWA_KV_HEADS = SWA_HEADS // 4
SWA_WINDOW = 128
SWA_BLOCK = 128
FFN_HIDDEN = -(-8 * D_MODEL // (3 * 256)) * 256

N_EVEN = (DEPTH + 1) // 2
N_ODD = DEPTH // 2
EVEN_SPLITS = (GLA_HEADS * GLA_DK, GLA_HEADS * GLA_DK, GLA_HEADS * GLA_DV, GLA_HEADS * GLA_DV,
               2 * GLA_LOWRANK, NA_HEADS * NA_DH, NA_HEADS * NA_DH, NA_HEADS * NA_DH)
EVEN_IN = sum(EVEN_SPLITS)
EVEN_MIX = GLA_HEADS * GLA_DV + NA_HEADS * NA_DH
ODD_SPLITS = (SSD_INNER, SSD_CONV_DIM, 2 * SSD_HEADS, SWA_HEADS * SWA_DH,
              SWA_KV_HEADS * SWA_DH, SWA_KV_HEADS * SWA_DH)
ODD_IN = sum(ODD_SPLITS)
ODD_MIX = SSD_INNER + SWA_HEADS * SWA_DH

kernel_name = 'bidir_hybrid_gla_na_ssd_swa'


def split_cols(t, sizes):
    idx = [int(i) for i in np.cumsum(sizes)[:-1]]
    return jnp.split(t, idx, axis=-1)


def rms_norm(x, g):
    xf = x.astype(jnp.float32)
    y = xf * lax.rsqrt(jnp.mean(xf * xf, axis=-1, keepdims=True) + EPS)
    return (y * g.astype(jnp.float32)).astype(x.dtype)


def rope(x, pos):
    half = x.shape[-1] // 2
    inv = ROPE_THETA ** (-jnp.arange(half, dtype=jnp.float32) / half)
    ang = pos.astype(jnp.float32)[:, None] * inv[None, :]
    cos, sin = jnp.cos(ang), jnp.sin(ang)
    xf = x.astype(jnp.float32)
    x1, x2 = xf[..., :half], xf[..., half:]
    return jnp.concatenate([x1 * cos - x2 * sin, x2 * cos + x1 * sin], axis=-1).astype(x.dtype)


def flip_t(t, axis):
    return jnp.flip(t, axis=axis)


def gla_scan(q, k, v, log_a):
    bn, h, t, dk = q.shape
    dv = v.shape[-1]
    c = GLA_CHUNK
    n = t // c

    def to_chunks(a):
        return jnp.moveaxis(a.astype(jnp.float32).reshape(bn, h, n, c, a.shape[-1]), 2, 0)

    qc, kc, vc, ac = to_chunks(q), to_chunks(k), to_chunks(v), to_chunks(log_a)
    causal = jnp.tril(jnp.ones((c, c), dtype=bool))

    def step(s_prev, inp):
        qi, ki, vi, ai = inp
        b = jnp.cumsum(ai, axis=2)
        diff = b[:, :, :, None, :] - b[:, :, None, :, :]
        decay = jnp.exp(jnp.where(causal[:, :, None], diff, -jnp.inf))
        attn = jnp.einsum('bhtc,bhsc,bhtsc->bhts', qi, ki, decay)
        o = (jnp.einsum('bhts,bhsv->bhtv', attn, vi)
             + jnp.einsum('bhtc,bhcv->bhtv', qi * jnp.exp(b), s_prev))
        b_last = b[:, :, -1:, :]
        s_new = (jnp.exp(b_last[:, :, 0, :])[..., None] * s_prev
                 + jnp.einsum('bhsc,bhsv->bhcv', ki * jnp.exp(b_last - b), vi))
        return s_new, o

    s0 = jnp.zeros((bn, h, dk, dv), jnp.float32)
    _, o = lax.scan(step, s0, (qc, kc, vc, ac))
    return jnp.moveaxis(o, 0, 2).reshape(bn, h, t, dv)


def gla_mixer(q, k, v, g, lr, w_decay, b_decay, norm_g):
    bn, t, _ = q.shape

    def heads(a, d):
        return a.reshape(bn, t, GLA_HEADS, d).transpose(0, 2, 1, 3)

    qh = heads(q, GLA_DK) * (GLA_DK ** -0.5)
    kh = heads(k, GLA_DK)
    vh = heads(v, GLA_DV)
    lr = lr.reshape(bn, t, 2, GLA_LOWRANK)
    z = jnp.einsum('btdr,drk->dbtk', lr, w_decay) + b_decay[:, None, None, :]
    log_a = jax.nn.log_sigmoid(z.astype(jnp.float32)) / GLA_TAU
    o_f = gla_scan(qh, kh, vh, heads(log_a[0], GLA_DK))
    o_b = flip_t(gla_scan(flip_t(qh, 2), flip_t(kh, 2), flip_t(vh, 2),
                          flip_t(heads(log_a[1], GLA_DK), 2)), 2)
    o = o_f + o_b
    o = o * lax.rsqrt(jnp.mean(o * o, axis=-1, keepdims=True) + EPS)
    o = o.transpose(0, 2, 1, 3).reshape(bn, t, GLA_HEADS * GLA_DV) * norm_g.astype(jnp.float32)
    return (o * jax.nn.silu(g.astype(jnp.float32))).astype(q.dtype)


def na_mixer(q, k, v, rpb):
    bn, t, _ = q.shape
    rows = t // GRID_W
    kh_ = min(NA_KH, rows)

    def heads(a):
        return a.reshape(bn, t, NA_HEADS, NA_DH).transpose(0, 2, 1, 3)

    qh = heads(q) * (NA_DH ** -0.5)
    khd = heads(k)
    vhd = heads(v)
    cols = jnp.arange(GRID_W)
    col_start = jnp.clip(cols - NA_KW // 2, 0, GRID_W - NA_KW)
    col_idx = col_start[:, None] + jnp.arange(NA_KW)[None, :]
    dc = col_idx - cols[:, None] + (NA_KW - 1)

    def row_block(r):
        row_start = jnp.clip(r - kh_ // 2, 0, rows - kh_)
        row_idx = row_start + jnp.arange(kh_)
        dr = row_idx - r + (NA_KH - 1)
        tok = (row_idx[None, :, None] * GRID_W + col_idx[:, None, :]).reshape(GRID_W, kh_ * NA_KW)
        qb = lax.dynamic_slice_in_dim(qh, r * GRID_W, GRID_W, axis=2)
        kb = khd[:, :, tok]
        vb = vhd[:, :, tok]
        bias = rpb[:, dr[None, :, None], dc[:, None, :]].reshape(NA_HEADS, GRID_W, kh_ * NA_KW)
        s = jnp.einsum('bhqd,bhqkd->bhqk', qb, kb).astype(jnp.float32) + bias[None].astype(jnp.float32)
        p = jax.nn.softmax(s, axis=-1)
        return jnp.einsum('bhqk,bhqkd->bhqd', p.astype(vb.dtype), vb)

    o = lax.map(row_block, jnp.arange(rows))
    return o.transpose(1, 0, 3, 2, 4).reshape(bn, t, NA_HEADS * NA_DH)


def depthwise_conv(x, w, b):
    kw, cc = w.shape
    y = lax.conv_general_dilated(x, w[:, None, :].astype(x.dtype), window_strides=(1,),
                                 padding=[(kw // 2, kw // 2)],
                                 dimension_numbers=('NWC', 'WIO', 'NWC'),
                                 feature_group_count=cc)
    return y + b.astype(x.dtype)


def ssd_scan(xdt, da, bm, cm):
    bn, t, g, j, p = xdt.shape
    s = bm.shape[-1]
    l = SSD_CHUNK
    n = t // l
    x = xdt.reshape(bn, n, l, g, j, p)
    bc = bm.reshape(bn, n, l, g, s)
    cc = cm.reshape(bn, n, l, g, s)
    a_cum = jnp.cumsum(da.reshape(bn, n, l, g, j), axis=2)
    tri = jnp.tril(jnp.ones((l, l), dtype=bool))
    seg = a_cum[:, :, :, None] - a_cum[:, :, None]
    lmat = jnp.exp(jnp.where(tri[:, :, None, None], seg, -jnp.inf))
    cb = jnp.einsum('bnlgs,bnmgs->bnlmg', cc, bc)
    y_diag = jnp.einsum('bnlmg,bnlmgj,bnmgjp->bnlgjp', cb, lmat, x)
    decay_states = jnp.exp(a_cum[:, :, -1:] - a_cum)
    states = jnp.einsum('bnlgs,bnlgj,bnlgjp->bngjps', bc, decay_states, x)
    a_tot = a_cum[:, :, -1]
    a_tot_cum = jnp.cumsum(a_tot, axis=1)
    excl = a_tot_cum - a_tot
    tri_n = jnp.tril(jnp.ones((n, n), dtype=bool), k=-1)
    cdiff = excl[:, :, None] - a_tot_cum[:, None]
    cdec = jnp.exp(jnp.where(tri_n[:, :, None, None], cdiff, -jnp.inf))
    prev = jnp.einsum('bnmgj,bmgjps->bngjps', cdec, states)
    y_off = jnp.einsum('bnlgs,bngjps,bnlgj->bnlgjp', cc, prev, jnp.exp(a_cum))
    return (y_diag + y_off).reshape(bn, t, g, j, p)


def ssd_mixer(z, xbc, dt_raw, conv_w, conv_b, dt_bias, a_log, d_skip, norm_g):
    bn, t, _ = z.shape
    jh = SSD_HEADS // SSD_GROUPS
    xbc = jax.nn.silu(depthwise_conv(xbc, conv_w, conv_b)).astype(jnp.float32)
    xs, bm, cm = split_cols(xbc, (SSD_INNER, SSD_GROUPS * SSD_STATE, SSD_GROUPS * SSD_STATE))
    xh = xs.reshape(bn, t, SSD_GROUPS, jh, SSD_HEADDIM)
    bm = bm.reshape(bn, t, SSD_GROUPS, SSD_STATE)
    cm = cm.reshape(bn, t, SSD_GROUPS, SSD_STATE)
    dt = jax.nn.softplus(dt_raw.astype(jnp.float32).reshape(bn, t, 2, SSD_HEADS)
                         + dt_bias.astype(jnp.float32))
    a = -jnp.exp(a_log.astype(jnp.float32))
    dt_f = dt[:, :, 0].reshape(bn, t, SSD_GROUPS, jh)
    dt_b = dt[:, :, 1].reshape(bn, t, SSD_GROUPS, jh)
    da_f = dt_f * a[0].reshape(SSD_GROUPS, jh)
    da_b = dt_b * a[1].reshape(SSD_GROUPS, jh)
    y_f = ssd_scan(xh * dt_f[..., None], da_f, bm, cm)
    y_b = flip_t(ssd_scan(flip_t(xh * dt_b[..., None], 1), flip_t(da_b, 1),
                          flip_t(bm, 1), flip_t(cm, 1)), 1)
    y = y_f + y_b + xh * d_skip.astype(jnp.float32).reshape(SSD_GROUPS, jh)[:, :, None]
    y = y.reshape(bn, t, SSD_INNER) * jax.nn.silu(z.astype(jnp.float32))
    return rms_norm(y, norm_g).astype(z.dtype)


def swa_mixer(q, k, v, sink):
    bn, t, _ = q.shape
    gq = SWA_HEADS // SWA_KV_HEADS
    wb = SWA_BLOCK
    n = t // wb
    pos = jnp.arange(t)
    qh = rope(q.reshape(bn, t, SWA_HEADS, SWA_DH).transpose(0, 2, 1, 3), pos) * (SWA_DH ** -0.5)
    kh = rope(k.reshape(bn, t, SWA_KV_HEADS, SWA_DH).transpose(0, 2, 1, 3), pos)
    vh = v.reshape(bn, t, SWA_KV_HEADS, SWA_DH).transpose(0, 2, 1, 3)
    qb = qh.reshape(bn, SWA_KV_HEADS, gq, n, wb, SWA_DH)
    pad = ((0, 0), (0, 0), (wb, wb), (0, 0))

    def band(a):
        ap = jnp.pad(a, pad).reshape(bn, SWA_KV_HEADS, n + 2, wb, SWA_DH)
        return jnp.concatenate([ap[:, :, :-2], ap[:, :, 1:-1], ap[:, :, 2:]], axis=3)

    kb, vb = band(kh), band(vh)
    qi = pos.reshape(n, wb)
    kj = (jnp.arange(n)[:, None] - 1) * wb + jnp.arange(3 * wb)[None, :]
    valid = ((kj[:, None, :] >= 0) & (kj[:, None, :] < t)
             & (jnp.abs(qi[:, :, None] - kj[:, None, :]) <= SWA_WINDOW))
    s = jnp.einsum('bkgnqd,bkncd->bkgnqc', qb, kb).astype(jnp.float32)
    s = jnp.where(valid, s, -jnp.inf)
    sink_l = jnp.broadcast_to(sink.astype(jnp.float32).reshape(SWA_KV_HEADS, gq)[None, :, :, None, None, None],
                              s.shape[:-1] + (1,))
    p = jax.nn.softmax(jnp.concatenate([s, sink_l], axis=-1), axis=-1)[..., :-1]
    o = jnp.einsum('bkgnqc,bkncd->bkgnqd', p.astype(vb.dtype), vb)
    return o.transpose(0, 3, 4, 1, 2, 5).reshape(bn, t, SWA_HEADS * SWA_DH)


def setup_inputs(seed: int = 0) -> dict:
    key = jax.random.key(seed)
    ks = jax.random.split(key, 24)
    f32 = jnp.float32

    def nrm(k, shape, scale):
        return jax.random.normal(k, shape, f32) * scale

    ne, no = N_EVEN, N_ODD
    dt0 = jnp.exp(jax.random.uniform(ks[17], (no, 2, SSD_HEADS), f32,
                                     minval=math.log(1e-3), maxval=math.log(1e-1)))
    return {
        'x': nrm(ks[0], (BATCH, SEQ, D_MODEL), 1.0),
        'norm_gains': 1.0 + nrm(ks[1], (DEPTH, 4, D_MODEL), 0.05),
        'ffn_w_gate': nrm(ks[2], (DEPTH, D_MODEL, FFN_HIDDEN), D_MODEL ** -0.5),
        'ffn_w_up': nrm(ks[3], (DEPTH, D_MODEL, FFN_HIDDEN), D_MODEL ** -0.5),
        'ffn_w_down': nrm(ks[4], (DEPTH, FFN_HIDDEN, D_MODEL), FFN_HIDDEN ** -0.5),
        'even_w_in': nrm(ks[5], (ne, D_MODEL, EVEN_IN), D_MODEL ** -0.5),
        'even_w_out': nrm(ks[6], (ne, EVEN_MIX, D_MODEL), EVEN_MIX ** -0.5),
        'gla_w_decay': nrm(ks[7], (ne, 2, GLA_LOWRANK, GLA_HEADS * GLA_DK), GLA_LOWRANK ** -0.5),
        'gla_b_decay': nrm(ks[8], (ne, 2, GLA_HEADS * GLA_DK), 0.1),
        'gla_norm': 1.0 + nrm(ks[9], (ne, GLA_HEADS * GLA_DV), 0.05),
        'na_rpb': nrm(ks[10], (ne, NA_HEADS, 2 * NA_KH - 1, 2 * NA_KW - 1), 0.1),
        'odd_w_in': nrm(ks[11], (no, D_MODEL, ODD_IN), D_MODEL ** -0.5),
        'odd_w_out': nrm(ks[12], (no, ODD_MIX, D_MODEL), ODD_MIX ** -0.5),
        'ssd_conv_w': nrm(ks[13], (no, SSD_CONV, SSD_CONV_DIM), SSD_CONV ** -0.5),
        'ssd_conv_b': nrm(ks[14], (no, SSD_CONV_DIM), 0.02),
        'ssd_dt_bias': dt0 + jnp.log(-jnp.expm1(-dt0)),
        'ssd_a_log': jnp.log(jax.random.uniform(ks[15], (no, 2, SSD_HEADS), f32, minval=1.0, maxval=16.0)),
        'ssd_d': 1.0 + nrm(ks[16], (no, SSD_HEADS), 0.1),
        'ssd_norm': 1.0 + nrm(ks[18], (no, SSD_INNER), 0.05),
        'swa_sink': nrm(ks[19], (no, SWA_HEADS), 0.5),
    }


def reference(x, norm_gains, ffn_w_gate, ffn_w_up, ffn_w_down,
              even_w_in, even_w_out, gla_w_decay, gla_b_decay, gla_norm, na_rpb,
              odd_w_in, odd_w_out, ssd_conv_w, ssd_conv_b, ssd_dt_bias, ssd_a_log,
              ssd_d, ssd_norm, swa_sink):
    for layer in range(DEPTH):
        g = norm_gains[layer]
        h = rms_norm(x, g[0])
        i = layer // 2
        if layer % 2 == 0:
            proj = h @ even_w_in[i]
            gq, gk, gv, gg, glr, nq, nk, nv = split_cols(proj, EVEN_SPLITS)
            o_a = gla_mixer(gq, gk, gv, gg, glr, gla_w_decay[i], gla_b_decay[i], gla_norm[i])
            o_b = na_mixer(nq, nk, nv, na_rpb[i])
            mix = jnp.concatenate([o_a, o_b], axis=-1) @ even_w_out[i]
        else:
            proj = h @ odd_w_in[i]
            sz, sxbc, sdt, wq, wk, wv = split_cols(proj, ODD_SPLITS)
            o_c = ssd_mixer(sz, sxbc, sdt, ssd_conv_w[i], ssd_conv_b[i], ssd_dt_bias[i],
                            ssd_a_log[i], ssd_d[i], ssd_norm[i])
            o_d = swa_mixer(wq, wk, wv, swa_sink[i])
            mix = jnp.concatenate([o_c, o_d], axis=-1) @ odd_w_out[i]
        x = x + rms_norm(mix, g[1])
        h = rms_norm(x, g[2])
        f = (jax.nn.silu(h @ ffn_w_gate[layer]) * (h @ ffn_w_up[layer])) @ ffn_w_down[layer]
        x = x + rms_norm(f, g[3])
    return x
```

```python
import functools
import math

import jax
import jax.numpy as jnp
from jax import lax
from jax.experimental import pallas as pl
from jax.experimental.pallas import tpu as pltpu

F32 = jnp.float32
BF16 = jnp.bfloat16
HIGHEST = lax.Precision.HIGHEST

EPS = 1e-6
ROPE_THETA = 10000.0
GRID_W = 64
LANES = 128

GLA_DK = 64
GLA_DV = 128
GLA_HEADS = 8
GLA_LOWRANK = 16
GLA_TAU = 16.0
GLA_CHUNK = 64
GLA_SUB = 16
GLA_EXP_CLAMP = 60.0

NA_DH = 128
NA_HEADS = 8
NA_KH = 8
NA_KW = 16
NEG_BIG = -1e30

SSD_INNER = 1024
SSD_HEADDIM = 64
SSD_HEADS = 16
SSD_GROUPS = 2
SSD_STATE = 128
SSD_CONV = 5
SSD_CHUNK = 128
SSD_JH = SSD_HEADS // SSD_GROUPS

SWA_DH = 128
SWA_HEADS = 8
SWA_KV = 2
SWA_GQ = SWA_HEADS // SWA_KV
SWA_WINDOW = 128
SWA_BLOCK = 128

VMEM_LIMIT = 52 * 1024 * 1024


def _cparams(sem):
    return pltpu.CompilerParams(dimension_semantics=sem, vmem_limit_bytes=VMEM_LIMIT)


def _rms(x, g):
    return x * lax.rsqrt(jnp.mean(x * x, axis=-1, keepdims=True) + EPS) * g


def _silu(x):
    return x * (1.0 / (1.0 + jnp.exp(-x)))


def _dot_t(a, b):
    return lax.dot_general(a, b, (((1,), (1,)), ((), ())), preferred_element_type=F32)


def _inproj_body(x_ref, g_ref, w_ref, ws_ref, o_ref, os_ref, h_scr):
    @pl.when(pl.program_id(1) == 0)
    def _():
        h = _rms(x_ref[...], g_ref[...]).astype(BF16)
        h_scr[...] = h
        os_ref[...] = jnp.dot(h, ws_ref[...], preferred_element_type=F32)

    o_ref[...] = jnp.dot(h_scr[...], w_ref[...], preferred_element_type=F32).astype(o_ref.dtype)


def _inproj(x2, g, w_main, w_small, tm=512, tn=512):
    n, d = x2.shape
    nm = w_main.shape[1]
    ns = w_small.shape[1]
    return pl.pallas_call(
        _inproj_body,
        grid=(n // tm, nm // tn),
        in_specs=[
            pl.BlockSpec((tm, d), lambda i, j: (i, 0)),
            pl.BlockSpec((1, d), lambda i, j: (0, 0)),
            pl.BlockSpec((d, tn), lambda i, j: (0, j)),
            pl.BlockSpec((d, ns), lambda i, j: (0, 0)),
        ],
        out_specs=[
            pl.BlockSpec((tm, tn), lambda i, j: (i, j)),
            pl.BlockSpec((tm, ns), lambda i, j: (i, 0)),
        ],
        out_shape=[jax.ShapeDtypeStruct((n, nm), BF16), jax.ShapeDtypeStruct((n, ns), F32)],
        scratch_shapes=[pltpu.VMEM((tm, d), BF16)],
        compiler_params=_cparams(("parallel", "arbitrary")),
        name="inproj",
    )(x2, g, w_main, w_small)


def _outproj_even_body(a_ref, b_ref, wa_ref, wb_ref, g_ref, x_ref, o_ref):
    mix = (jnp.dot(a_ref[...], wa_ref[...], preferred_element_type=F32)
           + jnp.dot(b_ref[...], wb_ref[...], preferred_element_type=F32))
    o_ref[...] = x_ref[...] + _rms(mix, g_ref[...])


def _outproj_odd_body(y_ref, z_ref, ng_ref, b_ref, wa_ref, wb_ref, g_ref, x_ref, o_ref):
    y = y_ref[...].astype(F32) * _silu(z_ref[...].astype(F32))
    a = _rms(y, ng_ref[...]).astype(BF16)
    mix = (jnp.dot(a, wa_ref[...], preferred_element_type=F32)
           + jnp.dot(b_ref[...], wb_ref[...], preferred_element_type=F32))
    o_ref[...] = x_ref[...] + _rms(mix, g_ref[...])


def _outproj(x2, a, b, w_out, g, z_src=None, ssd_gain=None, tm=256):
    n, d = x2.shape
    half = a.shape[1]
    row = lambda i: (i, 0)
    fixed = lambda i: (0, 0)
    w_specs = [pl.BlockSpec((half, d), fixed), pl.BlockSpec((half, d), lambda i: (1, 0))]
    tail_specs = [pl.BlockSpec((1, d), fixed), pl.BlockSpec((tm, d), row)]
    if z_src is None:
        body = _outproj_even_body
        in_specs = [pl.BlockSpec((tm, half), row), pl.BlockSpec((tm, half), row)] + w_specs + tail_specs
        args = (a, b, w_out, w_out, g, x2)
    else:
        body = _outproj_odd_body
        in_specs = ([pl.BlockSpec((tm, half), row), pl.BlockSpec((tm, half), row),
                     pl.BlockSpec((1, half), fixed), pl.BlockSpec((tm, half), row)]
                    + w_specs + tail_specs)
        args = (a, z_src, ssd_gain, b, w_out, w_out, g, x2)
    return pl.pallas_call(
        body,
        grid=(n // tm,),
        in_specs=in_specs,
        out_specs=pl.BlockSpec((tm, d), row),
        out_shape=jax.ShapeDtypeStruct((n, d), F32),
        compiler_params=_cparams(("parallel",)),
        name="outproj",
    )(*args)


def _ffn_body(x_ref, g2_ref, wg_ref, wu_ref, wd_ref, g3_ref, o_ref, h_scr, acc_scr):
    k = pl.program_id(1)

    @pl.when(k == 0)
    def _():
        h_scr[...] = _rms(x_ref[...], g2_ref[...]).astype(BF16)
        acc_scr[...] = jnp.zeros_like(acc_scr)

    h = h_scr[...]
    gate = jnp.dot(h, wg_ref[...], preferred_element_type=F32)
    up = jnp.dot(h, wu_ref[...], preferred_element_type=F32)
    act = (_silu(gate) * up).astype(BF16)
    acc_scr[...] += jnp.dot(act, wd_ref[...], preferred_element_type=F32)

    @pl.when(k == pl.num_programs(1) - 1)
    def _():
        o_ref[...] = x_ref[...] + _rms(acc_scr[...], g3_ref[...])


def _ffn(x2, g2, wg, wu, wd, g3, tm=512, th=512):
    n, d = x2.shape
    hid = wg.shape[1]
    return pl.pallas_call(
        _ffn_body,
        grid=(n // tm, hid // th),
        in_specs=[
            pl.BlockSpec((tm, d), lambda i, k: (i, 0)),
            pl.BlockSpec((1, d), lambda i, k: (0, 0)),
            pl.BlockSpec((d, th), lambda i, k: (0, k)),
            pl.BlockSpec((d, th), lambda i, k: (0, k)),
            pl.BlockSpec((th, d), lambda i, k: (k, 0)),
            pl.BlockSpec((1, d), lambda i, k: (0, 0)),
        ],
        out_specs=pl.BlockSpec((tm, d), lambda i, k: (i, 0)),
        out_shape=jax.ShapeDtypeStruct((n, d), F32),
        scratch_shapes=[pltpu.VMEM((tm, d), BF16), pltpu.VMEM((tm, d), F32)],
        compiler_params=_cparams(("parallel", "arbitrary")),
        name="ffn",
    )(x2, g2, wg, wu, wd, g3)


def _gla_body(q_ref, k_ref, v_ref, g_ref, lr_ref, wd_ref, bd_ref, gn_ref, o_ref,
              la_scr, o_scr, st_scr, *, seq):
    c = GLA_CHUNK
    nc = seq // c
    nsub = c // GLA_SUB
    rb = min(512, seq)

    def la_blk(i, carry):
        rows = pl.ds(pl.multiple_of(i * rb, rb), rb)
        lr = lr_ref[0, rows, :]
        for d in range(2):
            z = jnp.dot(lr, wd_ref[d], precision=HIGHEST, preferred_element_type=F32) + bd_ref[d:d + 1, :]
            la = (jnp.minimum(z, 0.0) - jnp.log1p(jnp.exp(-jnp.abs(z)))) * (1.0 / GLA_TAU)
            la_scr[d, rows, :] = la
        return carry

    lax.fori_loop(0, seq // rb, la_blk, 0)

    row_i = lax.broadcasted_iota(jnp.int32, (c, c), 0)
    col_i = lax.broadcasted_iota(jnp.int32, (c, c), 1)
    lane = lax.broadcasted_iota(jnp.int32, (1, LANES), 1)
    hmask = (lane < GLA_DK, lane >= GLA_DK)

    def chunk(n, d):
        rows = pl.ds(pl.multiple_of(n * c, c), c)
        tri = (row_i >= col_i) if d == 0 else (row_i <= col_i)
        la = la_scr[d, rows, :]
        bc = jnp.dot(tri.astype(F32), la, precision=HIGHEST, preferred_element_type=F32)
        q = q_ref[0, rows, :].astype(F32) * (GLA_DK ** -0.5)
        k = k_ref[0, rows, :].astype(F32)
        v = v_ref[0, rows, :]
        b_edge = bc[c - 1:c] if d == 0 else bc[0:1]
        qhat = q * jnp.exp(bc)
        khat = (k * jnp.exp(b_edge - bc)).astype(BF16)
        att_rows = ([], [])
        for i in range(nsub):
            lo = GLA_SUB * i
            hi = lo + GLA_SUB
            if d == 0:
                ref = bc[lo - 1:lo] if i > 0 else jnp.zeros((1, LANES), F32)
            else:
                ref = bc[hi:hi + 1] if i < nsub - 1 else jnp.zeros((1, LANES), F32)
            qi = q[lo:hi] * jnp.exp(bc[lo:hi] - ref)
            ki = (k * jnp.exp(jnp.minimum(ref - bc, GLA_EXP_CLAMP))).astype(BF16)
            for hh in range(2):
                qim = jnp.where(hmask[hh], qi, 0.0).astype(BF16)
                att_rows[hh].append(_dot_t(qim, ki))
        outs = []
        for hh in range(2):
            att = jnp.concatenate(att_rows[hh], axis=0)
            att = jnp.where(tri, att, 0.0).astype(BF16)
            vh = v[:, GLA_DV * hh:GLA_DV * (hh + 1)]
            st = st_scr[hh]
            qh = jnp.where(hmask[hh], qhat, 0.0).astype(BF16)
            o = jnp.dot(att, vh, preferred_element_type=F32) + _dot_t(qh, st.astype(BF16))
            st_scr[hh] = st * jnp.exp(b_edge) + lax.dot_general(
                vh, khat, (((0,), (0,)), ((), ())), preferred_element_type=F32)
            outs.append(o)
        return jnp.concatenate(outs, axis=1)

    st_scr[...] = jnp.zeros_like(st_scr)

    def fwd(n, carry):
        rows = pl.ds(pl.multiple_of(n * c, c), c)
        o_scr[rows, :] = chunk(n, 0)
        return carry

    lax.fori_loop(0, nc, fwd, 0)
    st_scr[...] = jnp.zeros_like(st_scr)

    def bwd(i, carry):
        n = nc - 1 - i
        rows = pl.ds(pl.multiple_of(n * c, c), c)
        o = o_scr[rows, :] + chunk(n, 1)
        parts = []
        for hh in range(2):
            oh = o[:, GLA_DV * hh:GLA_DV * (hh + 1)]
            parts.append(oh * lax.rsqrt(jnp.mean(oh * oh, axis=-1, keepdims=True) + EPS))
        o = jnp.concatenate(parts, axis=1) * gn_ref[...]
        o_ref[0, rows, :] = (o * _silu(g_ref[0, rows, :].astype(F32))).astype(o_ref.dtype)
        return carry

    lax.fori_loop(0, nc, bwd, 0)


def _gla(proj3, small3, wdec_pad, bdec, gnorm):
    bn, seq, _ = proj3.shape
    npair = GLA_HEADS // 2
    qoff = 0
    koff = (GLA_HEADS * GLA_DK) // LANES
    voff = (2 * GLA_HEADS * GLA_DK) // (2 * GLA_DV)
    goff = voff + npair
    return pl.pallas_call(
        functools.partial(_gla_body, seq=seq),
        grid=(bn, npair),
        in_specs=[
            pl.BlockSpec((1, seq, LANES), lambda b, p: (b, 0, qoff + p)),
            pl.BlockSpec((1, seq, LANES), lambda b, p: (b, 0, koff + p)),
            pl.BlockSpec((1, seq, 2 * GLA_DV), lambda b, p: (b, 0, voff + p)),
            pl.BlockSpec((1, seq, 2 * GLA_DV), lambda b, p: (b, 0, goff + p)),
            pl.BlockSpec((1, seq, LANES), lambda b, p: (b, 0, 0)),
            pl.BlockSpec((2, LANES, LANES), lambda b, p: (0, 0, p)),
            pl.BlockSpec((2, LANES), lambda b, p: (0, p)),
            pl.BlockSpec((1, 2 * GLA_DV), lambda b, p: (0, p)),
        ],
        out_specs=pl.BlockSpec((1, seq, 2 * GLA_DV), lambda b, p: (b, 0, p)),
        out_shape=jax.ShapeDtypeStruct((bn, seq, GLA_HEADS * GLA_DV), BF16),
        scratch_shapes=[
            pltpu.VMEM((2, seq, LANES), F32),
            pltpu.VMEM((seq, 2 * GLA_DV), F32),
            pltpu.VMEM((2, GLA_DV, LANES), F32),
        ],
        compiler_params=_cparams(("parallel", "parallel")),
        name="gla",
    )(proj3, proj3, proj3, proj3, small3, wdec_pad, bdec, gnorm)


def _na_bias_table(rpb):
    cols = jnp.arange(GRID_W)
    cs = jnp.clip(cols - NA_KW // 2, 0, GRID_W - NA_KW)
    valid = (cols[None, :] >= cs[:, None]) & (cols[None, :] < cs[:, None] + NA_KW)
    dc = jnp.clip(cols[None, :] - cols[:, None] + NA_KW - 1, 0, 2 * NA_KW - 2)
    dr = jnp.arange(NA_KH)[:, None] + jnp.arange(NA_KH)[None, :]
    t = rpb[:, dr][:, :, :, dc]
    t = jnp.where(valid[None, None, None], t.astype(F32), NEG_BIG)
    return t.transpose(0, 1, 3, 2, 4).reshape(rpb.shape[0], NA_KH, GRID_W, NA_KH * GRID_W)


def _na_body(q_ref, k_ref, v_ref, bias_ref, o_ref, *, nrows):
    w = GRID_W
    nk = NA_KH * w

    def row(r, carry):
        rs = jnp.clip(r - NA_KH // 2, 0, nrows - NA_KH)
        off = rs - r + (NA_KH - 1)
        qrows = pl.ds(pl.multiple_of(r * w, w), w)
        krows = pl.ds(pl.multiple_of(rs * w, w), nk)
        s = _dot_t(q_ref[0, qrows, :], k_ref[0, krows, :]) * (NA_DH ** -0.5) + bias_ref[0, off]
        m = jnp.max(s, axis=-1, keepdims=True)
        p = jnp.exp(s - m)
        l = jnp.sum(p, axis=-1, keepdims=True)
        o = jnp.dot(p.astype(BF16), v_ref[0, krows, :], preferred_element_type=F32) / l
        o_ref[0, qrows, :] = o.astype(o_ref.dtype)
        return carry

    lax.fori_loop(0, nrows, row, 0)


def _na(proj3, bias_tab, col0):
    bn, seq, _ = proj3.shape
    nrows = seq // GRID_W
    qoff = col0 // NA_DH
    koff = qoff + NA_HEADS
    voff = koff + NA_HEADS
    return pl.pallas_call(
        functools.partial(_na_body, nrows=nrows),
        grid=(bn, NA_HEADS),
        in_specs=[
            pl.BlockSpec((1, seq, NA_DH), lambda b, h: (b, 0, qoff + h)),
            pl.BlockSpec((1, seq, NA_DH), lambda b, h: (b, 0, koff + h)),
            pl.BlockSpec((1, seq, NA_DH), lambda b, h: (b, 0, voff + h)),
            pl.BlockSpec((1, NA_KH, GRID_W, NA_KH * GRID_W), lambda b, h: (h, 0, 0, 0)),
        ],
        out_specs=pl.BlockSpec((1, seq, NA_DH), lambda b, h: (b, 0, h)),
        out_shape=jax.ShapeDtypeStruct((bn, seq, NA_HEADS * NA_DH), BF16),
        compiler_params=_cparams(("parallel", "parallel")),
        name="na",
    )(proj3, proj3, proj3, bias_tab)


CONV_PAD = 16


def _conv_body(x_ref, w_ref, b_ref, o_ref, xp_scr, *, seq):
    rb = min(256, seq)
    ch = x_ref.shape[-1]
    zeros = jnp.zeros((CONV_PAD, ch), xp_scr.dtype)
    xp_scr[0:CONV_PAD, :] = zeros
    xp_scr[seq + CONV_PAD:seq + 2 * CONV_PAD, :] = zeros
    xp_scr[CONV_PAD:seq + CONV_PAD, :] = x_ref[0]

    def blk(i, carry):
        r0 = pl.multiple_of(i * rb, rb)
        xw = xp_scr[pl.ds(r0, rb + 2 * CONV_PAD), :].astype(F32)
        acc = jnp.zeros((rb, ch), F32) + b_ref[...]
        for j in range(SSD_CONV):
            s0 = CONV_PAD - SSD_CONV // 2 + j
            acc = acc + xw[s0:s0 + rb] * w_ref[j:j + 1, :]
        o_ref[0, pl.ds(r0, rb), :] = _silu(acc).astype(o_ref.dtype)
        return carry

    lax.fori_loop(0, seq // rb, blk, 0)


def _conv(proj3, conv_w, conv_b, col0, tc=512):
    bn, seq, _ = proj3.shape
    cdim = conv_w.shape[1]
    c0 = col0 // tc
    return pl.pallas_call(
        functools.partial(_conv_body, seq=seq),
        grid=(bn, cdim // tc),
        in_specs=[
            pl.BlockSpec((1, seq, tc), lambda b, j: (b, 0, c0 + j)),
            pl.BlockSpec((SSD_CONV, tc), lambda b, j: (0, j)),
            pl.BlockSpec((1, tc), lambda b, j: (0, j)),
        ],
        out_specs=pl.BlockSpec((1, seq, tc), lambda b, j: (b, 0, j)),
        out_shape=jax.ShapeDtypeStruct((bn, seq, cdim), BF16),
        scratch_shapes=[pltpu.VMEM((seq + 2 * CONV_PAD, tc), BF16)],
        compiler_params=_cparams(("parallel", "parallel")),
        name="ssd_conv",
    )(proj3, conv_w, conv_b)


def _softplus(x):
    return jnp.maximum(x, 0.0) + jnp.log1p(jnp.exp(-jnp.abs(x)))


def _ssd_body(xs_ref, bm_ref, cm_ref, dt_ref, dtt_ref, pb_ref, pbt_ref, al_ref, alt_ref,
              dsk_ref, ex_ref, o_ref, st_scr, *, seq):
    l = SSD_CHUNK
    nc = seq // l
    gw = SSD_JH * SSD_HEADDIM
    row_i = lax.broadcasted_iota(jnp.int32, (l, l), 0)
    col_i = lax.broadcasted_iota(jnp.int32, (l, l), 1)
    lane = lax.broadcasted_iota(jnp.int32, (1, LANES), 1)
    first_head = lane < SSD_HEADDIM
    a_row = -jnp.exp(al_ref[0])
    a_col = -jnp.exp(alt_ref[0])

    def chunk(n, d):
        rows = pl.ds(pl.multiple_of(n * l, l), l)
        tri = (row_i >= col_i) if d == 0 else (row_i <= col_i)
        trif = tri.astype(F32)
        x = xs_ref[0, rows, :]
        bm = bm_ref[0, rows, :]
        cm = cm_ref[0, rows, :]
        cb = _dot_t(cm, bm)
        bmt = bm.astype(F32).T
        dtv = _softplus(dt_ref[0, rows, :] + pb_ref[0])
        acum = jnp.dot(trif, dtv * a_row, precision=HIGHEST, preferred_element_type=F32)
        dtv_t = _softplus(dtt_ref[0, 0, :, rows] + pbt_ref[0])
        acum_t = lax.dot_general(dtv_t * a_col, trif, (((1,), (1,)), ((), ())),
                                 precision=HIGHEST, preferred_element_type=F32)
        a_edge = acum[l - 1:l] if d == 0 else acum[0:1]
        a_edge_t = acum_t[:, l - 1:l] if d == 0 else acum_t[:, 0:1]
        w_state_t = dtv_t * jnp.exp(a_edge_t - acum_t)
        w_off = jnp.exp(acum)
        edge_ch = jnp.dot(jnp.broadcast_to(jnp.exp(a_edge), (8, LANES)), ex_ref[d],
                          precision=HIGHEST, preferred_element_type=F32)[0:1]
        st = st_scr[...]
        st_b = st.astype(BF16)
        ys = []
        new_st = []
        for jj in range(SSD_JH // 2):
            xp = x[:, LANES * jj:LANES * (jj + 1)]
            stp = st_b[:, LANES * jj:LANES * (jj + 1)]
            rhs = jnp.concatenate([xp, stp], axis=0)
            y2 = []
            s2 = []
            for hh in range(2):
                cidx = d * SSD_JH + 2 * jj + hh
                seg = acum[:, cidx:cidx + 1] - acum_t[cidx:cidx + 1, :]
                lmat = jnp.where(tri, jnp.exp(jnp.minimum(seg, 0.0)), 0.0)
                m = cb * lmat * dtv_t[cidx:cidx + 1, :]
                cw = cm.astype(F32) * w_off[:, cidx:cidx + 1]
                lhs = jnp.concatenate([m.astype(BF16), cw.astype(BF16)], axis=1)
                y2.append(jnp.dot(lhs, rhs, preferred_element_type=F32))
                bw = (bmt * w_state_t[cidx:cidx + 1, :]).astype(BF16)
                s2.append(jnp.dot(bw, xp, preferred_element_type=F32))
            ys.append(jnp.where(first_head, y2[0], y2[1]))
            new_st.append(jnp.where(first_head, s2[0], s2[1]))
        st_scr[...] = st * edge_ch + jnp.concatenate(new_st, axis=1)
        return jnp.concatenate(ys, axis=1)

    st_scr[...] = jnp.zeros_like(st_scr)

    def fwd(n, carry):
        rows = pl.ds(pl.multiple_of(n * l, l), l)
        o_ref[0, rows, :] = chunk(n, 0).astype(o_ref.dtype)
        return carry

    lax.fori_loop(0, nc, fwd, 0)
    st_scr[...] = jnp.zeros_like(st_scr)

    def bwd(i, carry):
        n = nc - 1 - i
        rows = pl.ds(pl.multiple_of(n * l, l), l)
        y = (o_ref[0, rows, :].astype(F32) + chunk(n, 1)
             + xs_ref[0, rows, :].astype(F32) * dsk_ref[...])
        o_ref[0, rows, :] = y.astype(o_ref.dtype)
        return carry

    lax.fori_loop(0, nc, bwd, 0)


def _ssd(xbc_act, dt3, dtt4, pb, pbt, al, alt, dskip, expand):
    bn, seq, _ = xbc_act.shape
    gw = SSD_JH * SSD_HEADDIM
    boff = SSD_INNER // SSD_STATE
    coff = boff + SSD_GROUPS
    nh2 = 2 * SSD_JH
    return pl.pallas_call(
        functools.partial(_ssd_body, seq=seq),
        grid=(bn, SSD_GROUPS),
        in_specs=[
            pl.BlockSpec((1, seq, gw), lambda b, g: (b, 0, g)),
            pl.BlockSpec((1, seq, SSD_STATE), lambda b, g: (b, 0, boff + g)),
            pl.BlockSpec((1, seq, SSD_STATE), lambda b, g: (b, 0, coff + g)),
            pl.BlockSpec((1, seq, LANES), lambda b, g: (b, 0, g)),
            pl.BlockSpec((1, 1, nh2, seq), lambda b, g: (b, g, 0, 0)),
            pl.BlockSpec((1, 1, LANES), lambda b, g: (g, 0, 0)),
            pl.BlockSpec((1, nh2, 1), lambda b, g: (g, 0, 0)),
            pl.BlockSpec((1, 1, LANES), lambda b, g: (g, 0, 0)),
            pl.BlockSpec((1, nh2, 1), lambda b, g: (g, 0, 0)),
            pl.BlockSpec((1, gw), lambda b, g: (0, g)),
            pl.BlockSpec((2, LANES, gw), lambda b, g: (0, 0, 0)),
        ],
        out_specs=pl.BlockSpec((1, seq, gw), lambda b, g: (b, 0, g)),
        out_shape=jax.ShapeDtypeStruct((bn, seq, SSD_INNER), BF16),
        scratch_shapes=[pltpu.VMEM((SSD_STATE, gw), F32)],
        compiler_params=_cparams(("parallel", "parallel")),
        name="ssd_scan",
    )(xbc_act, xbc_act, xbc_act, dt3, dtt4, pb, pbt, al, alt, dskip, expand)


def _rope_tables(seq):
    half = SWA_DH // 2
    inv = ROPE_THETA ** (-jnp.arange(half, dtype=F32) / half)
    ang = jnp.arange(seq, dtype=F32)[:, None] * inv[None, :]
    cos, sin = jnp.cos(ang), jnp.sin(ang)
    return jnp.concatenate([cos, cos], axis=1), jnp.concatenate([-sin, sin], axis=1)


def _rope(x, cos, sin_signed):
    return x * cos + pltpu.roll(x, SWA_DH // 2, 1) * sin_signed


def _swa_body(sink_ref, q_ref, k_ref, v_ref, cos_ref, sin_ref, o_ref, kr_scr, *, seq):
    wb = SWA_BLOCK
    nb = seq // wb
    nkeys = 3 * wb
    kvh = pl.program_id(1)
    rb = min(512, seq)

    def krope(i, carry):
        rows = pl.ds(pl.multiple_of(i * rb, rb), rb)
        kr_scr[rows, :] = _rope(k_ref[0, rows, :].astype(F32), cos_ref[rows, :],
                                sin_ref[rows, :]).astype(BF16)
        return carry

    lax.fori_loop(0, seq // rb, krope, 0)

    qpos_l = lax.broadcasted_iota(jnp.int32, (wb, nkeys), 0)
    kpos_l = lax.broadcasted_iota(jnp.int32, (wb, nkeys), 1)

    def blk(n, carry):
        rows = pl.ds(pl.multiple_of(n * wb, wb), wb)
        ks = pl.multiple_of(jnp.clip(n - 1, 0, nb - 3) * wb, wb)
        krows = pl.ds(ks, nkeys)
        cos = cos_ref[rows, :]
        sin = sin_ref[rows, :]
        kk = kr_scr[krows, :]
        vv = v_ref[0, krows, :]
        valid = jnp.abs((qpos_l + n * wb) - (kpos_l + ks)) <= SWA_WINDOW
        qb = q_ref[0, rows, :].astype(F32)
        for hh in range(SWA_GQ):
            sink = sink_ref[kvh * SWA_GQ + hh]
            qh = (_rope(qb[:, SWA_DH * hh:SWA_DH * (hh + 1)], cos, sin) * (SWA_DH ** -0.5)).astype(BF16)
            s = jnp.where(valid, _dot_t(qh, kk), NEG_BIG)
            m = jnp.maximum(jnp.max(s, axis=-1, keepdims=True), sink)
            p = jnp.exp(s - m)
            den = jnp.sum(p, axis=-1, keepdims=True) + jnp.exp(sink - m)
            o = jnp.dot(p.astype(BF16), vv, preferred_element_type=F32) / den
            o_ref[0, rows, SWA_DH * hh:SWA_DH * (hh + 1)] = o.astype(o_ref.dtype)
        return carry

    lax.fori_loop(0, nb, blk, 0)


def _swa(proj3, sink, cos, sin, col0):
    bn, seq, _ = proj3.shape
    qw = SWA_GQ * SWA_DH
    qoff = col0 // qw
    koff = (col0 + SWA_HEADS * SWA_DH) // SWA_DH
    voff = koff + SWA_KV
    return pl.pallas_call(
        functools.partial(_swa_body, seq=seq),
        grid=(bn, SWA_KV),
        in_specs=[
            pl.BlockSpec(memory_space=pltpu.SMEM),
            pl.BlockSpec((1, seq, qw), lambda b, h: (b, 0, qoff + h)),
            pl.BlockSpec((1, seq, SWA_DH), lambda b, h: (b, 0, koff + h)),
            pl.BlockSpec((1, seq, SWA_DH), lambda b, h: (b, 0, voff + h)),
            pl.BlockSpec((seq, SWA_DH), lambda b, h: (0, 0)),
            pl.BlockSpec((seq, SWA_DH), lambda b, h: (0, 0)),
        ],
        out_specs=pl.BlockSpec((1, seq, qw), lambda b, h: (b, 0, h)),
        out_shape=jax.ShapeDtypeStruct((bn, seq, SWA_HEADS * SWA_DH), BF16),
        scratch_shapes=[pltpu.VMEM((seq, SWA_DH), BF16)],
        compiler_params=_cparams(("parallel", "parallel")),
        name="swa",
    )(sink, proj3, proj3, proj3, cos, sin)


def _pad_cols(w, width):
    return jnp.pad(w, ((0, 0), (0, width - w.shape[1])))


def _even_mixers(x2, bn, seq, g0, w_in, w_decay, b_decay, gla_norm, rpb):
    d = x2.shape[1]
    n_gla = 2 * GLA_HEADS * GLA_DK + 2 * GLA_HEADS * GLA_DV
    n_lr = 2 * GLA_LOWRANK
    w_main = jnp.concatenate([w_in[:, :n_gla], w_in[:, n_gla + n_lr:]], axis=1).astype(BF16)
    w_small = _pad_cols(w_in[:, n_gla:n_gla + n_lr], LANES).astype(BF16)
    proj, small = _inproj(x2, g0, w_main, w_small)
    proj3 = proj.reshape(bn, seq, -1)
    small3 = small.reshape(bn, seq, -1)
    wdec = jnp.zeros((2, LANES, w_decay.shape[-1]), F32)
    for dd in range(2):
        wdec = wdec.at[dd, dd * GLA_LOWRANK:(dd + 1) * GLA_LOWRANK].set(w_decay[dd])
    o_a = _gla(proj3, small3, wdec, b_decay, gla_norm[None, :])
    o_b = _na(proj3, _na_bias_table(rpb), n_gla)
    return o_a.reshape(-1, o_a.shape[-1]), o_b.reshape(-1, o_b.shape[-1])


def _per_group(v):
    return v.reshape(2, SSD_GROUPS, SSD_JH).transpose(1, 0, 2).reshape(SSD_GROUPS, 2 * SSD_JH)


def _odd_mixers(x2, bn, seq, g0, w_in, conv_w, conv_b, dt_bias, a_log, d_skip, sink, cos, sin):
    conv_dim = SSD_INNER + 2 * SSD_GROUPS * SSD_STATE
    n_z = SSD_INNER
    n_dt = 2 * SSD_HEADS
    c_dt = n_z + conv_dim
    w_main = jnp.concatenate([w_in[:, :c_dt], w_in[:, c_dt + n_dt:]], axis=1).astype(BF16)
    w_dt = w_in[:, c_dt:c_dt + n_dt].reshape(-1, 2, SSD_GROUPS, SSD_JH).transpose(0, 2, 1, 3)
    w_dt = w_dt.reshape(-1, SSD_GROUPS, 2 * SSD_JH)
    w_small = jnp.pad(w_dt, ((0, 0), (0, 0), (0, LANES - 2 * SSD_JH))).reshape(-1, SSD_GROUPS * LANES)
    proj, small = _inproj(x2, g0, w_main, w_small.astype(BF16))
    proj3 = proj.reshape(bn, seq, -1)
    dt3 = small.reshape(bn, seq, SSD_GROUPS * LANES)
    dtt4 = dt3.reshape(bn, seq, SSD_GROUPS, LANES)[..., :2 * SSD_JH].transpose(0, 2, 3, 1)
    xbc_act = _conv(proj3, conv_w, conv_b[None, :], n_z)
    pbg = _per_group(dt_bias)
    alg = _per_group(a_log)
    pad = ((0, 0), (0, LANES - 2 * SSD_JH))
    head_of_ch = jnp.arange(SSD_JH * SSD_HEADDIM) // SSD_HEADDIM
    expand = (jnp.arange(LANES)[None, :, None]
              == (jnp.arange(2)[:, None, None] * SSD_JH + head_of_ch[None, None, :])).astype(F32)
    dskip = jnp.repeat(d_skip.astype(F32), SSD_HEADDIM)[None, :]
    y_pre = _ssd(xbc_act, dt3, dtt4,
                 jnp.pad(pbg, pad)[:, None, :], pbg[:, :, None],
                 jnp.pad(alg, pad)[:, None, :], alg[:, :, None], dskip, expand)
    o_d = _swa(proj3, sink, cos, sin, c_dt)
    return y_pre.reshape(-1, SSD_INNER), proj, o_d.reshape(-1, o_d.shape[-1])


def kernel(x, norm_gains, ffn_w_gate, ffn_w_up, ffn_w_down, even_w_in, even_w_out, gla_w_decay,
           gla_b_decay, gla_norm, na_rpb, odd_w_in, odd_w_out, ssd_conv_w, ssd_conv_b, ssd_dt_bias,
           ssd_a_log, ssd_d, ssd_norm, swa_sink):
    bn, seq, d = x.shape
    depth = norm_gains.shape[0]
    x2 = x.reshape(bn * seq, d)
    cos, sin = _rope_tables(seq)
    for layer in range(depth):
        g = norm_gains[layer][:, None, :]
        i = layer // 2
        if layer % 2 == 0:
            o_a, o_b = _even_mixers(x2, bn, seq, g[0], even_w_in[i], gla_w_decay[i], gla_b_decay[i],
                                    gla_norm[i], na_rpb[i])
            x2 = _outproj(x2, o_a, o_b, even_w_out[i].astype(BF16), g[1])
        else:
            y_pre, proj, o_d = _odd_mixers(x2, bn, seq, g[0], odd_w_in[i], ssd_conv_w[i], ssd_conv_b[i],
                                           ssd_dt_bias[i], ssd_a_log[i], ssd_d[i], swa_sink[i], cos, sin)
            x2 = _outproj(x2, y_pre, o_d, odd_w_out[i].astype(BF16), g[1],
                          z_src=proj, ssd_gain=ssd_norm[i][None, :])
        x2 = _ffn(x2, g[2], ffn_w_gate[layer].astype(BF16), ffn_w_up[layer].astype(BF16),
                  ffn_w_down[layer].astype(BF16), g[3])
    return x2.reshape(bn, seq, d)
```

```python
import functools
import math

import jax
import jax.numpy as jnp
from jax import lax
from jax.experimental import pallas as pl
from jax.experimental.pallas import tpu as pltpu

F32 = jnp.float32
BF16 = jnp.bfloat16
HIGHEST = lax.Precision.HIGHEST

EPS = 1e-6
ROPE_THETA = 10000.0
GRID_W = 64
LANES = 128

GLA_DK = 64
GLA_DV = 128
GLA_HEADS = 8
GLA_LOWRANK = 16
GLA_TAU = 16.0
GLA_CHUNK = 64
GLA_SUB = 16
GLA_EXP_CLAMP = 60.0
GLA_TRI_BLOCK = 256

NA_DH = 128
NA_HEADS = 8
NA_KH = 8
NA_KW = 16
NA_ROWS_PER_ITER = 4
NEG_BIG = -1e30

SSD_INNER = 1024
SSD_HEADDIM = 64
SSD_HEADS = 16
SSD_GROUPS = 2
SSD_STATE = 128
SSD_CONV = 5
SSD_CHUNK = 128
SSD_JH = SSD_HEADS // SSD_GROUPS

SWA_DH = 128
SWA_HEADS = 8
SWA_KV = 2
SWA_GQ = SWA_HEADS // SWA_KV
SWA_WINDOW = 128
SWA_BLOCK = 128

VMEM_LIMIT = 52 * 1024 * 1024


def _cparams(sem):
    return pltpu.CompilerParams(dimension_semantics=sem, vmem_limit_bytes=VMEM_LIMIT)


def _rms(x, g):
    return x * lax.rsqrt(jnp.mean(x * x, axis=-1, keepdims=True) + EPS) * g


def _silu(x):
    return x * (1.0 / (1.0 + jnp.exp(-x)))


def _dot_t(a, b):
    return lax.dot_general(a, b, (((1,), (1,)), ((), ())), preferred_element_type=F32)


def _inproj_body(x_ref, g_ref, w_ref, ws_ref, o_ref, os_ref, h_scr):
    @pl.when(pl.program_id(1) == 0)
    def _():
        h = _rms(x_ref[...], g_ref[...]).astype(BF16)
        h_scr[...] = h
        os_ref[...] = jnp.dot(h, ws_ref[...], preferred_element_type=F32)

    o_ref[...] = jnp.dot(h_scr[...], w_ref[...], preferred_element_type=F32).astype(o_ref.dtype)


def _inproj(x2, g, w_main, w_small, tm=1024, tn=512):
    n, d = x2.shape
    nm = w_main.shape[1]
    ns = w_small.shape[1]
    return pl.pallas_call(
        _inproj_body,
        grid=(n // tm, nm // tn),
        in_specs=[
            pl.BlockSpec((tm, d), lambda i, j: (i, 0)),
            pl.BlockSpec((1, d), lambda i, j: (0, 0)),
            pl.BlockSpec((d, tn), lambda i, j: (0, j)),
            pl.BlockSpec((d, ns), lambda i, j: (0, 0)),
        ],
        out_specs=[
            pl.BlockSpec((tm, tn), lambda i, j: (i, j)),
            pl.BlockSpec((tm, ns), lambda i, j: (i, 0)),
        ],
        out_shape=[jax.ShapeDtypeStruct((n, nm), BF16), jax.ShapeDtypeStruct((n, ns), F32)],
        scratch_shapes=[pltpu.VMEM((tm, d), BF16)],
        compiler_params=_cparams(("parallel", "arbitrary")),
        name="inproj",
    )(x2, g, w_main, w_small)


def _outproj_even_body(a_ref, b_ref, wa_ref, wb_ref, g_ref, x_ref, o_ref):
    mix = (jnp.dot(a_ref[...], wa_ref[...], preferred_element_type=F32)
           + jnp.dot(b_ref[...], wb_ref[...], preferred_element_type=F32))
    o_ref[...] = x_ref[...] + _rms(mix, g_ref[...])


def _outproj_odd_body(y_ref, z_ref, ng_ref, b_ref, wa_ref, wb_ref, g_ref, x_ref, o_ref):
    y = y_ref[...].astype(F32) * _silu(z_ref[...].astype(F32))
    a = _rms(y, ng_ref[...]).astype(BF16)
    mix = (jnp.dot(a, wa_ref[...], preferred_element_type=F32)
           + jnp.dot(b_ref[...], wb_ref[...], preferred_element_type=F32))
    o_ref[...] = x_ref[...] + _rms(mix, g_ref[...])


def _outproj(x2, a, b, w_out, g, z_src=None, ssd_gain=None, tm=256):
    n, d = x2.shape
    half = a.shape[1]
    row = lambda i: (i, 0)
    fixed = lambda i: (0, 0)
    w_specs = [pl.BlockSpec((half, d), fixed), pl.BlockSpec((half, d), lambda i: (1, 0))]
    tail_specs = [pl.BlockSpec((1, d), fixed), pl.BlockSpec((tm, d), row)]
    if z_src is None:
        body = _outproj_even_body
        in_specs = [pl.BlockSpec((tm, half), row), pl.BlockSpec((tm, half), row)] + w_specs + tail_specs
        args = (a, b, w_out, w_out, g, x2)
    else:
        body = _outproj_odd_body
        in_specs = ([pl.BlockSpec((tm, half), row), pl.BlockSpec((tm, half), row),
                     pl.BlockSpec((1, half), fixed), pl.BlockSpec((tm, half), row)]
                    + w_specs + tail_specs)
        args = (a, z_src, ssd_gain, b, w_out, w_out, g, x2)
    return pl.pallas_call(
        body,
        grid=(n // tm,),
        in_specs=in_specs,
        out_specs=pl.BlockSpec((tm, d), row),
        out_shape=jax.ShapeDtypeStruct((n, d), F32),
        compiler_params=_cparams(("parallel",)),
        name="outproj",
    )(*args)


def _ffn_body(x_ref, g2_ref, wg_ref, wu_ref, wd_ref, g3_ref, o_ref, h_scr, acc_scr):
    k = pl.program_id(1)

    @pl.when(k == 0)
    def _():
        h_scr[...] = _rms(x_ref[...], g2_ref[...]).astype(BF16)
        acc_scr[...] = jnp.zeros_like(acc_scr)

    h = h_scr[...]
    gate = jnp.dot(h, wg_ref[...], preferred_element_type=F32)
    up = jnp.dot(h, wu_ref[...], preferred_element_type=F32)
    act = (_silu(gate) * up).astype(BF16)
    acc_scr[...] += jnp.dot(act, wd_ref[...], preferred_element_type=F32)

    @pl.when(k == pl.num_programs(1) - 1)
    def _():
        o_ref[...] = x_ref[...] + _rms(acc_scr[...], g3_ref[...])


def _ffn(x2, g2, wg, wu, wd, g3, tm=512, th=512):
    n, d = x2.shape
    hid = wg.shape[1]
    return pl.pallas_call(
        _ffn_body,
        grid=(n // tm, hid // th),
        in_specs=[
            pl.BlockSpec((tm, d), lambda i, k: (i, 0)),
            pl.BlockSpec((1, d), lambda i, k: (0, 0)),
            pl.BlockSpec((d, th), lambda i, k: (0, k)),
            pl.BlockSpec((d, th), lambda i, k: (0, k)),
            pl.BlockSpec((th, d), lambda i, k: (k, 0)),
            pl.BlockSpec((1, d), lambda i, k: (0, 0)),
        ],
        out_specs=pl.BlockSpec((tm, d), lambda i, k: (i, 0)),
        out_shape=jax.ShapeDtypeStruct((n, d), F32),
        scratch_shapes=[pltpu.VMEM((tm, d), BF16), pltpu.VMEM((tm, d), F32)],
        compiler_params=_cparams(("parallel", "arbitrary")),
        name="ffn",
    )(x2, g2, wg, wu, wd, g3)


def _gla_body(q_ref, k_ref, v_ref, g_ref, lr_ref, wd_ref, bd_ref, gn_ref, o_ref,
              bc_scr, o_scr, st_scr, *, seq):
    c = GLA_CHUNK
    nc = seq // c
    half = nc // 2
    nsub = c // GLA_SUB
    rb = min(512, seq)
    row_i = lax.broadcasted_iota(jnp.int32, (c, c), 0)
    col_i = lax.broadcasted_iota(jnp.int32, (c, c), 1)
    tri = (row_i >= col_i, row_i <= col_i)
    lane = lax.broadcasted_iota(jnp.int32, (1, LANES), 1)
    hmask = (lane < GLA_DK, lane >= GLA_DK)

    w2 = jnp.concatenate([wd_ref[0], wd_ref[1]], axis=1)
    w_hi = w2.astype(BF16)
    w_lo = (w2 - w_hi.astype(F32)).astype(BF16)
    w_cat = jnp.concatenate([w_hi, w_lo, w_hi], axis=0)
    tb = GLA_TRI_BLOCK
    brow = lax.broadcasted_iota(jnp.int32, (tb, tb), 0)
    bcol = lax.broadcasted_iota(jnp.int32, (tb, tb), 1)
    same_chunk = (brow // c) == (bcol // c)
    blk_tri = ((same_chunk & (brow >= bcol)).astype(BF16), (same_chunk & (brow <= bcol)).astype(BF16))

    def pre(i, carry):
        r0 = pl.multiple_of(i * rb, rb)
        lr = lr_ref[0, pl.ds(r0, rb), :]
        lr_hi = lr.astype(BF16)
        lr_lo = (lr - lr_hi.astype(F32)).astype(BF16)
        z2 = jnp.dot(jnp.concatenate([lr_hi, lr_hi, lr_lo], axis=1), w_cat, preferred_element_type=F32)
        las = []
        for d in range(2):
            z = z2[:, LANES * d:LANES * (d + 1)] + bd_ref[d:d + 1, :]
            la = (jnp.minimum(z, 0.0) - jnp.log1p(jnp.exp(-jnp.abs(z)))) * (1.0 / GLA_TAU)
            la_hi = la.astype(BF16)
            las.append(jnp.concatenate([la_hi, (la - la_hi.astype(F32)).astype(BF16)], axis=1))
        for d in range(2):
            for t in range(rb // tb):
                s2 = jnp.dot(blk_tri[d], las[d][t * tb:(t + 1) * tb], preferred_element_type=F32)
                bc_scr[d, pl.ds(r0 + t * tb, tb), :] = s2[:, :LANES] + s2[:, LANES:]
        return carry

    lax.fori_loop(0, seq // rb, pre, 0)

    def stage1(n, d):
        rows = pl.ds(pl.multiple_of(n * c, c), c)
        bc = bc_scr[d, rows, :]
        q = q_ref[0, rows, :].astype(F32) * (GLA_DK ** -0.5)
        k = k_ref[0, rows, :].astype(F32)
        v = v_ref[0, rows, :]
        b_edge = bc[c - 1:c] if d == 0 else bc[0:1]
        qhat = q * jnp.exp(bc)
        khat = (k * jnp.exp(b_edge - bc)).astype(BF16)
        att_rows = ([], [])
        for i in range(nsub):
            lo = GLA_SUB * i
            hi = lo + GLA_SUB
            if d == 0:
                ref = bc[lo - 1:lo] if i > 0 else jnp.zeros((1, LANES), F32)
            else:
                ref = bc[hi:hi + 1] if i < nsub - 1 else jnp.zeros((1, LANES), F32)
            qi = q[lo:hi] * jnp.exp(bc[lo:hi] - ref)
            ki = (k * jnp.exp(jnp.minimum(ref - bc, GLA_EXP_CLAMP))).astype(BF16)
            for hh in range(2):
                qim = jnp.where(hmask[hh], qi, 0.0).astype(BF16)
                att_rows[hh].append(_dot_t(qim, ki))
        heads = []
        for hh in range(2):
            att = jnp.concatenate(att_rows[hh], axis=0)
            att = jnp.where(tri[d], att, 0.0).astype(BF16)
            vh = v[:, GLA_DV * hh:GLA_DV * (hh + 1)]
            kv = lax.dot_general(vh, khat, (((0,), (0,)), ((), ())), preferred_element_type=F32)
            qh = jnp.where(hmask[hh], qhat, 0.0).astype(BF16)
            heads.append((att, vh, kv, qh))
        return d, heads, jnp.exp(b_edge)

    def stage2(s1):
        d, heads, decay = s1
        outs = []
        for hh, (att, vh, kv, qh) in enumerate(heads):
            st = st_scr[d, hh]
            outs.append(jnp.dot(att, vh, preferred_element_type=F32) + _dot_t(qh, st.astype(BF16)))
            st_scr[d, hh] = st * decay + kv
        return jnp.concatenate(outs, axis=1)

    def finish(n, o):
        rows = pl.ds(pl.multiple_of(n * c, c), c)
        parts = []
        for hh in range(2):
            oh = o[:, GLA_DV * hh:GLA_DV * (hh + 1)]
            parts.append(oh * lax.rsqrt(jnp.mean(oh * oh, axis=-1, keepdims=True) + EPS))
        o = jnp.concatenate(parts, axis=1) * gn_ref[...]
        o_ref[0, rows, :] = (o * _silu(g_ref[0, rows, :].astype(F32))).astype(o_ref.dtype)

    st_scr[...] = jnp.zeros_like(st_scr)

    cpi = 2 if half % 2 == 0 else 1

    def walk(i, base_f, base_b):
        work = []
        for u in range(cpi):
            work.append((base_f + i * cpi + u, 0))
            work.append((base_b - i * cpi - u, 1))
        prepared = [stage1(n, d) for n, d in work]
        return [(n, stage2(s1)) for (n, _), s1 in zip(work, prepared)]

    def first(i, carry):
        for n, o in walk(i, 0, nc - 1):
            o_scr[pl.ds(pl.multiple_of(n * c, c), c), :] = o
        return carry

    lax.fori_loop(0, half // cpi, first, 0)

    def second(i, carry):
        for n, o in walk(i, half, half - 1):
            finish(n, o_scr[pl.ds(pl.multiple_of(n * c, c), c), :] + o)
        return carry

    lax.fori_loop(0, half // cpi, second, 0)


def _gla(proj3, small3, wdec_pad, bdec, gnorm):
    bn, seq, _ = proj3.shape
    npair = GLA_HEADS // 2
    qoff = 0
    koff = (GLA_HEADS * GLA_DK) // LANES
    voff = (2 * GLA_HEADS * GLA_DK) // (2 * GLA_DV)
    goff = voff + npair
    return pl.pallas_call(
        functools.partial(_gla_body, seq=seq),
        grid=(bn, npair),
        in_specs=[
            pl.BlockSpec((1, seq, LANES), lambda b, p: (b, 0, qoff + p)),
            pl.BlockSpec((1, seq, LANES), lambda b, p: (b, 0, koff + p)),
            pl.BlockSpec((1, seq, 2 * GLA_DV), lambda b, p: (b, 0, voff + p)),
            pl.BlockSpec((1, seq, 2 * GLA_DV), lambda b, p: (b, 0, goff + p)),
            pl.BlockSpec((1, seq, LANES), lambda b, p: (b, 0, 0)),
            pl.BlockSpec((2, LANES, LANES), lambda b, p: (0, 0, p)),
            pl.BlockSpec((2, LANES), lambda b, p: (0, p)),
            pl.BlockSpec((1, 2 * GLA_DV), lambda b, p: (0, p)),
        ],
        out_specs=pl.BlockSpec((1, seq, 2 * GLA_DV), lambda b, p: (b, 0, p)),
        out_shape=jax.ShapeDtypeStruct((bn, seq, GLA_HEADS * GLA_DV), BF16),
        scratch_shapes=[
            pltpu.VMEM((2, seq, LANES), F32),
            pltpu.VMEM((seq, 2 * GLA_DV), F32),
            pltpu.VMEM((2, 2, GLA_DV, LANES), F32),
        ],
        compiler_params=_cparams(("parallel", "parallel")),
        name="gla",
    )(proj3, proj3, proj3, proj3, small3, wdec_pad, bdec, gnorm)


def _na_bias_table(rpb):
    cols = jnp.arange(GRID_W)
    cs = jnp.clip(cols - NA_KW // 2, 0, GRID_W - NA_KW)
    valid = (cols[None, :] >= cs[:, None]) & (cols[None, :] < cs[:, None] + NA_KW)
    dc = jnp.clip(cols[None, :] - cols[:, None] + NA_KW - 1, 0, 2 * NA_KW - 2)
    dr = jnp.arange(NA_KH)[:, None] + jnp.arange(NA_KH)[None, :]
    t = rpb[:, dr][:, :, :, dc]
    t = jnp.where(valid[None, None, None], t.astype(F32), NEG_BIG)
    return t.transpose(0, 1, 3, 2, 4).reshape(rpb.shape[0], NA_KH, GRID_W, NA_KH * GRID_W)


def _na_body(q_ref, k_ref, v_ref, bias_ref, o_ref, *, nrows):
    w = GRID_W
    nk = NA_KH * w

    def group(i, carry):
        idx = []
        scores = []
        for u in range(NA_ROWS_PER_ITER):
            r = i * NA_ROWS_PER_ITER + u
            rs = jnp.clip(r - NA_KH // 2, 0, nrows - NA_KH)
            qrows = pl.ds(pl.multiple_of(r * w, w), w)
            krows = pl.ds(pl.multiple_of(rs * w, w), nk)
            idx.append((qrows, krows, rs - r + (NA_KH - 1)))
            scores.append(_dot_t(q_ref[0, qrows, :], k_ref[0, krows, :]))
        probs = []
        for (qrows, krows, off), s in zip(idx, scores):
            s = s * (NA_DH ** -0.5) + bias_ref[0, off]
            p = jnp.exp(s - jnp.max(s, axis=-1, keepdims=True))
            probs.append((p.astype(BF16), jnp.sum(p, axis=-1, keepdims=True)))
        for (qrows, krows, off), (p, l) in zip(idx, probs):
            o = jnp.dot(p, v_ref[0, krows, :], preferred_element_type=F32) / l
            o_ref[0, qrows, :] = o.astype(o_ref.dtype)
        return carry

    lax.fori_loop(0, nrows // NA_ROWS_PER_ITER, group, 0)


def _na(proj3, bias_tab, col0):
    bn, seq, _ = proj3.shape
    nrows = seq // GRID_W
    qoff = col0 // NA_DH
    koff = qoff + NA_HEADS
    voff = koff + NA_HEADS
    return pl.pallas_call(
        functools.partial(_na_body, nrows=nrows),
        grid=(bn, NA_HEADS),
        in_specs=[
            pl.BlockSpec((1, seq, NA_DH), lambda b, h: (b, 0, qoff + h)),
            pl.BlockSpec((1, seq, NA_DH), lambda b, h: (b, 0, koff + h)),
            pl.BlockSpec((1, seq, NA_DH), lambda b, h: (b, 0, voff + h)),
            pl.BlockSpec((1, NA_KH, GRID_W, NA_KH * GRID_W), lambda b, h: (h, 0, 0, 0)),
        ],
        out_specs=pl.BlockSpec((1, seq, NA_DH), lambda b, h: (b, 0, h)),
        out_shape=jax.ShapeDtypeStruct((bn, seq, NA_HEADS * NA_DH), BF16),
        compiler_params=_cparams(("parallel", "parallel")),
        name="na",
    )(proj3, proj3, proj3, bias_tab)


CONV_PAD = 16


def _conv_body(x_ref, w_ref, b_ref, o_ref, xp_scr, *, seq):
    rb = min(256, seq)
    ch = x_ref.shape[-1]
    zeros = jnp.zeros((CONV_PAD, ch), xp_scr.dtype)
    xp_scr[0:CONV_PAD, :] = zeros
    xp_scr[seq + CONV_PAD:seq + 2 * CONV_PAD, :] = zeros
    xp_scr[CONV_PAD:seq + CONV_PAD, :] = x_ref[0]

    def blk(i, carry):
        r0 = pl.multiple_of(i * rb, rb)
        xw = xp_scr[pl.ds(r0, rb + 2 * CONV_PAD), :].astype(F32)
        acc = jnp.zeros((rb, ch), F32) + b_ref[...]
        for j in range(SSD_CONV):
            s0 = CONV_PAD - SSD_CONV // 2 + j
            acc = acc + xw[s0:s0 + rb] * w_ref[j:j + 1, :]
        o_ref[0, pl.ds(r0, rb), :] = _silu(acc).astype(o_ref.dtype)
        return carry

    lax.fori_loop(0, seq // rb, blk, 0)


def _conv(proj3, conv_w, conv_b, col0, tc=512):
    bn, seq, _ = proj3.shape
    cdim = conv_w.shape[1]
    c0 = col0 // tc
    return pl.pallas_call(
        functools.partial(_conv_body, seq=seq),
        grid=(bn, cdim // tc),
        in_specs=[
            pl.BlockSpec((1, seq, tc), lambda b, j: (b, 0, c0 + j)),
            pl.BlockSpec((SSD_CONV, tc), lambda b, j: (0, j)),
            pl.BlockSpec((1, tc), lambda b, j: (0, j)),
        ],
        out_specs=pl.BlockSpec((1, seq, tc), lambda b, j: (b, 0, j)),
        out_shape=jax.ShapeDtypeStruct((bn, seq, cdim), BF16),
        scratch_shapes=[pltpu.VMEM((seq + 2 * CONV_PAD, tc), BF16)],
        compiler_params=_cparams(("parallel", "parallel")),
        name="ssd_conv",
    )(proj3, conv_w, conv_b)


def _softplus(x):
    return jnp.maximum(x, 0.0) + jnp.log1p(jnp.exp(-jnp.abs(x)))


def _ssd_body(xs_ref, bm_ref, cm_ref, dt_ref, dtt_ref, pb_ref, pbt_ref, al_ref, alt_ref,
              dsk_ref, ex_ref, o_ref, st_scr, *, seq):
    l = SSD_CHUNK
    nc = seq // l
    gw = SSD_JH * SSD_HEADDIM
    row_i = lax.broadcasted_iota(jnp.int32, (l, l), 0)
    col_i = lax.broadcasted_iota(jnp.int32, (l, l), 1)
    lane = lax.broadcasted_iota(jnp.int32, (1, LANES), 1)
    first_head = lane < SSD_HEADDIM
    a_row = -jnp.exp(al_ref[0])
    a_col = -jnp.exp(alt_ref[0])

    def chunk(n, d):
        rows = pl.ds(pl.multiple_of(n * l, l), l)
        tri = (row_i >= col_i) if d == 0 else (row_i <= col_i)
        trif = tri.astype(F32)
        x = xs_ref[0, rows, :]
        bm = bm_ref[0, rows, :]
        cm = cm_ref[0, rows, :]
        cb = _dot_t(cm, bm)
        bmt = bm.astype(F32).T
        dtv = _softplus(dt_ref[0, rows, :] + pb_ref[0])
        acum = jnp.dot(trif, dtv * a_row, precision=HIGHEST, preferred_element_type=F32)
        dtv_t = _softplus(dtt_ref[0, 0, :, rows] + pbt_ref[0])
        acum_t = lax.dot_general(dtv_t * a_col, trif, (((1,), (1,)), ((), ())),
                                 precision=HIGHEST, preferred_element_type=F32)
        a_edge = acum[l - 1:l] if d == 0 else acum[0:1]
        a_edge_t = acum_t[:, l - 1:l] if d == 0 else acum_t[:, 0:1]
        w_state_t = dtv_t * jnp.exp(a_edge_t - acum_t)
        w_off = jnp.exp(acum)
        edge_ch = jnp.dot(jnp.broadcast_to(jnp.exp(a_edge), (8, LANES)), ex_ref[d],
                          precision=HIGHEST, preferred_element_type=F32)[0:1]
        st = st_scr[...]
        st_b = st.astype(BF16)
        ys = []
        new_st = []
        for jj in range(SSD_JH // 2):
            xp = x[:, LANES * jj:LANES * (jj + 1)]
            stp = st_b[:, LANES * jj:LANES * (jj + 1)]
            rhs = jnp.concatenate([xp, stp], axis=0)
            y2 = []
            s2 = []
            for hh in range(2):
                cidx = d * SSD_JH + 2 * jj + hh
                seg = acum[:, cidx:cidx + 1] - acum_t[cidx:cidx + 1, :]
                lmat = jnp.where(tri, jnp.exp(jnp.minimum(seg, 0.0)), 0.0)
                m = cb * lmat * dtv_t[cidx:cidx + 1, :]
                cw = cm.astype(F32) * w_off[:, cidx:cidx + 1]
                lhs = jnp.concatenate([m.astype(BF16), cw.astype(BF16)], axis=1)
                y2.append(jnp.dot(lhs, rhs, preferred_element_type=F32))
                bw = (bmt * w_state_t[cidx:cidx + 1, :]).astype(BF16)
                s2.append(jnp.dot(bw, xp, preferred_element_type=F32))
            ys.append(jnp.where(first_head, y2[0], y2[1]))
            new_st.append(jnp.where(first_head, s2[0], s2[1]))
        st_scr[...] = st * edge_ch + jnp.concatenate(new_st, axis=1)
        return jnp.concatenate(ys, axis=1)

    st_scr[...] = jnp.zeros_like(st_scr)

    def fwd(n, carry):
        rows = pl.ds(pl.multiple_of(n * l, l), l)
        o_ref[0, rows, :] = chunk(n, 0).astype(o_ref.dtype)
        return carry

    lax.fori_loop(0, nc, fwd, 0)
    st_scr[...] = jnp.zeros_like(st_scr)

    def bwd(i, carry):
        n = nc - 1 - i
        rows = pl.ds(pl.multiple_of(n * l, l), l)
        y = (o_ref[0, rows, :].astype(F32) + chunk(n, 1)
             + xs_ref[0, rows, :].astype(F32) * dsk_ref[...])
        o_ref[0, rows, :] = y.astype(o_ref.dtype)
        return carry

    lax.fori_loop(0, nc, bwd, 0)


def _ssd(xbc_act, dt3, dtt4, pb, pbt, al, alt, dskip, expand):
    bn, seq, _ = xbc_act.shape
    gw = SSD_JH * SSD_HEADDIM
    boff = SSD_INNER // SSD_STATE
    coff = boff + SSD_GROUPS
    nh2 = 2 * SSD_JH
    return pl.pallas_call(
        functools.partial(_ssd_body, seq=seq),
        grid=(bn, SSD_GROUPS),
        in_specs=[
            pl.BlockSpec((1, seq, gw), lambda b, g: (b, 0, g)),
            pl.BlockSpec((1, seq, SSD_STATE), lambda b, g: (b, 0, boff + g)),
            pl.BlockSpec((1, seq, SSD_STATE), lambda b, g: (b, 0, coff + g)),
            pl.BlockSpec((1, seq, LANES), lambda b, g: (b, 0, g)),
            pl.BlockSpec((1, 1, nh2, seq), lambda b, g: (b, g, 0, 0)),
            pl.BlockSpec((1, 1, LANES), lambda b, g: (g, 0, 0)),
            pl.BlockSpec((1, nh2, 1), lambda b, g: (g, 0, 0)),
            pl.BlockSpec((1, 1, LANES), lambda b, g: (g, 0, 0)),
            pl.BlockSpec((1, nh2, 1), lambda b, g: (g, 0, 0)),
            pl.BlockSpec((1, gw), lambda b, g: (0, g)),
            pl.BlockSpec((2, LANES, gw), lambda b, g: (0, 0, 0)),
        ],
        out_specs=pl.BlockSpec((1, seq, gw), lambda b, g: (b, 0, g)),
        out_shape=jax.ShapeDtypeStruct((bn, seq, SSD_INNER), BF16),
        scratch_shapes=[pltpu.VMEM((SSD_STATE, gw), F32)],
        compiler_params=_cparams(("parallel", "parallel")),
        name="ssd_scan",
    )(xbc_act, xbc_act, xbc_act, dt3, dtt4, pb, pbt, al, alt, dskip, expand)


def _rope_tables(seq):
    half = SWA_DH // 2
    inv = ROPE_THETA ** (-jnp.arange(half, dtype=F32) / half)
    ang = jnp.arange(seq, dtype=F32)[:, None] * inv[None, :]
    cos, sin = jnp.cos(ang), jnp.sin(ang)
    return jnp.concatenate([cos, cos], axis=1), jnp.concatenate([-sin, sin], axis=1)


def _rope(x, cos, sin_signed):
    return x * cos + pltpu.roll(x, SWA_DH // 2, 1) * sin_signed


def _swa_body(sink_ref, q_ref, k_ref, v_ref, cos_ref, sin_ref, o_ref, kr_scr, *, seq):
    wb = SWA_BLOCK
    nb = seq // wb
    nkeys = 3 * wb
    kvh = pl.program_id(1)
    rb = min(512, seq)

    def krope(i, carry):
        rows = pl.ds(pl.multiple_of(i * rb, rb), rb)
        kr_scr[rows, :] = _rope(k_ref[0, rows, :].astype(F32), cos_ref[rows, :],
                                sin_ref[rows, :]).astype(BF16)
        return carry

    lax.fori_loop(0, seq // rb, krope, 0)

    qpos_l = lax.broadcasted_iota(jnp.int32, (wb, nkeys), 0)
    kpos_l = lax.broadcasted_iota(jnp.int32, (wb, nkeys), 1)

    def blk(n, carry):
        rows = pl.ds(pl.multiple_of(n * wb, wb), wb)
        ks = pl.multiple_of(jnp.clip(n - 1, 0, nb - 3) * wb, wb)
        krows = pl.ds(ks, nkeys)
        cos = cos_ref[rows, :]
        sin = sin_ref[rows, :]
        kk = kr_scr[krows, :]
        vv = v_ref[0, krows, :]
        valid = jnp.abs((qpos_l + n * wb) - (kpos_l + ks)) <= SWA_WINDOW
        qb = q_ref[0, rows, :].astype(F32)
        scores = []
        for hh in range(SWA_GQ):
            qh = (_rope(qb[:, SWA_DH * hh:SWA_DH * (hh + 1)], cos, sin) * (SWA_DH ** -0.5)).astype(BF16)
            scores.append(_dot_t(qh, kk))
        probs = []
        for hh, s in enumerate(scores):
            sink = sink_ref[kvh * SWA_GQ + hh]
            s = jnp.where(valid, s, NEG_BIG)
            m = jnp.maximum(jnp.max(s, axis=-1, keepdims=True), sink)
            p = jnp.exp(s - m)
            probs.append((p.astype(BF16), jnp.sum(p, axis=-1, keepdims=True) + jnp.exp(sink - m)))
        for hh, (p, den) in enumerate(probs):
            o = jnp.dot(p, vv, preferred_element_type=F32) / den
            o_ref[0, rows, SWA_DH * hh:SWA_DH * (hh + 1)] = o.astype(o_ref.dtype)
        return carry

    lax.fori_loop(0, nb, blk, 0)


def _swa(proj3, sink, cos, sin, col0):
    bn, seq, _ = proj3.shape
    qw = SWA_GQ * SWA_DH
    qoff = col0 // qw
    koff = (col0 + SWA_HEADS * SWA_DH) // SWA_DH
    voff = koff + SWA_KV
    return pl.pallas_call(
        functools.partial(_swa_body, seq=seq),
        grid=(bn, SWA_KV),
        in_specs=[
            pl.BlockSpec(memory_space=pltpu.SMEM),
            pl.BlockSpec((1, seq, qw), lambda b, h: (b, 0, qoff + h)),
            pl.BlockSpec((1, seq, SWA_DH), lambda b, h: (b, 0, koff + h)),
            pl.BlockSpec((1, seq, SWA_DH), lambda b, h: (b, 0, voff + h)),
            pl.BlockSpec((seq, SWA_DH), lambda b, h: (0, 0)),
            pl.BlockSpec((seq, SWA_DH), lambda b, h: (0, 0)),
        ],
        out_specs=pl.BlockSpec((1, seq, qw), lambda b, h: (b, 0, h)),
        out_shape=jax.ShapeDtypeStruct((bn, seq, SWA_HEADS * SWA_DH), BF16),
        scratch_shapes=[pltpu.VMEM((seq, SWA_DH), BF16)],
        compiler_params=_cparams(("parallel", "parallel")),
        name="swa",
    )(sink, proj3, proj3, proj3, cos, sin)


def _pad_cols(w, width):
    return jnp.pad(w, ((0, 0), (0, width - w.shape[1])))


def _even_mixers(x2, bn, seq, g0, w_in, w_decay, b_decay, gla_norm, rpb):
    d = x2.shape[1]
    n_gla = 2 * GLA_HEADS * GLA_DK + 2 * GLA_HEADS * GLA_DV
    n_lr = 2 * GLA_LOWRANK
    w_main = jnp.concatenate([w_in[:, :n_gla], w_in[:, n_gla + n_lr:]], axis=1).astype(BF16)
    w_small = _pad_cols(w_in[:, n_gla:n_gla + n_lr], LANES).astype(BF16)
    proj, small = _inproj(x2, g0, w_main, w_small)
    proj3 = proj.reshape(bn, seq, -1)
    small3 = small.reshape(bn, seq, -1)
    wdec = jnp.zeros((2, LANES, w_decay.shape[-1]), F32)
    for dd in range(2):
        wdec = wdec.at[dd, dd * GLA_LOWRANK:(dd + 1) * GLA_LOWRANK].set(w_decay[dd])
    o_a = _gla(proj3, small3, wdec, b_decay, gla_norm[None, :])
    o_b = _na(proj3, _na_bias_table(rpb), n_gla)
    return o_a.reshape(-1, o_a.shape[-1]), o_b.reshape(-1, o_b.shape[-1])


def _per_group(v):
    return v.reshape(2, SSD_GROUPS, SSD_JH).transpose(1, 0, 2).reshape(SSD_GROUPS, 2 * SSD_JH)


def _odd_mixers(x2, bn, seq, g0, w_in, conv_w, conv_b, dt_bias, a_log, d_skip, sink, cos, sin):
    conv_dim = SSD_INNER + 2 * SSD_GROUPS * SSD_STATE
    n_z = SSD_INNER
    n_dt = 2 * SSD_HEADS
    c_dt = n_z + conv_dim
    w_main = jnp.concatenate([w_in[:, :c_dt], w_in[:, c_dt + n_dt:]], axis=1).astype(BF16)
    w_dt = w_in[:, c_dt:c_dt + n_dt].reshape(-1, 2, SSD_GROUPS, SSD_JH).transpose(0, 2, 1, 3)
    w_dt = w_dt.reshape(-1, SSD_GROUPS, 2 * SSD_JH)
    w_small = jnp.pad(w_dt, ((0, 0), (0, 0), (0, LANES - 2 * SSD_JH))).reshape(-1, SSD_GROUPS * LANES)
    proj, small = _inproj(x2, g0, w_main, w_small.astype(BF16))
    proj3 = proj.reshape(bn, seq, -1)
    dt3 = small.reshape(bn, seq, SSD_GROUPS * LANES)
    dtt4 = dt3.reshape(bn, seq, SSD_GROUPS, LANES)[..., :2 * SSD_JH].transpose(0, 2, 3, 1)
    xbc_act = _conv(proj3, conv_w, conv_b[None, :], n_z)
    pbg = _per_group(dt_bias)
    alg = _per_group(a_log)
    pad = ((0, 0), (0, LANES - 2 * SSD_JH))
    head_of_ch = jnp.arange(SSD_JH * SSD_HEADDIM) // SSD_HEADDIM
    expand = (jnp.arange(LANES)[None, :, None]
              == (jnp.arange(2)[:, None, None] * SSD_JH + head_of_ch[None, None, :])).astype(F32)
    dskip = jnp.repeat(d_skip.astype(F32), SSD_HEADDIM)[None, :]
    y_pre = _ssd(xbc_act, dt3, dtt4,
                 jnp.pad(pbg, pad)[:, None, :], pbg[:, :, None],
                 jnp.pad(alg, pad)[:, None, :], alg[:, :, None], dskip, expand)
    o_d = _swa(proj3, sink, cos, sin, c_dt)
    return y_pre.reshape(-1, SSD_INNER), proj, o_d.reshape(-1, o_d.shape[-1])


def kernel(x, norm_gains, ffn_w_gate, ffn_w_up, ffn_w_down, even_w_in, even_w_out, gla_w_decay,
           gla_b_decay, gla_norm, na_rpb, odd_w_in, odd_w_out, ssd_conv_w, ssd_conv_b, ssd_dt_bias,
           ssd_a_log, ssd_d, ssd_norm, swa_sink):
    bn, seq, d = x.shape
    depth = norm_gains.shape[0]
    x2 = x.reshape(bn * seq, d)
    cos, sin = _rope_tables(seq)
    for layer in range(depth):
        g = norm_gains[layer][:, None, :]
        i = layer // 2
        if layer % 2 == 0:
            o_a, o_b = _even_mixers(x2, bn, seq, g[0], even_w_in[i], gla_w_decay[i], gla_b_decay[i],
                                    gla_norm[i], na_rpb[i])
            x2 = _outproj(x2, o_a, o_b, even_w_out[i].astype(BF16), g[1])
        else:
            y_pre, proj, o_d = _odd_mixers(x2, bn, seq, g[0], odd_w_in[i], ssd_conv_w[i], ssd_conv_b[i],
                                           ssd_dt_bias[i], ssd_a_log[i], ssd_d[i], swa_sink[i], cos, sin)
            x2 = _outproj(x2, y_pre, o_d, odd_w_out[i].astype(BF16), g[1],
                          z_src=proj, ssd_gain=ssd_norm[i][None, :])
        x2 = _ffn(x2, g[2], ffn_w_gate[layer].astype(BF16), ffn_w_up[layer].astype(BF16),
                  ffn_w_down[layer].astype(BF16), g[3])
    return x2.reshape(bn, seq, d)
```

```python
import functools
import math

import jax
import jax.numpy as jnp
import numpy as np
from jax import lax
from jax.experimental import pallas as pl
from jax.experimental.pallas import tpu as pltpu

F32 = jnp.float32
BF16 = jnp.bfloat16
HIGHEST = lax.Precision.HIGHEST

EPS = 1e-6
ROPE_THETA = 10000.0
GRID_W = 64
LANES = 128

GLA_DK = 64
GLA_DV = 128
GLA_HEADS = 8
GLA_LOWRANK = 16
GLA_TAU = 16.0
GLA_CHUNK = 64
GLA_SUB = 16
GLA_EXP_CLAMP = 60.0
GLA_TRI_BLOCK = 256

NA_DH = 128
NA_HEADS = 8
NA_KH = 8
NA_KW = 16
NA_ROWS_PER_ITER = 4
NEG_BIG = -1e30

SSD_INNER = 1024
SSD_HEADDIM = 64
SSD_HEADS = 16
SSD_GROUPS = 2
SSD_STATE = 128
SSD_CONV = 5
SSD_CHUNK = 128
SSD_JH = SSD_HEADS // SSD_GROUPS

SWA_DH = 128
SWA_HEADS = 8
SWA_KV = 2
SWA_GQ = SWA_HEADS // SWA_KV
SWA_WINDOW = 128
SWA_BLOCK = 128

VMEM_LIMIT = 52 * 1024 * 1024


def _cparams(sem):
    return pltpu.CompilerParams(dimension_semantics=sem, vmem_limit_bytes=VMEM_LIMIT)


def _rms(x, g):
    return x * lax.rsqrt(jnp.mean(x * x, axis=-1, keepdims=True) + EPS) * g


def _silu(x):
    return x * (1.0 / (1.0 + jnp.exp(-x)))


def _dot_t(a, b):
    return lax.dot_general(a, b, (((1,), (1,)), ((), ())), preferred_element_type=F32)


def _inproj_body(x_ref, g_ref, w_ref, ws_ref, o_ref, os_ref, h_scr):
    @pl.when(pl.program_id(1) == 0)
    def _():
        h = _rms(x_ref[...], g_ref[...]).astype(BF16)
        h_scr[...] = h
        os_ref[...] = jnp.dot(h, ws_ref[...], preferred_element_type=F32)

    o_ref[...] = jnp.dot(h_scr[...], w_ref[...], preferred_element_type=F32).astype(o_ref.dtype)


INPROJ_TN = 512


def _inproj(x2, g, w_main_t, w_small, tm=1024):
    n, d = x2.shape
    nj, _, tn = w_main_t.shape
    nm = nj * tn
    ns = w_small.shape[1]
    return pl.pallas_call(
        _inproj_body,
        grid=(n // tm, nj),
        in_specs=[
            pl.BlockSpec((tm, d), lambda i, j: (i, 0)),
            pl.BlockSpec((1, d), lambda i, j: (0, 0)),
            pl.BlockSpec((None, d, tn), lambda i, j: (j, 0, 0)),
            pl.BlockSpec((d, ns), lambda i, j: (0, 0)),
        ],
        out_specs=[
            pl.BlockSpec((tm, tn), lambda i, j: (i, j)),
            pl.BlockSpec((tm, ns), lambda i, j: (i, 0)),
        ],
        out_shape=[jax.ShapeDtypeStruct((n, nm), BF16), jax.ShapeDtypeStruct((n, ns), F32)],
        scratch_shapes=[pltpu.VMEM((tm, d), BF16)],
        compiler_params=_cparams(("parallel", "arbitrary")),
        name="inproj",
    )(x2, g, w_main_t, w_small)


def _outproj_even_body(a_ref, b_ref, wa_ref, wb_ref, g_ref, x_ref, o_ref):
    mix = (jnp.dot(a_ref[...], wa_ref[...], preferred_element_type=F32)
           + jnp.dot(b_ref[...], wb_ref[...], preferred_element_type=F32))
    o_ref[...] = x_ref[...] + _rms(mix, g_ref[...])


def _outproj_odd_body(y_ref, z_ref, ng_ref, b_ref, wa_ref, wb_ref, g_ref, x_ref, o_ref):
    y = y_ref[...].astype(F32) * _silu(z_ref[...].astype(F32))
    a = _rms(y, ng_ref[...]).astype(BF16)
    mix = (jnp.dot(a, wa_ref[...], preferred_element_type=F32)
           + jnp.dot(b_ref[...], wb_ref[...], preferred_element_type=F32))
    o_ref[...] = x_ref[...] + _rms(mix, g_ref[...])


def _outproj(x2, a, b, w_out, li, g, z_src=None, ssd_gain=None, tm=256):
    n, d = x2.shape
    half = a.shape[1]
    row = lambda i: (i, 0)
    fixed = lambda i: (0, 0)
    w_specs = [pl.BlockSpec((None, half, d), lambda i: (li, 0, 0)),
               pl.BlockSpec((None, half, d), lambda i: (li, 1, 0))]
    tail_specs = [pl.BlockSpec((1, d), fixed), pl.BlockSpec((tm, d), row)]
    if z_src is None:
        body = _outproj_even_body
        in_specs = [pl.BlockSpec((tm, half), row), pl.BlockSpec((tm, half), row)] + w_specs + tail_specs
        args = (a, b, w_out, w_out, g, x2)
    else:
        body = _outproj_odd_body
        in_specs = ([pl.BlockSpec((tm, half), row), pl.BlockSpec((tm, half), row),
                     pl.BlockSpec((1, half), fixed), pl.BlockSpec((tm, half), row)]
                    + w_specs + tail_specs)
        args = (a, z_src, ssd_gain, b, w_out, w_out, g, x2)
    return pl.pallas_call(
        body,
        grid=(n // tm,),
        in_specs=in_specs,
        out_specs=pl.BlockSpec((tm, d), row),
        out_shape=jax.ShapeDtypeStruct((n, d), F32),
        compiler_params=_cparams(("parallel",)),
        name="outproj",
    )(*args)


def _ffn_body(x_ref, g2_ref, wg_ref, wu_ref, wd_ref, g3_ref, o_ref, h_scr, acc_scr):
    k = pl.program_id(1)

    @pl.when(k == 0)
    def _():
        h_scr[...] = _rms(x_ref[...], g2_ref[...]).astype(BF16)
        acc_scr[...] = jnp.zeros_like(acc_scr)

    h = h_scr[...]
    gate = jnp.dot(h, wg_ref[...], preferred_element_type=F32)
    up = jnp.dot(h, wu_ref[...], preferred_element_type=F32)
    act = (_silu(gate) * up).astype(BF16)
    acc_scr[...] += jnp.dot(act, wd_ref[...], preferred_element_type=F32)

    @pl.when(k == pl.num_programs(1) - 1)
    def _():
        o_ref[...] = x_ref[...] + _rms(acc_scr[...], g3_ref[...])


FFN_TH = 512


def _tile_cols(w, tn):
    *lead, d, n = w.shape
    return jnp.moveaxis(w.reshape(*lead, d, n // tn, tn), -2, -3)


def _ffn(x2, g2, wg_t, wu_t, wd, g3, layer, tm=512):
    n, d = x2.shape
    th = wg_t.shape[-1]
    hid = wd.shape[1]
    return pl.pallas_call(
        _ffn_body,
        grid=(n // tm, hid // th),
        in_specs=[
            pl.BlockSpec((tm, d), lambda i, k: (i, 0)),
            pl.BlockSpec((1, d), lambda i, k: (0, 0)),
            pl.BlockSpec((None, None, d, th), lambda i, k: (layer, k, 0, 0)),
            pl.BlockSpec((None, None, d, th), lambda i, k: (layer, k, 0, 0)),
            pl.BlockSpec((None, th, d), lambda i, k: (layer, k, 0)),
            pl.BlockSpec((1, d), lambda i, k: (0, 0)),
        ],
        out_specs=pl.BlockSpec((tm, d), lambda i, k: (i, 0)),
        out_shape=jax.ShapeDtypeStruct((n, d), F32),
        scratch_shapes=[pltpu.VMEM((tm, d), BF16), pltpu.VMEM((tm, d), F32)],
        compiler_params=_cparams(("parallel", "arbitrary")),
        name="ffn",
    )(x2, g2, wg_t, wu_t, wd, g3)


def _gla_body(q_ref, k_ref, v_ref, g_ref, lr_ref, wd_ref, bd_ref, gn_ref, o_ref,
              bc_scr, o_scr, st_scr, *, seq):
    c = GLA_CHUNK
    nc = seq // c
    half = nc // 2
    nsub = c // GLA_SUB
    rb = min(512, seq)
    row_i = lax.broadcasted_iota(jnp.int32, (c, c), 0)
    col_i = lax.broadcasted_iota(jnp.int32, (c, c), 1)
    tri = (row_i >= col_i, row_i <= col_i)
    lane = lax.broadcasted_iota(jnp.int32, (1, LANES), 1)
    hmask = (lane < GLA_DK, lane >= GLA_DK)

    w2 = jnp.concatenate([wd_ref[0], wd_ref[1]], axis=1)
    w_hi = w2.astype(BF16)
    w_lo = (w2 - w_hi.astype(F32)).astype(BF16)
    w_cat = jnp.concatenate([w_hi, w_lo, w_hi], axis=0)
    tb = GLA_TRI_BLOCK
    brow = lax.broadcasted_iota(jnp.int32, (tb, tb), 0)
    bcol = lax.broadcasted_iota(jnp.int32, (tb, tb), 1)
    same_chunk = (brow // c) == (bcol // c)
    blk_tri = ((same_chunk & (brow >= bcol)).astype(BF16), (same_chunk & (brow <= bcol)).astype(BF16))

    def pre(i, carry):
        r0 = pl.multiple_of(i * rb, rb)
        lr = lr_ref[0, pl.ds(r0, rb), :]
        lr_hi = lr.astype(BF16)
        lr_lo = (lr - lr_hi.astype(F32)).astype(BF16)
        z2 = jnp.dot(jnp.concatenate([lr_hi, lr_hi, lr_lo], axis=1), w_cat, preferred_element_type=F32)
        las = []
        for d in range(2):
            z = z2[:, LANES * d:LANES * (d + 1)] + bd_ref[d:d + 1, :]
            la = (jnp.minimum(z, 0.0) - jnp.log1p(jnp.exp(-jnp.abs(z)))) * (1.0 / GLA_TAU)
            la_hi = la.astype(BF16)
            las.append(jnp.concatenate([la_hi, (la - la_hi.astype(F32)).astype(BF16)], axis=1))
        for d in range(2):
            for t in range(rb // tb):
                s2 = jnp.dot(blk_tri[d], las[d][t * tb:(t + 1) * tb], preferred_element_type=F32)
                bc_scr[d, pl.ds(r0 + t * tb, tb), :] = s2[:, :LANES] + s2[:, LANES:]
        return carry

    lax.fori_loop(0, seq // rb, pre, 0)

    def stage1(n, d):
        rows = pl.ds(pl.multiple_of(n * c, c), c)
        bc = bc_scr[d, rows, :]
        q = q_ref[0, rows, :].astype(F32) * (GLA_DK ** -0.5)
        k = k_ref[0, rows, :].astype(F32)
        v = v_ref[0, rows, :]
        b_edge = bc[c - 1:c] if d == 0 else bc[0:1]
        qhat = q * jnp.exp(bc)
        khat = (k * jnp.exp(b_edge - bc)).astype(BF16)
        att_rows = ([], [])
        for i in range(nsub):
            lo = GLA_SUB * i
            hi = lo + GLA_SUB
            if d == 0:
                ref = bc[lo - 1:lo] if i > 0 else jnp.zeros((1, LANES), F32)
            else:
                ref = bc[hi:hi + 1] if i < nsub - 1 else jnp.zeros((1, LANES), F32)
            qi = q[lo:hi] * jnp.exp(bc[lo:hi] - ref)
            ki = (k * jnp.exp(jnp.minimum(ref - bc, GLA_EXP_CLAMP))).astype(BF16)
            for hh in range(2):
                qim = jnp.where(hmask[hh], qi, 0.0).astype(BF16)
                att_rows[hh].append(_dot_t(qim, ki))
        heads = []
        for hh in range(2):
            att = jnp.concatenate(att_rows[hh], axis=0)
            att = jnp.where(tri[d], att, 0.0).astype(BF16)
            vh = v[:, GLA_DV * hh:GLA_DV * (hh + 1)]
            kv = lax.dot_general(vh, khat, (((0,), (0,)), ((), ())), preferred_element_type=F32)
            qh = jnp.where(hmask[hh], qhat, 0.0).astype(BF16)
            heads.append((att, vh, kv, qh))
        return d, heads, jnp.exp(b_edge)

    def stage2(s1):
        d, heads, decay = s1
        outs = []
        for hh, (att, vh, kv, qh) in enumerate(heads):
            st = st_scr[d, hh]
            outs.append(jnp.dot(att, vh, preferred_element_type=F32) + _dot_t(qh, st.astype(BF16)))
            st_scr[d, hh] = st * decay + kv
        return jnp.concatenate(outs, axis=1)

    def finish(n, o):
        rows = pl.ds(pl.multiple_of(n * c, c), c)
        parts = []
        for hh in range(2):
            oh = o[:, GLA_DV * hh:GLA_DV * (hh + 1)]
            parts.append(oh * lax.rsqrt(jnp.mean(oh * oh, axis=-1, keepdims=True) + EPS))
        o = jnp.concatenate(parts, axis=1) * gn_ref[...]
        o_ref[0, rows, :] = (o * _silu(g_ref[0, rows, :].astype(F32))).astype(o_ref.dtype)

    st_scr[...] = jnp.zeros_like(st_scr)

    cpi = 2 if half % 2 == 0 else 1

    def walk(i, base_f, base_b):
        work = []
        for u in range(cpi):
            work.append((base_f + i * cpi + u, 0))
            work.append((base_b - i * cpi - u, 1))
        prepared = [stage1(n, d) for n, d in work]
        return [(n, stage2(s1)) for (n, _), s1 in zip(work, prepared)]

    def first(i, carry):
        for n, o in walk(i, 0, nc - 1):
            o_scr[pl.ds(pl.multiple_of(n * c, c), c), :] = o
        return carry

    lax.fori_loop(0, half // cpi, first, 0)

    def second(i, carry):
        for n, o in walk(i, half, half - 1):
            finish(n, o_scr[pl.ds(pl.multiple_of(n * c, c), c), :] + o)
        return carry

    lax.fori_loop(0, half // cpi, second, 0)


def _gla(proj3, small3, wdec_pad, bdec, gnorm):
    bn, seq, _ = proj3.shape
    npair = GLA_HEADS // 2
    qoff = 0
    koff = (GLA_HEADS * GLA_DK) // LANES
    voff = (2 * GLA_HEADS * GLA_DK) // (2 * GLA_DV)
    goff = voff + npair
    return pl.pallas_call(
        functools.partial(_gla_body, seq=seq),
        grid=(bn, npair),
        in_specs=[
            pl.BlockSpec((1, seq, LANES), lambda b, p: (b, 0, qoff + p)),
            pl.BlockSpec((1, seq, LANES), lambda b, p: (b, 0, koff + p)),
            pl.BlockSpec((1, seq, 2 * GLA_DV), lambda b, p: (b, 0, voff + p)),
            pl.BlockSpec((1, seq, 2 * GLA_DV), lambda b, p: (b, 0, goff + p)),
            pl.BlockSpec((1, seq, LANES), lambda b, p: (b, 0, 0)),
            pl.BlockSpec((2, LANES, LANES), lambda b, p: (0, 0, p)),
            pl.BlockSpec((2, LANES), lambda b, p: (0, p)),
            pl.BlockSpec((1, 2 * GLA_DV), lambda b, p: (0, p)),
        ],
        out_specs=pl.BlockSpec((1, seq, 2 * GLA_DV), lambda b, p: (b, 0, p)),
        out_shape=jax.ShapeDtypeStruct((bn, seq, GLA_HEADS * GLA_DV), BF16),
        scratch_shapes=[
            pltpu.VMEM((2, seq, LANES), F32),
            pltpu.VMEM((seq, 2 * GLA_DV), F32),
            pltpu.VMEM((2, 2, GLA_DV, LANES), F32),
        ],
        compiler_params=_cparams(("parallel", "parallel")),
        name="gla",
    )(proj3, proj3, proj3, proj3, small3, wdec_pad, bdec, gnorm)


def _na_bias_table(rpb):
    nh, ndr, ndc = rpb.shape
    cols = np.arange(GRID_W)
    cs = np.clip(cols - NA_KW // 2, 0, GRID_W - NA_KW)
    valid = (cols[None, :] >= cs[:, None]) & (cols[None, :] < cs[:, None] + NA_KW)
    dc = cols[None, :] - cols[:, None] + NA_KW - 1
    onehot = (np.arange(ndc)[:, None, None] == dc[None]) & valid[None]
    toep = jnp.dot(rpb.reshape(nh * ndr, ndc).astype(F32),
                   jnp.asarray(onehot.reshape(ndc, -1), F32), precision=HIGHEST)
    toep = jnp.where(jnp.asarray(valid.reshape(1, -1)), toep, NEG_BIG)
    toep = toep.reshape(nh, ndr, GRID_W, GRID_W)
    t = jnp.stack([toep[:, off:off + NA_KH] for off in range(NA_KH)], axis=1)
    return t.transpose(0, 1, 3, 2, 4).reshape(nh, NA_KH, GRID_W, NA_KH * GRID_W)


def _na_body(q_ref, k_ref, v_ref, bias_ref, o_ref, *, nrows):
    w = GRID_W
    nk = NA_KH * w

    def group(i, carry):
        idx = []
        scores = []
        for u in range(NA_ROWS_PER_ITER):
            r = i * NA_ROWS_PER_ITER + u
            rs = jnp.clip(r - NA_KH // 2, 0, nrows - NA_KH)
            qrows = pl.ds(pl.multiple_of(r * w, w), w)
            krows = pl.ds(pl.multiple_of(rs * w, w), nk)
            idx.append((qrows, krows, rs - r + (NA_KH - 1)))
            scores.append(_dot_t(q_ref[0, qrows, :], k_ref[0, krows, :]))
        probs = []
        for (qrows, krows, off), s in zip(idx, scores):
            s = s * (NA_DH ** -0.5) + bias_ref[0, off]
            p = jnp.exp(s - jnp.max(s, axis=-1, keepdims=True))
            probs.append((p.astype(BF16), jnp.sum(p, axis=-1, keepdims=True)))
        for (qrows, krows, off), (p, l) in zip(idx, probs):
            o = jnp.dot(p, v_ref[0, krows, :], preferred_element_type=F32) / l
            o_ref[0, qrows, :] = o.astype(o_ref.dtype)
        return carry

    lax.fori_loop(0, nrows // NA_ROWS_PER_ITER, group, 0)


def _na(proj3, bias_tab, col0):
    bn, seq, _ = proj3.shape
    nrows = seq // GRID_W
    qoff = col0 // NA_DH
    koff = qoff + NA_HEADS
    voff = koff + NA_HEADS
    return pl.pallas_call(
        functools.partial(_na_body, nrows=nrows),
        grid=(bn, NA_HEADS),
        in_specs=[
            pl.BlockSpec((1, seq, NA_DH), lambda b, h: (b, 0, qoff + h)),
            pl.BlockSpec((1, seq, NA_DH), lambda b, h: (b, 0, koff + h)),
            pl.BlockSpec((1, seq, NA_DH), lambda b, h: (b, 0, voff + h)),
            pl.BlockSpec((1, NA_KH, GRID_W, NA_KH * GRID_W), lambda b, h: (h, 0, 0, 0)),
        ],
        out_specs=pl.BlockSpec((1, seq, NA_DH), lambda b, h: (b, 0, h)),
        out_shape=jax.ShapeDtypeStruct((bn, seq, NA_HEADS * NA_DH), BF16),
        compiler_params=_cparams(("parallel", "parallel")),
        name="na",
    )(proj3, proj3, proj3, bias_tab)


CONV_PAD = 16


def _conv_body(x_ref, w_ref, b_ref, o_ref, xp_scr, *, seq):
    rb = min(256, seq)
    ch = x_ref.shape[-1]
    zeros = jnp.zeros((CONV_PAD, ch), xp_scr.dtype)
    xp_scr[0:CONV_PAD, :] = zeros
    xp_scr[seq + CONV_PAD:seq + 2 * CONV_PAD, :] = zeros
    xp_scr[CONV_PAD:seq + CONV_PAD, :] = x_ref[0]

    def blk(i, carry):
        r0 = pl.multiple_of(i * rb, rb)
        xw = xp_scr[pl.ds(r0, rb + 2 * CONV_PAD), :].astype(F32)
        acc = jnp.zeros((rb, ch), F32) + b_ref[...]
        for j in range(SSD_CONV):
            s0 = CONV_PAD - SSD_CONV // 2 + j
            acc = acc + xw[s0:s0 + rb] * w_ref[j:j + 1, :]
        o_ref[0, pl.ds(r0, rb), :] = _silu(acc).astype(o_ref.dtype)
        return carry

    lax.fori_loop(0, seq // rb, blk, 0)


def _conv(proj3, conv_w, conv_b, col0, tc=512):
    bn, seq, _ = proj3.shape
    cdim = conv_w.shape[1]
    c0 = col0 // tc
    return pl.pallas_call(
        functools.partial(_conv_body, seq=seq),
        grid=(bn, cdim // tc),
        in_specs=[
            pl.BlockSpec((1, seq, tc), lambda b, j: (b, 0, c0 + j)),
            pl.BlockSpec((SSD_CONV, tc), lambda b, j: (0, j)),
            pl.BlockSpec((1, tc), lambda b, j: (0, j)),
        ],
        out_specs=pl.BlockSpec((1, seq, tc), lambda b, j: (b, 0, j)),
        out_shape=jax.ShapeDtypeStruct((bn, seq, cdim), BF16),
        scratch_shapes=[pltpu.VMEM((seq + 2 * CONV_PAD, tc), BF16)],
        compiler_params=_cparams(("parallel", "parallel")),
        name="ssd_conv",
    )(proj3, conv_w, conv_b)


def _softplus(x):
    return jnp.maximum(x, 0.0) + jnp.log1p(jnp.exp(-jnp.abs(x)))


def _ssd_body(xs_ref, bm_ref, cm_ref, dt_ref, pb_ref, pbt_ref, al_ref, alt_ref,
              dsk_ref, ex_ref, o_ref, st_scr, *, seq):
    l = SSD_CHUNK
    nc = seq // l
    gw = SSD_JH * SSD_HEADDIM
    row_i = lax.broadcasted_iota(jnp.int32, (l, l), 0)
    col_i = lax.broadcasted_iota(jnp.int32, (l, l), 1)
    lane = lax.broadcasted_iota(jnp.int32, (1, LANES), 1)
    first_head = lane < SSD_HEADDIM
    a_row = -jnp.exp(al_ref[0])
    a_col = -jnp.exp(alt_ref[0])

    def chunk(n, d):
        rows = pl.ds(pl.multiple_of(n * l, l), l)
        tri = (row_i >= col_i) if d == 0 else (row_i <= col_i)
        trif = tri.astype(F32)
        x = xs_ref[0, rows, :]
        bm = bm_ref[0, rows, :]
        cm = cm_ref[0, rows, :]
        cb = _dot_t(cm, bm)
        bmt = bm.astype(F32).T
        dt_raw = dt_ref[0, rows, :]
        dtv = _softplus(dt_raw + pb_ref[0])
        acum = jnp.dot(trif, dtv * a_row, precision=HIGHEST, preferred_element_type=F32)
        dtv_t = _softplus(dt_raw.T[0:2 * SSD_JH] + pbt_ref[0])
        acum_t = lax.dot_general(dtv_t * a_col, trif, (((1,), (1,)), ((), ())),
                                 precision=HIGHEST, preferred_element_type=F32)
        a_edge = acum[l - 1:l] if d == 0 else acum[0:1]
        a_edge_t = acum_t[:, l - 1:l] if d == 0 else acum_t[:, 0:1]
        w_state_t = dtv_t * jnp.exp(a_edge_t - acum_t)
        w_off = jnp.exp(acum)
        edge_ch = jnp.dot(jnp.broadcast_to(jnp.exp(a_edge), (8, LANES)), ex_ref[d],
                          precision=HIGHEST, preferred_element_type=F32)[0:1]
        st = st_scr[...]
        st_b = st.astype(BF16)
        ys = []
        new_st = []
        for jj in range(SSD_JH // 2):
            xp = x[:, LANES * jj:LANES * (jj + 1)]
            stp = st_b[:, LANES * jj:LANES * (jj + 1)]
            rhs = jnp.concatenate([xp, stp], axis=0)
            y2 = []
            s2 = []
            for hh in range(2):
                cidx = d * SSD_JH + 2 * jj + hh
                seg = acum[:, cidx:cidx + 1] - acum_t[cidx:cidx + 1, :]
                lmat = jnp.where(tri, jnp.exp(jnp.minimum(seg, 0.0)), 0.0)
                m = cb * lmat * dtv_t[cidx:cidx + 1, :]
                cw = cm.astype(F32) * w_off[:, cidx:cidx + 1]
                lhs = jnp.concatenate([m.astype(BF16), cw.astype(BF16)], axis=1)
                y2.append(jnp.dot(lhs, rhs, preferred_element_type=F32))
                bw = (bmt * w_state_t[cidx:cidx + 1, :]).astype(BF16)
                s2.append(jnp.dot(bw, xp, preferred_element_type=F32))
            ys.append(jnp.where(first_head, y2[0], y2[1]))
            new_st.append(jnp.where(first_head, s2[0], s2[1]))
        st_scr[...] = st * edge_ch + jnp.concatenate(new_st, axis=1)
        return jnp.concatenate(ys, axis=1)

    st_scr[...] = jnp.zeros_like(st_scr)

    def fwd(n, carry):
        rows = pl.ds(pl.multiple_of(n * l, l), l)
        o_ref[0, rows, :] = chunk(n, 0).astype(o_ref.dtype)
        return carry

    lax.fori_loop(0, nc, fwd, 0)
    st_scr[...] = jnp.zeros_like(st_scr)

    def bwd(i, carry):
        n = nc - 1 - i
        rows = pl.ds(pl.multiple_of(n * l, l), l)
        y = (o_ref[0, rows, :].astype(F32) + chunk(n, 1)
             + xs_ref[0, rows, :].astype(F32) * dsk_ref[...])
        o_ref[0, rows, :] = y.astype(o_ref.dtype)
        return carry

    lax.fori_loop(0, nc, bwd, 0)


def _ssd(xbc_act, dt3, pb, pbt, al, alt, dskip, expand):
    bn, seq, _ = xbc_act.shape
    gw = SSD_JH * SSD_HEADDIM
    boff = SSD_INNER // SSD_STATE
    coff = boff + SSD_GROUPS
    nh2 = 2 * SSD_JH
    return pl.pallas_call(
        functools.partial(_ssd_body, seq=seq),
        grid=(bn, SSD_GROUPS),
        in_specs=[
            pl.BlockSpec((1, seq, gw), lambda b, g: (b, 0, g)),
            pl.BlockSpec((1, seq, SSD_STATE), lambda b, g: (b, 0, boff + g)),
            pl.BlockSpec((1, seq, SSD_STATE), lambda b, g: (b, 0, coff + g)),
            pl.BlockSpec((1, seq, LANES), lambda b, g: (b, 0, g)),
            pl.BlockSpec((1, 1, LANES), lambda b, g: (g, 0, 0)),
            pl.BlockSpec((1, nh2, 1), lambda b, g: (g, 0, 0)),
            pl.BlockSpec((1, 1, LANES), lambda b, g: (g, 0, 0)),
            pl.BlockSpec((1, nh2, 1), lambda b, g: (g, 0, 0)),
            pl.BlockSpec((1, gw), lambda b, g: (0, g)),
            pl.BlockSpec((2, LANES, gw), lambda b, g: (0, 0, 0)),
        ],
        out_specs=pl.BlockSpec((1, seq, gw), lambda b, g: (b, 0, g)),
        out_shape=jax.ShapeDtypeStruct((bn, seq, SSD_INNER), BF16),
        scratch_shapes=[pltpu.VMEM((SSD_STATE, gw), F32)],
        compiler_params=_cparams(("parallel", "parallel")),
        name="ssd_scan",
    )(xbc_act, xbc_act, xbc_act, dt3, pb, pbt, al, alt, dskip, expand)


def _rope_tables(seq):
    half = SWA_DH // 2
    inv = ROPE_THETA ** (-np.arange(half, dtype=np.float64) / half)
    ang = np.arange(seq, dtype=np.float64)[:, None] * inv[None, :]
    cos, sin = np.cos(ang), np.sin(ang)
    return (jnp.asarray(np.concatenate([cos, cos], axis=1), F32),
            jnp.asarray(np.concatenate([-sin, sin], axis=1), F32))


def _rope(x, cos, sin_signed):
    return x * cos + pltpu.roll(x, SWA_DH // 2, 1) * sin_signed


def _swa_body(sink_ref, q_ref, k_ref, v_ref, cos_ref, sin_ref, o_ref, kr_scr, *, seq):
    wb = SWA_BLOCK
    nb = seq // wb
    nkeys = 3 * wb
    kvh = pl.program_id(1)
    rb = min(512, seq)

    def krope(i, carry):
        rows = pl.ds(pl.multiple_of(i * rb, rb), rb)
        kr_scr[rows, :] = _rope(k_ref[0, rows, :].astype(F32), cos_ref[rows, :],
                                sin_ref[rows, :]).astype(BF16)
        return carry

    lax.fori_loop(0, seq // rb, krope, 0)

    qpos_l = lax.broadcasted_iota(jnp.int32, (wb, nkeys), 0)
    kpos_l = lax.broadcasted_iota(jnp.int32, (wb, nkeys), 1)

    def blk(n, carry):
        rows = pl.ds(pl.multiple_of(n * wb, wb), wb)
        ks = pl.multiple_of(jnp.clip(n - 1, 0, nb - 3) * wb, wb)
        krows = pl.ds(ks, nkeys)
        cos = cos_ref[rows, :]
        sin = sin_ref[rows, :]
        kk = kr_scr[krows, :]
        vv = v_ref[0, krows, :]
        valid = jnp.abs((qpos_l + n * wb) - (kpos_l + ks)) <= SWA_WINDOW
        qb = q_ref[0, rows, :].astype(F32)
        scores = []
        for hh in range(SWA_GQ):
            qh = (_rope(qb[:, SWA_DH * hh:SWA_DH * (hh + 1)], cos, sin) * (SWA_DH ** -0.5)).astype(BF16)
            scores.append(_dot_t(qh, kk))
        probs = []
        for hh, s in enumerate(scores):
            sink = sink_ref[kvh * SWA_GQ + hh]
            s = jnp.where(valid, s, NEG_BIG)
            m = jnp.maximum(jnp.max(s, axis=-1, keepdims=True), sink)
            p = jnp.exp(s - m)
            probs.append((p.astype(BF16), jnp.sum(p, axis=-1, keepdims=True) + jnp.exp(sink - m)))
        for hh, (p, den) in enumerate(probs):
            o = jnp.dot(p, vv, preferred_element_type=F32) / den
            o_ref[0, rows, SWA_DH * hh:SWA_DH * (hh + 1)] = o.astype(o_ref.dtype)
        return carry

    lax.fori_loop(0, nb, blk, 0)


def _swa(proj3, sink, cos, sin, col0):
    bn, seq, _ = proj3.shape
    qw = SWA_GQ * SWA_DH
    qoff = col0 // qw
    koff = (col0 + SWA_HEADS * SWA_DH) // SWA_DH
    voff = koff + SWA_KV
    return pl.pallas_call(
        functools.partial(_swa_body, seq=seq),
        grid=(bn, SWA_KV),
        in_specs=[
            pl.BlockSpec(memory_space=pltpu.SMEM),
            pl.BlockSpec((1, seq, qw), lambda b, h: (b, 0, qoff + h)),
            pl.BlockSpec((1, seq, SWA_DH), lambda b, h: (b, 0, koff + h)),
            pl.BlockSpec((1, seq, SWA_DH), lambda b, h: (b, 0, voff + h)),
            pl.BlockSpec((seq, SWA_DH), lambda b, h: (0, 0)),
            pl.BlockSpec((seq, SWA_DH), lambda b, h: (0, 0)),
        ],
        out_specs=pl.BlockSpec((1, seq, qw), lambda b, h: (b, 0, h)),
        out_shape=jax.ShapeDtypeStruct((bn, seq, SWA_HEADS * SWA_DH), BF16),
        scratch_shapes=[pltpu.VMEM((seq, SWA_DH), BF16)],
        compiler_params=_cparams(("parallel", "parallel")),
        name="swa",
    )(sink, proj3, proj3, proj3, cos, sin)


def _pad_cols(w, width):
    return jnp.pad(w, ((0, 0), (0, width - w.shape[1])))


def _even_mixers(x2, bn, seq, g0, w_in, w_decay, b_decay, gla_norm, rpb):
    d = x2.shape[1]
    n_gla = 2 * GLA_HEADS * GLA_DK + 2 * GLA_HEADS * GLA_DV
    n_lr = 2 * GLA_LOWRANK
    w_main = jnp.concatenate([w_in[:, :n_gla], w_in[:, n_gla + n_lr:]], axis=1).astype(BF16)
    w_small = _pad_cols(w_in[:, n_gla:n_gla + n_lr], LANES).astype(BF16)
    proj, small = _inproj(x2, g0, _tile_cols(w_main, INPROJ_TN), w_small)
    proj3 = proj.reshape(bn, seq, -1)
    small3 = small.reshape(bn, seq, -1)
    wdec = jnp.zeros((2, LANES, w_decay.shape[-1]), F32)
    for dd in range(2):
        wdec = wdec.at[dd, dd * GLA_LOWRANK:(dd + 1) * GLA_LOWRANK].set(w_decay[dd])
    o_a = _gla(proj3, small3, wdec, b_decay, gla_norm[None, :])
    o_b = _na(proj3, _na_bias_table(rpb), n_gla)
    return o_a.reshape(-1, o_a.shape[-1]), o_b.reshape(-1, o_b.shape[-1])


def _per_group(v):
    return v.reshape(2, SSD_GROUPS, SSD_JH).transpose(1, 0, 2).reshape(SSD_GROUPS, 2 * SSD_JH)


def _odd_mixers(x2, bn, seq, g0, w_in, conv_w, conv_b, dt_bias, a_log, d_skip, sink, cos, sin):
    conv_dim = SSD_INNER + 2 * SSD_GROUPS * SSD_STATE
    n_z = SSD_INNER
    n_dt = 2 * SSD_HEADS
    c_dt = n_z + conv_dim
    w_main = jnp.concatenate([w_in[:, :c_dt], w_in[:, c_dt + n_dt:]], axis=1).astype(BF16)
    w_dt = w_in[:, c_dt:c_dt + n_dt].reshape(-1, 2, SSD_GROUPS, SSD_JH).transpose(0, 2, 1, 3)
    w_dt = w_dt.reshape(-1, SSD_GROUPS, 2 * SSD_JH)
    w_small = jnp.pad(w_dt, ((0, 0), (0, 0), (0, LANES - 2 * SSD_JH))).reshape(-1, SSD_GROUPS * LANES)
    proj, small = _inproj(x2, g0, _tile_cols(w_main, INPROJ_TN), w_small.astype(BF16))
    proj3 = proj.reshape(bn, seq, -1)
    dt3 = small.reshape(bn, seq, SSD_GROUPS * LANES)
    xbc_act = _conv(proj3, conv_w, conv_b[None, :], n_z)
    pbg = _per_group(dt_bias)
    alg = _per_group(a_log)
    pad = ((0, 0), (0, LANES - 2 * SSD_JH))
    head_of_ch = jnp.arange(SSD_JH * SSD_HEADDIM) // SSD_HEADDIM
    expand = (jnp.arange(LANES)[None, :, None]
              == (jnp.arange(2)[:, None, None] * SSD_JH + head_of_ch[None, None, :])).astype(F32)
    dskip = jnp.repeat(d_skip.astype(F32), SSD_HEADDIM)[None, :]
    y_pre = _ssd(xbc_act, dt3,
                 jnp.pad(pbg, pad)[:, None, :], pbg[:, :, None],
                 jnp.pad(alg, pad)[:, None, :], alg[:, :, None], dskip, expand)
    o_d = _swa(proj3, sink, cos, sin, c_dt)
    return y_pre.reshape(-1, SSD_INNER), proj, o_d.reshape(-1, o_d.shape[-1])


def kernel(x, norm_gains, ffn_w_gate, ffn_w_up, ffn_w_down, even_w_in, even_w_out, gla_w_decay,
           gla_b_decay, gla_norm, na_rpb, odd_w_in, odd_w_out, ssd_conv_w, ssd_conv_b, ssd_dt_bias,
           ssd_a_log, ssd_d, ssd_norm, swa_sink):
    bn, seq, d = x.shape
    depth = norm_gains.shape[0]
    x2 = x.reshape(bn * seq, d)
    cos, sin = _rope_tables(seq)
    wg_t = _tile_cols(ffn_w_gate.astype(BF16), FFN_TH)
    wu_t = _tile_cols(ffn_w_up.astype(BF16), FFN_TH)
    wd = ffn_w_down.astype(BF16)
    even_w_out_b = even_w_out.astype(BF16)
    odd_w_out_b = odd_w_out.astype(BF16)
    for layer in range(depth):
        g = norm_gains[layer][:, None, :]
        i = layer // 2
        if layer % 2 == 0:
            o_a, o_b = _even_mixers(x2, bn, seq, g[0], even_w_in[i], gla_w_decay[i], gla_b_decay[i],
                                    gla_norm[i], na_rpb[i])
            x2 = _outproj(x2, o_a, o_b, even_w_out_b, i, g[1])
        else:
            y_pre, proj, o_d = _odd_mixers(x2, bn, seq, g[0], odd_w_in[i], ssd_conv_w[i], ssd_conv_b[i],
                                           ssd_dt_bias[i], ssd_a_log[i], ssd_d[i], swa_sink[i], cos, sin)
            x2 = _outproj(x2, y_pre, o_d, odd_w_out_b, i, g[1],
                          z_src=proj, ssd_gain=ssd_norm[i][None, :])
        x2 = _ffn(x2, g[2], wg_t, wu_t, wd, g[3], layer)
    return x2.reshape(bn, seq, d)
```

```python
import functools
import math

import jax
import jax.numpy as jnp
import numpy as np
from jax import lax
from jax.experimental import pallas as pl
from jax.experimental.pallas import tpu as pltpu

F32 = jnp.float32
BF16 = jnp.bfloat16
HIGHEST = lax.Precision.HIGHEST

EPS = 1e-6
ROPE_THETA = 10000.0
GRID_W = 64
LANES = 128

GLA_DK = 64
GLA_DV = 128
GLA_HEADS = 8
GLA_LOWRANK = 16
GLA_TAU = 16.0
GLA_CHUNK = 64
GLA_SUB = 16
GLA_EXP_CLAMP = 60.0
GLA_TRI_BLOCK = 256

NA_DH = 128
NA_HEADS = 8
NA_KH = 8
NA_KW = 16
NA_ROWS_PER_ITER = 4
NEG_BIG = -1e30

SSD_INNER = 1024
SSD_HEADDIM = 64
SSD_HEADS = 16
SSD_GROUPS = 2
SSD_STATE = 128
SSD_CONV = 5
SSD_CHUNK = 128
SSD_JH = SSD_HEADS // SSD_GROUPS
SSD_ROW_SLAB = 64

SWA_DH = 128
SWA_HEADS = 8
SWA_KV = 2
SWA_GQ = SWA_HEADS // SWA_KV
SWA_WINDOW = 128
SWA_BLOCK = 128

VMEM_LIMIT = 52 * 1024 * 1024


def _cparams(sem):
    return pltpu.CompilerParams(dimension_semantics=sem, vmem_limit_bytes=VMEM_LIMIT)


def _rms(x, g):
    return x * lax.rsqrt(jnp.mean(x * x, axis=-1, keepdims=True) + EPS) * g


def _silu(x):
    return x * (1.0 / (1.0 + jnp.exp(-x)))


def _dot_t(a, b):
    return lax.dot_general(a, b, (((1,), (1,)), ((), ())), preferred_element_type=F32)


def _inproj_body(x_ref, g_ref, w_ref, ws_ref, o_ref, os_ref, h_scr):
    @pl.when(pl.program_id(1) == 0)
    def _():
        h = _rms(x_ref[...], g_ref[...]).astype(BF16)
        h_scr[...] = h
        os_ref[...] = jnp.dot(h, ws_ref[...], preferred_element_type=F32)

    o_ref[...] = jnp.dot(h_scr[...], w_ref[...], preferred_element_type=F32).astype(o_ref.dtype)


def _inproj(x2, g, w_main, w_small, tm=1024, tn=1024):
    n, d = x2.shape
    nm = w_main.shape[1]
    ns = w_small.shape[1]
    return pl.pallas_call(
        _inproj_body,
        grid=(n // tm, nm // tn),
        in_specs=[
            pl.BlockSpec((tm, d), lambda i, j: (i, 0)),
            pl.BlockSpec((1, d), lambda i, j: (0, 0)),
            pl.BlockSpec((d, tn), lambda i, j: (0, j)),
            pl.BlockSpec((d, ns), lambda i, j: (0, 0)),
        ],
        out_specs=[
            pl.BlockSpec((tm, tn), lambda i, j: (i, j)),
            pl.BlockSpec((tm, ns), lambda i, j: (i, 0)),
        ],
        out_shape=[jax.ShapeDtypeStruct((n, nm), BF16), jax.ShapeDtypeStruct((n, ns), F32)],
        scratch_shapes=[pltpu.VMEM((tm, d), BF16)],
        compiler_params=_cparams(("parallel", "arbitrary")),
        name="inproj",
    )(x2, g, w_main, w_small)


OUTPROJ_SLAB = 128


def _outproj_even_body(a_ref, b_ref, wa_ref, wb_ref, g_ref, x_ref, o_ref):
    for r0 in range(0, x_ref.shape[0], OUTPROJ_SLAB):
        rs = pl.ds(r0, OUTPROJ_SLAB)
        mix = (jnp.dot(a_ref[rs, :], wa_ref[...], preferred_element_type=F32)
               + jnp.dot(b_ref[rs, :], wb_ref[...], preferred_element_type=F32))
        o_ref[rs, :] = x_ref[rs, :] + _rms(mix, g_ref[...])


def _outproj_odd_body(y_ref, z_ref, ng_ref, b_ref, wa_ref, wb_ref, g_ref, x_ref, o_ref):
    for r0 in range(0, x_ref.shape[0], OUTPROJ_SLAB):
        rs = pl.ds(r0, OUTPROJ_SLAB)
        y = y_ref[rs, :].astype(F32) * _silu(z_ref[rs, :].astype(F32))
        a = _rms(y, ng_ref[...]).astype(BF16)
        mix = (jnp.dot(a, wa_ref[...], preferred_element_type=F32)
               + jnp.dot(b_ref[rs, :], wb_ref[...], preferred_element_type=F32))
        o_ref[rs, :] = x_ref[rs, :] + _rms(mix, g_ref[...])


def _outproj(x2, a, b, w_out, li, g, z_src=None, ssd_gain=None, tm=512):
    n, d = x2.shape
    half = a.shape[1]
    row = lambda i: (i, 0)
    fixed = lambda i: (0, 0)
    w_specs = [pl.BlockSpec((None, half, d), lambda i: (li, 0, 0)),
               pl.BlockSpec((None, half, d), lambda i: (li, 1, 0))]
    tail_specs = [pl.BlockSpec((1, d), fixed), pl.BlockSpec((tm, d), row)]
    if z_src is None:
        body = _outproj_even_body
        in_specs = [pl.BlockSpec((tm, half), row), pl.BlockSpec((tm, half), row)] + w_specs + tail_specs
        args = (a, b, w_out, w_out, g, x2)
    else:
        body = _outproj_odd_body
        in_specs = ([pl.BlockSpec((tm, half), row), pl.BlockSpec((tm, half), row),
                     pl.BlockSpec((1, half), fixed), pl.BlockSpec((tm, half), row)]
                    + w_specs + tail_specs)
        args = (a, z_src, ssd_gain, b, w_out, w_out, g, x2)
    return pl.pallas_call(
        body,
        grid=(n // tm,),
        in_specs=in_specs,
        out_specs=pl.BlockSpec((tm, d), row),
        out_shape=jax.ShapeDtypeStruct((n, d), F32),
        compiler_params=_cparams(("parallel",)),
        name="outproj",
    )(*args)


FFN_SLAB = 128


def _ffn_body(x_ref, g2_ref, wg_ref, wu_ref, wd_ref, g3_ref, o_ref, h_scr, acc_scr):
    k = pl.program_id(1)
    last = pl.num_programs(1) - 1
    slabs = [pl.ds(r0, FFN_SLAB) for r0 in range(0, x_ref.shape[0], FFN_SLAB)]

    def partial_down(h):
        gate = jnp.dot(h, wg_ref[...], preferred_element_type=F32)
        up = jnp.dot(h, wu_ref[...], preferred_element_type=F32)
        return jnp.dot((_silu(gate) * up).astype(BF16), wd_ref[...], preferred_element_type=F32)

    @pl.when(k == 0)
    def _():
        for rs in slabs:
            h = _rms(x_ref[rs, :], g2_ref[...]).astype(BF16)
            h_scr[rs, :] = h
            acc_scr[rs, :] = partial_down(h)

    @pl.when(jnp.logical_and(k > 0, k < last))
    def _():
        acc_scr[...] += partial_down(h_scr[...])

    @pl.when(k == last)
    def _():
        for rs in slabs:
            f = acc_scr[rs, :] + partial_down(h_scr[rs, :])
            o_ref[rs, :] = x_ref[rs, :] + _rms(f, g3_ref[...])


def _ffn(x2, g2, wg, wu, wd, g3, layer, tm=512, th=512):
    n, d = x2.shape
    hid = wd.shape[1]
    return pl.pallas_call(
        _ffn_body,
        grid=(n // tm, hid // th),
        in_specs=[
            pl.BlockSpec((tm, d), lambda i, k: (i, 0)),
            pl.BlockSpec((1, d), lambda i, k: (0, 0)),
            pl.BlockSpec((None, d, th), lambda i, k: (layer, 0, k)),
            pl.BlockSpec((None, d, th), lambda i, k: (layer, 0, k)),
            pl.BlockSpec((None, th, d), lambda i, k: (layer, k, 0)),
            pl.BlockSpec((1, d), lambda i, k: (0, 0)),
        ],
        out_specs=pl.BlockSpec((tm, d), lambda i, k: (i, 0)),
        out_shape=jax.ShapeDtypeStruct((n, d), F32),
        scratch_shapes=[pltpu.VMEM((tm, d), BF16), pltpu.VMEM((tm, d), F32)],
        compiler_params=_cparams(("parallel", "arbitrary")),
        name="ffn",
    )(x2, g2, wg, wu, wd, g3)


def _gla_body(q_ref, k_ref, v_ref, g_ref, lr_ref, wd_ref, bd_ref, gn_ref, o_ref,
              bc_scr, o_scr, st_scr, *, seq):
    c = GLA_CHUNK
    nc = seq // c
    half = nc // 2
    nsub = c // GLA_SUB
    rb = min(512, seq)
    row_i = lax.broadcasted_iota(jnp.int32, (c, c), 0)
    col_i = lax.broadcasted_iota(jnp.int32, (c, c), 1)
    tri = (row_i >= col_i, row_i <= col_i)
    lane = lax.broadcasted_iota(jnp.int32, (1, LANES), 1)
    hmask = (lane < GLA_DK, lane >= GLA_DK)

    w2 = jnp.concatenate([wd_ref[0], wd_ref[1]], axis=1)
    w_hi = w2.astype(BF16)
    w_lo = (w2 - w_hi.astype(F32)).astype(BF16)
    w_cat = jnp.concatenate([w_hi, w_lo, w_hi], axis=0)
    tb = GLA_TRI_BLOCK
    brow = lax.broadcasted_iota(jnp.int32, (tb, tb), 0)
    bcol = lax.broadcasted_iota(jnp.int32, (tb, tb), 1)
    same_chunk = (brow // c) == (bcol // c)
    blk_tri = ((same_chunk & (brow >= bcol)).astype(BF16), (same_chunk & (brow <= bcol)).astype(BF16))

    def pre(i, carry):
        r0 = pl.multiple_of(i * rb, rb)
        lr = lr_ref[0, pl.ds(r0, rb), :]
        lr_hi = lr.astype(BF16)
        lr_lo = (lr - lr_hi.astype(F32)).astype(BF16)
        z2 = jnp.dot(jnp.concatenate([lr_hi, lr_hi, lr_lo], axis=1), w_cat, preferred_element_type=F32)
        las = []
        for d in range(2):
            z = z2[:, LANES * d:LANES * (d + 1)] + bd_ref[d:d + 1, :]
            la = (jnp.minimum(z, 0.0) - jnp.log1p(jnp.exp(-jnp.abs(z)))) * (1.0 / GLA_TAU)
            la_hi = la.astype(BF16)
            las.append(jnp.concatenate([la_hi, (la - la_hi.astype(F32)).astype(BF16)], axis=1))
        for d in range(2):
            for t in range(rb // tb):
                s2 = jnp.dot(blk_tri[d], las[d][t * tb:(t + 1) * tb], preferred_element_type=F32)
                bc_scr[d, pl.ds(r0 + t * tb, tb), :] = s2[:, :LANES] + s2[:, LANES:]
        return carry

    lax.fori_loop(0, seq // rb, pre, 0)

    def stage1(n, d):
        rows = pl.ds(pl.multiple_of(n * c, c), c)
        bc = bc_scr[d, rows, :]
        q = q_ref[0, rows, :].astype(F32) * (GLA_DK ** -0.5)
        k = k_ref[0, rows, :].astype(F32)
        v = v_ref[0, rows, :]
        b_edge = bc[c - 1:c] if d == 0 else bc[0:1]
        qhat = q * jnp.exp(bc)
        khat = (k * jnp.exp(b_edge - bc)).astype(BF16)
        att_rows = ([], [])
        for i in range(nsub):
            lo = GLA_SUB * i
            hi = lo + GLA_SUB
            if d == 0:
                ref = bc[lo - 1:lo] if i > 0 else jnp.zeros((1, LANES), F32)
            else:
                ref = bc[hi:hi + 1] if i < nsub - 1 else jnp.zeros((1, LANES), F32)
            qi = q[lo:hi] * jnp.exp(bc[lo:hi] - ref)
            ki = (k * jnp.exp(jnp.minimum(ref - bc, GLA_EXP_CLAMP))).astype(BF16)
            for hh in range(2):
                qim = jnp.where(hmask[hh], qi, 0.0).astype(BF16)
                att_rows[hh].append(_dot_t(qim, ki))
        heads = []
        for hh in range(2):
            att = jnp.concatenate(att_rows[hh], axis=0)
            att = jnp.where(tri[d], att, 0.0).astype(BF16)
            vh = v[:, GLA_DV * hh:GLA_DV * (hh + 1)]
            kv = lax.dot_general(vh, khat, (((0,), (0,)), ((), ())), preferred_element_type=F32)
            qh = jnp.where(hmask[hh], qhat, 0.0).astype(BF16)
            heads.append((att, vh, kv, qh))
        return d, heads, jnp.exp(b_edge)

    def stage2(s1):
        d, heads, decay = s1
        outs = []
        for hh, (att, vh, kv, qh) in enumerate(heads):
            st = st_scr[d, hh]
            outs.append(jnp.dot(att, vh, preferred_element_type=F32) + _dot_t(qh, st.astype(BF16)))
            st_scr[d, hh] = st * decay + kv
        return jnp.concatenate(outs, axis=1)

    def finish(n, o):
        rows = pl.ds(pl.multiple_of(n * c, c), c)
        parts = []
        for hh in range(2):
            oh = o[:, GLA_DV * hh:GLA_DV * (hh + 1)]
            parts.append(oh * lax.rsqrt(jnp.mean(oh * oh, axis=-1, keepdims=True) + EPS))
        o = jnp.concatenate(parts, axis=1) * gn_ref[...]
        o_ref[0, rows, :] = (o * _silu(g_ref[0, rows, :].astype(F32))).astype(o_ref.dtype)

    st_scr[...] = jnp.zeros_like(st_scr)

    cpi = 2 if half % 2 == 0 else 1

    def walk(i, base_f, base_b):
        work = []
        for u in range(cpi):
            work.append((base_f + i * cpi + u, 0))
            work.append((base_b - i * cpi - u, 1))
        prepared = [stage1(n, d) for n, d in work]
        return [(n, stage2(s1)) for (n, _), s1 in zip(work, prepared)]

    def first(i, carry):
        for n, o in walk(i, 0, nc - 1):
            o_scr[pl.ds(pl.multiple_of(n * c, c), c), :] = o
        return carry

    lax.fori_loop(0, half // cpi, first, 0)

    def second(i, carry):
        for n, o in walk(i, half, half - 1):
            finish(n, o_scr[pl.ds(pl.multiple_of(n * c, c), c), :] + o)
        return carry

    lax.fori_loop(0, half // cpi, second, 0)


def _gla(proj3, small3, wdec_pad, bdec, gnorm):
    bn, seq, _ = proj3.shape
    npair = GLA_HEADS // 2
    qoff = 0
    koff = (GLA_HEADS * GLA_DK) // LANES
    voff = (2 * GLA_HEADS * GLA_DK) // (2 * GLA_DV)
    goff = voff + npair
    return pl.pallas_call(
        functools.partial(_gla_body, seq=seq),
        grid=(bn, npair),
        in_specs=[
            pl.BlockSpec((1, seq, LANES), lambda b, p: (b, 0, qoff + p)),
            pl.BlockSpec((1, seq, LANES), lambda b, p: (b, 0, koff + p)),
            pl.BlockSpec((1, seq, 2 * GLA_DV), lambda b, p: (b, 0, voff + p)),
            pl.BlockSpec((1, seq, 2 * GLA_DV), lambda b, p: (b, 0, goff + p)),
            pl.BlockSpec((1, seq, LANES), lambda b, p: (b, 0, 0)),
            pl.BlockSpec((2, LANES, LANES), lambda b, p: (0, 0, p)),
            pl.BlockSpec((2, LANES), lambda b, p: (0, p)),
            pl.BlockSpec((1, 2 * GLA_DV), lambda b, p: (0, p)),
        ],
        out_specs=pl.BlockSpec((1, seq, 2 * GLA_DV), lambda b, p: (b, 0, p)),
        out_shape=jax.ShapeDtypeStruct((bn, seq, GLA_HEADS * GLA_DV), BF16),
        scratch_shapes=[
            pltpu.VMEM((2, seq, LANES), F32),
            pltpu.VMEM((seq, 2 * GLA_DV), F32),
            pltpu.VMEM((2, 2, GLA_DV, LANES), F32),
        ],
        compiler_params=_cparams(("parallel", "parallel")),
        name="gla",
    )(proj3, proj3, proj3, proj3, small3, wdec_pad, bdec, gnorm)


def _na_bias_table(rpb):
    nh, ndr, ndc = rpb.shape
    cols = np.arange(GRID_W)
    cs = np.clip(cols - NA_KW // 2, 0, GRID_W - NA_KW)
    valid = (cols[None, :] >= cs[:, None]) & (cols[None, :] < cs[:, None] + NA_KW)
    dc = cols[None, :] - cols[:, None] + NA_KW - 1
    onehot = (np.arange(ndc)[:, None, None] == dc[None]) & valid[None]
    toep = jnp.dot(rpb.reshape(nh * ndr, ndc).astype(F32),
                   jnp.asarray(onehot.reshape(ndc, -1), F32), precision=HIGHEST)
    toep = jnp.where(jnp.asarray(valid.reshape(1, -1)), toep, NEG_BIG)
    toep = toep.reshape(nh, ndr, GRID_W, GRID_W)
    t = jnp.stack([toep[:, off:off + NA_KH] for off in range(NA_KH)], axis=1)
    return t.transpose(0, 1, 3, 2, 4).reshape(nh, NA_KH, GRID_W, NA_KH * GRID_W)


def _na_body(q_ref, k_ref, v_ref, bias_ref, o_ref, *, nrows):
    w = GRID_W
    nk = NA_KH * w

    def group(i, carry):
        idx = []
        scores = []
        for u in range(NA_ROWS_PER_ITER):
            r = i * NA_ROWS_PER_ITER + u
            rs = jnp.clip(r - NA_KH // 2, 0, nrows - NA_KH)
            qrows = pl.ds(pl.multiple_of(r * w, w), w)
            krows = pl.ds(pl.multiple_of(rs * w, w), nk)
            idx.append((qrows, krows, rs - r + (NA_KH - 1)))
            scores.append(_dot_t(q_ref[0, qrows, :], k_ref[0, krows, :]))
        probs = []
        for (qrows, krows, off), s in zip(idx, scores):
            s = s * (NA_DH ** -0.5) + bias_ref[0, off]
            p = jnp.exp(s - jnp.max(s, axis=-1, keepdims=True))
            probs.append((p.astype(BF16), jnp.sum(p, axis=-1, keepdims=True)))
        for (qrows, krows, off), (p, l) in zip(idx, probs):
            o = jnp.dot(p, v_ref[0, krows, :], preferred_element_type=F32) / l
            o_ref[0, qrows, :] = o.astype(o_ref.dtype)
        return carry

    lax.fori_loop(0, nrows // NA_ROWS_PER_ITER, group, 0)


def _na(proj3, bias_tab, col0):
    bn, seq, _ = proj3.shape
    nrows = seq // GRID_W
    qoff = col0 // NA_DH
    koff = qoff + NA_HEADS
    voff = koff + NA_HEADS
    return pl.pallas_call(
        functools.partial(_na_body, nrows=nrows),
        grid=(bn, NA_HEADS),
        in_specs=[
            pl.BlockSpec((1, seq, NA_DH), lambda b, h: (b, 0, qoff + h)),
            pl.BlockSpec((1, seq, NA_DH), lambda b, h: (b, 0, koff + h)),
            pl.BlockSpec((1, seq, NA_DH), lambda b, h: (b, 0, voff + h)),
            pl.BlockSpec((1, NA_KH, GRID_W, NA_KH * GRID_W), lambda b, h: (h, 0, 0, 0)),
        ],
        out_specs=pl.BlockSpec((1, seq, NA_DH), lambda b, h: (b, 0, h)),
        out_shape=jax.ShapeDtypeStruct((bn, seq, NA_HEADS * NA_DH), BF16),
        compiler_params=_cparams(("parallel", "parallel")),
        name="na",
    )(proj3, proj3, proj3, bias_tab)


CONV_PAD = 16


def _conv_body(x_ref, w_ref, b_ref, o_ref, xp_scr, *, seq):
    rb = min(256, seq)
    ch = x_ref.shape[-1]
    zeros = jnp.zeros((CONV_PAD, ch), xp_scr.dtype)
    xp_scr[0:CONV_PAD, :] = zeros
    xp_scr[seq + CONV_PAD:seq + 2 * CONV_PAD, :] = zeros
    xp_scr[CONV_PAD:seq + CONV_PAD, :] = x_ref[0]

    def blk(i, carry):
        r0 = pl.multiple_of(i * rb, rb)
        xw = xp_scr[pl.ds(r0, rb + 2 * CONV_PAD), :].astype(F32)
        acc = jnp.zeros((rb, ch), F32) + b_ref[...]
        for j in range(SSD_CONV):
            s0 = CONV_PAD - SSD_CONV // 2 + j
            acc = acc + xw[s0:s0 + rb] * w_ref[j:j + 1, :]
        o_ref[0, pl.ds(r0, rb), :] = _silu(acc).astype(o_ref.dtype)
        return carry

    lax.fori_loop(0, seq // rb, blk, 0)


def _conv(proj3, conv_w, conv_b, col0, tc=512):
    bn, seq, _ = proj3.shape
    cdim = conv_w.shape[1]
    c0 = col0 // tc
    return pl.pallas_call(
        functools.partial(_conv_body, seq=seq),
        grid=(bn, cdim // tc),
        in_specs=[
            pl.BlockSpec((1, seq, tc), lambda b, j: (b, 0, c0 + j)),
            pl.BlockSpec((SSD_CONV, tc), lambda b, j: (0, j)),
            pl.BlockSpec((1, tc), lambda b, j: (0, j)),
        ],
        out_specs=pl.BlockSpec((1, seq, tc), lambda b, j: (b, 0, j)),
        out_shape=jax.ShapeDtypeStruct((bn, seq, cdim), BF16),
        scratch_shapes=[pltpu.VMEM((seq + 2 * CONV_PAD, tc), BF16)],
        compiler_params=_cparams(("parallel", "parallel")),
        name="ssd_conv",
    )(proj3, conv_w, conv_b)


def _softplus(x):
    return jnp.maximum(x, 0.0) + jnp.log1p(jnp.exp(-jnp.abs(x)))


def _ssd_body(xs_ref, bm_ref, cm_ref, dt_ref, pb_ref, al_ref, dsk_ref, ex_ref, o_ref,
              st_scr, run_scr, dec_scr, *, seq):
    l = SSD_CHUNK
    nc = seq // l
    nh2 = 2 * SSD_JH
    row_i = lax.broadcasted_iota(jnp.int32, (l, l), 0)
    col_i = lax.broadcasted_iota(jnp.int32, (l, l), 1)
    lane = lax.broadcasted_iota(jnp.int32, (1, LANES), 1)
    first_head = lane < SSD_HEADDIM
    bwd_row = lax.broadcasted_iota(jnp.int32, (nh2, 1), 0) >= SSD_JH
    a_col = -jnp.exp(al_ref[0])

    lower = row_i > col_i
    diag = row_i == col_i
    triu_b = (row_i <= col_i).astype(BF16)

    def decays(n):
        rows = pl.ds(pl.multiple_of(n * l, l), l)
        dtv = _softplus(dt_ref[0, rows, :].T[0:nh2] + pb_ref[0])
        da = dtv * a_col
        hi = da.astype(BF16)
        r1 = da - hi.astype(F32)
        mid = r1.astype(BF16)
        lo = (r1 - mid.astype(F32)).astype(BF16)
        p3 = jnp.dot(jnp.concatenate([hi, mid, lo], axis=0), triu_b, preferred_element_type=F32)
        pre = p3[0:nh2] + p3[nh2:2 * nh2] + p3[2 * nh2:]
        total = pre[:, l - 1:l]
        acum = jnp.where(bwd_row, total - pre + da, pre)
        dec_scr[0, n] = dtv
        dec_scr[1, n] = acum
        dec_scr[2, n] = dtv * jnp.exp(total - acum)
        dec_scr[3, n] = jnp.broadcast_to(jnp.exp(total), (nh2, l))

    def chunk_state(n, d):
        rows = pl.ds(pl.multiple_of(n * l, l), l)
        w_state_t = dec_scr[2, n]
        x = xs_ref[0, rows, :]
        bmt = bm_ref[0, rows, :].astype(F32).T
        edge_ch = jnp.sum(dec_scr[3, n][:, 0:1] * ex_ref[d], axis=0, keepdims=True)
        new_st = []
        for jj in range(SSD_JH // 2):
            xp = x[:, LANES * jj:LANES * (jj + 1)]
            s2 = []
            for hh in range(2):
                cidx = d * SSD_JH + 2 * jj + hh
                bw = (bmt * w_state_t[cidx:cidx + 1, :]).astype(BF16)
                s2.append(jnp.dot(bw, xp, preferred_element_type=F32))
            new_st.append(jnp.where(first_head, s2[0], s2[1]))
        st = run_scr[d]
        st_scr[d, n] = st.astype(BF16)
        run_scr[d] = st * edge_ch + jnp.concatenate(new_st, axis=1)

    def chunk_out(n):
        rows = pl.ds(pl.multiple_of(n * l, l), l)
        dtv_t = dec_scr[0, n]
        acum_t = dec_scr[1, n]
        acum = acum_t.T
        x = xs_ref[0, rows, :]
        cm = cm_ref[0, rows, :]
        cmf = cm.astype(F32)
        cb = _dot_t(cm, bm_ref[0, rows, :])
        st_f = st_scr[0, n]
        st_b = st_scr[1, n]
        rhs = [jnp.concatenate([x[:, LANES * jj:LANES * (jj + 1)],
                                st_f[:, LANES * jj:LANES * (jj + 1)],
                                st_b[:, LANES * jj:LANES * (jj + 1)]], axis=0)
               for jj in range(SSD_JH // 2)]
        for r0 in range(0, l, SSD_ROW_SLAB):
            rs = slice(r0, r0 + SSD_ROW_SLAB)
            ys = []
            for jj in range(SSD_JH // 2):
                y2 = []
                for hh in range(2):
                    cf = 2 * jj + hh
                    cr = SSD_JH + cf
                    col_f = jnp.broadcast_to(acum[rs, cf:cf + 1], (SSD_ROW_SLAB, l))
                    col_r = jnp.broadcast_to(acum[rs, cr:cr + 1], (SSD_ROW_SLAB, l))
                    seg = jnp.where(lower[rs], col_f - acum_t[cf:cf + 1, :],
                                    col_r - acum_t[cr:cr + 1, :])
                    wgt = (jnp.where(lower[rs], dtv_t[cf:cf + 1, :], dtv_t[cr:cr + 1, :])
                           + jnp.where(diag[rs], dtv_t[cf:cf + 1, :], 0.0))
                    m = cb[rs] * jnp.exp(jnp.minimum(seg, 0.0)) * wgt
                    lhs = jnp.concatenate([m.astype(BF16),
                                           (cmf[rs] * jnp.exp(col_f)).astype(BF16),
                                           (cmf[rs] * jnp.exp(col_r)).astype(BF16)], axis=1)
                    y2.append(jnp.dot(lhs, rhs[jj], preferred_element_type=F32))
                ys.append(jnp.where(first_head, y2[0], y2[1]))
            y = jnp.concatenate(ys, axis=1) + x[rs].astype(F32) * dsk_ref[...]
            o_ref[0, pl.ds(pl.multiple_of(n * l, l) + r0, SSD_ROW_SLAB), :] = y.astype(o_ref.dtype)

    def prep(i, carry):
        decays(i)
        return carry

    lax.fori_loop(0, nc, prep, 0, unroll=2)
    run_scr[...] = jnp.zeros_like(run_scr)

    def states(i, carry):
        chunk_state(i, 0)
        chunk_state(nc - 1 - i, 1)
        return carry

    lax.fori_loop(0, nc, states, 0)

    def outputs(n, carry):
        chunk_out(n)
        return carry

    lax.fori_loop(0, nc, outputs, 0)


def _ssd(xbc_act, dt3, pb, al, dskip, expand):
    bn, seq, _ = xbc_act.shape
    gw = SSD_JH * SSD_HEADDIM
    boff = SSD_INNER // SSD_STATE
    coff = boff + SSD_GROUPS
    return pl.pallas_call(
        functools.partial(_ssd_body, seq=seq),
        grid=(bn, SSD_GROUPS),
        in_specs=[
            pl.BlockSpec((1, seq, gw), lambda b, g: (b, 0, g)),
            pl.BlockSpec((1, seq, SSD_STATE), lambda b, g: (b, 0, boff + g)),
            pl.BlockSpec((1, seq, SSD_STATE), lambda b, g: (b, 0, coff + g)),
            pl.BlockSpec((1, seq, LANES), lambda b, g: (b, 0, g)),
            pl.BlockSpec((1, 2 * SSD_JH, 1), lambda b, g: (g, 0, 0)),
            pl.BlockSpec((1, 2 * SSD_JH, 1), lambda b, g: (g, 0, 0)),
            pl.BlockSpec((1, gw), lambda b, g: (0, g)),
            pl.BlockSpec((2, 2 * SSD_JH, gw), lambda b, g: (0, 0, 0)),
        ],
        out_specs=pl.BlockSpec((1, seq, gw), lambda b, g: (b, 0, g)),
        out_shape=jax.ShapeDtypeStruct((bn, seq, SSD_INNER), BF16),
        scratch_shapes=[pltpu.VMEM((2, seq // SSD_CHUNK, SSD_STATE, gw), BF16),
                        pltpu.VMEM((2, SSD_STATE, gw), F32),
                        pltpu.VMEM((4, seq // SSD_CHUNK, 2 * SSD_JH, SSD_CHUNK), F32)],
        compiler_params=_cparams(("parallel", "parallel")),
        name="ssd_scan",
    )(xbc_act, xbc_act, xbc_act, dt3, pb, al, dskip, expand)


def _rope_tables(seq):
    half = SWA_DH // 2
    inv = ROPE_THETA ** (-np.arange(half, dtype=np.float64) / half)
    ang = np.arange(seq, dtype=np.float64)[:, None] * inv[None, :]
    cos, sin = np.cos(ang), np.sin(ang)
    return (jnp.asarray(np.concatenate([cos, cos], axis=1), F32),
            jnp.asarray(np.concatenate([-sin, sin], axis=1), F32))


def _rope(x, cos, sin_signed):
    return x * cos + pltpu.roll(x, SWA_DH // 2, 1) * sin_signed


def _swa_body(sink_ref, q_ref, k_ref, v_ref, cos_ref, sin_ref, o_ref, kr_scr, *, seq):
    wb = SWA_BLOCK
    nb = seq // wb
    nkeys = 3 * wb
    kvh = pl.program_id(1)
    rb = min(512, seq)

    def krope(i, carry):
        rows = pl.ds(pl.multiple_of(i * rb, rb), rb)
        kr_scr[rows, :] = _rope(k_ref[0, rows, :].astype(F32), cos_ref[rows, :],
                                sin_ref[rows, :]).astype(BF16)
        return carry

    lax.fori_loop(0, seq // rb, krope, 0)

    qpos_l = lax.broadcasted_iota(jnp.int32, (wb, nkeys), 0)
    kpos_l = lax.broadcasted_iota(jnp.int32, (wb, nkeys), 1)

    def blk(n, carry):
        rows = pl.ds(pl.multiple_of(n * wb, wb), wb)
        ks = pl.multiple_of(jnp.clip(n - 1, 0, nb - 3) * wb, wb)
        krows = pl.ds(ks, nkeys)
        cos = cos_ref[rows, :]
        sin = sin_ref[rows, :]
        kk = kr_scr[krows, :]
        vv = v_ref[0, krows, :]
        valid = jnp.abs((qpos_l + n * wb) - (kpos_l + ks)) <= SWA_WINDOW
        qb = q_ref[0, rows, :].astype(F32)
        scores = []
        for hh in range(SWA_GQ):
            qh = (_rope(qb[:, SWA_DH * hh:SWA_DH * (hh + 1)], cos, sin) * (SWA_DH ** -0.5)).astype(BF16)
            scores.append(_dot_t(qh, kk))
        probs = []
        for hh, s in enumerate(scores):
            sink = sink_ref[kvh * SWA_GQ + hh]
            s = jnp.where(valid, s, NEG_BIG)
            m = jnp.maximum(jnp.max(s, axis=-1, keepdims=True), sink)
            p = jnp.exp(s - m)
            probs.append((p.astype(BF16), jnp.sum(p, axis=-1, keepdims=True) + jnp.exp(sink - m)))
        for hh, (p, den) in enumerate(probs):
            o = jnp.dot(p, vv, preferred_element_type=F32) / den
            o_ref[0, rows, SWA_DH * hh:SWA_DH * (hh + 1)] = o.astype(o_ref.dtype)
        return carry

    lax.fori_loop(0, nb, blk, 0)


def _swa(proj3, sink, cos, sin, col0):
    bn, seq, _ = proj3.shape
    qw = SWA_GQ * SWA_DH
    qoff = col0 // qw
    koff = (col0 + SWA_HEADS * SWA_DH) // SWA_DH
    voff = koff + SWA_KV
    return pl.pallas_call(
        functools.partial(_swa_body, seq=seq),
        grid=(bn, SWA_KV),
        in_specs=[
            pl.BlockSpec(memory_space=pltpu.SMEM),
            pl.BlockSpec((1, seq, qw), lambda b, h: (b, 0, qoff + h)),
            pl.BlockSpec((1, seq, SWA_DH), lambda b, h: (b, 0, koff + h)),
            pl.BlockSpec((1, seq, SWA_DH), lambda b, h: (b, 0, voff + h)),
            pl.BlockSpec((seq, SWA_DH), lambda b, h: (0, 0)),
            pl.BlockSpec((seq, SWA_DH), lambda b, h: (0, 0)),
        ],
        out_specs=pl.BlockSpec((1, seq, qw), lambda b, h: (b, 0, h)),
        out_shape=jax.ShapeDtypeStruct((bn, seq, SWA_HEADS * SWA_DH), BF16),
        scratch_shapes=[pltpu.VMEM((seq, SWA_DH), BF16)],
        compiler_params=_cparams(("parallel", "parallel")),
        name="swa",
    )(sink, proj3, proj3, proj3, cos, sin)


def _pad_cols(w, width):
    return jnp.pad(w, ((0, 0), (0, width - w.shape[1])))


def _even_mixers(x2, bn, seq, g0, w_in, w_decay, b_decay, gla_norm, rpb):
    d = x2.shape[1]
    n_gla = 2 * GLA_HEADS * GLA_DK + 2 * GLA_HEADS * GLA_DV
    n_lr = 2 * GLA_LOWRANK
    w_main = jnp.concatenate([w_in[:, :n_gla], w_in[:, n_gla + n_lr:]], axis=1).astype(BF16)
    w_small = _pad_cols(w_in[:, n_gla:n_gla + n_lr], LANES).astype(BF16)
    proj, small = _inproj(x2, g0, w_main, w_small)
    proj3 = proj.reshape(bn, seq, -1)
    small3 = small.reshape(bn, seq, -1)
    wdec = jnp.zeros((2, LANES, w_decay.shape[-1]), F32)
    for dd in range(2):
        wdec = wdec.at[dd, dd * GLA_LOWRANK:(dd + 1) * GLA_LOWRANK].set(w_decay[dd])
    o_a = _gla(proj3, small3, wdec, b_decay, gla_norm[None, :])
    o_b = _na(proj3, _na_bias_table(rpb), n_gla)
    return o_a.reshape(-1, o_a.shape[-1]), o_b.reshape(-1, o_b.shape[-1])


def _per_group(v):
    return v.reshape(2, SSD_GROUPS, SSD_JH).transpose(1, 0, 2).reshape(SSD_GROUPS, 2 * SSD_JH)


def _odd_mixers(x2, bn, seq, g0, w_in, conv_w, conv_b, dt_bias, a_log, d_skip, sink, cos, sin):
    conv_dim = SSD_INNER + 2 * SSD_GROUPS * SSD_STATE
    n_z = SSD_INNER
    n_dt = 2 * SSD_HEADS
    c_dt = n_z + conv_dim
    w_main = jnp.concatenate([w_in[:, :c_dt], w_in[:, c_dt + n_dt:]], axis=1).astype(BF16)
    w_dt = w_in[:, c_dt:c_dt + n_dt].reshape(-1, 2, SSD_GROUPS, SSD_JH).transpose(0, 2, 1, 3)
    w_dt = w_dt.reshape(-1, SSD_GROUPS, 2 * SSD_JH)
    w_small = jnp.pad(w_dt, ((0, 0), (0, 0), (0, LANES - 2 * SSD_JH))).reshape(-1, SSD_GROUPS * LANES)
    proj, small = _inproj(x2, g0, w_main, w_small.astype(BF16))
    proj3 = proj.reshape(bn, seq, -1)
    dt3 = small.reshape(bn, seq, SSD_GROUPS * LANES)
    xbc_act = _conv(proj3, conv_w, conv_b[None, :], n_z)
    pbg = _per_group(dt_bias)
    alg = _per_group(a_log)
    head_of_ch = np.arange(SSD_JH * SSD_HEADDIM) // SSD_HEADDIM
    expand = jnp.asarray(np.arange(2 * SSD_JH)[None, :, None]
                         == (np.arange(2)[:, None, None] * SSD_JH + head_of_ch[None, None, :]), F32)
    dskip = jnp.repeat(d_skip.astype(F32), SSD_HEADDIM)[None, :]
    y_pre = _ssd(xbc_act, dt3, pbg[:, :, None], alg[:, :, None], dskip, expand)
    o_d = _swa(proj3, sink, cos, sin, c_dt)
    return y_pre.reshape(-1, SSD_INNER), proj, o_d.reshape(-1, o_d.shape[-1])


def kernel(x, norm_gains, ffn_w_gate, ffn_w_up, ffn_w_down, even_w_in, even_w_out, gla_w_decay,
           gla_b_decay, gla_norm, na_rpb, odd_w_in, odd_w_out, ssd_conv_w, ssd_conv_b, ssd_dt_bias,
           ssd_a_log, ssd_d, ssd_norm, swa_sink):
    bn, seq, d = x.shape
    depth = norm_gains.shape[0]
    x2 = x.reshape(bn * seq, d)
    cos, sin = _rope_tables(seq)
    wg = ffn_w_gate.astype(BF16)
    wu = ffn_w_up.astype(BF16)
    wd = ffn_w_down.astype(BF16)
    even_w_out_b = even_w_out.astype(BF16)
    odd_w_out_b = odd_w_out.astype(BF16)
    for layer in range(depth):
        g = norm_gains[layer][:, None, :]
        i = layer // 2
        if layer % 2 == 0:
            o_a, o_b = _even_mixers(x2, bn, seq, g[0], even_w_in[i], gla_w_decay[i], gla_b_decay[i],
                                    gla_norm[i], na_rpb[i])
            x2 = _outproj(x2, o_a, o_b, even_w_out_b, i, g[1])
        else:
            y_pre, proj, o_d = _odd_mixers(x2, bn, seq, g[0], odd_w_in[i], ssd_conv_w[i], ssd_conv_b[i],
                                           ssd_dt_bias[i], ssd_a_log[i], ssd_d[i], swa_sink[i], cos, sin)
            x2 = _outproj(x2, y_pre, o_d, odd_w_out_b, i, g[1],
                          z_src=proj, ssd_gain=ssd_norm[i][None, :])
        x2 = _ffn(x2, g[2], wg, wu, wd, g[3], layer)
    return x2.reshape(bn, seq, d)
```

```python
import functools
import math

import jax
import jax.numpy as jnp
import numpy as np
from jax import lax
from jax.experimental import pallas as pl
from jax.experimental.pallas import tpu as pltpu

F32 = jnp.float32
BF16 = jnp.bfloat16
HIGHEST = lax.Precision.HIGHEST

EPS = 1e-6
ROPE_THETA = 10000.0
GRID_W = 64
LANES = 128

GLA_DK = 64
GLA_DV = 128
GLA_HEADS = 8
GLA_LOWRANK = 16
GLA_TAU = 16.0
GLA_CHUNK = 64
GLA_SUB = 16
GLA_EXP_CLAMP = 60.0
GLA_TRI_BLOCK = 256

NA_DH = 128
NA_HEADS = 8
NA_KH = 8
NA_KW = 16
NA_ROWS_PER_ITER = 8
NEG_BIG = -1e30

SSD_INNER = 1024
SSD_HEADDIM = 64
SSD_HEADS = 16
SSD_GROUPS = 2
SSD_STATE = 128
SSD_CONV = 5
SSD_CHUNK = 128
SSD_JH = SSD_HEADS // SSD_GROUPS
SSD_ROW_SLAB = 64

SWA_DH = 128
SWA_HEADS = 8
SWA_KV = 2
SWA_GQ = SWA_HEADS // SWA_KV
SWA_WINDOW = 128
SWA_BLOCK = 128

VMEM_LIMIT = 52 * 1024 * 1024


def _cparams(sem):
    return pltpu.CompilerParams(dimension_semantics=sem, vmem_limit_bytes=VMEM_LIMIT)


def _rms(x, g):
    return x * lax.rsqrt(jnp.mean(x * x, axis=-1, keepdims=True) + EPS) * g


def _silu(x):
    return x * (1.0 / (1.0 + jnp.exp(-x)))


def _dot_t(a, b):
    return lax.dot_general(a, b, (((1,), (1,)), ((), ())), preferred_element_type=F32)


def _inproj_body(x_ref, g_ref, w_ref, ws_ref, o_ref, os_ref, h_scr):
    @pl.when(pl.program_id(1) == 0)
    def _():
        h = _rms(x_ref[...], g_ref[...]).astype(BF16)
        h_scr[...] = h
        os_ref[...] = jnp.dot(h, ws_ref[...], preferred_element_type=F32)

    o_ref[...] = jnp.dot(h_scr[...], w_ref[...], preferred_element_type=F32).astype(o_ref.dtype)


def _inproj(x2, g, w_main, w_small, tm=1024, tn=2048):
    n, d = x2.shape
    nm = w_main.shape[1]
    ns = w_small.shape[1]
    return pl.pallas_call(
        _inproj_body,
        grid=(n // tm, nm // tn),
        in_specs=[
            pl.BlockSpec((tm, d), lambda i, j: (i, 0)),
            pl.BlockSpec((1, d), lambda i, j: (0, 0)),
            pl.BlockSpec((d, tn), lambda i, j: (0, j)),
            pl.BlockSpec((d, ns), lambda i, j: (0, 0)),
        ],
        out_specs=[
            pl.BlockSpec((tm, tn), lambda i, j: (i, j)),
            pl.BlockSpec((tm, ns), lambda i, j: (i, 0)),
        ],
        out_shape=[jax.ShapeDtypeStruct((n, nm), BF16), jax.ShapeDtypeStruct((n, ns), F32)],
        scratch_shapes=[pltpu.VMEM((tm, d), BF16)],
        compiler_params=_cparams(("parallel", "arbitrary")),
        name="inproj",
    )(x2, g, w_main, w_small)


def _outproj_even_body(a_ref, b_ref, wa_ref, wb_ref, g_ref, x_ref, o_ref):
    mix = (jnp.dot(a_ref[...], wa_ref[...], preferred_element_type=F32)
           + jnp.dot(b_ref[...], wb_ref[...], preferred_element_type=F32))
    o_ref[...] = x_ref[...] + _rms(mix, g_ref[...])


def _outproj_odd_body(y_ref, z_ref, ng_ref, b_ref, wa_ref, wb_ref, g_ref, x_ref, o_ref):
    y = y_ref[...].astype(F32) * _silu(z_ref[...].astype(F32))
    a = _rms(y, ng_ref[...]).astype(BF16)
    mix = (jnp.dot(a, wa_ref[...], preferred_element_type=F32)
           + jnp.dot(b_ref[...], wb_ref[...], preferred_element_type=F32))
    o_ref[...] = x_ref[...] + _rms(mix, g_ref[...])


def _outproj(x2, a, b, w_out, li, g, z_src=None, ssd_gain=None, tm=512):
    n, d = x2.shape
    half = a.shape[1]
    row = lambda i: (i, 0)
    fixed = lambda i: (0, 0)
    w_specs = [pl.BlockSpec((None, half, d), lambda i: (li, 0, 0)),
               pl.BlockSpec((None, half, d), lambda i: (li, 1, 0))]
    tail_specs = [pl.BlockSpec((1, d), fixed), pl.BlockSpec((tm, d), row)]
    if z_src is None:
        body = _outproj_even_body
        in_specs = [pl.BlockSpec((tm, half), row), pl.BlockSpec((tm, half), row)] + w_specs + tail_specs
        args = (a, b, w_out, w_out, g, x2)
    else:
        body = _outproj_odd_body
        in_specs = ([pl.BlockSpec((tm, half), row), pl.BlockSpec((tm, half), row),
                     pl.BlockSpec((1, half), fixed), pl.BlockSpec((tm, half), row)]
                    + w_specs + tail_specs)
        args = (a, z_src, ssd_gain, b, w_out, w_out, g, x2)
    return pl.pallas_call(
        body,
        grid=(n // tm,),
        in_specs=in_specs,
        out_specs=pl.BlockSpec((tm, d), row),
        out_shape=jax.ShapeDtypeStruct((n, d), F32),
        compiler_params=_cparams(("parallel",)),
        name="outproj",
    )(*args)


def _ffn_body(x_ref, g2_ref, wg_ref, wu_ref, wd_ref, g3_ref, o_ref, h_scr, acc_scr):
    k = pl.program_id(1)

    @pl.when(k == 0)
    def _():
        h_scr[...] = _rms(x_ref[...], g2_ref[...]).astype(BF16)
        acc_scr[...] = jnp.zeros_like(acc_scr)

    h = h_scr[...]
    gate = jnp.dot(h, wg_ref[...], preferred_element_type=F32)
    up = jnp.dot(h, wu_ref[...], preferred_element_type=F32)
    act = (_silu(gate) * up).astype(BF16)
    acc_scr[...] += jnp.dot(act, wd_ref[...], preferred_element_type=F32)

    @pl.when(k == pl.num_programs(1) - 1)
    def _():
        o_ref[...] = x_ref[...] + _rms(acc_scr[...], g3_ref[...])


def _ffn(x2, g2, wg, wu, wd, g3, layer, tm=512, th=512):
    n, d = x2.shape
    hid = wd.shape[1]
    return pl.pallas_call(
        _ffn_body,
        grid=(n // tm, hid // th),
        in_specs=[
            pl.BlockSpec((tm, d), lambda i, k: (i, 0)),
            pl.BlockSpec((1, d), lambda i, k: (0, 0)),
            pl.BlockSpec((None, d, th), lambda i, k: (layer, 0, k)),
            pl.BlockSpec((None, d, th), lambda i, k: (layer, 0, k)),
            pl.BlockSpec((None, th, d), lambda i, k: (layer, k, 0)),
            pl.BlockSpec((1, d), lambda i, k: (0, 0)),
        ],
        out_specs=pl.BlockSpec((tm, d), lambda i, k: (i, 0)),
        out_shape=jax.ShapeDtypeStruct((n, d), F32),
        scratch_shapes=[pltpu.VMEM((tm, d), BF16), pltpu.VMEM((tm, d), F32)],
        compiler_params=_cparams(("parallel", "arbitrary")),
        name="ffn",
    )(x2, g2, wg, wu, wd, g3)


def _gla_body(q_ref, k_ref, v_ref, g_ref, lr_ref, wd_ref, bd_ref, gn_ref, o_ref,
              bc_scr, o_scr, st_scr, *, seq):
    c = GLA_CHUNK
    nc = seq // c
    half = nc // 2
    nsub = c // GLA_SUB
    rb = min(512, seq)
    row_i = lax.broadcasted_iota(jnp.int32, (c, c), 0)
    col_i = lax.broadcasted_iota(jnp.int32, (c, c), 1)
    tri = (row_i >= col_i, row_i <= col_i)
    lane = lax.broadcasted_iota(jnp.int32, (1, LANES), 1)
    hmask = (lane < GLA_DK, lane >= GLA_DK)

    w2 = jnp.concatenate([wd_ref[0], wd_ref[1]], axis=1)
    w_hi = w2.astype(BF16)
    w_lo = (w2 - w_hi.astype(F32)).astype(BF16)
    w_cat = jnp.concatenate([w_hi, w_lo, w_hi], axis=0)
    tb = GLA_TRI_BLOCK
    brow = lax.broadcasted_iota(jnp.int32, (tb, tb), 0)
    bcol = lax.broadcasted_iota(jnp.int32, (tb, tb), 1)
    same_chunk = (brow // c) == (bcol // c)
    blk_tri = ((same_chunk & (brow >= bcol)).astype(BF16), (same_chunk & (brow <= bcol)).astype(BF16))

    def pre(i, carry):
        r0 = pl.multiple_of(i * rb, rb)
        lr = lr_ref[0, pl.ds(r0, rb), :]
        lr_hi = lr.astype(BF16)
        lr_lo = (lr - lr_hi.astype(F32)).astype(BF16)
        z2 = jnp.dot(jnp.concatenate([lr_hi, lr_hi, lr_lo], axis=1), w_cat, preferred_element_type=F32)
        las = []
        for d in range(2):
            z = z2[:, LANES * d:LANES * (d + 1)] + bd_ref[d:d + 1, :]
            la = (jnp.minimum(z, 0.0) - jnp.log1p(jnp.exp(-jnp.abs(z)))) * (1.0 / GLA_TAU)
            la_hi = la.astype(BF16)
            las.append(jnp.concatenate([la_hi, (la - la_hi.astype(F32)).astype(BF16)], axis=1))
        for d in range(2):
            for t in range(rb // tb):
                s2 = jnp.dot(blk_tri[d], las[d][t * tb:(t + 1) * tb], preferred_element_type=F32)
                bc_scr[d, pl.ds(r0 + t * tb, tb), :] = s2[:, :LANES] + s2[:, LANES:]
        return carry

    lax.fori_loop(0, seq // rb, pre, 0)

    def stage1(n, d):
        rows = pl.ds(pl.multiple_of(n * c, c), c)
        bc = bc_scr[d, rows, :]
        q = q_ref[0, rows, :].astype(F32) * (GLA_DK ** -0.5)
        k = k_ref[0, rows, :].astype(F32)
        v = v_ref[0, rows, :]
        b_edge = bc[c - 1:c] if d == 0 else bc[0:1]
        qhat = q * jnp.exp(bc)
        khat = (k * jnp.exp(b_edge - bc)).astype(BF16)
        att_rows = ([], [])
        for i in range(nsub):
            lo = GLA_SUB * i
            hi = lo + GLA_SUB
            if d == 0:
                ref = bc[lo - 1:lo] if i > 0 else jnp.zeros((1, LANES), F32)
            else:
                ref = bc[hi:hi + 1] if i < nsub - 1 else jnp.zeros((1, LANES), F32)
            qi = q[lo:hi] * jnp.exp(bc[lo:hi] - ref)
            ki = (k * jnp.exp(jnp.minimum(ref - bc, GLA_EXP_CLAMP))).astype(BF16)
            for hh in range(2):
                qim = jnp.where(hmask[hh], qi, 0.0).astype(BF16)
                att_rows[hh].append(_dot_t(qim, ki))
        heads = []
        for hh in range(2):
            att = jnp.concatenate(att_rows[hh], axis=0)
            att = jnp.where(tri[d], att, 0.0).astype(BF16)
            vh = v[:, GLA_DV * hh:GLA_DV * (hh + 1)]
            kv = lax.dot_general(vh, khat, (((0,), (0,)), ((), ())), preferred_element_type=F32)
            qh = jnp.where(hmask[hh], qhat, 0.0).astype(BF16)
            heads.append((att, vh, kv, qh))
        return d, heads, jnp.exp(b_edge)

    def stage2(s1, states):
        d, heads, decay = s1
        outs = []
        for hh, (att, vh, kv, qh) in enumerate(heads):
            st = states[d][hh]
            outs.append(jnp.dot(att, vh, preferred_element_type=F32) + _dot_t(qh, st.astype(BF16)))
            states[d][hh] = st * decay + kv
        return jnp.concatenate(outs, axis=1)

    def finish(n, o):
        rows = pl.ds(pl.multiple_of(n * c, c), c)
        parts = []
        for hh in range(2):
            oh = o[:, GLA_DV * hh:GLA_DV * (hh + 1)]
            parts.append(oh * lax.rsqrt(jnp.mean(oh * oh, axis=-1, keepdims=True) + EPS))
        o = jnp.concatenate(parts, axis=1) * gn_ref[...]
        o_ref[0, rows, :] = (o * _silu(g_ref[0, rows, :].astype(F32))).astype(o_ref.dtype)

    st_scr[...] = jnp.zeros_like(st_scr)

    cpi = next(u for u in (4, 2, 1) if half % u == 0)

    def walk(i, base_f, base_b):
        work = []
        for u in range(cpi):
            work.append((base_f + i * cpi + u, 0))
            work.append((base_b - i * cpi - u, 1))
        prepared = [stage1(n, d) for n, d in work]
        states = [[st_scr[d, hh] for hh in range(2)] for d in range(2)]
        outs = [(n, stage2(s1, states)) for (n, _), s1 in zip(work, prepared)]
        for d in range(2):
            for hh in range(2):
                st_scr[d, hh] = states[d][hh]
        return outs

    def first(i, carry):
        for n, o in walk(i, 0, nc - 1):
            o_scr[pl.ds(pl.multiple_of(n * c, c), c), :] = o
        return carry

    lax.fori_loop(0, half // cpi, first, 0)

    def second(i, carry):
        for n, o in walk(i, half, half - 1):
            finish(n, o_scr[pl.ds(pl.multiple_of(n * c, c), c), :] + o)
        return carry

    lax.fori_loop(0, half // cpi, second, 0)


def _gla(proj3, small3, wdec_pad, bdec, gnorm):
    bn, seq, _ = proj3.shape
    npair = GLA_HEADS // 2
    qoff = 0
    koff = (GLA_HEADS * GLA_DK) // LANES
    voff = (2 * GLA_HEADS * GLA_DK) // (2 * GLA_DV)
    goff = voff + npair
    return pl.pallas_call(
        functools.partial(_gla_body, seq=seq),
        grid=(bn, npair),
        in_specs=[
            pl.BlockSpec((1, seq, LANES), lambda b, p: (b, 0, qoff + p)),
            pl.BlockSpec((1, seq, LANES), lambda b, p: (b, 0, koff + p)),
            pl.BlockSpec((1, seq, 2 * GLA_DV), lambda b, p: (b, 0, voff + p)),
            pl.BlockSpec((1, seq, 2 * GLA_DV), lambda b, p: (b, 0, goff + p)),
            pl.BlockSpec((1, seq, LANES), lambda b, p: (b, 0, 0)),
            pl.BlockSpec((2, LANES, LANES), lambda b, p: (0, 0, p)),
            pl.BlockSpec((2, LANES), lambda b, p: (0, p)),
            pl.BlockSpec((1, 2 * GLA_DV), lambda b, p: (0, p)),
        ],
        out_specs=pl.BlockSpec((1, seq, 2 * GLA_DV), lambda b, p: (b, 0, p)),
        out_shape=jax.ShapeDtypeStruct((bn, seq, GLA_HEADS * GLA_DV), BF16),
        scratch_shapes=[
            pltpu.VMEM((2, seq, LANES), F32),
            pltpu.VMEM((seq, 2 * GLA_DV), F32),
            pltpu.VMEM((2, 2, GLA_DV, LANES), F32),
        ],
        compiler_params=_cparams(("parallel", "parallel")),
        name="gla",
    )(proj3, proj3, proj3, proj3, small3, wdec_pad, bdec, gnorm)


def _na_bias_table(rpb):
    nh, ndr, ndc = rpb.shape
    cols = np.arange(GRID_W)
    cs = np.clip(cols - NA_KW // 2, 0, GRID_W - NA_KW)
    valid = (cols[None, :] >= cs[:, None]) & (cols[None, :] < cs[:, None] + NA_KW)
    dc = cols[None, :] - cols[:, None] + NA_KW - 1
    onehot = (np.arange(ndc)[:, None, None] == dc[None]) & valid[None]
    toep = jnp.dot(rpb.reshape(nh * ndr, ndc).astype(F32),
                   jnp.asarray(onehot.reshape(ndc, -1), F32), precision=HIGHEST)
    toep = jnp.where(jnp.asarray(valid.reshape(1, -1)), toep, NEG_BIG)
    toep = toep.reshape(nh, ndr, GRID_W, GRID_W)
    t = jnp.stack([toep[:, off:off + NA_KH] for off in range(NA_KH)], axis=1)
    return t.transpose(0, 1, 3, 2, 4).reshape(nh, NA_KH, GRID_W, NA_KH * GRID_W)


def _na_body(q_ref, k_ref, v_ref, bias_ref, o_ref, *, nrows):
    w = GRID_W
    nk = NA_KH * w

    def group(i, carry):
        idx = []
        scores = []
        for u in range(NA_ROWS_PER_ITER):
            r = i * NA_ROWS_PER_ITER + u
            rs = jnp.clip(r - NA_KH // 2, 0, nrows - NA_KH)
            qrows = pl.ds(pl.multiple_of(r * w, w), w)
            krows = pl.ds(pl.multiple_of(rs * w, w), nk)
            idx.append((qrows, krows, rs - r + (NA_KH - 1)))
            scores.append(_dot_t(q_ref[0, qrows, :], k_ref[0, krows, :]))
        probs = []
        for (qrows, krows, off), s in zip(idx, scores):
            s = s * (NA_DH ** -0.5) + bias_ref[0, off]
            p = jnp.exp(s - jnp.max(s, axis=-1, keepdims=True))
            probs.append((p.astype(BF16), jnp.sum(p, axis=-1, keepdims=True)))
        for (qrows, krows, off), (p, l) in zip(idx, probs):
            o = jnp.dot(p, v_ref[0, krows, :], preferred_element_type=F32) / l
            o_ref[0, qrows, :] = o.astype(o_ref.dtype)
        return carry

    lax.fori_loop(0, nrows // NA_ROWS_PER_ITER, group, 0)


def _na(proj3, bias_tab, col0):
    bn, seq, _ = proj3.shape
    nrows = seq // GRID_W
    qoff = col0 // NA_DH
    koff = qoff + NA_HEADS
    voff = koff + NA_HEADS
    return pl.pallas_call(
        functools.partial(_na_body, nrows=nrows),
        grid=(bn, NA_HEADS),
        in_specs=[
            pl.BlockSpec((1, seq, NA_DH), lambda b, h: (b, 0, qoff + h)),
            pl.BlockSpec((1, seq, NA_DH), lambda b, h: (b, 0, koff + h)),
            pl.BlockSpec((1, seq, NA_DH), lambda b, h: (b, 0, voff + h)),
            pl.BlockSpec((1, NA_KH, GRID_W, NA_KH * GRID_W), lambda b, h: (h, 0, 0, 0)),
        ],
        out_specs=pl.BlockSpec((1, seq, NA_DH), lambda b, h: (b, 0, h)),
        out_shape=jax.ShapeDtypeStruct((bn, seq, NA_HEADS * NA_DH), BF16),
        compiler_params=_cparams(("parallel", "parallel")),
        name="na",
    )(proj3, proj3, proj3, bias_tab)


CONV_PAD = 16


def _conv_body(x_ref, w_ref, b_ref, o_ref, xp_scr, *, seq):
    rb = min(256, seq)
    ch = x_ref.shape[-1]
    zeros = jnp.zeros((CONV_PAD, ch), xp_scr.dtype)
    xp_scr[0:CONV_PAD, :] = zeros
    xp_scr[seq + CONV_PAD:seq + 2 * CONV_PAD, :] = zeros
    xp_scr[CONV_PAD:seq + CONV_PAD, :] = x_ref[0]

    def blk(i, carry):
        r0 = pl.multiple_of(i * rb, rb)
        xw = xp_scr[pl.ds(r0, rb + 2 * CONV_PAD), :].astype(F32)
        acc = jnp.zeros((rb, ch), F32) + b_ref[...]
        for j in range(SSD_CONV):
            s0 = CONV_PAD - SSD_CONV // 2 + j
            acc = acc + xw[s0:s0 + rb] * w_ref[j:j + 1, :]
        o_ref[0, pl.ds(r0, rb), :] = _silu(acc).astype(o_ref.dtype)
        return carry

    lax.fori_loop(0, seq // rb, blk, 0)


def _conv(proj3, conv_w, conv_b, col0, tc=512):
    bn, seq, _ = proj3.shape
    cdim = conv_w.shape[1]
    c0 = col0 // tc
    return pl.pallas_call(
        functools.partial(_conv_body, seq=seq),
        grid=(bn, cdim // tc),
        in_specs=[
            pl.BlockSpec((1, seq, tc), lambda b, j: (b, 0, c0 + j)),
            pl.BlockSpec((SSD_CONV, tc), lambda b, j: (0, j)),
            pl.BlockSpec((1, tc), lambda b, j: (0, j)),
        ],
        out_specs=pl.BlockSpec((1, seq, tc), lambda b, j: (b, 0, j)),
        out_shape=jax.ShapeDtypeStruct((bn, seq, cdim), BF16),
        scratch_shapes=[pltpu.VMEM((seq + 2 * CONV_PAD, tc), BF16)],
        compiler_params=_cparams(("parallel", "parallel")),
        name="ssd_conv",
    )(proj3, conv_w, conv_b)


def _softplus(x):
    return jnp.maximum(x, 0.0) + jnp.log1p(jnp.exp(-jnp.abs(x)))


def _ssd_body(xs_ref, bm_ref, cm_ref, dt_ref, pb_ref, al_ref, dsk_ref, ex_ref, o_ref,
              st_scr, run_scr, dec_scr, *, seq):
    l = SSD_CHUNK
    nc = seq // l
    nh2 = 2 * SSD_JH
    row_i = lax.broadcasted_iota(jnp.int32, (l, l), 0)
    col_i = lax.broadcasted_iota(jnp.int32, (l, l), 1)
    lane = lax.broadcasted_iota(jnp.int32, (1, LANES), 1)
    first_head = lane < SSD_HEADDIM
    bwd_row = lax.broadcasted_iota(jnp.int32, (nh2, 1), 0) >= SSD_JH
    a_col = -jnp.exp(al_ref[0])

    lower = row_i > col_i
    diag = row_i == col_i
    triu_b = (row_i <= col_i).astype(BF16)

    def decays(n):
        rows = pl.ds(pl.multiple_of(n * l, l), l)
        dtv = _softplus(dt_ref[0, rows, :].T[0:nh2] + pb_ref[0])
        da = dtv * a_col
        hi = da.astype(BF16)
        r1 = da - hi.astype(F32)
        mid = r1.astype(BF16)
        lo = (r1 - mid.astype(F32)).astype(BF16)
        p3 = jnp.dot(jnp.concatenate([hi, mid, lo], axis=0), triu_b, preferred_element_type=F32)
        pre = p3[0:nh2] + p3[nh2:2 * nh2] + p3[2 * nh2:]
        total = pre[:, l - 1:l]
        acum = jnp.where(bwd_row, total - pre + da, pre)
        dec_scr[0, n] = dtv
        dec_scr[1, n] = acum
        dec_scr[2, n] = dtv * jnp.exp(total - acum)
        dec_scr[3, n] = jnp.broadcast_to(jnp.exp(total), (nh2, l))

    def chunk_state(n, d):
        rows = pl.ds(pl.multiple_of(n * l, l), l)
        w_state_t = dec_scr[2, n]
        x = xs_ref[0, rows, :]
        bmt = bm_ref[0, rows, :].astype(F32).T
        edge_ch = jnp.sum(dec_scr[3, n][:, 0:1] * ex_ref[d], axis=0, keepdims=True)
        new_st = []
        for jj in range(SSD_JH // 2):
            xp = x[:, LANES * jj:LANES * (jj + 1)]
            s2 = []
            for hh in range(2):
                cidx = d * SSD_JH + 2 * jj + hh
                bw = (bmt * w_state_t[cidx:cidx + 1, :]).astype(BF16)
                s2.append(jnp.dot(bw, xp, preferred_element_type=F32))
            new_st.append(jnp.where(first_head, s2[0], s2[1]))
        st = run_scr[d]
        st_scr[d, n] = st.astype(BF16)
        run_scr[d] = st * edge_ch + jnp.concatenate(new_st, axis=1)

    def chunk_out(n):
        rows = pl.ds(pl.multiple_of(n * l, l), l)
        dtv_t = dec_scr[0, n]
        acum_t = dec_scr[1, n]
        acum = acum_t.T
        x = xs_ref[0, rows, :]
        cm = cm_ref[0, rows, :]
        cmf = cm.astype(F32)
        cb = _dot_t(cm, bm_ref[0, rows, :])
        st_f = st_scr[0, n]
        st_b = st_scr[1, n]
        rhs = [jnp.concatenate([x[:, LANES * jj:LANES * (jj + 1)],
                                st_f[:, LANES * jj:LANES * (jj + 1)],
                                st_b[:, LANES * jj:LANES * (jj + 1)]], axis=0)
               for jj in range(SSD_JH // 2)]
        for r0 in range(0, l, SSD_ROW_SLAB):
            rs = slice(r0, r0 + SSD_ROW_SLAB)
            ys = []
            for jj in range(SSD_JH // 2):
                y2 = []
                for hh in range(2):
                    cf = 2 * jj + hh
                    cr = SSD_JH + cf
                    col_f = jnp.broadcast_to(acum[rs, cf:cf + 1], (SSD_ROW_SLAB, l))
                    col_r = jnp.broadcast_to(acum[rs, cr:cr + 1], (SSD_ROW_SLAB, l))
                    seg = jnp.where(lower[rs], col_f - acum_t[cf:cf + 1, :],
                                    col_r - acum_t[cr:cr + 1, :])
                    wgt = (jnp.where(lower[rs], dtv_t[cf:cf + 1, :], dtv_t[cr:cr + 1, :])
                           + jnp.where(diag[rs], dtv_t[cf:cf + 1, :], 0.0))
                    m = cb[rs] * jnp.exp(jnp.minimum(seg, 0.0)) * wgt
                    lhs = jnp.concatenate([m.astype(BF16),
                                           (cmf[rs] * jnp.exp(col_f)).astype(BF16),
                                           (cmf[rs] * jnp.exp(col_r)).astype(BF16)], axis=1)
                    y2.append(jnp.dot(lhs, rhs[jj], preferred_element_type=F32))
                ys.append(jnp.where(first_head, y2[0], y2[1]))
            y = jnp.concatenate(ys, axis=1) + x[rs].astype(F32) * dsk_ref[...]
            o_ref[0, pl.ds(pl.multiple_of(n * l, l) + r0, SSD_ROW_SLAB), :] = y.astype(o_ref.dtype)

    def prep(i, carry):
        decays(i)
        return carry

    lax.fori_loop(0, nc, prep, 0, unroll=2)
    run_scr[...] = jnp.zeros_like(run_scr)

    def states(i, carry):
        chunk_state(i, 0)
        chunk_state(nc - 1 - i, 1)
        return carry

    lax.fori_loop(0, nc, states, 0)

    def outputs(n, carry):
        chunk_out(n)
        return carry

    lax.fori_loop(0, nc, outputs, 0)


def _ssd(xbc_act, dt3, pb, al, dskip, expand):
    bn, seq, _ = xbc_act.shape
    gw = SSD_JH * SSD_HEADDIM
    boff = SSD_INNER // SSD_STATE
    coff = boff + SSD_GROUPS
    return pl.pallas_call(
        functools.partial(_ssd_body, seq=seq),
        grid=(bn, SSD_GROUPS),
        in_specs=[
            pl.BlockSpec((1, seq, gw), lambda b, g: (b, 0, g)),
            pl.BlockSpec((1, seq, SSD_STATE), lambda b, g: (b, 0, boff + g)),
            pl.BlockSpec((1, seq, SSD_STATE), lambda b, g: (b, 0, coff + g)),
            pl.BlockSpec((1, seq, LANES), lambda b, g: (b, 0, g)),
            pl.BlockSpec((1, 2 * SSD_JH, 1), lambda b, g: (g, 0, 0)),
            pl.BlockSpec((1, 2 * SSD_JH, 1), lambda b, g: (g, 0, 0)),
            pl.BlockSpec((1, gw), lambda b, g: (0, g)),
            pl.BlockSpec((2, 2 * SSD_JH, gw), lambda b, g: (0, 0, 0)),
        ],
        out_specs=pl.BlockSpec((1, seq, gw), lambda b, g: (b, 0, g)),
        out_shape=jax.ShapeDtypeStruct((bn, seq, SSD_INNER), BF16),
        scratch_shapes=[pltpu.VMEM((2, seq // SSD_CHUNK, SSD_STATE, gw), BF16),
                        pltpu.VMEM((2, SSD_STATE, gw), F32),
                        pltpu.VMEM((4, seq // SSD_CHUNK, 2 * SSD_JH, SSD_CHUNK), F32)],
        compiler_params=_cparams(("parallel", "parallel")),
        name="ssd_scan",
    )(xbc_act, xbc_act, xbc_act, dt3, pb, al, dskip, expand)


def _rope_tables(seq):
    half = SWA_DH // 2
    inv = ROPE_THETA ** (-np.arange(half, dtype=np.float64) / half)
    ang = np.arange(seq, dtype=np.float64)[:, None] * inv[None, :]
    cos, sin = np.cos(ang), np.sin(ang)
    return (jnp.asarray(np.concatenate([cos, cos], axis=1), F32),
            jnp.asarray(np.concatenate([-sin, sin], axis=1), F32))


def _rope(x, cos, sin_signed):
    return x * cos + pltpu.roll(x, SWA_DH // 2, 1) * sin_signed


def _swa_body(sink_ref, q_ref, k_ref, v_ref, cos_ref, sin_ref, o_ref, kr_scr, *, seq):
    wb = SWA_BLOCK
    nb = seq // wb
    nkeys = 3 * wb
    kvh = pl.program_id(1)
    rb = min(512, seq)

    def krope(i, carry):
        rows = pl.ds(pl.multiple_of(i * rb, rb), rb)
        kr_scr[rows, :] = _rope(k_ref[0, rows, :].astype(F32), cos_ref[rows, :],
                                sin_ref[rows, :]).astype(BF16)
        return carry

    lax.fori_loop(0, seq // rb, krope, 0)

    qpos_l = lax.broadcasted_iota(jnp.int32, (wb, nkeys), 0)
    kpos_l = lax.broadcasted_iota(jnp.int32, (wb, nkeys), 1)

    def blk(n, carry):
        rows = pl.ds(pl.multiple_of(n * wb, wb), wb)
        ks = pl.multiple_of(jnp.clip(n - 1, 0, nb - 3) * wb, wb)
        krows = pl.ds(ks, nkeys)
        cos = cos_ref[rows, :]
        sin = sin_ref[rows, :]
        kk = kr_scr[krows, :]
        vv = v_ref[0, krows, :]
        valid = jnp.abs((qpos_l + n * wb) - (kpos_l + ks)) <= SWA_WINDOW
        qb = q_ref[0, rows, :].astype(F32)
        scores = []
        for hh in range(SWA_GQ):
            qh = (_rope(qb[:, SWA_DH * hh:SWA_DH * (hh + 1)], cos, sin) * (SWA_DH ** -0.5)).astype(BF16)
            scores.append(_dot_t(qh, kk))
        probs = []
        for hh, s in enumerate(scores):
            sink = sink_ref[kvh * SWA_GQ + hh]
            s = jnp.where(valid, s, NEG_BIG)
            m = jnp.maximum(jnp.max(s, axis=-1, keepdims=True), sink)
            p = jnp.exp(s - m)
            probs.append((p.astype(BF16), jnp.sum(p, axis=-1, keepdims=True) + jnp.exp(sink - m)))
        for hh, (p, den) in enumerate(probs):
            o = jnp.dot(p, vv, preferred_element_type=F32) / den
            o_ref[0, rows, SWA_DH * hh:SWA_DH * (hh + 1)] = o.astype(o_ref.dtype)
        return carry

    lax.fori_loop(0, nb, blk, 0)


def _swa(proj3, sink, cos, sin, col0):
    bn, seq, _ = proj3.shape
    qw = SWA_GQ * SWA_DH
    qoff = col0 // qw
    koff = (col0 + SWA_HEADS * SWA_DH) // SWA_DH
    voff = koff + SWA_KV
    return pl.pallas_call(
        functools.partial(_swa_body, seq=seq),
        grid=(bn, SWA_KV),
        in_specs=[
            pl.BlockSpec(memory_space=pltpu.SMEM),
            pl.BlockSpec((1, seq, qw), lambda b, h: (b, 0, qoff + h)),
            pl.BlockSpec((1, seq, SWA_DH), lambda b, h: (b, 0, koff + h)),
            pl.BlockSpec((1, seq, SWA_DH), lambda b, h: (b, 0, voff + h)),
            pl.BlockSpec((seq, SWA_DH), lambda b, h: (0, 0)),
            pl.BlockSpec((seq, SWA_DH), lambda b, h: (0, 0)),
        ],
        out_specs=pl.BlockSpec((1, seq, qw), lambda b, h: (b, 0, h)),
        out_shape=jax.ShapeDtypeStruct((bn, seq, SWA_HEADS * SWA_DH), BF16),
        scratch_shapes=[pltpu.VMEM((seq, SWA_DH), BF16)],
        compiler_params=_cparams(("parallel", "parallel")),
        name="swa",
    )(sink, proj3, proj3, proj3, cos, sin)


def _pad_cols(w, width):
    return jnp.pad(w, ((0, 0), (0, width - w.shape[1])))


def _even_mixers(x2, bn, seq, g0, w_in, w_decay, b_decay, gla_norm, rpb):
    d = x2.shape[1]
    n_gla = 2 * GLA_HEADS * GLA_DK + 2 * GLA_HEADS * GLA_DV
    n_lr = 2 * GLA_LOWRANK
    w_main = jnp.concatenate([w_in[:, :n_gla], w_in[:, n_gla + n_lr:]], axis=1).astype(BF16)
    w_small = _pad_cols(w_in[:, n_gla:n_gla + n_lr], LANES).astype(BF16)
    proj, small = _inproj(x2, g0, w_main, w_small)
    proj3 = proj.reshape(bn, seq, -1)
    small3 = small.reshape(bn, seq, -1)
    wdec = jnp.zeros((2, LANES, w_decay.shape[-1]), F32)
    for dd in range(2):
        wdec = wdec.at[dd, dd * GLA_LOWRANK:(dd + 1) * GLA_LOWRANK].set(w_decay[dd])
    o_a = _gla(proj3, small3, wdec, b_decay, gla_norm[None, :])
    o_b = _na(proj3, _na_bias_table(rpb), n_gla)
    return o_a.reshape(-1, o_a.shape[-1]), o_b.reshape(-1, o_b.shape[-1])


def _per_group(v):
    return v.reshape(2, SSD_GROUPS, SSD_JH).transpose(1, 0, 2).reshape(SSD_GROUPS, 2 * SSD_JH)


def _odd_mixers(x2, bn, seq, g0, w_in, conv_w, conv_b, dt_bias, a_log, d_skip, sink, cos, sin):
    conv_dim = SSD_INNER + 2 * SSD_GROUPS * SSD_STATE
    n_z = SSD_INNER
    n_dt = 2 * SSD_HEADS
    c_dt = n_z + conv_dim
    w_main = jnp.concatenate([w_in[:, :c_dt], w_in[:, c_dt + n_dt:]], axis=1).astype(BF16)
    w_dt = w_in[:, c_dt:c_dt + n_dt].reshape(-1, 2, SSD_GROUPS, SSD_JH).transpose(0, 2, 1, 3)
    w_dt = w_dt.reshape(-1, SSD_GROUPS, 2 * SSD_JH)
    w_small = jnp.pad(w_dt, ((0, 0), (0, 0), (0, LANES - 2 * SSD_JH))).reshape(-1, SSD_GROUPS * LANES)
    proj, small = _inproj(x2, g0, w_main, w_small.astype(BF16))
    proj3 = proj.reshape(bn, seq, -1)
    dt3 = small.reshape(bn, seq, SSD_GROUPS * LANES)
    xbc_act = _conv(proj3, conv_w, conv_b[None, :], n_z)
    pbg = _per_group(dt_bias)
    alg = _per_group(a_log)
    head_of_ch = np.arange(SSD_JH * SSD_HEADDIM) // SSD_HEADDIM
    expand = jnp.asarray(np.arange(2 * SSD_JH)[None, :, None]
                         == (np.arange(2)[:, None, None] * SSD_JH + head_of_ch[None, None, :]), F32)
    dskip = jnp.repeat(d_skip.astype(F32), SSD_HEADDIM)[None, :]
    y_pre = _ssd(xbc_act, dt3, pbg[:, :, None], alg[:, :, None], dskip, expand)
    o_d = _swa(proj3, sink, cos, sin, c_dt)
    return y_pre.reshape(-1, SSD_INNER), proj, o_d.reshape(-1, o_d.shape[-1])


def kernel(x, norm_gains, ffn_w_gate, ffn_w_up, ffn_w_down, even_w_in, even_w_out, gla_w_decay,
           gla_b_decay, gla_norm, na_rpb, odd_w_in, odd_w_out, ssd_conv_w, ssd_conv_b, ssd_dt_bias,
           ssd_a_log, ssd_d, ssd_norm, swa_sink):
    bn, seq, d = x.shape
    depth = norm_gains.shape[0]
    x2 = x.reshape(bn * seq, d)
    cos, sin = _rope_tables(seq)
    wg = ffn_w_gate.astype(BF16)
    wu = ffn_w_up.astype(BF16)
    wd = ffn_w_down.astype(BF16)
    even_w_out_b = even_w_out.astype(BF16)
    odd_w_out_b = odd_w_out.astype(BF16)
    for layer in range(depth):
        g = norm_gains[layer][:, None, :]
        i = layer // 2
        if layer % 2 == 0:
            o_a, o_b = _even_mixers(x2, bn, seq, g[0], even_w_in[i], gla_w_decay[i], gla_b_decay[i],
                                    gla_norm[i], na_rpb[i])
            x2 = _outproj(x2, o_a, o_b, even_w_out_b, i, g[1])
        else:
            y_pre, proj, o_d = _odd_mixers(x2, bn, seq, g[0], odd_w_in[i], ssd_conv_w[i], ssd_conv_b[i],
                                           ssd_dt_bias[i], ssd_a_log[i], ssd_d[i], swa_sink[i], cos, sin)
            x2 = _outproj(x2, y_pre, o_d, odd_w_out_b, i, g[1],
                          z_src=proj, ssd_gain=ssd_norm[i][None, :])
        x2 = _ffn(x2, g[2], wg, wu, wd, g[3], layer)
    return x2.reshape(bn, seq, d)
```

```python
import functools
import math

import jax
import jax.numpy as jnp
import numpy as np
from jax import lax
from jax.experimental import pallas as pl
from jax.experimental.pallas import tpu as pltpu

F32 = jnp.float32
BF16 = jnp.bfloat16
HIGHEST = lax.Precision.HIGHEST

EPS = 1e-6
ROPE_THETA = 10000.0
GRID_W = 64
LANES = 128

GLA_DK = 64
GLA_DV = 128
GLA_HEADS = 8
GLA_LOWRANK = 16
GLA_TAU = 16.0
GLA_CHUNK = 64
GLA_SUB = 16
GLA_EXP_CLAMP = 60.0
GLA_TRI_BLOCK = 256

NA_DH = 128
NA_HEADS = 8
NA_KH = 8
NA_KW = 16
NA_ROWS_PER_ITER = 8
NEG_BIG = -1e30

SSD_INNER = 1024
SSD_HEADDIM = 64
SSD_HEADS = 16
SSD_GROUPS = 2
SSD_STATE = 128
SSD_CONV = 5
SSD_CHUNK = 128
SSD_JH = SSD_HEADS // SSD_GROUPS
SSD_ROW_SLAB = 64

SWA_DH = 128
SWA_HEADS = 8
SWA_KV = 2
SWA_GQ = SWA_HEADS // SWA_KV
SWA_WINDOW = 128
SWA_BLOCK = 128

VMEM_LIMIT = 52 * 1024 * 1024


FFN_NORM_ROWS = 256
FFN_VMEM_LIMIT = 58 * 1024 * 1024


def _cparams(sem, vmem_limit=VMEM_LIMIT):
    return pltpu.CompilerParams(dimension_semantics=sem, vmem_limit_bytes=vmem_limit)


def _rms(x, g):
    return x * lax.rsqrt(jnp.mean(x * x, axis=-1, keepdims=True) + EPS) * g


def _silu(x):
    return x * (1.0 / (1.0 + jnp.exp(-x)))


def _dot_t(a, b):
    return lax.dot_general(a, b, (((1,), (1,)), ((), ())), preferred_element_type=F32)


def _inproj_body(x_ref, g_ref, w_ref, ws_ref, o_ref, os_ref, h_scr):
    @pl.when(pl.program_id(1) == 0)
    def _():
        h = _rms(x_ref[...], g_ref[...]).astype(BF16)
        h_scr[...] = h
        os_ref[...] = jnp.dot(h, ws_ref[...], preferred_element_type=F32)

    o_ref[...] = jnp.dot(h_scr[...], w_ref[...], preferred_element_type=F32).astype(o_ref.dtype)


def _inproj(x2, g, w_main, w_small, tm=1024, tn=2048):
    n, d = x2.shape
    nm = w_main.shape[1]
    ns = w_small.shape[1]
    return pl.pallas_call(
        _inproj_body,
        grid=(n // tm, nm // tn),
        in_specs=[
            pl.BlockSpec((tm, d), lambda i, j: (i, 0)),
            pl.BlockSpec((1, d), lambda i, j: (0, 0)),
            pl.BlockSpec((d, tn), lambda i, j: (0, j)),
            pl.BlockSpec((d, ns), lambda i, j: (0, 0)),
        ],
        out_specs=[
            pl.BlockSpec((tm, tn), lambda i, j: (i, j)),
            pl.BlockSpec((tm, ns), lambda i, j: (i, 0)),
        ],
        out_shape=[jax.ShapeDtypeStruct((n, nm), BF16), jax.ShapeDtypeStruct((n, ns), F32)],
        scratch_shapes=[pltpu.VMEM((tm, d), BF16)],
        compiler_params=_cparams(("parallel", "arbitrary")),
        name="inproj",
    )(x2, g, w_main, w_small)


def _outproj_even_body(a_ref, b_ref, wa_ref, wb_ref, g_ref, x_ref, o_ref):
    mix = (jnp.dot(a_ref[...], wa_ref[...], preferred_element_type=F32)
           + jnp.dot(b_ref[...], wb_ref[...], preferred_element_type=F32))
    o_ref[...] = x_ref[...] + _rms(mix, g_ref[...])


def _outproj_odd_body(y_ref, z_ref, ng_ref, b_ref, wa_ref, wb_ref, g_ref, x_ref, o_ref):
    y = y_ref[...].astype(F32) * _silu(z_ref[...].astype(F32))
    a = _rms(y, ng_ref[...]).astype(BF16)
    mix = (jnp.dot(a, wa_ref[...], preferred_element_type=F32)
           + jnp.dot(b_ref[...], wb_ref[...], preferred_element_type=F32))
    o_ref[...] = x_ref[...] + _rms(mix, g_ref[...])


def _outproj(x2, a, b, w_out, li, g, z_src=None, ssd_gain=None, tm=512):
    n, d = x2.shape
    half = a.shape[1]
    row = lambda i: (i, 0)
    fixed = lambda i: (0, 0)
    w_specs = [pl.BlockSpec((None, half, d), lambda i: (li, 0, 0)),
               pl.BlockSpec((None, half, d), lambda i: (li, 1, 0))]
    tail_specs = [pl.BlockSpec((1, d), fixed), pl.BlockSpec((tm, d), row)]
    if z_src is None:
        body = _outproj_even_body
        in_specs = [pl.BlockSpec((tm, half), row), pl.BlockSpec((tm, half), row)] + w_specs + tail_specs
        args = (a, b, w_out, w_out, g, x2)
    else:
        body = _outproj_odd_body
        in_specs = ([pl.BlockSpec((tm, half), row), pl.BlockSpec((tm, half), row),
                     pl.BlockSpec((1, half), fixed), pl.BlockSpec((tm, half), row)]
                    + w_specs + tail_specs)
        args = (a, z_src, ssd_gain, b, w_out, w_out, g, x2)
    return pl.pallas_call(
        body,
        grid=(n // tm,),
        in_specs=in_specs,
        out_specs=pl.BlockSpec((tm, d), row),
        out_shape=jax.ShapeDtypeStruct((n, d), F32),
        compiler_params=_cparams(("parallel",)),
        name="outproj",
    )(*args)


def _ffn_body(x_ref, g2_ref, wg_ref, wu_ref, wd_ref, g3_ref, o_ref, h_scr):
    k = pl.program_id(1)
    tm = x_ref.shape[0]
    norm_rows = [pl.ds(r0, FFN_NORM_ROWS) for r0 in range(0, tm, FFN_NORM_ROWS)]

    @pl.when(k == 0)
    def _():
        for rs in norm_rows:
            h_scr[rs, :] = _rms(x_ref[rs, :], g2_ref[...]).astype(BF16)
        o_ref[...] = jnp.zeros_like(o_ref)

    for rs in (pl.ds(0, tm // 2), pl.ds(tm // 2, tm // 2)):
        h = h_scr[rs, :]
        gate = jnp.dot(h, wg_ref[...], preferred_element_type=F32)
        up = jnp.dot(h, wu_ref[...], preferred_element_type=F32)
        act = (_silu(gate) * up).astype(BF16)
        o_ref[rs, :] += jnp.dot(act, wd_ref[...], preferred_element_type=F32)

    @pl.when(k == pl.num_programs(1) - 1)
    def _():
        for rs in norm_rows:
            o_ref[rs, :] = x_ref[rs, :] + _rms(o_ref[rs, :], g3_ref[...])


def _ffn(x2, g2, wg, wu, wd, g3, layer, tm=1024, th=512):
    n, d = x2.shape
    hid = wd.shape[1]
    return pl.pallas_call(
        _ffn_body,
        grid=(n // tm, hid // th),
        in_specs=[
            pl.BlockSpec((tm, d), lambda i, k: (i, 0)),
            pl.BlockSpec((1, d), lambda i, k: (0, 0)),
            pl.BlockSpec((None, d, th), lambda i, k: (layer, 0, k)),
            pl.BlockSpec((None, d, th), lambda i, k: (layer, 0, k)),
            pl.BlockSpec((None, th, d), lambda i, k: (layer, k, 0)),
            pl.BlockSpec((1, d), lambda i, k: (0, 0)),
        ],
        out_specs=pl.BlockSpec((tm, d), lambda i, k: (i, 0)),
        out_shape=jax.ShapeDtypeStruct((n, d), F32),
        scratch_shapes=[pltpu.VMEM((tm, d), BF16)],
        compiler_params=_cparams(("parallel", "arbitrary"), FFN_VMEM_LIMIT),
        name="ffn",
    )(x2, g2, wg, wu, wd, g3)


def _gla_body(q_ref, k_ref, v_ref, g_ref, lr_ref, wd_ref, bd_ref, gn_ref, o_ref,
              bc_scr, o_scr, st_scr, *, seq):
    c = GLA_CHUNK
    nc = seq // c
    half = nc // 2
    nsub = c // GLA_SUB
    rb = min(512, seq)
    row_i = lax.broadcasted_iota(jnp.int32, (c, c), 0)
    col_i = lax.broadcasted_iota(jnp.int32, (c, c), 1)
    tri = (row_i >= col_i, row_i <= col_i)
    lane = lax.broadcasted_iota(jnp.int32, (1, LANES), 1)
    hmask = (lane < GLA_DK, lane >= GLA_DK)

    w2 = jnp.concatenate([wd_ref[0], wd_ref[1]], axis=1)
    w_hi = w2.astype(BF16)
    w_lo = (w2 - w_hi.astype(F32)).astype(BF16)
    w_cat = jnp.concatenate([w_hi, w_lo, w_hi], axis=0)
    tb = GLA_TRI_BLOCK
    brow = lax.broadcasted_iota(jnp.int32, (tb, tb), 0)
    bcol = lax.broadcasted_iota(jnp.int32, (tb, tb), 1)
    same_chunk = (brow // c) == (bcol // c)
    blk_tri = ((same_chunk & (brow >= bcol)).astype(BF16), (same_chunk & (brow <= bcol)).astype(BF16))

    def pre(i, carry):
        r0 = pl.multiple_of(i * rb, rb)
        lr = lr_ref[0, pl.ds(r0, rb), :]
        lr_hi = lr.astype(BF16)
        lr_lo = (lr - lr_hi.astype(F32)).astype(BF16)
        z2 = jnp.dot(jnp.concatenate([lr_hi, lr_hi, lr_lo], axis=1), w_cat, preferred_element_type=F32)
        las = []
        for d in range(2):
            z = z2[:, LANES * d:LANES * (d + 1)] + bd_ref[d:d + 1, :]
            la = (jnp.minimum(z, 0.0) - jnp.log1p(jnp.exp(-jnp.abs(z)))) * (1.0 / GLA_TAU)
            la_hi = la.astype(BF16)
            las.append(jnp.concatenate([la_hi, (la - la_hi.astype(F32)).astype(BF16)], axis=1))
        for d in range(2):
            for t in range(rb // tb):
                s2 = jnp.dot(blk_tri[d], las[d][t * tb:(t + 1) * tb], preferred_element_type=F32)
                bc_scr[d, pl.ds(r0 + t * tb, tb), :] = s2[:, :LANES] + s2[:, LANES:]
        return carry

    lax.fori_loop(0, seq // rb, pre, 0)

    def stage1(n, d):
        rows = pl.ds(pl.multiple_of(n * c, c), c)
        bc = bc_scr[d, rows, :]
        q = q_ref[0, rows, :].astype(F32) * (GLA_DK ** -0.5)
        k = k_ref[0, rows, :].astype(F32)
        v = v_ref[0, rows, :]
        b_edge = bc[c - 1:c] if d == 0 else bc[0:1]
        qhat = q * jnp.exp(bc)
        khat = (k * jnp.exp(b_edge - bc)).astype(BF16)
        att_rows = ([], [])
        for i in range(nsub):
            lo = GLA_SUB * i
            hi = lo + GLA_SUB
            if d == 0:
                ref = bc[lo - 1:lo] if i > 0 else jnp.zeros((1, LANES), F32)
            else:
                ref = bc[hi:hi + 1] if i < nsub - 1 else jnp.zeros((1, LANES), F32)
            qi = q[lo:hi] * jnp.exp(bc[lo:hi] - ref)
            ki = (k * jnp.exp(jnp.minimum(ref - bc, GLA_EXP_CLAMP))).astype(BF16)
            for hh in range(2):
                qim = jnp.where(hmask[hh], qi, 0.0).astype(BF16)
                att_rows[hh].append(_dot_t(qim, ki))
        heads = []
        for hh in range(2):
            att = jnp.concatenate(att_rows[hh], axis=0)
            att = jnp.where(tri[d], att, 0.0).astype(BF16)
            vh = v[:, GLA_DV * hh:GLA_DV * (hh + 1)]
            kv = lax.dot_general(vh, khat, (((0,), (0,)), ((), ())), preferred_element_type=F32)
            qh = jnp.where(hmask[hh], qhat, 0.0).astype(BF16)
            heads.append((att, vh, kv, qh))
        return d, heads, jnp.exp(b_edge)

    def stage2(s1, states):
        d, heads, decay = s1
        outs = []
        for hh, (att, vh, kv, qh) in enumerate(heads):
            st = states[d][hh]
            outs.append(jnp.dot(att, vh, preferred_element_type=F32) + _dot_t(qh, st.astype(BF16)))
            states[d][hh] = st * decay + kv
        return jnp.concatenate(outs, axis=1)

    def finish(n, o):
        rows = pl.ds(pl.multiple_of(n * c, c), c)
        parts = []
        for hh in range(2):
            oh = o[:, GLA_DV * hh:GLA_DV * (hh + 1)]
            parts.append(oh * lax.rsqrt(jnp.mean(oh * oh, axis=-1, keepdims=True) + EPS))
        o = jnp.concatenate(parts, axis=1) * gn_ref[...]
        o_ref[0, rows, :] = (o * _silu(g_ref[0, rows, :].astype(F32))).astype(o_ref.dtype)

    st_scr[...] = jnp.zeros_like(st_scr)

    cpi = next(u for u in (4, 2, 1) if half % u == 0)

    def walk(i, base_f, base_b):
        work = []
        for u in range(cpi):
            work.append((base_f + i * cpi + u, 0))
            work.append((base_b - i * cpi - u, 1))
        prepared = [stage1(n, d) for n, d in work]
        states = [[st_scr[d, hh] for hh in range(2)] for d in range(2)]
        outs = [(n, stage2(s1, states)) for (n, _), s1 in zip(work, prepared)]
        for d in range(2):
            for hh in range(2):
                st_scr[d, hh] = states[d][hh]
        return outs

    def first(i, carry):
        for n, o in walk(i, 0, nc - 1):
            o_scr[pl.ds(pl.multiple_of(n * c, c), c), :] = o
        return carry

    lax.fori_loop(0, half // cpi, first, 0)

    def second(i, carry):
        for n, o in walk(i, half, half - 1):
            finish(n, o_scr[pl.ds(pl.multiple_of(n * c, c), c), :] + o)
        return carry

    lax.fori_loop(0, half // cpi, second, 0)


def _gla(proj3, small3, wdec_pad, bdec, gnorm):
    bn, seq, _ = proj3.shape
    npair = GLA_HEADS // 2
    qoff = 0
    koff = (GLA_HEADS * GLA_DK) // LANES
    voff = (2 * GLA_HEADS * GLA_DK) // (2 * GLA_DV)
    goff = voff + npair
    return pl.pallas_call(
        functools.partial(_gla_body, seq=seq),
        grid=(bn, npair),
        in_specs=[
            pl.BlockSpec((1, seq, LANES), lambda b, p: (b, 0, qoff + p)),
            pl.BlockSpec((1, seq, LANES), lambda b, p: (b, 0, koff + p)),
            pl.BlockSpec((1, seq, 2 * GLA_DV), lambda b, p: (b, 0, voff + p)),
            pl.BlockSpec((1, seq, 2 * GLA_DV), lambda b, p: (b, 0, goff + p)),
            pl.BlockSpec((1, seq, LANES), lambda b, p: (b, 0, 0)),
            pl.BlockSpec((2, LANES, LANES), lambda b, p: (0, 0, p)),
            pl.BlockSpec((2, LANES), lambda b, p: (0, p)),
            pl.BlockSpec((1, 2 * GLA_DV), lambda b, p: (0, p)),
        ],
        out_specs=pl.BlockSpec((1, seq, 2 * GLA_DV), lambda b, p: (b, 0, p)),
        out_shape=jax.ShapeDtypeStruct((bn, seq, GLA_HEADS * GLA_DV), BF16),
        scratch_shapes=[
            pltpu.VMEM((2, seq, LANES), F32),
            pltpu.VMEM((seq, 2 * GLA_DV), F32),
            pltpu.VMEM((2, 2, GLA_DV, LANES), F32),
        ],
        compiler_params=_cparams(("parallel", "parallel")),
        name="gla",
    )(proj3, proj3, proj3, proj3, small3, wdec_pad, bdec, gnorm)


def _na_bias_table(rpb):
    nh, ndr, ndc = rpb.shape
    cols = np.arange(GRID_W)
    cs = np.clip(cols - NA_KW // 2, 0, GRID_W - NA_KW)
    valid = (cols[None, :] >= cs[:, None]) & (cols[None, :] < cs[:, None] + NA_KW)
    dc = cols[None, :] - cols[:, None] + NA_KW - 1
    onehot = (np.arange(ndc)[:, None, None] == dc[None]) & valid[None]
    toep = jnp.dot(rpb.reshape(nh * ndr, ndc).astype(F32),
                   jnp.asarray(onehot.reshape(ndc, -1), F32), precision=HIGHEST)
    toep = jnp.where(jnp.asarray(valid.reshape(1, -1)), toep, NEG_BIG)
    toep = toep.reshape(nh, ndr, GRID_W, GRID_W)
    t = jnp.stack([toep[:, off:off + NA_KH] for off in range(NA_KH)], axis=1)
    return t.transpose(0, 1, 3, 2, 4).reshape(nh, NA_KH, GRID_W, NA_KH * GRID_W)


def _na_body(q_ref, k_ref, v_ref, bias_ref, o_ref, *, nrows):
    w = GRID_W
    nk = NA_KH * w

    def group(i, carry):
        idx = []
        scores = []
        for u in range(NA_ROWS_PER_ITER):
            r = i * NA_ROWS_PER_ITER + u
            rs = jnp.clip(r - NA_KH // 2, 0, nrows - NA_KH)
            qrows = pl.ds(pl.multiple_of(r * w, w), w)
            krows = pl.ds(pl.multiple_of(rs * w, w), nk)
            idx.append((qrows, krows, rs - r + (NA_KH - 1)))
            scores.append(_dot_t(q_ref[0, qrows, :], k_ref[0, krows, :]))
        probs = []
        for (qrows, krows, off), s in zip(idx, scores):
            s = s * (NA_DH ** -0.5) + bias_ref[0, off]
            p = jnp.exp(s - jnp.max(s, axis=-1, keepdims=True))
            probs.append((p.astype(BF16), jnp.sum(p, axis=-1, keepdims=True)))
        for (qrows, krows, off), (p, l) in zip(idx, probs):
            o = jnp.dot(p, v_ref[0, krows, :], preferred_element_type=F32) / l
            o_ref[0, qrows, :] = o.astype(o_ref.dtype)
        return carry

    lax.fori_loop(0, nrows // NA_ROWS_PER_ITER, group, 0)


def _na(proj3, bias_tab, col0):
    bn, seq, _ = proj3.shape
    nrows = seq // GRID_W
    qoff = col0 // NA_DH
    koff = qoff + NA_HEADS
    voff = koff + NA_HEADS
    return pl.pallas_call(
        functools.partial(_na_body, nrows=nrows),
        grid=(bn, NA_HEADS),
        in_specs=[
            pl.BlockSpec((1, seq, NA_DH), lambda b, h: (b, 0, qoff + h)),
            pl.BlockSpec((1, seq, NA_DH), lambda b, h: (b, 0, koff + h)),
            pl.BlockSpec((1, seq, NA_DH), lambda b, h: (b, 0, voff + h)),
            pl.BlockSpec((1, NA_KH, GRID_W, NA_KH * GRID_W), lambda b, h: (h, 0, 0, 0)),
        ],
        out_specs=pl.BlockSpec((1, seq, NA_DH), lambda b, h: (b, 0, h)),
        out_shape=jax.ShapeDtypeStruct((bn, seq, NA_HEADS * NA_DH), BF16),
        compiler_params=_cparams(("parallel", "parallel")),
        name="na",
    )(proj3, proj3, proj3, bias_tab)


CONV_PAD = 16


def _conv_body(x_ref, w_ref, b_ref, o_ref, xp_scr, *, seq):
    rb = min(256, seq)
    ch = x_ref.shape[-1]
    zeros = jnp.zeros((CONV_PAD, ch), xp_scr.dtype)
    xp_scr[0:CONV_PAD, :] = zeros
    xp_scr[seq + CONV_PAD:seq + 2 * CONV_PAD, :] = zeros
    xp_scr[CONV_PAD:seq + CONV_PAD, :] = x_ref[0]

    def blk(i, carry):
        r0 = pl.multiple_of(i * rb, rb)
        xw = xp_scr[pl.ds(r0, rb + 2 * CONV_PAD), :].astype(F32)
        acc = jnp.zeros((rb, ch), F32) + b_ref[...]
        for j in range(SSD_CONV):
            s0 = CONV_PAD - SSD_CONV // 2 + j
            acc = acc + xw[s0:s0 + rb] * w_ref[j:j + 1, :]
        o_ref[0, pl.ds(r0, rb), :] = _silu(acc).astype(o_ref.dtype)
        return carry

    lax.fori_loop(0, seq // rb, blk, 0)


def _conv(proj3, conv_w, conv_b, col0, tc=512):
    bn, seq, _ = proj3.shape
    cdim = conv_w.shape[1]
    c0 = col0 // tc
    return pl.pallas_call(
        functools.partial(_conv_body, seq=seq),
        grid=(bn, cdim // tc),
        in_specs=[
            pl.BlockSpec((1, seq, tc), lambda b, j: (b, 0, c0 + j)),
            pl.BlockSpec((SSD_CONV, tc), lambda b, j: (0, j)),
            pl.BlockSpec((1, tc), lambda b, j: (0, j)),
        ],
        out_specs=pl.BlockSpec((1, seq, tc), lambda b, j: (b, 0, j)),
        out_shape=jax.ShapeDtypeStruct((bn, seq, cdim), BF16),
        scratch_shapes=[pltpu.VMEM((seq + 2 * CONV_PAD, tc), BF16)],
        compiler_params=_cparams(("parallel", "parallel")),
        name="ssd_conv",
    )(proj3, conv_w, conv_b)


def _softplus(x):
    return jnp.maximum(x, 0.0) + jnp.log1p(jnp.exp(-jnp.abs(x)))


def _ssd_body(xs_ref, bm_ref, cm_ref, dt_ref, pb_ref, al_ref, dsk_ref, ex_ref, o_ref,
              st_scr, run_scr, dec_scr, *, seq):
    l = SSD_CHUNK
    nc = seq // l
    nh2 = 2 * SSD_JH
    row_i = lax.broadcasted_iota(jnp.int32, (l, l), 0)
    col_i = lax.broadcasted_iota(jnp.int32, (l, l), 1)
    lane = lax.broadcasted_iota(jnp.int32, (1, LANES), 1)
    first_head = lane < SSD_HEADDIM
    bwd_row = lax.broadcasted_iota(jnp.int32, (nh2, 1), 0) >= SSD_JH
    a_col = -jnp.exp(al_ref[0])

    lower = row_i > col_i
    diag = row_i == col_i
    triu_b = (row_i <= col_i).astype(BF16)

    def decays(n):
        rows = pl.ds(pl.multiple_of(n * l, l), l)
        dtv = _softplus(dt_ref[0, rows, :].T[0:nh2] + pb_ref[0])
        da = dtv * a_col
        hi = da.astype(BF16)
        r1 = da - hi.astype(F32)
        mid = r1.astype(BF16)
        lo = (r1 - mid.astype(F32)).astype(BF16)
        p3 = jnp.dot(jnp.concatenate([hi, mid, lo], axis=0), triu_b, preferred_element_type=F32)
        pre = p3[0:nh2] + p3[nh2:2 * nh2] + p3[2 * nh2:]
        total = pre[:, l - 1:l]
        acum = jnp.where(bwd_row, total - pre + da, pre)
        dec_scr[0, n] = dtv
        dec_scr[1, n] = acum
        dec_scr[2, n] = dtv * jnp.exp(total - acum)
        dec_scr[3, n] = jnp.broadcast_to(jnp.exp(total), (nh2, l))

    def chunk_state(n, d):
        rows = pl.ds(pl.multiple_of(n * l, l), l)
        w_state_t = dec_scr[2, n]
        x = xs_ref[0, rows, :]
        bmt = bm_ref[0, rows, :].astype(F32).T
        edge_ch = jnp.sum(dec_scr[3, n][:, 0:1] * ex_ref[d], axis=0, keepdims=True)
        new_st = []
        for jj in range(SSD_JH // 2):
            xp = x[:, LANES * jj:LANES * (jj + 1)]
            s2 = []
            for hh in range(2):
                cidx = d * SSD_JH + 2 * jj + hh
                bw = (bmt * w_state_t[cidx:cidx + 1, :]).astype(BF16)
                s2.append(jnp.dot(bw, xp, preferred_element_type=F32))
            new_st.append(jnp.where(first_head, s2[0], s2[1]))
        st = run_scr[d]
        st_scr[d, n] = st.astype(BF16)
        run_scr[d] = st * edge_ch + jnp.concatenate(new_st, axis=1)

    def chunk_out(n):
        rows = pl.ds(pl.multiple_of(n * l, l), l)
        dtv_t = dec_scr[0, n]
        acum_t = dec_scr[1, n]
        acum = acum_t.T
        x = xs_ref[0, rows, :]
        cm = cm_ref[0, rows, :]
        cmf = cm.astype(F32)
        cb = _dot_t(cm, bm_ref[0, rows, :])
        st_f = st_scr[0, n]
        st_b = st_scr[1, n]
        rhs = [jnp.concatenate([x[:, LANES * jj:LANES * (jj + 1)],
                                st_f[:, LANES * jj:LANES * (jj + 1)],
                                st_b[:, LANES * jj:LANES * (jj + 1)]], axis=0)
               for jj in range(SSD_JH // 2)]
        for r0 in range(0, l, SSD_ROW_SLAB):
            rs = slice(r0, r0 + SSD_ROW_SLAB)
            ys = []
            for jj in range(SSD_JH // 2):
                y2 = []
                for hh in range(2):
                    cf = 2 * jj + hh
                    cr = SSD_JH + cf
                    col_f = jnp.broadcast_to(acum[rs, cf:cf + 1], (SSD_ROW_SLAB, l))
                    col_r = jnp.broadcast_to(acum[rs, cr:cr + 1], (SSD_ROW_SLAB, l))
                    seg = jnp.where(lower[rs], col_f - acum_t[cf:cf + 1, :],
                                    col_r - acum_t[cr:cr + 1, :])
                    wgt = (jnp.where(lower[rs], dtv_t[cf:cf + 1, :], dtv_t[cr:cr + 1, :])
                           + jnp.where(diag[rs], dtv_t[cf:cf + 1, :], 0.0))
                    m = cb[rs] * jnp.exp(jnp.minimum(seg, 0.0)) * wgt
                    lhs = jnp.concatenate([m.astype(BF16),
                                           (cmf[rs] * jnp.exp(col_f)).astype(BF16),
                                           (cmf[rs] * jnp.exp(col_r)).astype(BF16)], axis=1)
                    y2.append(jnp.dot(lhs, rhs[jj], preferred_element_type=F32))
                ys.append(jnp.where(first_head, y2[0], y2[1]))
            y = jnp.concatenate(ys, axis=1) + x[rs].astype(F32) * dsk_ref[...]
            o_ref[0, pl.ds(pl.multiple_of(n * l, l) + r0, SSD_ROW_SLAB), :] = y.astype(o_ref.dtype)

    def prep(i, carry):
        decays(i)
        return carry

    lax.fori_loop(0, nc, prep, 0, unroll=2)
    run_scr[...] = jnp.zeros_like(run_scr)

    def states(i, carry):
        chunk_state(i, 0)
        chunk_state(nc - 1 - i, 1)
        return carry

    lax.fori_loop(0, nc, states, 0)

    def outputs(n, carry):
        chunk_out(n)
        return carry

    lax.fori_loop(0, nc, outputs, 0)


def _ssd(xbc_act, dt3, pb, al, dskip, expand):
    bn, seq, _ = xbc_act.shape
    gw = SSD_JH * SSD_HEADDIM
    boff = SSD_INNER // SSD_STATE
    coff = boff + SSD_GROUPS
    return pl.pallas_call(
        functools.partial(_ssd_body, seq=seq),
        grid=(bn, SSD_GROUPS),
        in_specs=[
            pl.BlockSpec((1, seq, gw), lambda b, g: (b, 0, g)),
            pl.BlockSpec((1, seq, SSD_STATE), lambda b, g: (b, 0, boff + g)),
            pl.BlockSpec((1, seq, SSD_STATE), lambda b, g: (b, 0, coff + g)),
            pl.BlockSpec((1, seq, LANES), lambda b, g: (b, 0, g)),
            pl.BlockSpec((1, 2 * SSD_JH, 1), lambda b, g: (g, 0, 0)),
            pl.BlockSpec((1, 2 * SSD_JH, 1), lambda b, g: (g, 0, 0)),
            pl.BlockSpec((1, gw), lambda b, g: (0, g)),
            pl.BlockSpec((2, 2 * SSD_JH, gw), lambda b, g: (0, 0, 0)),
        ],
        out_specs=pl.BlockSpec((1, seq, gw), lambda b, g: (b, 0, g)),
        out_shape=jax.ShapeDtypeStruct((bn, seq, SSD_INNER), BF16),
        scratch_shapes=[pltpu.VMEM((2, seq // SSD_CHUNK, SSD_STATE, gw), BF16),
                        pltpu.VMEM((2, SSD_STATE, gw), F32),
                        pltpu.VMEM((4, seq // SSD_CHUNK, 2 * SSD_JH, SSD_CHUNK), F32)],
        compiler_params=_cparams(("parallel", "parallel")),
        name="ssd_scan",
    )(xbc_act, xbc_act, xbc_act, dt3, pb, al, dskip, expand)


def _rope_tables(seq):
    half = SWA_DH // 2
    inv = ROPE_THETA ** (-np.arange(half, dtype=np.float64) / half)
    ang = np.arange(seq, dtype=np.float64)[:, None] * inv[None, :]
    cos, sin = np.cos(ang), np.sin(ang)
    return (jnp.asarray(np.concatenate([cos, cos], axis=1), F32),
            jnp.asarray(np.concatenate([-sin, sin], axis=1), F32))


def _rope(x, cos, sin_signed):
    return x * cos + pltpu.roll(x, SWA_DH // 2, 1) * sin_signed


def _swa_body(sink_ref, q_ref, k_ref, v_ref, cos_ref, sin_ref, o_ref, kr_scr, *, seq):
    wb = SWA_BLOCK
    nb = seq // wb
    nkeys = 3 * wb
    kvh = pl.program_id(1)
    rb = min(512, seq)

    def krope(i, carry):
        rows = pl.ds(pl.multiple_of(i * rb, rb), rb)
        kr_scr[rows, :] = _rope(k_ref[0, rows, :].astype(F32), cos_ref[rows, :],
                                sin_ref[rows, :]).astype(BF16)
        return carry

    lax.fori_loop(0, seq // rb, krope, 0)

    qpos_l = lax.broadcasted_iota(jnp.int32, (wb, nkeys), 0)
    kpos_l = lax.broadcasted_iota(jnp.int32, (wb, nkeys), 1)

    def blk(n, carry):
        rows = pl.ds(pl.multiple_of(n * wb, wb), wb)
        ks = pl.multiple_of(jnp.clip(n - 1, 0, nb - 3) * wb, wb)
        krows = pl.ds(ks, nkeys)
        cos = cos_ref[rows, :]
        sin = sin_ref[rows, :]
        kk = kr_scr[krows, :]
        vv = v_ref[0, krows, :]
        valid = jnp.abs((qpos_l + n * wb) - (kpos_l + ks)) <= SWA_WINDOW
        qb = q_ref[0, rows, :].astype(F32)
        scores = []
        for hh in range(SWA_GQ):
            qh = (_rope(qb[:, SWA_DH * hh:SWA_DH * (hh + 1)], cos, sin) * (SWA_DH ** -0.5)).astype(BF16)
            scores.append(_dot_t(qh, kk))
        probs = []
        for hh, s in enumerate(scores):
            sink = sink_ref[kvh * SWA_GQ + hh]
            s = jnp.where(valid, s, NEG_BIG)
            m = jnp.maximum(jnp.max(s, axis=-1, keepdims=True), sink)
            p = jnp.exp(s - m)
            probs.append((p.astype(BF16), jnp.sum(p, axis=-1, keepdims=True) + jnp.exp(sink - m)))
        for hh, (p, den) in enumerate(probs):
            o = jnp.dot(p, vv, preferred_element_type=F32) / den
            o_ref[0, rows, SWA_DH * hh:SWA_DH * (hh + 1)] = o.astype(o_ref.dtype)
        return carry

    lax.fori_loop(0, nb, blk, 0)


def _swa(proj3, sink, cos, sin, col0):
    bn, seq, _ = proj3.shape
    qw = SWA_GQ * SWA_DH
    qoff = col0 // qw
    koff = (col0 + SWA_HEADS * SWA_DH) // SWA_DH
    voff = koff + SWA_KV
    return pl.pallas_call(
        functools.partial(_swa_body, seq=seq),
        grid=(bn, SWA_KV),
        in_specs=[
            pl.BlockSpec(memory_space=pltpu.SMEM),
            pl.BlockSpec((1, seq, qw), lambda b, h: (b, 0, qoff + h)),
            pl.BlockSpec((1, seq, SWA_DH), lambda b, h: (b, 0, koff + h)),
            pl.BlockSpec((1, seq, SWA_DH), lambda b, h: (b, 0, voff + h)),
            pl.BlockSpec((seq, SWA_DH), lambda b, h: (0, 0)),
            pl.BlockSpec((seq, SWA_DH), lambda b, h: (0, 0)),
        ],
        out_specs=pl.BlockSpec((1, seq, qw), lambda b, h: (b, 0, h)),
        out_shape=jax.ShapeDtypeStruct((bn, seq, SWA_HEADS * SWA_DH), BF16),
        scratch_shapes=[pltpu.VMEM((seq, SWA_DH), BF16)],
        compiler_params=_cparams(("parallel", "parallel")),
        name="swa",
    )(sink, proj3, proj3, proj3, cos, sin)


def _pad_cols(w, width):
    return jnp.pad(w, ((0, 0), (0, width - w.shape[1])))


def _even_mixers(x2, bn, seq, g0, w_in, w_decay, b_decay, gla_norm, rpb):
    d = x2.shape[1]
    n_gla = 2 * GLA_HEADS * GLA_DK + 2 * GLA_HEADS * GLA_DV
    n_lr = 2 * GLA_LOWRANK
    w_main = jnp.concatenate([w_in[:, :n_gla], w_in[:, n_gla + n_lr:]], axis=1).astype(BF16)
    w_small = _pad_cols(w_in[:, n_gla:n_gla + n_lr], LANES).astype(BF16)
    proj, small = _inproj(x2, g0, w_main, w_small)
    proj3 = proj.reshape(bn, seq, -1)
    small3 = small.reshape(bn, seq, -1)
    wdec = jnp.zeros((2, LANES, w_decay.shape[-1]), F32)
    for dd in range(2):
        wdec = wdec.at[dd, dd * GLA_LOWRANK:(dd + 1) * GLA_LOWRANK].set(w_decay[dd])
    o_a = _gla(proj3, small3, wdec, b_decay, gla_norm[None, :])
    o_b = _na(proj3, _na_bias_table(rpb), n_gla)
    return o_a.reshape(-1, o_a.shape[-1]), o_b.reshape(-1, o_b.shape[-1])


def _per_group(v):
    return v.reshape(2, SSD_GROUPS, SSD_JH).transpose(1, 0, 2).reshape(SSD_GROUPS, 2 * SSD_JH)


def _odd_mixers(x2, bn, seq, g0, w_in, conv_w, conv_b, dt_bias, a_log, d_skip, sink, cos, sin):
    conv_dim = SSD_INNER + 2 * SSD_GROUPS * SSD_STATE
    n_z = SSD_INNER
    n_dt = 2 * SSD_HEADS
    c_dt = n_z + conv_dim
    w_main = jnp.concatenate([w_in[:, :c_dt], w_in[:, c_dt + n_dt:]], axis=1).astype(BF16)
    w_dt = w_in[:, c_dt:c_dt + n_dt].reshape(-1, 2, SSD_GROUPS, SSD_JH).transpose(0, 2, 1, 3)
    w_dt = w_dt.reshape(-1, SSD_GROUPS, 2 * SSD_JH)
    w_small = jnp.pad(w_dt, ((0, 0), (0, 0), (0, LANES - 2 * SSD_JH))).reshape(-1, SSD_GROUPS * LANES)
    proj, small = _inproj(x2, g0, w_main, w_small.astype(BF16))
    proj3 = proj.reshape(bn, seq, -1)
    dt3 = small.reshape(bn, seq, SSD_GROUPS * LANES)
    xbc_act = _conv(proj3, conv_w, conv_b[None, :], n_z)
    pbg = _per_group(dt_bias)
    alg = _per_group(a_log)
    head_of_ch = np.arange(SSD_JH * SSD_HEADDIM) // SSD_HEADDIM
    expand = jnp.asarray(np.arange(2 * SSD_JH)[None, :, None]
                         == (np.arange(2)[:, None, None] * SSD_JH + head_of_ch[None, None, :]), F32)
    dskip = jnp.repeat(d_skip.astype(F32), SSD_HEADDIM)[None, :]
    y_pre = _ssd(xbc_act, dt3, pbg[:, :, None], alg[:, :, None], dskip, expand)
    o_d = _swa(proj3, sink, cos, sin, c_dt)
    return y_pre.reshape(-1, SSD_INNER), proj, o_d.reshape(-1, o_d.shape[-1])


def kernel(x, norm_gains, ffn_w_gate, ffn_w_up, ffn_w_down, even_w_in, even_w_out, gla_w_decay,
           gla_b_decay, gla_norm, na_rpb, odd_w_in, odd_w_out, ssd_conv_w, ssd_conv_b, ssd_dt_bias,
           ssd_a_log, ssd_d, ssd_norm, swa_sink):
    bn, seq, d = x.shape
    depth = norm_gains.shape[0]
    x2 = x.reshape(bn * seq, d)
    cos, sin = _rope_tables(seq)
    wg = ffn_w_gate.astype(BF16)
    wu = ffn_w_up.astype(BF16)
    wd = ffn_w_down.astype(BF16)
    even_w_out_b = even_w_out.astype(BF16)
    odd_w_out_b = odd_w_out.astype(BF16)
    for layer in range(depth):
        g = norm_gains[layer][:, None, :]
        i = layer // 2
        if layer % 2 == 0:
            o_a, o_b = _even_mixers(x2, bn, seq, g[0], even_w_in[i], gla_w_decay[i], gla_b_decay[i],
                                    gla_norm[i], na_rpb[i])
            x2 = _outproj(x2, o_a, o_b, even_w_out_b, i, g[1])
        else:
            y_pre, proj, o_d = _odd_mixers(x2, bn, seq, g[0], odd_w_in[i], ssd_conv_w[i], ssd_conv_b[i],
                                           ssd_dt_bias[i], ssd_a_log[i], ssd_d[i], swa_sink[i], cos, sin)
            x2 = _outproj(x2, y_pre, o_d, odd_w_out_b, i, g[1],
                          z_src=proj, ssd_gain=ssd_norm[i][None, :])
        x2 = _ffn(x2, g[2], wg, wu, wd, g[3], layer)
    return x2.reshape(bn, seq, d)
```

```python
import functools
import math

import jax
import jax.numpy as jnp
import numpy as np
from jax import lax
from jax.experimental import pallas as pl
from jax.experimental.pallas import tpu as pltpu

F32 = jnp.float32
BF16 = jnp.bfloat16
HIGHEST = lax.Precision.HIGHEST

EPS = 1e-6
ROPE_THETA = 10000.0
GRID_W = 64
LANES = 128

GLA_DK = 64
GLA_DV = 128
GLA_HEADS = 8
GLA_LOWRANK = 16
GLA_TAU = 16.0
GLA_CHUNK = 64
GLA_SUB = 16
GLA_EXP_CLAMP = 60.0
GLA_TRI_BLOCK = 256

NA_DH = 128
NA_HEADS = 8
NA_KH = 8
NA_KW = 16
NA_ROWS_PER_ITER = 8
NEG_BIG = -1e30

SSD_INNER = 1024
SSD_HEADDIM = 64
SSD_HEADS = 16
SSD_GROUPS = 2
SSD_STATE = 128
SSD_CONV = 5
SSD_CHUNK = 128
SSD_JH = SSD_HEADS // SSD_GROUPS
SSD_ROW_SLAB = 64

SWA_DH = 128
SWA_HEADS = 8
SWA_KV = 2
SWA_GQ = SWA_HEADS // SWA_KV
SWA_WINDOW = 128
SWA_BLOCK = 128

VMEM_LIMIT = 52 * 1024 * 1024


FFN_NORM_ROWS = 256
FFN_VMEM_LIMIT = 58 * 1024 * 1024


def _cparams(sem, vmem_limit=VMEM_LIMIT):
    return pltpu.CompilerParams(dimension_semantics=sem, vmem_limit_bytes=vmem_limit)


def _rms(x, g):
    return x * lax.rsqrt(jnp.mean(x * x, axis=-1, keepdims=True) + EPS) * g


def _silu(x):
    return x * (1.0 / (1.0 + jnp.exp(-x)))


def _dot_t(a, b):
    return lax.dot_general(a, b, (((1,), (1,)), ((), ())), preferred_element_type=F32)


def _inproj_body(x_ref, g_ref, w_ref, ws_ref, o_ref, os_ref, h_scr):
    j = pl.program_id(1)
    half = x_ref.shape[0] // 2

    @pl.when(j == 0)
    def _():
        for r0 in (0, half):
            rs = pl.ds(r0, half)
            h = _rms(x_ref[rs, :], g_ref[...]).astype(BF16)
            h_scr[rs, :] = h
            os_ref[rs, :] = jnp.dot(h, ws_ref[...], preferred_element_type=F32)
            o_ref[rs, :] = jnp.dot(h, w_ref[...], preferred_element_type=F32).astype(o_ref.dtype)

    @pl.when(j > 0)
    def _():
        o_ref[...] = jnp.dot(h_scr[...], w_ref[...], preferred_element_type=F32).astype(o_ref.dtype)


def _inproj(x2, g, w_main, w_small, tm=1024, tn=2048):
    n, d = x2.shape
    nm = w_main.shape[1]
    ns = w_small.shape[1]
    return pl.pallas_call(
        _inproj_body,
        grid=(n // tm, nm // tn),
        in_specs=[
            pl.BlockSpec((tm, d), lambda i, j: (i, 0)),
            pl.BlockSpec((1, d), lambda i, j: (0, 0)),
            pl.BlockSpec((d, tn), lambda i, j: (0, j)),
            pl.BlockSpec((d, ns), lambda i, j: (0, 0)),
        ],
        out_specs=[
            pl.BlockSpec((tm, tn), lambda i, j: (i, j)),
            pl.BlockSpec((tm, ns), lambda i, j: (i, 0)),
        ],
        out_shape=[jax.ShapeDtypeStruct((n, nm), BF16), jax.ShapeDtypeStruct((n, ns), F32)],
        scratch_shapes=[pltpu.VMEM((tm, d), BF16)],
        compiler_params=_cparams(("parallel", "arbitrary")),
        name="inproj",
    )(x2, g, w_main, w_small)


def _outproj_even_body(a_ref, b_ref, wa_ref, wb_ref, g_ref, x_ref, o_ref):
    mix = (jnp.dot(a_ref[...], wa_ref[...], preferred_element_type=F32)
           + jnp.dot(b_ref[...], wb_ref[...], preferred_element_type=F32))
    o_ref[...] = x_ref[...] + _rms(mix, g_ref[...])


def _outproj_odd_body(y_ref, z_ref, ng_ref, b_ref, wa_ref, wb_ref, g_ref, x_ref, o_ref):
    y = y_ref[...].astype(F32) * _silu(z_ref[...].astype(F32))
    a = _rms(y, ng_ref[...]).astype(BF16)
    mix = (jnp.dot(a, wa_ref[...], preferred_element_type=F32)
           + jnp.dot(b_ref[...], wb_ref[...], preferred_element_type=F32))
    o_ref[...] = x_ref[...] + _rms(mix, g_ref[...])


def _outproj(x2, a, b, w_out, li, g, z_src=None, ssd_gain=None, tm=512):
    n, d = x2.shape
    half = a.shape[1]
    row = lambda i: (i, 0)
    fixed = lambda i: (0, 0)
    w_specs = [pl.BlockSpec((None, half, d), lambda i: (li, 0, 0)),
               pl.BlockSpec((None, half, d), lambda i: (li, 1, 0))]
    tail_specs = [pl.BlockSpec((1, d), fixed), pl.BlockSpec((tm, d), row)]
    if z_src is None:
        body = _outproj_even_body
        in_specs = [pl.BlockSpec((tm, half), row), pl.BlockSpec((tm, half), row)] + w_specs + tail_specs
        args = (a, b, w_out, w_out, g, x2)
    else:
        body = _outproj_odd_body
        in_specs = ([pl.BlockSpec((tm, half), row), pl.BlockSpec((tm, half), row),
                     pl.BlockSpec((1, half), fixed), pl.BlockSpec((tm, half), row)]
                    + w_specs + tail_specs)
        args = (a, z_src, ssd_gain, b, w_out, w_out, g, x2)
    return pl.pallas_call(
        body,
        grid=(n // tm,),
        in_specs=in_specs,
        out_specs=pl.BlockSpec((tm, d), row),
        out_shape=jax.ShapeDtypeStruct((n, d), F32),
        compiler_params=_cparams(("parallel",)),
        name="outproj",
    )(*args)


def _ffn_body(x_ref, g2_ref, wg_ref, wu_ref, wd_ref, g3_ref, o_ref, h_scr):
    k = pl.program_id(1)
    last = pl.num_programs(1) - 1
    half = x_ref.shape[0] // 2

    def pieces(r0):
        return [pl.ds(r0 + p, FFN_NORM_ROWS) for p in range(0, half, FFN_NORM_ROWS)]

    def partial_down(rs):
        h = h_scr[rs, :]
        gate = jnp.dot(h, wg_ref[...], preferred_element_type=F32)
        up = jnp.dot(h, wu_ref[...], preferred_element_type=F32)
        return jnp.dot((_silu(gate) * up).astype(BF16), wd_ref[...], preferred_element_type=F32)

    @pl.when(k == 0)
    def _():
        for r0 in (0, half):
            for ps in pieces(r0):
                h_scr[ps, :] = _rms(x_ref[ps, :], g2_ref[...]).astype(BF16)
            o_ref[pl.ds(r0, half), :] = partial_down(pl.ds(r0, half))

    @pl.when(jnp.logical_and(k > 0, k < last))
    def _():
        for r0 in (0, half):
            o_ref[pl.ds(r0, half), :] += partial_down(pl.ds(r0, half))

    @pl.when(k == last)
    def _():
        for r0 in (0, half):
            o_ref[pl.ds(r0, half), :] += partial_down(pl.ds(r0, half))
            for ps in pieces(r0):
                o_ref[ps, :] = x_ref[ps, :] + _rms(o_ref[ps, :], g3_ref[...])


def _ffn(x2, g2, wg, wu, wd, g3, layer, tm=1024, th=512):
    n, d = x2.shape
    hid = wd.shape[1]
    return pl.pallas_call(
        _ffn_body,
        grid=(n // tm, hid // th),
        in_specs=[
            pl.BlockSpec((tm, d), lambda i, k: (i, 0)),
            pl.BlockSpec((1, d), lambda i, k: (0, 0)),
            pl.BlockSpec((None, d, th), lambda i, k: (layer, 0, k)),
            pl.BlockSpec((None, d, th), lambda i, k: (layer, 0, k)),
            pl.BlockSpec((None, th, d), lambda i, k: (layer, k, 0)),
            pl.BlockSpec((1, d), lambda i, k: (0, 0)),
        ],
        out_specs=pl.BlockSpec((tm, d), lambda i, k: (i, 0)),
        out_shape=jax.ShapeDtypeStruct((n, d), F32),
        scratch_shapes=[pltpu.VMEM((tm, d), BF16)],
        compiler_params=_cparams(("parallel", "arbitrary"), FFN_VMEM_LIMIT),
        name="ffn",
    )(x2, g2, wg, wu, wd, g3)


def _gla_body(q_ref, k_ref, v_ref, g_ref, lr_ref, wd_ref, bd_ref, gn_ref, o_ref,
              bc_scr, o_scr, st_scr, *, seq):
    c = GLA_CHUNK
    nc = seq // c
    half = nc // 2
    nsub = c // GLA_SUB
    rb = min(512, seq)
    row_i = lax.broadcasted_iota(jnp.int32, (c, c), 0)
    col_i = lax.broadcasted_iota(jnp.int32, (c, c), 1)
    tri = (row_i >= col_i, row_i <= col_i)
    lane = lax.broadcasted_iota(jnp.int32, (1, LANES), 1)
    hmask = (lane < GLA_DK, lane >= GLA_DK)

    w2 = jnp.concatenate([wd_ref[0], wd_ref[1]], axis=1)
    w_hi = w2.astype(BF16)
    w_lo = (w2 - w_hi.astype(F32)).astype(BF16)
    w_cat = jnp.concatenate([w_hi, w_lo, w_hi], axis=0)
    tb = GLA_TRI_BLOCK
    brow = lax.broadcasted_iota(jnp.int32, (tb, tb), 0)
    bcol = lax.broadcasted_iota(jnp.int32, (tb, tb), 1)
    same_chunk = (brow // c) == (bcol // c)
    blk_tri = ((same_chunk & (brow >= bcol)).astype(BF16), (same_chunk & (brow <= bcol)).astype(BF16))

    def pre(i, carry):
        r0 = pl.multiple_of(i * rb, rb)
        lr = lr_ref[0, pl.ds(r0, rb), :]
        lr_hi = lr.astype(BF16)
        lr_lo = (lr - lr_hi.astype(F32)).astype(BF16)
        z2 = jnp.dot(jnp.concatenate([lr_hi, lr_hi, lr_lo], axis=1), w_cat, preferred_element_type=F32)
        las = []
        for d in range(2):
            z = z2[:, LANES * d:LANES * (d + 1)] + bd_ref[d:d + 1, :]
            la = (jnp.minimum(z, 0.0) - jnp.log1p(jnp.exp(-jnp.abs(z)))) * (1.0 / GLA_TAU)
            la_hi = la.astype(BF16)
            las.append(jnp.concatenate([la_hi, (la - la_hi.astype(F32)).astype(BF16)], axis=1))
        for d in range(2):
            for t in range(rb // tb):
                s2 = jnp.dot(blk_tri[d], las[d][t * tb:(t + 1) * tb], preferred_element_type=F32)
                bc_scr[d, pl.ds(r0 + t * tb, tb), :] = s2[:, :LANES] + s2[:, LANES:]
        return carry

    lax.fori_loop(0, seq // rb, pre, 0)

    def stage1(n, d):
        rows = pl.ds(pl.multiple_of(n * c, c), c)
        bc = bc_scr[d, rows, :]
        q = q_ref[0, rows, :].astype(F32) * (GLA_DK ** -0.5)
        k = k_ref[0, rows, :].astype(F32)
        v = v_ref[0, rows, :]
        b_edge = bc[c - 1:c] if d == 0 else bc[0:1]
        qhat = q * jnp.exp(bc)
        khat = (k * jnp.exp(b_edge - bc)).astype(BF16)
        att_rows = ([], [])
        for i in range(nsub):
            lo = GLA_SUB * i
            hi = lo + GLA_SUB
            if d == 0:
                ref = bc[lo - 1:lo] if i > 0 else jnp.zeros((1, LANES), F32)
            else:
                ref = bc[hi:hi + 1] if i < nsub - 1 else jnp.zeros((1, LANES), F32)
            qi = q[lo:hi] * jnp.exp(bc[lo:hi] - ref)
            ki = (k * jnp.exp(jnp.minimum(ref - bc, GLA_EXP_CLAMP))).astype(BF16)
            for hh in range(2):
                qim = jnp.where(hmask[hh], qi, 0.0).astype(BF16)
                att_rows[hh].append(_dot_t(qim, ki))
        heads = []
        for hh in range(2):
            att = jnp.concatenate(att_rows[hh], axis=0)
            att = jnp.where(tri[d], att, 0.0).astype(BF16)
            vh = v[:, GLA_DV * hh:GLA_DV * (hh + 1)]
            kv = lax.dot_general(vh, khat, (((0,), (0,)), ((), ())), preferred_element_type=F32)
            qh = jnp.where(hmask[hh], qhat, 0.0).astype(BF16)
            heads.append((att, vh, kv, qh))
        return d, heads, jnp.exp(b_edge)

    def stage2(s1, states):
        d, heads, decay = s1
        outs = []
        for hh, (att, vh, kv, qh) in enumerate(heads):
            st = states[d][hh]
            outs.append(jnp.dot(att, vh, preferred_element_type=F32) + _dot_t(qh, st.astype(BF16)))
            states[d][hh] = st * decay + kv
        return jnp.concatenate(outs, axis=1)

    def finish(n, o):
        rows = pl.ds(pl.multiple_of(n * c, c), c)
        parts = []
        for hh in range(2):
            oh = o[:, GLA_DV * hh:GLA_DV * (hh + 1)]
            parts.append(oh * lax.rsqrt(jnp.mean(oh * oh, axis=-1, keepdims=True) + EPS))
        o = jnp.concatenate(parts, axis=1) * gn_ref[...]
        o_ref[0, rows, :] = (o * _silu(g_ref[0, rows, :].astype(F32))).astype(o_ref.dtype)

    st_scr[...] = jnp.zeros_like(st_scr)

    cpi = next(u for u in (4, 2, 1) if half % u == 0)

    def walk(i, base_f, base_b):
        work = []
        for u in range(cpi):
            work.append((base_f + i * cpi + u, 0))
            work.append((base_b - i * cpi - u, 1))
        prepared = [stage1(n, d) for n, d in work]
        states = [[st_scr[d, hh] for hh in range(2)] for d in range(2)]
        outs = [(n, stage2(s1, states)) for (n, _), s1 in zip(work, prepared)]
        for d in range(2):
            for hh in range(2):
                st_scr[d, hh] = states[d][hh]
        return outs

    def first(i, carry):
        for n, o in walk(i, 0, nc - 1):
            o_scr[pl.ds(pl.multiple_of(n * c, c), c), :] = o
        return carry

    lax.fori_loop(0, half // cpi, first, 0)

    def second(i, carry):
        for n, o in walk(i, half, half - 1):
            finish(n, o_scr[pl.ds(pl.multiple_of(n * c, c), c), :] + o)
        return carry

    lax.fori_loop(0, half // cpi, second, 0)


def _gla(proj3, small3, wdec_pad, bdec, gnorm):
    bn, seq, _ = proj3.shape
    npair = GLA_HEADS // 2
    qoff = 0
    koff = (GLA_HEADS * GLA_DK) // LANES
    voff = (2 * GLA_HEADS * GLA_DK) // (2 * GLA_DV)
    goff = voff + npair
    return pl.pallas_call(
        functools.partial(_gla_body, seq=seq),
        grid=(bn, npair),
        in_specs=[
            pl.BlockSpec((1, seq, LANES), lambda b, p: (b, 0, qoff + p)),
            pl.BlockSpec((1, seq, LANES), lambda b, p: (b, 0, koff + p)),
            pl.BlockSpec((1, seq, 2 * GLA_DV), lambda b, p: (b, 0, voff + p)),
            pl.BlockSpec((1, seq, 2 * GLA_DV), lambda b, p: (b, 0, goff + p)),
            pl.BlockSpec((1, seq, LANES), lambda b, p: (b, 0, 0)),
            pl.BlockSpec((2, LANES, LANES), lambda b, p: (0, 0, p)),
            pl.BlockSpec((2, LANES), lambda b, p: (0, p)),
            pl.BlockSpec((1, 2 * GLA_DV), lambda b, p: (0, p)),
        ],
        out_specs=pl.BlockSpec((1, seq, 2 * GLA_DV), lambda b, p: (b, 0, p)),
        out_shape=jax.ShapeDtypeStruct((bn, seq, GLA_HEADS * GLA_DV), BF16),
        scratch_shapes=[
            pltpu.VMEM((2, seq, LANES), F32),
            pltpu.VMEM((seq, 2 * GLA_DV), F32),
            pltpu.VMEM((2, 2, GLA_DV, LANES), F32),
        ],
        compiler_params=_cparams(("parallel", "parallel")),
        name="gla",
    )(proj3, proj3, proj3, proj3, small3, wdec_pad, bdec, gnorm)


def _na_bias_table(rpb):
    nh, ndr, ndc = rpb.shape
    cols = np.arange(GRID_W)
    cs = np.clip(cols - NA_KW // 2, 0, GRID_W - NA_KW)
    valid = (cols[None, :] >= cs[:, None]) & (cols[None, :] < cs[:, None] + NA_KW)
    dc = cols[None, :] - cols[:, None] + NA_KW - 1
    onehot = (np.arange(ndc)[:, None, None] == dc[None]) & valid[None]
    toep = jnp.dot(rpb.reshape(nh * ndr, ndc).astype(F32),
                   jnp.asarray(onehot.reshape(ndc, -1), F32), precision=HIGHEST)
    toep = jnp.where(jnp.asarray(valid.reshape(1, -1)), toep, NEG_BIG)
    toep = toep.reshape(nh, ndr, GRID_W, GRID_W)
    t = jnp.stack([toep[:, off:off + NA_KH] for off in range(NA_KH)], axis=1)
    return t.transpose(0, 1, 3, 2, 4).reshape(nh, NA_KH, GRID_W, NA_KH * GRID_W)


def _na_body(q_ref, k_ref, v_ref, bias_ref, o_ref, *, nrows):
    w = GRID_W
    nk = NA_KH * w

    def group(i, carry):
        idx = []
        scores = []
        for u in range(NA_ROWS_PER_ITER):
            r = i * NA_ROWS_PER_ITER + u
            rs = jnp.clip(r - NA_KH // 2, 0, nrows - NA_KH)
            qrows = pl.ds(pl.multiple_of(r * w, w), w)
            krows = pl.ds(pl.multiple_of(rs * w, w), nk)
            idx.append((qrows, krows, rs - r + (NA_KH - 1)))
            scores.append(_dot_t(q_ref[0, qrows, :], k_ref[0, krows, :]))
        probs = []
        for (qrows, krows, off), s in zip(idx, scores):
            s = s * (NA_DH ** -0.5) + bias_ref[0, off]
            p = jnp.exp(s - jnp.max(s, axis=-1, keepdims=True))
            probs.append((p.astype(BF16), jnp.sum(p, axis=-1, keepdims=True)))
        for (qrows, krows, off), (p, l) in zip(idx, probs):
            o = jnp.dot(p, v_ref[0, krows, :], preferred_element_type=F32) / l
            o_ref[0, qrows, :] = o.astype(o_ref.dtype)
        return carry

    lax.fori_loop(0, nrows // NA_ROWS_PER_ITER, group, 0)


def _na(proj3, bias_tab, col0):
    bn, seq, _ = proj3.shape
    nrows = seq // GRID_W
    qoff = col0 // NA_DH
    koff = qoff + NA_HEADS
    voff = koff + NA_HEADS
    return pl.pallas_call(
        functools.partial(_na_body, nrows=nrows),
        grid=(bn, NA_HEADS),
        in_specs=[
            pl.BlockSpec((1, seq, NA_DH), lambda b, h: (b, 0, qoff + h)),
            pl.BlockSpec((1, seq, NA_DH), lambda b, h: (b, 0, koff + h)),
            pl.BlockSpec((1, seq, NA_DH), lambda b, h: (b, 0, voff + h)),
            pl.BlockSpec((1, NA_KH, GRID_W, NA_KH * GRID_W), lambda b, h: (h, 0, 0, 0)),
        ],
        out_specs=pl.BlockSpec((1, seq, NA_DH), lambda b, h: (b, 0, h)),
        out_shape=jax.ShapeDtypeStruct((bn, seq, NA_HEADS * NA_DH), BF16),
        compiler_params=_cparams(("parallel", "parallel")),
        name="na",
    )(proj3, proj3, proj3, bias_tab)


CONV_PAD = 16


def _conv_body(x_ref, w_ref, b_ref, o_ref, xp_scr, *, seq):
    rb = min(256, seq)
    ch = x_ref.shape[-1]
    zeros = jnp.zeros((CONV_PAD, ch), xp_scr.dtype)
    xp_scr[0:CONV_PAD, :] = zeros
    xp_scr[seq + CONV_PAD:seq + 2 * CONV_PAD, :] = zeros
    xp_scr[CONV_PAD:seq + CONV_PAD, :] = x_ref[0]

    def blk(i, carry):
        r0 = pl.multiple_of(i * rb, rb)
        xw = xp_scr[pl.ds(r0, rb + 2 * CONV_PAD), :].astype(F32)
        acc = jnp.zeros((rb, ch), F32) + b_ref[...]
        for j in range(SSD_CONV):
            s0 = CONV_PAD - SSD_CONV // 2 + j
            acc = acc + xw[s0:s0 + rb] * w_ref[j:j + 1, :]
        o_ref[0, pl.ds(r0, rb), :] = _silu(acc).astype(o_ref.dtype)
        return carry

    lax.fori_loop(0, seq // rb, blk, 0)


def _conv(proj3, conv_w, conv_b, col0, tc=512):
    bn, seq, _ = proj3.shape
    cdim = conv_w.shape[1]
    c0 = col0 // tc
    return pl.pallas_call(
        functools.partial(_conv_body, seq=seq),
        grid=(bn, cdim // tc),
        in_specs=[
            pl.BlockSpec((1, seq, tc), lambda b, j: (b, 0, c0 + j)),
            pl.BlockSpec((SSD_CONV, tc), lambda b, j: (0, j)),
            pl.BlockSpec((1, tc), lambda b, j: (0, j)),
        ],
        out_specs=pl.BlockSpec((1, seq, tc), lambda b, j: (b, 0, j)),
        out_shape=jax.ShapeDtypeStruct((bn, seq, cdim), BF16),
        scratch_shapes=[pltpu.VMEM((seq + 2 * CONV_PAD, tc), BF16)],
        compiler_params=_cparams(("parallel", "parallel")),
        name="ssd_conv",
    )(proj3, conv_w, conv_b)


def _softplus(x):
    return jnp.maximum(x, 0.0) + jnp.log1p(jnp.exp(-jnp.abs(x)))


def _ssd_body(xs_ref, bm_ref, cm_ref, dt_ref, pb_ref, al_ref, dsk_ref, ex_ref, o_ref,
              st_scr, run_scr, dec_scr, *, seq):
    l = SSD_CHUNK
    nc = seq // l
    nh2 = 2 * SSD_JH
    row_i = lax.broadcasted_iota(jnp.int32, (l, l), 0)
    col_i = lax.broadcasted_iota(jnp.int32, (l, l), 1)
    lane = lax.broadcasted_iota(jnp.int32, (1, LANES), 1)
    first_head = lane < SSD_HEADDIM
    bwd_row = lax.broadcasted_iota(jnp.int32, (nh2, 1), 0) >= SSD_JH
    a_col = -jnp.exp(al_ref[0])

    lower = row_i > col_i
    diag = row_i == col_i
    triu_b = (row_i <= col_i).astype(BF16)

    def decays(n):
        rows = pl.ds(pl.multiple_of(n * l, l), l)
        dtv = _softplus(dt_ref[0, rows, :].T[0:nh2] + pb_ref[0])
        da = dtv * a_col
        hi = da.astype(BF16)
        r1 = da - hi.astype(F32)
        mid = r1.astype(BF16)
        lo = (r1 - mid.astype(F32)).astype(BF16)
        p3 = jnp.dot(jnp.concatenate([hi, mid, lo], axis=0), triu_b, preferred_element_type=F32)
        pre = p3[0:nh2] + p3[nh2:2 * nh2] + p3[2 * nh2:]
        total = pre[:, l - 1:l]
        acum = jnp.where(bwd_row, total - pre + da, pre)
        dec_scr[0, n] = dtv
        dec_scr[1, n] = acum
        dec_scr[2, n] = dtv * jnp.exp(total - acum)
        dec_scr[3, n] = jnp.broadcast_to(jnp.exp(total), (nh2, l))

    def chunk_state(n, d):
        rows = pl.ds(pl.multiple_of(n * l, l), l)
        w_state_t = dec_scr[2, n]
        x = xs_ref[0, rows, :]
        bmt = bm_ref[0, rows, :].astype(F32).T
        edge_ch = jnp.sum(dec_scr[3, n][:, 0:1] * ex_ref[d], axis=0, keepdims=True)
        new_st = []
        for jj in range(SSD_JH // 2):
            xp = x[:, LANES * jj:LANES * (jj + 1)]
            s2 = []
            for hh in range(2):
                cidx = d * SSD_JH + 2 * jj + hh
                bw = (bmt * w_state_t[cidx:cidx + 1, :]).astype(BF16)
                s2.append(jnp.dot(bw, xp, preferred_element_type=F32))
            new_st.append(jnp.where(first_head, s2[0], s2[1]))
        st = run_scr[d]
        st_scr[d, n] = st.astype(BF16)
        run_scr[d] = st * edge_ch + jnp.concatenate(new_st, axis=1)

    def chunk_out(n):
        rows = pl.ds(pl.multiple_of(n * l, l), l)
        dtv_t = dec_scr[0, n]
        acum_t = dec_scr[1, n]
        acum = acum_t.T
        x = xs_ref[0, rows, :]
        cm = cm_ref[0, rows, :]
        cmf = cm.astype(F32)
        cb = _dot_t(cm, bm_ref[0, rows, :])
        st_f = st_scr[0, n]
        st_b = st_scr[1, n]
        rhs = [jnp.concatenate([x[:, LANES * jj:LANES * (jj + 1)],
                                st_f[:, LANES * jj:LANES * (jj + 1)],
                                st_b[:, LANES * jj:LANES * (jj + 1)]], axis=0)
               for jj in range(SSD_JH // 2)]
        for r0 in range(0, l, SSD_ROW_SLAB):
            rs = slice(r0, r0 + SSD_ROW_SLAB)
            ys = []
            for jj in range(SSD_JH // 2):
                y2 = []
                for hh in range(2):
                    cf = 2 * jj + hh
                    cr = SSD_JH + cf
                    col_f = jnp.broadcast_to(acum[rs, cf:cf + 1], (SSD_ROW_SLAB, l))
                    col_r = jnp.broadcast_to(acum[rs, cr:cr + 1], (SSD_ROW_SLAB, l))
                    seg = jnp.where(lower[rs], col_f - acum_t[cf:cf + 1, :],
                                    col_r - acum_t[cr:cr + 1, :])
                    wgt = (jnp.where(lower[rs], dtv_t[cf:cf + 1, :], dtv_t[cr:cr + 1, :])
                           + jnp.where(diag[rs], dtv_t[cf:cf + 1, :], 0.0))
                    m = cb[rs] * jnp.exp(jnp.minimum(seg, 0.0)) * wgt
                    lhs = jnp.concatenate([m.astype(BF16),
                                           (cmf[rs] * jnp.exp(col_f)).astype(BF16),
                                           (cmf[rs] * jnp.exp(col_r)).astype(BF16)], axis=1)
                    y2.append(jnp.dot(lhs, rhs[jj], preferred_element_type=F32))
                ys.append(jnp.where(first_head, y2[0], y2[1]))
            y = jnp.concatenate(ys, axis=1) + x[rs].astype(F32) * dsk_ref[...]
            o_ref[0, pl.ds(pl.multiple_of(n * l, l) + r0, SSD_ROW_SLAB), :] = y.astype(o_ref.dtype)

    def prep(i, carry):
        decays(i)
        return carry

    lax.fori_loop(0, nc, prep, 0, unroll=2)
    run_scr[...] = jnp.zeros_like(run_scr)

    def states(i, carry):
        chunk_state(i, 0)
        chunk_state(nc - 1 - i, 1)
        return carry

    lax.fori_loop(0, nc, states, 0)

    def outputs(n, carry):
        chunk_out(n)
        return carry

    lax.fori_loop(0, nc, outputs, 0)


def _ssd(xbc_act, dt3, pb, al, dskip, expand):
    bn, seq, _ = xbc_act.shape
    gw = SSD_JH * SSD_HEADDIM
    boff = SSD_INNER // SSD_STATE
    coff = boff + SSD_GROUPS
    return pl.pallas_call(
        functools.partial(_ssd_body, seq=seq),
        grid=(bn, SSD_GROUPS),
        in_specs=[
            pl.BlockSpec((1, seq, gw), lambda b, g: (b, 0, g)),
            pl.BlockSpec((1, seq, SSD_STATE), lambda b, g: (b, 0, boff + g)),
            pl.BlockSpec((1, seq, SSD_STATE), lambda b, g: (b, 0, coff + g)),
            pl.BlockSpec((1, seq, LANES), lambda b, g: (b, 0, g)),
            pl.BlockSpec((1, 2 * SSD_JH, 1), lambda b, g: (g, 0, 0)),
            pl.BlockSpec((1, 2 * SSD_JH, 1), lambda b, g: (g, 0, 0)),
            pl.BlockSpec((1, gw), lambda b, g: (0, g)),
            pl.BlockSpec((2, 2 * SSD_JH, gw), lambda b, g: (0, 0, 0)),
        ],
        out_specs=pl.BlockSpec((1, seq, gw), lambda b, g: (b, 0, g)),
        out_shape=jax.ShapeDtypeStruct((bn, seq, SSD_INNER), BF16),
        scratch_shapes=[pltpu.VMEM((2, seq // SSD_CHUNK, SSD_STATE, gw), BF16),
                        pltpu.VMEM((2, SSD_STATE, gw), F32),
                        pltpu.VMEM((4, seq // SSD_CHUNK, 2 * SSD_JH, SSD_CHUNK), F32)],
        compiler_params=_cparams(("parallel", "parallel")),
        name="ssd_scan",
    )(xbc_act, xbc_act, xbc_act, dt3, pb, al, dskip, expand)


def _rope_tables(seq):
    half = SWA_DH // 2
    inv = ROPE_THETA ** (-np.arange(half, dtype=np.float64) / half)
    ang = np.arange(seq, dtype=np.float64)[:, None] * inv[None, :]
    cos, sin = np.cos(ang), np.sin(ang)
    return (jnp.asarray(np.concatenate([cos, cos], axis=1), F32),
            jnp.asarray(np.concatenate([-sin, sin], axis=1), F32))


def _rope(x, cos, sin_signed):
    return x * cos + pltpu.roll(x, SWA_DH // 2, 1) * sin_signed


def _swa_body(sink_ref, q_ref, k_ref, v_ref, cos_ref, sin_ref, o_ref, kr_scr, *, seq):
    wb = SWA_BLOCK
    nb = seq // wb
    nkeys = 3 * wb
    kvh = pl.program_id(1)
    rb = min(512, seq)

    def krope(i, carry):
        rows = pl.ds(pl.multiple_of(i * rb, rb), rb)
        kr_scr[rows, :] = _rope(k_ref[0, rows, :].astype(F32), cos_ref[rows, :],
                                sin_ref[rows, :]).astype(BF16)
        return carry

    lax.fori_loop(0, seq // rb, krope, 0)

    qpos_l = lax.broadcasted_iota(jnp.int32, (wb, nkeys), 0)
    kpos_l = lax.broadcasted_iota(jnp.int32, (wb, nkeys), 1)

    def blk(n, carry):
        rows = pl.ds(pl.multiple_of(n * wb, wb), wb)
        ks = pl.multiple_of(jnp.clip(n - 1, 0, nb - 3) * wb, wb)
        krows = pl.ds(ks, nkeys)
        cos = cos_ref[rows, :]
        sin = sin_ref[rows, :]
        kk = kr_scr[krows, :]
        vv = v_ref[0, krows, :]
        valid = jnp.abs((qpos_l + n * wb) - (kpos_l + ks)) <= SWA_WINDOW
        qb = q_ref[0, rows, :].astype(F32)
        scores = []
        for hh in range(SWA_GQ):
            qh = (_rope(qb[:, SWA_DH * hh:SWA_DH * (hh + 1)], cos, sin) * (SWA_DH ** -0.5)).astype(BF16)
            scores.append(_dot_t(qh, kk))
        probs = []
        for hh, s in enumerate(scores):
            sink = sink_ref[kvh * SWA_GQ + hh]
            s = jnp.where(valid, s, NEG_BIG)
            m = jnp.maximum(jnp.max(s, axis=-1, keepdims=True), sink)
            p = jnp.exp(s - m)
            probs.append((p.astype(BF16), jnp.sum(p, axis=-1, keepdims=True) + jnp.exp(sink - m)))
        for hh, (p, den) in enumerate(probs):
            o = jnp.dot(p, vv, preferred_element_type=F32) / den
            o_ref[0, rows, SWA_DH * hh:SWA_DH * (hh + 1)] = o.astype(o_ref.dtype)
        return carry

    lax.fori_loop(0, nb, blk, 0)


def _swa(proj3, sink, cos, sin, col0):
    bn, seq, _ = proj3.shape
    qw = SWA_GQ * SWA_DH
    qoff = col0 // qw
    koff = (col0 + SWA_HEADS * SWA_DH) // SWA_DH
    voff = koff + SWA_KV
    return pl.pallas_call(
        functools.partial(_swa_body, seq=seq),
        grid=(bn, SWA_KV),
        in_specs=[
            pl.BlockSpec(memory_space=pltpu.SMEM),
            pl.BlockSpec((1, seq, qw), lambda b, h: (b, 0, qoff + h)),
            pl.BlockSpec((1, seq, SWA_DH), lambda b, h: (b, 0, koff + h)),
            pl.BlockSpec((1, seq, SWA_DH), lambda b, h: (b, 0, voff + h)),
            pl.BlockSpec((seq, SWA_DH), lambda b, h: (0, 0)),
            pl.BlockSpec((seq, SWA_DH), lambda b, h: (0, 0)),
        ],
        out_specs=pl.BlockSpec((1, seq, qw), lambda b, h: (b, 0, h)),
        out_shape=jax.ShapeDtypeStruct((bn, seq, SWA_HEADS * SWA_DH), BF16),
        scratch_shapes=[pltpu.VMEM((seq, SWA_DH), BF16)],
        compiler_params=_cparams(("parallel", "parallel")),
        name="swa",
    )(sink, proj3, proj3, proj3, cos, sin)


def _pad_cols(w, width):
    return jnp.pad(w, ((0, 0), (0, width - w.shape[1])))


def _even_mixers(x2, bn, seq, g0, w_in, w_decay, b_decay, gla_norm, rpb):
    d = x2.shape[1]
    n_gla = 2 * GLA_HEADS * GLA_DK + 2 * GLA_HEADS * GLA_DV
    n_lr = 2 * GLA_LOWRANK
    w_main = jnp.concatenate([w_in[:, :n_gla], w_in[:, n_gla + n_lr:]], axis=1).astype(BF16)
    w_small = _pad_cols(w_in[:, n_gla:n_gla + n_lr], LANES).astype(BF16)
    proj, small = _inproj(x2, g0, w_main, w_small)
    proj3 = proj.reshape(bn, seq, -1)
    small3 = small.reshape(bn, seq, -1)
    wdec = jnp.zeros((2, LANES, w_decay.shape[-1]), F32)
    for dd in range(2):
        wdec = wdec.at[dd, dd * GLA_LOWRANK:(dd + 1) * GLA_LOWRANK].set(w_decay[dd])
    o_a = _gla(proj3, small3, wdec, b_decay, gla_norm[None, :])
    o_b = _na(proj3, _na_bias_table(rpb), n_gla)
    return o_a.reshape(-1, o_a.shape[-1]), o_b.reshape(-1, o_b.shape[-1])


def _per_group(v):
    return v.reshape(2, SSD_GROUPS, SSD_JH).transpose(1, 0, 2).reshape(SSD_GROUPS, 2 * SSD_JH)


def _odd_mixers(x2, bn, seq, g0, w_in, conv_w, conv_b, dt_bias, a_log, d_skip, sink, cos, sin):
    conv_dim = SSD_INNER + 2 * SSD_GROUPS * SSD_STATE
    n_z = SSD_INNER
    n_dt = 2 * SSD_HEADS
    c_dt = n_z + conv_dim
    w_main = jnp.concatenate([w_in[:, :c_dt], w_in[:, c_dt + n_dt:]], axis=1).astype(BF16)
    w_dt = w_in[:, c_dt:c_dt + n_dt].reshape(-1, 2, SSD_GROUPS, SSD_JH).transpose(0, 2, 1, 3)
    w_dt = w_dt.reshape(-1, SSD_GROUPS, 2 * SSD_JH)
    w_small = jnp.pad(w_dt, ((0, 0), (0, 0), (0, LANES - 2 * SSD_JH))).reshape(-1, SSD_GROUPS * LANES)
    proj, small = _inproj(x2, g0, w_main, w_small.astype(BF16))
    proj3 = proj.reshape(bn, seq, -1)
    dt3 = small.reshape(bn, seq, SSD_GROUPS * LANES)
    xbc_act = _conv(proj3, conv_w, conv_b[None, :], n_z)
    pbg = _per_group(dt_bias)
    alg = _per_group(a_log)
    head_of_ch = np.arange(SSD_JH * SSD_HEADDIM) // SSD_HEADDIM
    expand = jnp.asarray(np.arange(2 * SSD_JH)[None, :, None]
                         == (np.arange(2)[:, None, None] * SSD_JH + head_of_ch[None, None, :]), F32)
    dskip = jnp.repeat(d_skip.astype(F32), SSD_HEADDIM)[None, :]
    y_pre = _ssd(xbc_act, dt3, pbg[:, :, None], alg[:, :, None], dskip, expand)
    o_d = _swa(proj3, sink, cos, sin, c_dt)
    return y_pre.reshape(-1, SSD_INNER), proj, o_d.reshape(-1, o_d.shape[-1])


def kernel(x, norm_gains, ffn_w_gate, ffn_w_up, ffn_w_down, even_w_in, even_w_out, gla_w_decay,
           gla_b_decay, gla_norm, na_rpb, odd_w_in, odd_w_out, ssd_conv_w, ssd_conv_b, ssd_dt_bias,
           ssd_a_log, ssd_d, ssd_norm, swa_sink):
    bn, seq, d = x.shape
    depth = norm_gains.shape[0]
    x2 = x.reshape(bn * seq, d)
    cos, sin = _rope_tables(seq)
    wg = ffn_w_gate.astype(BF16)
    wu = ffn_w_up.astype(BF16)
    wd = ffn_w_down.astype(BF16)
    even_w_out_b = even_w_out.astype(BF16)
    odd_w_out_b = odd_w_out.astype(BF16)
    for layer in range(depth):
        g = norm_gains[layer][:, None, :]
        i = layer // 2
        if layer % 2 == 0:
            o_a, o_b = _even_mixers(x2, bn, seq, g[0], even_w_in[i], gla_w_decay[i], gla_b_decay[i],
                                    gla_norm[i], na_rpb[i])
            x2 = _outproj(x2, o_a, o_b, even_w_out_b, i, g[1])
        else:
            y_pre, proj, o_d = _odd_mixers(x2, bn, seq, g[0], odd_w_in[i], ssd_conv_w[i], ssd_conv_b[i],
                                           ssd_dt_bias[i], ssd_a_log[i], ssd_d[i], swa_sink[i], cos, sin)
            x2 = _outproj(x2, y_pre, o_d, odd_w_out_b, i, g[1],
                          z_src=proj, ssd_gain=ssd_norm[i][None, :])
        x2 = _ffn(x2, g[2], wg, wu, wd, g[3], layer)
    return x2.reshape(bn, seq, d)
```

```python
import functools
import math

import jax
import jax.numpy as jnp
import numpy as np
from jax import lax
from jax.experimental import pallas as pl
from jax.experimental.pallas import tpu as pltpu

F32 = jnp.float32
BF16 = jnp.bfloat16
HIGHEST = lax.Precision.HIGHEST

EPS = 1e-6
ROPE_THETA = 10000.0
GRID_W = 64
LANES = 128

GLA_DK = 64
GLA_DV = 128
GLA_HEADS = 8
GLA_LOWRANK = 16
GLA_TAU = 16.0
GLA_CHUNK = 64
GLA_SUB = 16
GLA_EXP_CLAMP = 60.0
GLA_TRI_BLOCK = 256

NA_DH = 128
NA_HEADS = 8
NA_KH = 8
NA_KW = 16
NA_ROWS_PER_ITER = 8
NEG_BIG = -1e30

SSD_INNER = 1024
SSD_HEADDIM = 64
SSD_HEADS = 16
SSD_GROUPS = 2
SSD_STATE = 128
SSD_CONV = 5
SSD_CHUNK = 128
SSD_JH = SSD_HEADS // SSD_GROUPS
SSD_ROW_SLAB = 64

SWA_DH = 128
SWA_HEADS = 8
SWA_KV = 2
SWA_GQ = SWA_HEADS // SWA_KV
SWA_WINDOW = 128
SWA_BLOCK = 128

VMEM_LIMIT = 52 * 1024 * 1024


FFN_NORM_ROWS = 256
FFN_VMEM_LIMIT = 58 * 1024 * 1024


def _cparams(sem, vmem_limit=VMEM_LIMIT):
    return pltpu.CompilerParams(dimension_semantics=sem, vmem_limit_bytes=vmem_limit)


def _rms(x, g):
    return x * lax.rsqrt(jnp.mean(x * x, axis=-1, keepdims=True) + EPS) * g


def _silu(x):
    return x * (1.0 / (1.0 + jnp.exp(-x)))


def _dot_t(a, b):
    return lax.dot_general(a, b, (((1,), (1,)), ((), ())), preferred_element_type=F32)


def _inproj_body(x_ref, g_ref, w_ref, ws_ref, o_ref, os_ref, h_scr):
    j = pl.program_id(1)
    half = x_ref.shape[0] // 2

    @pl.when(j == 0)
    def _():
        for r0 in (0, half):
            rs = pl.ds(r0, half)
            h = _rms(x_ref[rs, :], g_ref[...]).astype(BF16)
            h_scr[rs, :] = h
            os_ref[rs, :] = jnp.dot(h, ws_ref[...], preferred_element_type=F32)
            o_ref[rs, :] = jnp.dot(h, w_ref[...], preferred_element_type=F32).astype(o_ref.dtype)

    @pl.when(j > 0)
    def _():
        o_ref[...] = jnp.dot(h_scr[...], w_ref[...], preferred_element_type=F32).astype(o_ref.dtype)


def _inproj(x2, g, w_main, w_small, tm=1024, tn=2048):
    n, d = x2.shape
    nm = w_main.shape[1]
    ns = w_small.shape[1]
    return pl.pallas_call(
        _inproj_body,
        grid=(n // tm, nm // tn),
        in_specs=[
            pl.BlockSpec((tm, d), lambda i, j: (i, 0)),
            pl.BlockSpec((1, d), lambda i, j: (0, 0)),
            pl.BlockSpec((d, tn), lambda i, j: (0, j)),
            pl.BlockSpec((d, ns), lambda i, j: (0, 0)),
        ],
        out_specs=[
            pl.BlockSpec((tm, tn), lambda i, j: (i, j)),
            pl.BlockSpec((tm, ns), lambda i, j: (i, 0)),
        ],
        out_shape=[jax.ShapeDtypeStruct((n, nm), BF16), jax.ShapeDtypeStruct((n, ns), F32)],
        scratch_shapes=[pltpu.VMEM((tm, d), BF16)],
        compiler_params=_cparams(("parallel", "arbitrary")),
        name="inproj",
    )(x2, g, w_main, w_small)


def _outproj_even_body(a_ref, b_ref, wa_ref, wb_ref, g_ref, x_ref, o_ref):
    mix = (jnp.dot(a_ref[...], wa_ref[...], preferred_element_type=F32)
           + jnp.dot(b_ref[...], wb_ref[...], preferred_element_type=F32))
    o_ref[...] = x_ref[...] + _rms(mix, g_ref[...])


def _outproj_odd_body(y_ref, z_ref, ng_ref, b_ref, wa_ref, wb_ref, g_ref, x_ref, o_ref):
    y = y_ref[...].astype(F32) * _silu(z_ref[...].astype(F32))
    a = _rms(y, ng_ref[...]).astype(BF16)
    mix = (jnp.dot(a, wa_ref[...], preferred_element_type=F32)
           + jnp.dot(b_ref[...], wb_ref[...], preferred_element_type=F32))
    o_ref[...] = x_ref[...] + _rms(mix, g_ref[...])


def _outproj(x2, a, b, w_out, li, g, z_src=None, ssd_gain=None, tm=512):
    n, d = x2.shape
    half = a.shape[1]
    row = lambda i: (i, 0)
    fixed = lambda i: (0, 0)
    w_specs = [pl.BlockSpec((None, half, d), lambda i: (li, 0, 0)),
               pl.BlockSpec((None, half, d), lambda i: (li, 1, 0))]
    tail_specs = [pl.BlockSpec((1, d), fixed), pl.BlockSpec((tm, d), row)]
    if z_src is None:
        body = _outproj_even_body
        in_specs = [pl.BlockSpec((tm, half), row), pl.BlockSpec((tm, half), row)] + w_specs + tail_specs
        args = (a, b, w_out, w_out, g, x2)
    else:
        body = _outproj_odd_body
        in_specs = ([pl.BlockSpec((tm, half), row), pl.BlockSpec((tm, half), row),
                     pl.BlockSpec((1, half), fixed), pl.BlockSpec((tm, half), row)]
                    + w_specs + tail_specs)
        args = (a, z_src, ssd_gain, b, w_out, w_out, g, x2)
    return pl.pallas_call(
        body,
        grid=(n // tm,),
        in_specs=in_specs,
        out_specs=pl.BlockSpec((tm, d), row),
        out_shape=jax.ShapeDtypeStruct((n, d), F32),
        compiler_params=_cparams(("parallel",)),
        name="outproj",
    )(*args)


def _ffn_body(x_ref, g2_ref, wg_ref, wu_ref, wd_ref, g3_ref, o_ref, h_scr):
    k = pl.program_id(1)
    last = pl.num_programs(1) - 1
    half = x_ref.shape[0] // 2

    def pieces(r0):
        return [pl.ds(r0 + p, FFN_NORM_ROWS) for p in range(0, half, FFN_NORM_ROWS)]

    def partial_down(rs):
        h = h_scr[rs, :]
        gate = jnp.dot(h, wg_ref[...], preferred_element_type=F32)
        up = jnp.dot(h, wu_ref[...], preferred_element_type=F32)
        return jnp.dot((_silu(gate) * up).astype(BF16), wd_ref[...], preferred_element_type=F32)

    @pl.when(k == 0)
    def _():
        for r0 in (0, half):
            for ps in pieces(r0):
                h_scr[ps, :] = _rms(x_ref[ps, :], g2_ref[...]).astype(BF16)
            o_ref[pl.ds(r0, half), :] = partial_down(pl.ds(r0, half))

    @pl.when(jnp.logical_and(k > 0, k < last))
    def _():
        for r0 in (0, half):
            o_ref[pl.ds(r0, half), :] += partial_down(pl.ds(r0, half))

    @pl.when(k == last)
    def _():
        for r0 in (0, half):
            o_ref[pl.ds(r0, half), :] += partial_down(pl.ds(r0, half))
            for ps in pieces(r0):
                o_ref[ps, :] = x_ref[ps, :] + _rms(o_ref[ps, :], g3_ref[...])


def _ffn(x2, g2, wg, wu, wd, g3, layer, tm=1024, th=512):
    n, d = x2.shape
    hid = wd.shape[1]
    return pl.pallas_call(
        _ffn_body,
        grid=(n // tm, hid // th),
        in_specs=[
            pl.BlockSpec((tm, d), lambda i, k: (i, 0)),
            pl.BlockSpec((1, d), lambda i, k: (0, 0)),
            pl.BlockSpec((None, d, th), lambda i, k: (layer, 0, k)),
            pl.BlockSpec((None, d, th), lambda i, k: (layer, 0, k)),
            pl.BlockSpec((None, th, d), lambda i, k: (layer, k, 0)),
            pl.BlockSpec((1, d), lambda i, k: (0, 0)),
        ],
        out_specs=pl.BlockSpec((tm, d), lambda i, k: (i, 0)),
        out_shape=jax.ShapeDtypeStruct((n, d), F32),
        scratch_shapes=[pltpu.VMEM((tm, d), BF16)],
        compiler_params=_cparams(("parallel", "arbitrary"), FFN_VMEM_LIMIT),
        name="ffn",
    )(x2, g2, wg, wu, wd, g3)


def _gla_body(q_ref, k_ref, v_ref, g_ref, lr_ref, wd_ref, bd_ref, gn_ref, o_ref,
              bc_scr, o_scr, st_scr, *, seq):
    c = GLA_CHUNK
    nc = seq // c
    half = nc // 2
    nsub = c // GLA_SUB
    rb = min(512, seq)
    row_i = lax.broadcasted_iota(jnp.int32, (c, c), 0)
    col_i = lax.broadcasted_iota(jnp.int32, (c, c), 1)
    tri = (row_i >= col_i, row_i <= col_i)
    lane = lax.broadcasted_iota(jnp.int32, (1, LANES), 1)
    hmask = (lane < GLA_DK, lane >= GLA_DK)
    own_head = ((lax.broadcasted_iota(jnp.int32, (2 * GLA_DV, LANES), 0) < GLA_DV)
                == (lax.broadcasted_iota(jnp.int32, (2 * GLA_DV, LANES), 1) < GLA_DK))

    w2 = jnp.concatenate([wd_ref[0], wd_ref[1]], axis=1)
    w_hi = w2.astype(BF16)
    w_lo = (w2 - w_hi.astype(F32)).astype(BF16)
    w_cat = jnp.concatenate([w_hi, w_lo, w_hi], axis=0)
    tb = GLA_TRI_BLOCK
    brow = lax.broadcasted_iota(jnp.int32, (tb, tb), 0)
    bcol = lax.broadcasted_iota(jnp.int32, (tb, tb), 1)
    same_chunk = (brow // c) == (bcol // c)
    blk_tri = ((same_chunk & (brow >= bcol)).astype(BF16), (same_chunk & (brow <= bcol)).astype(BF16))

    def pre(i, carry):
        r0 = pl.multiple_of(i * rb, rb)
        lr = lr_ref[0, pl.ds(r0, rb), :]
        lr_hi = lr.astype(BF16)
        lr_lo = (lr - lr_hi.astype(F32)).astype(BF16)
        z2 = jnp.dot(jnp.concatenate([lr_hi, lr_hi, lr_lo], axis=1), w_cat, preferred_element_type=F32)
        las = []
        for d in range(2):
            z = z2[:, LANES * d:LANES * (d + 1)] + bd_ref[d:d + 1, :]
            la = (jnp.minimum(z, 0.0) - jnp.log(1.0 + jnp.exp(-jnp.abs(z)))) * (1.0 / GLA_TAU)
            la_hi = la.astype(BF16)
            las.append(jnp.concatenate([la_hi, (la - la_hi.astype(F32)).astype(BF16)], axis=1))
        for d in range(2):
            for t in range(rb // tb):
                s2 = jnp.dot(blk_tri[d], las[d][t * tb:(t + 1) * tb], preferred_element_type=F32)
                bc_scr[d, pl.ds(r0 + t * tb, tb), :] = s2[:, :LANES] + s2[:, LANES:]
        return carry

    lax.fori_loop(0, seq // rb, pre, 0)

    def stage1(n, d):
        rows = pl.ds(pl.multiple_of(n * c, c), c)
        bc = bc_scr[d, rows, :]
        q = q_ref[0, rows, :].astype(F32) * (GLA_DK ** -0.5)
        k = k_ref[0, rows, :].astype(F32)
        v = v_ref[0, rows, :]
        b_edge = bc[c - 1:c] if d == 0 else bc[0:1]
        qhat = q * jnp.exp(bc)
        khat = (k * jnp.exp(b_edge - bc)).astype(BF16)
        att_rows = ([], [])
        for i in range(nsub):
            lo = GLA_SUB * i
            hi = lo + GLA_SUB
            if d == 0:
                ref = bc[lo - 1:lo] if i > 0 else jnp.zeros((1, LANES), F32)
            else:
                ref = bc[hi:hi + 1] if i < nsub - 1 else jnp.zeros((1, LANES), F32)
            qi = q[lo:hi] * jnp.exp(bc[lo:hi] - ref)
            ki = (k * jnp.exp(jnp.minimum(ref - bc, GLA_EXP_CLAMP))).astype(BF16)
            q2 = jnp.concatenate([jnp.where(hmask[0], qi, 0.0), jnp.where(hmask[1], qi, 0.0)], axis=0)
            a2 = _dot_t(q2.astype(BF16), ki)
            att_rows[0].append(a2[:GLA_SUB])
            att_rows[1].append(a2[GLA_SUB:])
        kv2 = lax.dot_general(v, khat, (((0,), (0,)), ((), ())), preferred_element_type=F32)
        kv2 = jnp.where(own_head, kv2, 0.0)
        att2 = jnp.concatenate(
            [jnp.where(tri[d], jnp.concatenate(att_rows[hh], axis=0), 0.0).astype(BF16)
             for hh in range(2)], axis=1)
        zero_v = jnp.zeros((c, GLA_DV), BF16)
        v_blockdiag = jnp.concatenate(
            [jnp.concatenate([v[:, :GLA_DV], zero_v], axis=1),
             jnp.concatenate([zero_v, v[:, GLA_DV:]], axis=1)], axis=0)
        return d, att2, v_blockdiag, kv2, qhat.astype(BF16), jnp.exp(b_edge)

    def stage2(s1, states):
        d, att2, v_blockdiag, kv2, qhat, decay = s1
        st = states[d]
        o = jnp.dot(att2, v_blockdiag, preferred_element_type=F32) + _dot_t(qhat, st.astype(BF16))
        states[d] = st * decay + kv2
        return o

    def finish(n, o):
        rows = pl.ds(pl.multiple_of(n * c, c), c)
        parts = []
        for hh in range(2):
            oh = o[:, GLA_DV * hh:GLA_DV * (hh + 1)]
            parts.append(oh * lax.rsqrt(jnp.mean(oh * oh, axis=-1, keepdims=True) + EPS))
        o = jnp.concatenate(parts, axis=1) * gn_ref[...]
        o_ref[0, rows, :] = (o * _silu(g_ref[0, rows, :].astype(F32))).astype(o_ref.dtype)

    st_scr[...] = jnp.zeros_like(st_scr)

    cpi = next(u for u in (4, 2, 1) if half % u == 0)

    def walk(i, base_f, base_b):
        work = []
        for u in range(cpi):
            work.append((base_f + i * cpi + u, 0))
            work.append((base_b - i * cpi - u, 1))
        prepared = [stage1(n, d) for n, d in work]
        states = [st_scr[d] for d in range(2)]
        outs = [(n, stage2(s1, states)) for (n, _), s1 in zip(work, prepared)]
        for d in range(2):
            st_scr[d] = states[d]
        return outs

    def first(i, carry):
        for n, o in walk(i, 0, nc - 1):
            o_scr[pl.ds(pl.multiple_of(n * c, c), c), :] = o
        return carry

    lax.fori_loop(0, half // cpi, first, 0)

    def second(i, carry):
        for n, o in walk(i, half, half - 1):
            finish(n, o_scr[pl.ds(pl.multiple_of(n * c, c), c), :] + o)
        return carry

    lax.fori_loop(0, half // cpi, second, 0)


def _gla(proj3, small3, wdec_pad, bdec, gnorm):
    bn, seq, _ = proj3.shape
    npair = GLA_HEADS // 2
    qoff = 0
    koff = (GLA_HEADS * GLA_DK) // LANES
    voff = (2 * GLA_HEADS * GLA_DK) // (2 * GLA_DV)
    goff = voff + npair
    return pl.pallas_call(
        functools.partial(_gla_body, seq=seq),
        grid=(bn, npair),
        in_specs=[
            pl.BlockSpec((1, seq, LANES), lambda b, p: (b, 0, qoff + p)),
            pl.BlockSpec((1, seq, LANES), lambda b, p: (b, 0, koff + p)),
            pl.BlockSpec((1, seq, 2 * GLA_DV), lambda b, p: (b, 0, voff + p)),
            pl.BlockSpec((1, seq, 2 * GLA_DV), lambda b, p: (b, 0, goff + p)),
            pl.BlockSpec((1, seq, LANES), lambda b, p: (b, 0, 0)),
            pl.BlockSpec((2, LANES, LANES), lambda b, p: (0, 0, p)),
            pl.BlockSpec((2, LANES), lambda b, p: (0, p)),
            pl.BlockSpec((1, 2 * GLA_DV), lambda b, p: (0, p)),
        ],
        out_specs=pl.BlockSpec((1, seq, 2 * GLA_DV), lambda b, p: (b, 0, p)),
        out_shape=jax.ShapeDtypeStruct((bn, seq, GLA_HEADS * GLA_DV), BF16),
        scratch_shapes=[
            pltpu.VMEM((2, seq, LANES), F32),
            pltpu.VMEM((seq, 2 * GLA_DV), F32),
            pltpu.VMEM((2, 2 * GLA_DV, LANES), F32),
        ],
        compiler_params=_cparams(("parallel", "parallel")),
        name="gla",
    )(proj3, proj3, proj3, proj3, small3, wdec_pad, bdec, gnorm)


def _na_bias_table(rpb):
    nh, ndr, ndc = rpb.shape
    cols = np.arange(GRID_W)
    cs = np.clip(cols - NA_KW // 2, 0, GRID_W - NA_KW)
    valid = (cols[None, :] >= cs[:, None]) & (cols[None, :] < cs[:, None] + NA_KW)
    dc = cols[None, :] - cols[:, None] + NA_KW - 1
    onehot = (np.arange(ndc)[:, None, None] == dc[None]) & valid[None]
    toep = jnp.dot(rpb.reshape(nh * ndr, ndc).astype(F32),
                   jnp.asarray(onehot.reshape(ndc, -1), F32), precision=HIGHEST)
    toep = jnp.where(jnp.asarray(valid.reshape(1, -1)), toep, NEG_BIG)
    toep = toep.reshape(nh, ndr, GRID_W, GRID_W)
    t = jnp.stack([toep[:, off:off + NA_KH] for off in range(NA_KH)], axis=1)
    return t.transpose(0, 1, 3, 2, 4).reshape(nh, NA_KH, GRID_W, NA_KH * GRID_W)


def _na_body(q_ref, k_ref, v_ref, bias_ref, o_ref, *, nrows):
    w = GRID_W
    nk = NA_KH * w

    def group(i, carry):
        idx = []
        scores = []
        for u in range(NA_ROWS_PER_ITER):
            r = i * NA_ROWS_PER_ITER + u
            rs = jnp.clip(r - NA_KH // 2, 0, nrows - NA_KH)
            qrows = pl.ds(pl.multiple_of(r * w, w), w)
            krows = pl.ds(pl.multiple_of(rs * w, w), nk)
            idx.append((qrows, krows, rs - r + (NA_KH - 1)))
            scores.append(_dot_t(q_ref[0, qrows, :], k_ref[0, krows, :]))
        probs = []
        for (qrows, krows, off), s in zip(idx, scores):
            s = s * (NA_DH ** -0.5) + bias_ref[0, off]
            p = jnp.exp(s - jnp.max(s, axis=-1, keepdims=True))
            probs.append((p.astype(BF16), jnp.sum(p, axis=-1, keepdims=True)))
        for (qrows, krows, off), (p, l) in zip(idx, probs):
            o = jnp.dot(p, v_ref[0, krows, :], preferred_element_type=F32) / l
            o_ref[0, qrows, :] = o.astype(o_ref.dtype)
        return carry

    lax.fori_loop(0, nrows // NA_ROWS_PER_ITER, group, 0)


def _na(proj3, bias_tab, col0):
    bn, seq, _ = proj3.shape
    nrows = seq // GRID_W
    qoff = col0 // NA_DH
    koff = qoff + NA_HEADS
    voff = koff + NA_HEADS
    return pl.pallas_call(
        functools.partial(_na_body, nrows=nrows),
        grid=(bn, NA_HEADS),
        in_specs=[
            pl.BlockSpec((1, seq, NA_DH), lambda b, h: (b, 0, qoff + h)),
            pl.BlockSpec((1, seq, NA_DH), lambda b, h: (b, 0, koff + h)),
            pl.BlockSpec((1, seq, NA_DH), lambda b, h: (b, 0, voff + h)),
            pl.BlockSpec((1, NA_KH, GRID_W, NA_KH * GRID_W), lambda b, h: (h, 0, 0, 0)),
        ],
        out_specs=pl.BlockSpec((1, seq, NA_DH), lambda b, h: (b, 0, h)),
        out_shape=jax.ShapeDtypeStruct((bn, seq, NA_HEADS * NA_DH), BF16),
        compiler_params=_cparams(("parallel", "parallel")),
        name="na",
    )(proj3, proj3, proj3, bias_tab)


CONV_PAD = 16


def _conv_body(x_ref, w_ref, b_ref, o_ref, xp_scr, *, seq):
    rb = min(256, seq)
    ch = x_ref.shape[-1]
    zeros = jnp.zeros((CONV_PAD, ch), xp_scr.dtype)
    xp_scr[0:CONV_PAD, :] = zeros
    xp_scr[seq + CONV_PAD:seq + 2 * CONV_PAD, :] = zeros
    xp_scr[CONV_PAD:seq + CONV_PAD, :] = x_ref[0]

    def blk(i, carry):
        r0 = pl.multiple_of(i * rb, rb)
        xw = xp_scr[pl.ds(r0, rb + 2 * CONV_PAD), :].astype(F32)
        acc = jnp.zeros((rb, ch), F32) + b_ref[...]
        for j in range(SSD_CONV):
            s0 = CONV_PAD - SSD_CONV // 2 + j
            acc = acc + xw[s0:s0 + rb] * w_ref[j:j + 1, :]
        o_ref[0, pl.ds(r0, rb), :] = _silu(acc).astype(o_ref.dtype)
        return carry

    lax.fori_loop(0, seq // rb, blk, 0)


def _conv(proj3, conv_w, conv_b, col0, tc=512):
    bn, seq, _ = proj3.shape
    cdim = conv_w.shape[1]
    c0 = col0 // tc
    return pl.pallas_call(
        functools.partial(_conv_body, seq=seq),
        grid=(bn, cdim // tc),
        in_specs=[
            pl.BlockSpec((1, seq, tc), lambda b, j: (b, 0, c0 + j)),
            pl.BlockSpec((SSD_CONV, tc), lambda b, j: (0, j)),
            pl.BlockSpec((1, tc), lambda b, j: (0, j)),
        ],
        out_specs=pl.BlockSpec((1, seq, tc), lambda b, j: (b, 0, j)),
        out_shape=jax.ShapeDtypeStruct((bn, seq, cdim), BF16),
        scratch_shapes=[pltpu.VMEM((seq + 2 * CONV_PAD, tc), BF16)],
        compiler_params=_cparams(("parallel", "parallel")),
        name="ssd_conv",
    )(proj3, conv_w, conv_b)


def _softplus(x):
    return jnp.maximum(x, 0.0) + jnp.log(1.0 + jnp.exp(-jnp.abs(x)))


def _ssd_body(xs_ref, bm_ref, cm_ref, dt_ref, pb_ref, al_ref, dsk_ref, ex_ref, o_ref,
              st_scr, run_scr, dec_scr, *, seq):
    l = SSD_CHUNK
    nc = seq // l
    nh2 = 2 * SSD_JH
    row_i = lax.broadcasted_iota(jnp.int32, (l, l), 0)
    col_i = lax.broadcasted_iota(jnp.int32, (l, l), 1)
    lane = lax.broadcasted_iota(jnp.int32, (1, LANES), 1)
    first_head = lane < SSD_HEADDIM
    bwd_row = lax.broadcasted_iota(jnp.int32, (nh2, 1), 0) >= SSD_JH
    a_col = -jnp.exp(al_ref[0])

    lower = row_i > col_i
    diag = row_i == col_i
    triu_b = (row_i <= col_i).astype(BF16)

    def decays(n):
        rows = pl.ds(pl.multiple_of(n * l, l), l)
        dtv = _softplus(dt_ref[0, rows, :].T[0:nh2] + pb_ref[0])
        da = dtv * a_col
        hi = da.astype(BF16)
        r1 = da - hi.astype(F32)
        mid = r1.astype(BF16)
        lo = (r1 - mid.astype(F32)).astype(BF16)
        p3 = jnp.dot(jnp.concatenate([hi, mid, lo], axis=0), triu_b, preferred_element_type=F32)
        pre = p3[0:nh2] + p3[nh2:2 * nh2] + p3[2 * nh2:]
        total = pre[:, l - 1:l]
        acum = jnp.where(bwd_row, total - pre + da, pre)
        dec_scr[0, n] = dtv
        dec_scr[1, n] = acum
        dec_scr[2, n] = dtv * jnp.exp(total - acum)
        dec_scr[3, n] = jnp.broadcast_to(jnp.exp(total), (nh2, l))

    def chunk_state(n, d):
        rows = pl.ds(pl.multiple_of(n * l, l), l)
        w_state_t = dec_scr[2, n]
        x = xs_ref[0, rows, :]
        bmt = bm_ref[0, rows, :].astype(F32).T
        edge_ch = jnp.sum(dec_scr[3, n][:, 0:1] * ex_ref[d], axis=0, keepdims=True)
        new_st = []
        ns = SSD_STATE
        for jj in range(SSD_JH // 2):
            xp = x[:, LANES * jj:LANES * (jj + 1)]
            bw2 = jnp.concatenate(
                [(bmt * w_state_t[d * SSD_JH + 2 * jj + hh:d * SSD_JH + 2 * jj + hh + 1, :]).astype(BF16)
                 for hh in range(2)], axis=0)
            s2 = jnp.dot(bw2, xp, preferred_element_type=F32)
            new_st.append(jnp.where(first_head, s2[:ns], s2[ns:]))
        st = run_scr[d]
        st_scr[d, n] = st.astype(BF16)
        run_scr[d] = st * edge_ch + jnp.concatenate(new_st, axis=1)

    def chunk_out(n):
        rows = pl.ds(pl.multiple_of(n * l, l), l)
        dtv_t = dec_scr[0, n]
        acum_t = dec_scr[1, n]
        acum = acum_t.T
        x = xs_ref[0, rows, :]
        cm = cm_ref[0, rows, :]
        cmf = cm.astype(F32)
        cb = _dot_t(cm, bm_ref[0, rows, :])
        st_f = st_scr[0, n]
        st_b = st_scr[1, n]
        rhs = [jnp.concatenate([x[:, LANES * jj:LANES * (jj + 1)],
                                st_f[:, LANES * jj:LANES * (jj + 1)],
                                st_b[:, LANES * jj:LANES * (jj + 1)]], axis=0)
               for jj in range(SSD_JH // 2)]
        slabs = [slice(r0, r0 + SSD_ROW_SLAB) for r0 in range(0, l, SSD_ROW_SLAB)]
        ys = [[] for _ in slabs]
        for jj in range(SSD_JH // 2):
            lhs = []
            for rs in slabs:
                for hh in range(2):
                    cf = 2 * jj + hh
                    cr = SSD_JH + cf
                    col_f = jnp.broadcast_to(acum[rs, cf:cf + 1], (SSD_ROW_SLAB, l))
                    col_r = jnp.broadcast_to(acum[rs, cr:cr + 1], (SSD_ROW_SLAB, l))
                    seg = jnp.where(lower[rs], col_f - acum_t[cf:cf + 1, :],
                                    col_r - acum_t[cr:cr + 1, :])
                    wgt = (jnp.where(lower[rs], dtv_t[cf:cf + 1, :], dtv_t[cr:cr + 1, :])
                           + jnp.where(diag[rs], dtv_t[cf:cf + 1, :], 0.0))
                    m = cb[rs] * jnp.exp(jnp.minimum(seg, 0.0)) * wgt
                    lhs.append(jnp.concatenate([m.astype(BF16),
                                                (cmf[rs] * jnp.exp(col_f)).astype(BF16),
                                                (cmf[rs] * jnp.exp(col_r)).astype(BF16)], axis=1))
            y4 = jnp.dot(jnp.concatenate(lhs, axis=0), rhs[jj], preferred_element_type=F32)
            for si in range(len(slabs)):
                r0 = 2 * si * SSD_ROW_SLAB
                ys[si].append(jnp.where(first_head, y4[r0:r0 + SSD_ROW_SLAB],
                                        y4[r0 + SSD_ROW_SLAB:r0 + 2 * SSD_ROW_SLAB]))
        for rs, y_parts in zip(slabs, ys):
            y = jnp.concatenate(y_parts, axis=1) + x[rs].astype(F32) * dsk_ref[...]
            o_ref[0, pl.ds(pl.multiple_of(n * l, l) + rs.start, SSD_ROW_SLAB), :] = y.astype(o_ref.dtype)

    def prep(i, carry):
        decays(i)
        return carry

    lax.fori_loop(0, nc, prep, 0, unroll=2)
    run_scr[...] = jnp.zeros_like(run_scr)

    def states(i, carry):
        chunk_state(i, 0)
        chunk_state(nc - 1 - i, 1)
        return carry

    lax.fori_loop(0, nc, states, 0)

    def outputs(n, carry):
        chunk_out(n)
        return carry

    lax.fori_loop(0, nc, outputs, 0)


def _ssd(xbc_act, dt3, pb, al, dskip, expand):
    bn, seq, _ = xbc_act.shape
    gw = SSD_JH * SSD_HEADDIM
    boff = SSD_INNER // SSD_STATE
    coff = boff + SSD_GROUPS
    return pl.pallas_call(
        functools.partial(_ssd_body, seq=seq),
        grid=(bn, SSD_GROUPS),
        in_specs=[
            pl.BlockSpec((1, seq, gw), lambda b, g: (b, 0, g)),
            pl.BlockSpec((1, seq, SSD_STATE), lambda b, g: (b, 0, boff + g)),
            pl.BlockSpec((1, seq, SSD_STATE), lambda b, g: (b, 0, coff + g)),
            pl.BlockSpec((1, seq, LANES), lambda b, g: (b, 0, g)),
            pl.BlockSpec((1, 2 * SSD_JH, 1), lambda b, g: (g, 0, 0)),
            pl.BlockSpec((1, 2 * SSD_JH, 1), lambda b, g: (g, 0, 0)),
            pl.BlockSpec((1, gw), lambda b, g: (0, g)),
            pl.BlockSpec((2, 2 * SSD_JH, gw), lambda b, g: (0, 0, 0)),
        ],
        out_specs=pl.BlockSpec((1, seq, gw), lambda b, g: (b, 0, g)),
        out_shape=jax.ShapeDtypeStruct((bn, seq, SSD_INNER), BF16),
        scratch_shapes=[pltpu.VMEM((2, seq // SSD_CHUNK, SSD_STATE, gw), BF16),
                        pltpu.VMEM((2, SSD_STATE, gw), F32),
                        pltpu.VMEM((4, seq // SSD_CHUNK, 2 * SSD_JH, SSD_CHUNK), F32)],
        compiler_params=_cparams(("parallel", "parallel")),
        name="ssd_scan",
    )(xbc_act, xbc_act, xbc_act, dt3, pb, al, dskip, expand)


def _rope_tables(seq):
    half = SWA_DH // 2
    inv = ROPE_THETA ** (-np.arange(half, dtype=np.float64) / half)
    ang = np.arange(seq, dtype=np.float64)[:, None] * inv[None, :]
    cos, sin = np.cos(ang), np.sin(ang)
    return (jnp.asarray(np.concatenate([cos, cos], axis=1), F32),
            jnp.asarray(np.concatenate([-sin, sin], axis=1), F32))


def _rope(x, cos, sin_signed):
    return x * cos + pltpu.roll(x, SWA_DH // 2, 1) * sin_signed


def _swa_body(sink_ref, q_ref, k_ref, v_ref, cos_ref, sin_ref, o_ref, kr_scr, *, seq):
    wb = SWA_BLOCK
    nb = seq // wb
    nkeys = 3 * wb
    kvh = pl.program_id(1)
    rb = min(512, seq)

    def krope(i, carry):
        rows = pl.ds(pl.multiple_of(i * rb, rb), rb)
        kr_scr[rows, :] = _rope(k_ref[0, rows, :].astype(F32), cos_ref[rows, :],
                                sin_ref[rows, :]).astype(BF16)
        return carry

    lax.fori_loop(0, seq // rb, krope, 0)

    qpos_l = lax.broadcasted_iota(jnp.int32, (wb, nkeys), 0)
    kpos_l = lax.broadcasted_iota(jnp.int32, (wb, nkeys), 1)

    def blk(n, carry):
        rows = pl.ds(pl.multiple_of(n * wb, wb), wb)
        ks = pl.multiple_of(jnp.clip(n - 1, 0, nb - 3) * wb, wb)
        krows = pl.ds(ks, nkeys)
        cos = cos_ref[rows, :]
        sin = sin_ref[rows, :]
        kk = kr_scr[krows, :]
        vv = v_ref[0, krows, :]
        valid = jnp.abs((qpos_l + n * wb) - (kpos_l + ks)) <= SWA_WINDOW
        qb = q_ref[0, rows, :].astype(F32)
        q4 = jnp.concatenate(
            [(_rope(qb[:, SWA_DH * hh:SWA_DH * (hh + 1)], cos, sin) * (SWA_DH ** -0.5)).astype(BF16)
             for hh in range(SWA_GQ)], axis=0)
        s4 = _dot_t(q4, kk)
        probs = []
        dens = []
        for hh in range(SWA_GQ):
            sink = sink_ref[kvh * SWA_GQ + hh]
            s = jnp.where(valid, s4[wb * hh:wb * (hh + 1)], NEG_BIG)
            m = jnp.maximum(jnp.max(s, axis=-1, keepdims=True), sink)
            p = jnp.exp(s - m)
            probs.append(p.astype(BF16))
            dens.append(jnp.sum(p, axis=-1, keepdims=True) + jnp.exp(sink - m))
        o4 = jnp.dot(jnp.concatenate(probs, axis=0), vv, preferred_element_type=F32)
        for hh in range(SWA_GQ):
            o = o4[wb * hh:wb * (hh + 1)] / dens[hh]
            o_ref[0, rows, SWA_DH * hh:SWA_DH * (hh + 1)] = o.astype(o_ref.dtype)
        return carry

    lax.fori_loop(0, nb, blk, 0)


def _swa(proj3, sink, cos, sin, col0):
    bn, seq, _ = proj3.shape
    qw = SWA_GQ * SWA_DH
    qoff = col0 // qw
    koff = (col0 + SWA_HEADS * SWA_DH) // SWA_DH
    voff = koff + SWA_KV
    return pl.pallas_call(
        functools.partial(_swa_body, seq=seq),
        grid=(bn, SWA_KV),
        in_specs=[
            pl.BlockSpec(memory_space=pltpu.SMEM),
            pl.BlockSpec((1, seq, qw), lambda b, h: (b, 0, qoff + h)),
            pl.BlockSpec((1, seq, SWA_DH), lambda b, h: (b, 0, koff + h)),
            pl.BlockSpec((1, seq, SWA_DH), lambda b, h: (b, 0, voff + h)),
            pl.BlockSpec((seq, SWA_DH), lambda b, h: (0, 0)),
            pl.BlockSpec((seq, SWA_DH), lambda b, h: (0, 0)),
        ],
        out_specs=pl.BlockSpec((1, seq, qw), lambda b, h: (b, 0, h)),
        out_shape=jax.ShapeDtypeStruct((bn, seq, SWA_HEADS * SWA_DH), BF16),
        scratch_shapes=[pltpu.VMEM((seq, SWA_DH), BF16)],
        compiler_params=_cparams(("parallel", "parallel")),
        name="swa",
    )(sink, proj3, proj3, proj3, cos, sin)


def _pad_cols(w, width):
    return jnp.pad(w, ((0, 0), (0, width - w.shape[1])))


def _even_mixers(x2, bn, seq, g0, w_in, w_decay, b_decay, gla_norm, rpb):
    d = x2.shape[1]
    n_gla = 2 * GLA_HEADS * GLA_DK + 2 * GLA_HEADS * GLA_DV
    n_lr = 2 * GLA_LOWRANK
    w_main = jnp.concatenate([w_in[:, :n_gla], w_in[:, n_gla + n_lr:]], axis=1).astype(BF16)
    w_small = _pad_cols(w_in[:, n_gla:n_gla + n_lr], LANES).astype(BF16)
    proj, small = _inproj(x2, g0, w_main, w_small)
    proj3 = proj.reshape(bn, seq, -1)
    small3 = small.reshape(bn, seq, -1)
    wdec = jnp.zeros((2, LANES, w_decay.shape[-1]), F32)
    for dd in range(2):
        wdec = wdec.at[dd, dd * GLA_LOWRANK:(dd + 1) * GLA_LOWRANK].set(w_decay[dd])
    o_a = _gla(proj3, small3, wdec, b_decay, gla_norm[None, :])
    o_b = _na(proj3, _na_bias_table(rpb), n_gla)
    return o_a.reshape(-1, o_a.shape[-1]), o_b.reshape(-1, o_b.shape[-1])


def _per_group(v):
    return v.reshape(2, SSD_GROUPS, SSD_JH).transpose(1, 0, 2).reshape(SSD_GROUPS, 2 * SSD_JH)


def _odd_mixers(x2, bn, seq, g0, w_in, conv_w, conv_b, dt_bias, a_log, d_skip, sink, cos, sin):
    conv_dim = SSD_INNER + 2 * SSD_GROUPS * SSD_STATE
    n_z = SSD_INNER
    n_dt = 2 * SSD_HEADS
    c_dt = n_z + conv_dim
    w_main = jnp.concatenate([w_in[:, :c_dt], w_in[:, c_dt + n_dt:]], axis=1).astype(BF16)
    w_dt = w_in[:, c_dt:c_dt + n_dt].reshape(-1, 2, SSD_GROUPS, SSD_JH).transpose(0, 2, 1, 3)
    w_dt = w_dt.reshape(-1, SSD_GROUPS, 2 * SSD_JH)
    w_small = jnp.pad(w_dt, ((0, 0), (0, 0), (0, LANES - 2 * SSD_JH))).reshape(-1, SSD_GROUPS * LANES)
    proj, small = _inproj(x2, g0, w_main, w_small.astype(BF16))
    proj3 = proj.reshape(bn, seq, -1)
    dt3 = small.reshape(bn, seq, SSD_GROUPS * LANES)
    xbc_act = _conv(proj3, conv_w, conv_b[None, :], n_z)
    pbg = _per_group(dt_bias)
    alg = _per_group(a_log)
    head_of_ch = np.arange(SSD_JH * SSD_HEADDIM) // SSD_HEADDIM
    expand = jnp.asarray(np.arange(2 * SSD_JH)[None, :, None]
                         == (np.arange(2)[:, None, None] * SSD_JH + head_of_ch[None, None, :]), F32)
    dskip = jnp.repeat(d_skip.astype(F32), SSD_HEADDIM)[None, :]
    y_pre = _ssd(xbc_act, dt3, pbg[:, :, None], alg[:, :, None], dskip, expand)
    o_d = _swa(proj3, sink, cos, sin, c_dt)
    return y_pre.reshape(-1, SSD_INNER), proj, o_d.reshape(-1, o_d.shape[-1])


def kernel(x, norm_gains, ffn_w_gate, ffn_w_up, ffn_w_down, even_w_in, even_w_out, gla_w_decay,
           gla_b_decay, gla_norm, na_rpb, odd_w_in, odd_w_out, ssd_conv_w, ssd_conv_b, ssd_dt_bias,
           ssd_a_log, ssd_d, ssd_norm, swa_sink):
    bn, seq, d = x.shape
    depth = norm_gains.shape[0]
    x2 = x.reshape(bn * seq, d)
    cos, sin = _rope_tables(seq)
    wg = ffn_w_gate.astype(BF16)
    wu = ffn_w_up.astype(BF16)
    wd = ffn_w_down.astype(BF16)
    even_w_out_b = even_w_out.astype(BF16)
    odd_w_out_b = odd_w_out.astype(BF16)
    for layer in range(depth):
        g = norm_gains[layer][:, None, :]
        i = layer // 2
        if layer % 2 == 0:
            o_a, o_b = _even_mixers(x2, bn, seq, g[0], even_w_in[i], gla_w_decay[i], gla_b_decay[i],
                                    gla_norm[i], na_rpb[i])
            x2 = _outproj(x2, o_a, o_b, even_w_out_b, i, g[1])
        else:
            y_pre, proj, o_d = _odd_mixers(x2, bn, seq, g[0], odd_w_in[i], ssd_conv_w[i], ssd_conv_b[i],
                                           ssd_dt_bias[i], ssd_a_log[i], ssd_d[i], swa_sink[i], cos, sin)
            x2 = _outproj(x2, y_pre, o_d, odd_w_out_b, i, g[1],
                          z_src=proj, ssd_gain=ssd_norm[i][None, :])
        x2 = _ffn(x2, g[2], wg, wu, wd, g[3], layer)
    return x2.reshape(bn, seq, d)
```

```python
import functools

import jax
import jax.numpy as jnp
import numpy as np
from jax import lax
from jax.experimental import pallas as pl
from jax.experimental.pallas import tpu as pltpu

F32 = jnp.float32
BF16 = jnp.bfloat16
HIGHEST = lax.Precision.HIGHEST

EPS = 1e-6
ROPE_THETA = 10000.0
GRID_W = 64
LANES = 128

GLA_DK = 64
GLA_DV = 128
GLA_HEADS = 8
GLA_LOWRANK = 16
GLA_TAU = 16.0
GLA_CHUNK = 64
GLA_SUB = 16
GLA_EXP_CLAMP = 60.0
GLA_TRI_BLOCK = 256

NA_DH = 128
NA_HEADS = 8
NA_KH = 8
NA_KW = 16
NA_ROWS_PER_ITER = 16
NEG_BIG = -1e30

SSD_INNER = 1024
SSD_HEADDIM = 64
SSD_HEADS = 16
SSD_GROUPS = 2
SSD_STATE = 128
SSD_CONV = 5
SSD_CHUNK = 128
SSD_JH = SSD_HEADS // SSD_GROUPS
SSD_ROW_SLAB = 64

SWA_DH = 128
SWA_HEADS = 8
SWA_KV = 2
SWA_GQ = SWA_HEADS // SWA_KV
SWA_WINDOW = 128
SWA_BLOCK = 128

VMEM_LIMIT = 52 * 1024 * 1024


FFN_NORM_ROWS = 256
FFN_VMEM_LIMIT = 58 * 1024 * 1024


def _cparams(sem, vmem_limit=VMEM_LIMIT):
    return pltpu.CompilerParams(dimension_semantics=sem, vmem_limit_bytes=vmem_limit)


def _rms(x, g):
    return x * lax.rsqrt(jnp.mean(x * x, axis=-1, keepdims=True) + EPS) * g


def _silu(x):
    return x * (1.0 / (1.0 + jnp.exp(-x)))


def _dot_t(a, b):
    return lax.dot_general(a, b, (((1,), (1,)), ((), ())), preferred_element_type=F32)


def _inproj_body(x_ref, g_ref, w_ref, ws_ref, o_ref, os_ref, h_scr):
    j = pl.program_id(1)
    half = x_ref.shape[0] // 2

    @pl.when(j == 0)
    def _():
        for r0 in (0, half):
            rs = pl.ds(r0, half)
            h = _rms(x_ref[rs, :], g_ref[...]).astype(BF16)
            h_scr[rs, :] = h
            os_ref[rs, :] = jnp.dot(h, ws_ref[...], preferred_element_type=F32)
            o_ref[rs, :] = jnp.dot(h, w_ref[...], preferred_element_type=F32).astype(o_ref.dtype)

    @pl.when(j > 0)
    def _():
        o_ref[...] = jnp.dot(h_scr[...], w_ref[...], preferred_element_type=F32).astype(o_ref.dtype)


def _inproj(x2, g, w_main, w_small, tm=1024, tn=2048):
    n, d = x2.shape
    nm = w_main.shape[1]
    ns = w_small.shape[1]
    return pl.pallas_call(
        _inproj_body,
        grid=(n // tm, nm // tn),
        in_specs=[
            pl.BlockSpec((tm, d), lambda i, j: (i, 0)),
            pl.BlockSpec((1, d), lambda i, j: (0, 0)),
            pl.BlockSpec((d, tn), lambda i, j: (0, j)),
            pl.BlockSpec((d, ns), lambda i, j: (0, 0)),
        ],
        out_specs=[
            pl.BlockSpec((tm, tn), lambda i, j: (i, j)),
            pl.BlockSpec((tm, ns), lambda i, j: (i, 0)),
        ],
        out_shape=[jax.ShapeDtypeStruct((n, nm), BF16), jax.ShapeDtypeStruct((n, ns), F32)],
        scratch_shapes=[pltpu.VMEM((tm, d), BF16)],
        compiler_params=_cparams(("parallel", "arbitrary")),
        name="inproj",
    )(x2, g, w_main, w_small)


def _outproj_even_body(a_ref, b_ref, wa_ref, wb_ref, g_ref, x_ref, o_ref):
    mix = (jnp.dot(a_ref[...], wa_ref[...], preferred_element_type=F32)
           + jnp.dot(b_ref[...], wb_ref[...], preferred_element_type=F32))
    o_ref[...] = x_ref[...] + _rms(mix, g_ref[...])


def _outproj_odd_body(y_ref, z_ref, ng_ref, b_ref, wa_ref, wb_ref, g_ref, x_ref, o_ref):
    y = y_ref[...].astype(F32) * _silu(z_ref[...].astype(F32))
    a = _rms(y, ng_ref[...]).astype(BF16)
    mix = (jnp.dot(a, wa_ref[...], preferred_element_type=F32)
           + jnp.dot(b_ref[...], wb_ref[...], preferred_element_type=F32))
    o_ref[...] = x_ref[...] + _rms(mix, g_ref[...])


def _outproj(x2, a, b, w_out, li, g, z_src=None, ssd_gain=None, tm=512):
    n, d = x2.shape
    half = a.shape[1]
    row = lambda i: (i, 0)
    fixed = lambda i: (0, 0)
    w_specs = [pl.BlockSpec((None, half, d), lambda i: (li, 0, 0)),
               pl.BlockSpec((None, half, d), lambda i: (li, 1, 0))]
    tail_specs = [pl.BlockSpec((1, d), fixed), pl.BlockSpec((tm, d), row)]
    if z_src is None:
        body = _outproj_even_body
        in_specs = [pl.BlockSpec((tm, half), row), pl.BlockSpec((tm, half), row)] + w_specs + tail_specs
        args = (a, b, w_out, w_out, g, x2)
    else:
        body = _outproj_odd_body
        in_specs = ([pl.BlockSpec((tm, half), row), pl.BlockSpec((tm, half), row),
                     pl.BlockSpec((1, half), fixed), pl.BlockSpec((tm, half), row)]
                    + w_specs + tail_specs)
        args = (a, z_src, ssd_gain, b, w_out, w_out, g, x2)
    return pl.pallas_call(
        body,
        grid=(n // tm,),
        in_specs=in_specs,
        out_specs=pl.BlockSpec((tm, d), row),
        out_shape=jax.ShapeDtypeStruct((n, d), F32),
        compiler_params=_cparams(("parallel",)),
        name="outproj",
    )(*args)


def _ffn_body(x_ref, g2_ref, wg_ref, wu_ref, wd_ref, g3_ref, o_ref, h_scr):
    k = pl.program_id(1)
    last = pl.num_programs(1) - 1
    half = x_ref.shape[0] // 2

    def pieces(r0):
        return [pl.ds(r0 + p, FFN_NORM_ROWS) for p in range(0, half, FFN_NORM_ROWS)]

    def partial_down(rs):
        h = h_scr[rs, :]
        gate = jnp.dot(h, wg_ref[...], preferred_element_type=F32)
        up = jnp.dot(h, wu_ref[...], preferred_element_type=F32)
        return jnp.dot((_silu(gate) * up).astype(BF16), wd_ref[...], preferred_element_type=F32)

    @pl.when(k == 0)
    def _():
        for r0 in (0, half):
            for ps in pieces(r0):
                h_scr[ps, :] = _rms(x_ref[ps, :], g2_ref[...]).astype(BF16)
            o_ref[pl.ds(r0, half), :] = partial_down(pl.ds(r0, half))

    @pl.when(jnp.logical_and(k > 0, k < last))
    def _():
        for r0 in (0, half):
            o_ref[pl.ds(r0, half), :] += partial_down(pl.ds(r0, half))

    @pl.when(k == last)
    def _():
        for r0 in (0, half):
            o_ref[pl.ds(r0, half), :] += partial_down(pl.ds(r0, half))
            for ps in pieces(r0):
                o_ref[ps, :] = x_ref[ps, :] + _rms(o_ref[ps, :], g3_ref[...])


def _ffn(x2, g2, wg, wu, wd, g3, layer, tm=1024, th=512):
    n, d = x2.shape
    hid = wd.shape[1]
    return pl.pallas_call(
        _ffn_body,
        grid=(n // tm, hid // th),
        in_specs=[
            pl.BlockSpec((tm, d), lambda i, k: (i, 0)),
            pl.BlockSpec((1, d), lambda i, k: (0, 0)),
            pl.BlockSpec((None, d, th), lambda i, k: (layer, 0, k)),
            pl.BlockSpec((None, d, th), lambda i, k: (layer, 0, k)),
            pl.BlockSpec((None, th, d), lambda i, k: (layer, k, 0)),
            pl.BlockSpec((1, d), lambda i, k: (0, 0)),
        ],
        out_specs=pl.BlockSpec((tm, d), lambda i, k: (i, 0)),
        out_shape=jax.ShapeDtypeStruct((n, d), F32),
        scratch_shapes=[pltpu.VMEM((tm, d), BF16)],
        compiler_params=_cparams(("parallel", "arbitrary"), FFN_VMEM_LIMIT),
        name="ffn",
    )(x2, g2, wg, wu, wd, g3)


def _gla_body(q_ref, k_ref, v_ref, g_ref, lr_ref, wd_ref, bd_ref, gn_ref, o_ref,
              bc_scr, o_scr, st_scr, *, seq):
    c = GLA_CHUNK
    nc = seq // c
    half = nc // 2
    nsub = c // GLA_SUB
    rb = min(512, seq)
    row_i = lax.broadcasted_iota(jnp.int32, (c, c), 0)
    col_i = lax.broadcasted_iota(jnp.int32, (c, c), 1)
    tri = (row_i >= col_i, row_i <= col_i)
    lane = lax.broadcasted_iota(jnp.int32, (1, LANES), 1)
    hmask = (lane < GLA_DK, lane >= GLA_DK)
    own_head = ((lax.broadcasted_iota(jnp.int32, (2 * GLA_DV, LANES), 0) < GLA_DV)
                == (lax.broadcasted_iota(jnp.int32, (2 * GLA_DV, LANES), 1) < GLA_DK))

    w2 = jnp.concatenate([wd_ref[0], wd_ref[1]], axis=1)
    w_hi = w2.astype(BF16)
    w_lo = (w2 - w_hi.astype(F32)).astype(BF16)
    w_cat = jnp.concatenate([w_hi, w_lo, w_hi], axis=0)
    tb = GLA_TRI_BLOCK
    brow = lax.broadcasted_iota(jnp.int32, (tb, tb), 0)
    bcol = lax.broadcasted_iota(jnp.int32, (tb, tb), 1)
    same_chunk = (brow // c) == (bcol // c)
    blk_tri = ((same_chunk & (brow >= bcol)).astype(BF16), (same_chunk & (brow <= bcol)).astype(BF16))

    def pre(i, carry):
        r0 = pl.multiple_of(i * rb, rb)
        lr = lr_ref[0, pl.ds(r0, rb), :]
        lr_hi = lr.astype(BF16)
        lr_lo = (lr - lr_hi.astype(F32)).astype(BF16)
        z2 = jnp.dot(jnp.concatenate([lr_hi, lr_hi, lr_lo], axis=1), w_cat, preferred_element_type=F32)
        las = []
        for d in range(2):
            z = z2[:, LANES * d:LANES * (d + 1)] + bd_ref[d:d + 1, :]
            la = (jnp.minimum(z, 0.0) - jnp.log(1.0 + jnp.exp(-jnp.abs(z)))) * (1.0 / GLA_TAU)
            la_hi = la.astype(BF16)
            las.append(jnp.concatenate([la_hi, (la - la_hi.astype(F32)).astype(BF16)], axis=1))
        for d in range(2):
            for t in range(rb // tb):
                s2 = jnp.dot(blk_tri[d], las[d][t * tb:(t + 1) * tb], preferred_element_type=F32)
                bc_scr[d, pl.ds(r0 + t * tb, tb), :] = s2[:, :LANES] + s2[:, LANES:]
        return carry

    lax.fori_loop(0, seq // rb, pre, 0)

    def stage1(n, d):
        rows = pl.ds(pl.multiple_of(n * c, c), c)
        bc = bc_scr[d, rows, :]
        q = q_ref[0, rows, :].astype(F32) * (GLA_DK ** -0.5)
        k = k_ref[0, rows, :].astype(F32)
        v = v_ref[0, rows, :]
        b_edge = bc[c - 1:c] if d == 0 else bc[0:1]
        qhat = q * jnp.exp(bc)
        khat = (k * jnp.exp(b_edge - bc)).astype(BF16)
        att_rows = ([], [])
        for i in range(nsub):
            lo = GLA_SUB * i
            hi = lo + GLA_SUB
            if d == 0:
                ref = bc[lo - 1:lo] if i > 0 else jnp.zeros((1, LANES), F32)
            else:
                ref = bc[hi:hi + 1] if i < nsub - 1 else jnp.zeros((1, LANES), F32)
            qi = q[lo:hi] * jnp.exp(bc[lo:hi] - ref)
            ki = (k * jnp.exp(jnp.minimum(ref - bc, GLA_EXP_CLAMP))).astype(BF16)
            q2 = jnp.concatenate([jnp.where(hmask[0], qi, 0.0), jnp.where(hmask[1], qi, 0.0)], axis=0)
            a2 = _dot_t(q2.astype(BF16), ki)
            att_rows[0].append(a2[:GLA_SUB])
            att_rows[1].append(a2[GLA_SUB:])
        kv2 = lax.dot_general(v, khat, (((0,), (0,)), ((), ())), preferred_element_type=F32)
        kv2 = jnp.where(own_head, kv2, 0.0)
        att2 = jnp.concatenate(
            [jnp.where(tri[d], jnp.concatenate(att_rows[hh], axis=0), 0.0).astype(BF16)
             for hh in range(2)], axis=1)
        zero_v = jnp.zeros((c, GLA_DV), BF16)
        v_blockdiag = jnp.concatenate(
            [jnp.concatenate([v[:, :GLA_DV], zero_v], axis=1),
             jnp.concatenate([zero_v, v[:, GLA_DV:]], axis=1)], axis=0)
        return d, att2, v_blockdiag, kv2, qhat.astype(BF16), jnp.exp(b_edge)

    def stage2(s1, states):
        d, att2, v_blockdiag, kv2, qhat, decay = s1
        st = states[d]
        o = jnp.dot(att2, v_blockdiag, preferred_element_type=F32) + _dot_t(qhat, st.astype(BF16))
        states[d] = st * decay + kv2
        return o

    def finish(n, o):
        rows = pl.ds(pl.multiple_of(n * c, c), c)
        parts = []
        for hh in range(2):
            oh = o[:, GLA_DV * hh:GLA_DV * (hh + 1)]
            parts.append(oh * lax.rsqrt(jnp.mean(oh * oh, axis=-1, keepdims=True) + EPS))
        o = jnp.concatenate(parts, axis=1) * gn_ref[...]
        o_ref[0, rows, :] = (o * _silu(g_ref[0, rows, :].astype(F32))).astype(o_ref.dtype)

    st_scr[...] = jnp.zeros_like(st_scr)

    cpi = next(u for u in (4, 2, 1) if half % u == 0)

    def walk(i, base_f, base_b):
        work = []
        for u in range(cpi):
            work.append((base_f + i * cpi + u, 0))
            work.append((base_b - i * cpi - u, 1))
        prepared = [stage1(n, d) for n, d in work]
        states = [st_scr[d] for d in range(2)]
        outs = [(n, stage2(s1, states)) for (n, _), s1 in zip(work, prepared)]
        for d in range(2):
            st_scr[d] = states[d]
        return outs

    def first(i, carry):
        for n, o in walk(i, 0, nc - 1):
            o_scr[pl.ds(pl.multiple_of(n * c, c), c), :] = o
        return carry

    lax.fori_loop(0, half // cpi, first, 0)

    def second(i, carry):
        for n, o in walk(i, half, half - 1):
            finish(n, o_scr[pl.ds(pl.multiple_of(n * c, c), c), :] + o)
        return carry

    lax.fori_loop(0, half // cpi, second, 0)


def _gla(proj3, small3, wdec_pad, bdec, gnorm):
    bn, seq, _ = proj3.shape
    npair = GLA_HEADS // 2
    qoff = 0
    koff = (GLA_HEADS * GLA_DK) // LANES
    voff = (2 * GLA_HEADS * GLA_DK) // (2 * GLA_DV)
    goff = voff + npair
    return pl.pallas_call(
        functools.partial(_gla_body, seq=seq),
        grid=(bn, npair),
        in_specs=[
            pl.BlockSpec((1, seq, LANES), lambda b, p: (b, 0, qoff + p)),
            pl.BlockSpec((1, seq, LANES), lambda b, p: (b, 0, koff + p)),
            pl.BlockSpec((1, seq, 2 * GLA_DV), lambda b, p: (b, 0, voff + p)),
            pl.BlockSpec((1, seq, 2 * GLA_DV), lambda b, p: (b, 0, goff + p)),
            pl.BlockSpec((1, seq, LANES), lambda b, p: (b, 0, 0)),
            pl.BlockSpec((2, LANES, LANES), lambda b, p: (0, 0, p)),
            pl.BlockSpec((2, LANES), lambda b, p: (0, p)),
            pl.BlockSpec((1, 2 * GLA_DV), lambda b, p: (0, p)),
        ],
        out_specs=pl.BlockSpec((1, seq, 2 * GLA_DV), lambda b, p: (b, 0, p)),
        out_shape=jax.ShapeDtypeStruct((bn, seq, GLA_HEADS * GLA_DV), BF16),
        scratch_shapes=[
            pltpu.VMEM((2, seq, LANES), F32),
            pltpu.VMEM((seq, 2 * GLA_DV), F32),
            pltpu.VMEM((2, 2 * GLA_DV, LANES), F32),
        ],
        compiler_params=_cparams(("parallel", "parallel")),
        name="gla",
    )(proj3, proj3, proj3, proj3, small3, wdec_pad, bdec, gnorm)


def _na_bias_table(rpb):
    nh, ndr, ndc = rpb.shape
    cols = np.arange(GRID_W)
    cs = np.clip(cols - NA_KW // 2, 0, GRID_W - NA_KW)
    valid = (cols[None, :] >= cs[:, None]) & (cols[None, :] < cs[:, None] + NA_KW)
    dc = cols[None, :] - cols[:, None] + NA_KW - 1
    onehot = (np.arange(ndc)[:, None, None] == dc[None]) & valid[None]
    toep = jnp.dot(rpb.reshape(nh * ndr, ndc).astype(F32),
                   jnp.asarray(onehot.reshape(ndc, -1), F32), precision=HIGHEST)
    toep = jnp.where(jnp.asarray(valid.reshape(1, -1)), toep, NEG_BIG)
    toep = toep.reshape(nh, ndr, GRID_W, GRID_W)
    t = jnp.stack([toep[:, off:off + NA_KH] for off in range(NA_KH)], axis=1)
    return t.transpose(0, 1, 3, 2, 4).reshape(nh, NA_KH, GRID_W, NA_KH * GRID_W)


def _na_body(q_ref, k_ref, v_ref, bias_ref, o_ref, *, nrows):
    w = GRID_W
    nk = NA_KH * w

    def group(i, carry):
        idx = []
        scores = []
        for u in range(NA_ROWS_PER_ITER):
            r = i * NA_ROWS_PER_ITER + u
            rs = jnp.clip(r - NA_KH // 2, 0, nrows - NA_KH)
            qrows = pl.ds(pl.multiple_of(r * w, w), w)
            krows = pl.ds(pl.multiple_of(rs * w, w), nk)
            idx.append((qrows, krows, rs - r + (NA_KH - 1)))
            scores.append(_dot_t(q_ref[0, qrows, :], k_ref[0, krows, :]))
        probs = []
        for (qrows, krows, off), s in zip(idx, scores):
            s = s * (NA_DH ** -0.5) + bias_ref[0, off]
            p = jnp.exp(s - jnp.max(s, axis=-1, keepdims=True))
            probs.append((p.astype(BF16), jnp.sum(p, axis=-1, keepdims=True)))
        for (qrows, krows, off), (p, l) in zip(idx, probs):
            o = jnp.dot(p, v_ref[0, krows, :], preferred_element_type=F32) / l
            o_ref[0, qrows, :] = o.astype(o_ref.dtype)
        return carry

    lax.fori_loop(0, nrows // NA_ROWS_PER_ITER, group, 0)


def _na(proj3, bias_tab, col0):
    bn, seq, _ = proj3.shape
    nrows = seq // GRID_W
    qoff = col0 // NA_DH
    koff = qoff + NA_HEADS
    voff = koff + NA_HEADS
    return pl.pallas_call(
        functools.partial(_na_body, nrows=nrows),
        grid=(bn, NA_HEADS),
        in_specs=[
            pl.BlockSpec((1, seq, NA_DH), lambda b, h: (b, 0, qoff + h)),
            pl.BlockSpec((1, seq, NA_DH), lambda b, h: (b, 0, koff + h)),
            pl.BlockSpec((1, seq, NA_DH), lambda b, h: (b, 0, voff + h)),
            pl.BlockSpec((1, NA_KH, GRID_W, NA_KH * GRID_W), lambda b, h: (h, 0, 0, 0)),
        ],
        out_specs=pl.BlockSpec((1, seq, NA_DH), lambda b, h: (b, 0, h)),
        out_shape=jax.ShapeDtypeStruct((bn, seq, NA_HEADS * NA_DH), BF16),
        compiler_params=_cparams(("parallel", "parallel")),
        name="na",
    )(proj3, proj3, proj3, bias_tab)


CONV_PAD = 16


def _conv_body(x_ref, w_ref, b_ref, o_ref, xp_scr, *, seq):
    rb = min(256, seq)
    ch = x_ref.shape[-1]
    zeros = jnp.zeros((CONV_PAD, ch), xp_scr.dtype)
    xp_scr[0:CONV_PAD, :] = zeros
    xp_scr[seq + CONV_PAD:seq + 2 * CONV_PAD, :] = zeros
    xp_scr[CONV_PAD:seq + CONV_PAD, :] = x_ref[0]

    def blk(i, carry):
        r0 = pl.multiple_of(i * rb, rb)
        xw = xp_scr[pl.ds(r0, rb + 2 * CONV_PAD), :].astype(F32)
        acc = jnp.zeros((rb, ch), F32) + b_ref[...]
        for j in range(SSD_CONV):
            s0 = CONV_PAD - SSD_CONV // 2 + j
            acc = acc + xw[s0:s0 + rb] * w_ref[j:j + 1, :]
        o_ref[0, pl.ds(r0, rb), :] = _silu(acc).astype(o_ref.dtype)
        return carry

    lax.fori_loop(0, seq // rb, blk, 0)


def _conv(proj3, conv_w, conv_b, col0, tc=512):
    bn, seq, _ = proj3.shape
    cdim = conv_w.shape[1]
    c0 = col0 // tc
    return pl.pallas_call(
        functools.partial(_conv_body, seq=seq),
        grid=(bn, cdim // tc),
        in_specs=[
            pl.BlockSpec((1, seq, tc), lambda b, j: (b, 0, c0 + j)),
            pl.BlockSpec((SSD_CONV, tc), lambda b, j: (0, j)),
            pl.BlockSpec((1, tc), lambda b, j: (0, j)),
        ],
        out_specs=pl.BlockSpec((1, seq, tc), lambda b, j: (b, 0, j)),
        out_shape=jax.ShapeDtypeStruct((bn, seq, cdim), BF16),
        scratch_shapes=[pltpu.VMEM((seq + 2 * CONV_PAD, tc), BF16)],
        compiler_params=_cparams(("parallel", "parallel")),
        name="ssd_conv",
    )(proj3, conv_w, conv_b)


def _softplus(x):
    return jnp.maximum(x, 0.0) + jnp.log(1.0 + jnp.exp(-jnp.abs(x)))


def _ssd_body(xs_ref, bm_ref, cm_ref, dt_ref, pb_ref, al_ref, dsk_ref, ex_ref, o_ref,
              st_scr, run_scr, dec_scr, *, seq):
    l = SSD_CHUNK
    nc = seq // l
    nh2 = 2 * SSD_JH
    row_i = lax.broadcasted_iota(jnp.int32, (l, l), 0)
    col_i = lax.broadcasted_iota(jnp.int32, (l, l), 1)
    lane = lax.broadcasted_iota(jnp.int32, (1, LANES), 1)
    first_head = lane < SSD_HEADDIM
    bwd_row = lax.broadcasted_iota(jnp.int32, (nh2, 1), 0) >= SSD_JH
    a_col = -jnp.exp(al_ref[0])

    lower = row_i > col_i
    diag = row_i == col_i
    triu_b = (row_i <= col_i).astype(BF16)

    def decays(n):
        rows = pl.ds(pl.multiple_of(n * l, l), l)
        dtv = _softplus(dt_ref[0, rows, :].T[0:nh2] + pb_ref[0])
        da = dtv * a_col
        hi = da.astype(BF16)
        r1 = da - hi.astype(F32)
        mid = r1.astype(BF16)
        lo = (r1 - mid.astype(F32)).astype(BF16)
        p3 = jnp.dot(jnp.concatenate([hi, mid, lo], axis=0), triu_b, preferred_element_type=F32)
        pre = p3[0:nh2] + p3[nh2:2 * nh2] + p3[2 * nh2:]
        total = pre[:, l - 1:l]
        acum = jnp.where(bwd_row, total - pre + da, pre)
        dec_scr[0, n] = dtv
        dec_scr[1, n] = acum
        dec_scr[2, n] = dtv * jnp.exp(total - acum)
        dec_scr[3, n] = jnp.broadcast_to(jnp.exp(total), (nh2, l))
        dec_scr[4, n] = acum - jnp.log(dtv)

    def chunk_state(n, d):
        rows = pl.ds(pl.multiple_of(n * l, l), l)
        w_state_t = dec_scr[2, n]
        x = xs_ref[0, rows, :]
        bmt = bm_ref[0, rows, :].astype(F32).T
        edge_ch = jnp.sum(dec_scr[3, n][:, 0:1] * ex_ref[d], axis=0, keepdims=True)
        new_st = []
        ns = SSD_STATE
        for jj in range(SSD_JH // 2):
            xp = x[:, LANES * jj:LANES * (jj + 1)]
            bw2 = jnp.concatenate(
                [(bmt * w_state_t[d * SSD_JH + 2 * jj + hh:d * SSD_JH + 2 * jj + hh + 1, :]).astype(BF16)
                 for hh in range(2)], axis=0)
            s2 = jnp.dot(bw2, xp, preferred_element_type=F32)
            new_st.append(jnp.where(first_head, s2[:ns], s2[ns:]))
        st = run_scr[d]
        st_scr[d, n] = st.astype(BF16)
        run_scr[d] = st * edge_ch + jnp.concatenate(new_st, axis=1)

    def chunk_out(n):
        rows = pl.ds(pl.multiple_of(n * l, l), l)
        dtv_t = dec_scr[0, n]
        acum = dec_scr[1, n].T
        key_t = dec_scr[4, n]
        x = xs_ref[0, rows, :]
        cm = cm_ref[0, rows, :]
        cmf = cm.astype(F32)
        cb = _dot_t(cm, bm_ref[0, rows, :])
        st_f = st_scr[0, n]
        st_b = st_scr[1, n]
        rhs = [jnp.concatenate([x[:, LANES * jj:LANES * (jj + 1)],
                                st_f[:, LANES * jj:LANES * (jj + 1)],
                                st_b[:, LANES * jj:LANES * (jj + 1)]], axis=0)
               for jj in range(SSD_JH // 2)]
        slabs = [slice(r0, r0 + SSD_ROW_SLAB) for r0 in range(0, l, SSD_ROW_SLAB)]
        ys = [[] for _ in slabs]
        for jj in range(SSD_JH // 2):
            lhs = []
            for rs in slabs:
                for hh in range(2):
                    cf = 2 * jj + hh
                    cr = SSD_JH + cf
                    col_f = jnp.broadcast_to(acum[rs, cf:cf + 1], (SSD_ROW_SLAB, l))
                    col_r = jnp.broadcast_to(acum[rs, cr:cr + 1], (SSD_ROW_SLAB, l))
                    seg = jnp.where(lower[rs], col_f - key_t[cf:cf + 1, :], col_r - key_t[cr:cr + 1, :])
                    m = cb[rs] * (jnp.exp(seg) + jnp.where(diag[rs], dtv_t[cf:cf + 1, :], 0.0))
                    lhs.append(jnp.concatenate([m.astype(BF16),
                                                (cmf[rs] * jnp.exp(col_f)).astype(BF16),
                                                (cmf[rs] * jnp.exp(col_r)).astype(BF16)], axis=1))
            y4 = jnp.dot(jnp.concatenate(lhs, axis=0), rhs[jj], preferred_element_type=F32)
            for si in range(len(slabs)):
                r0 = 2 * si * SSD_ROW_SLAB
                ys[si].append(jnp.where(first_head, y4[r0:r0 + SSD_ROW_SLAB],
                                        y4[r0 + SSD_ROW_SLAB:r0 + 2 * SSD_ROW_SLAB]))
        for rs, y_parts in zip(slabs, ys):
            y = jnp.concatenate(y_parts, axis=1) + x[rs].astype(F32) * dsk_ref[...]
            o_ref[0, pl.ds(pl.multiple_of(n * l, l) + rs.start, SSD_ROW_SLAB), :] = y.astype(o_ref.dtype)

    def prep(i, carry):
        decays(i)
        return carry

    lax.fori_loop(0, nc, prep, 0, unroll=2)
    run_scr[...] = jnp.zeros_like(run_scr)

    def states(i, carry):
        chunk_state(i, 0)
        chunk_state(nc - 1 - i, 1)
        return carry

    lax.fori_loop(0, nc, states, 0)

    def outputs(n, carry):
        chunk_out(n)
        return carry

    lax.fori_loop(0, nc, outputs, 0)


def _ssd(xbc_act, dt3, pb, al, dskip, expand):
    bn, seq, _ = xbc_act.shape
    gw = SSD_JH * SSD_HEADDIM
    boff = SSD_INNER // SSD_STATE
    coff = boff + SSD_GROUPS
    return pl.pallas_call(
        functools.partial(_ssd_body, seq=seq),
        grid=(bn, SSD_GROUPS),
        in_specs=[
            pl.BlockSpec((1, seq, gw), lambda b, g: (b, 0, g)),
            pl.BlockSpec((1, seq, SSD_STATE), lambda b, g: (b, 0, boff + g)),
            pl.BlockSpec((1, seq, SSD_STATE), lambda b, g: (b, 0, coff + g)),
            pl.BlockSpec((1, seq, LANES), lambda b, g: (b, 0, g)),
            pl.BlockSpec((1, 2 * SSD_JH, 1), lambda b, g: (g, 0, 0)),
            pl.BlockSpec((1, 2 * SSD_JH, 1), lambda b, g: (g, 0, 0)),
            pl.BlockSpec((1, gw), lambda b, g: (0, g)),
            pl.BlockSpec((2, 2 * SSD_JH, gw), lambda b, g: (0, 0, 0)),
        ],
        out_specs=pl.BlockSpec((1, seq, gw), lambda b, g: (b, 0, g)),
        out_shape=jax.ShapeDtypeStruct((bn, seq, SSD_INNER), BF16),
        scratch_shapes=[pltpu.VMEM((2, seq // SSD_CHUNK, SSD_STATE, gw), BF16),
                        pltpu.VMEM((2, SSD_STATE, gw), F32),
                        pltpu.VMEM((5, seq // SSD_CHUNK, 2 * SSD_JH, SSD_CHUNK), F32)],
        compiler_params=_cparams(("parallel", "parallel")),
        name="ssd_scan",
    )(xbc_act, xbc_act, xbc_act, dt3, pb, al, dskip, expand)


def _rope_tables(seq):
    half = SWA_DH // 2
    inv = ROPE_THETA ** (-np.arange(half, dtype=np.float64) / half)
    ang = np.arange(seq, dtype=np.float64)[:, None] * inv[None, :]
    cos, sin = np.cos(ang), np.sin(ang)
    return (jnp.asarray(np.concatenate([cos, cos], axis=1), F32),
            jnp.asarray(np.concatenate([-sin, sin], axis=1), F32))


def _rope(x, cos, sin_signed):
    return x * cos + pltpu.roll(x, SWA_DH // 2, 1) * sin_signed


def _swa_body(sink_ref, q_ref, k_ref, v_ref, cos_ref, sin_ref, o_ref, kr_scr, *, seq):
    wb = SWA_BLOCK
    nb = seq // wb
    nkeys = 3 * wb
    kvh = pl.program_id(1)
    rb = min(512, seq)

    def krope(i, carry):
        rows = pl.ds(pl.multiple_of(i * rb, rb), rb)
        kr_scr[rows, :] = _rope(k_ref[0, rows, :].astype(F32), cos_ref[rows, :],
                                sin_ref[rows, :]).astype(BF16)
        return carry

    lax.fori_loop(0, seq // rb, krope, 0)

    qpos_l = lax.broadcasted_iota(jnp.int32, (wb, nkeys), 0)
    kpos_l = lax.broadcasted_iota(jnp.int32, (wb, nkeys), 1)

    def blk(n, carry):
        rows = pl.ds(pl.multiple_of(n * wb, wb), wb)
        ks = pl.multiple_of(jnp.clip(n - 1, 0, nb - 3) * wb, wb)
        krows = pl.ds(ks, nkeys)
        cos = cos_ref[rows, :]
        sin = sin_ref[rows, :]
        kk = kr_scr[krows, :]
        vv = v_ref[0, krows, :]
        valid = jnp.abs((qpos_l + n * wb) - (kpos_l + ks)) <= SWA_WINDOW
        qb = q_ref[0, rows, :].astype(F32)
        q4 = jnp.concatenate(
            [(_rope(qb[:, SWA_DH * hh:SWA_DH * (hh + 1)], cos, sin) * (SWA_DH ** -0.5)).astype(BF16)
             for hh in range(SWA_GQ)], axis=0)
        s4 = _dot_t(q4, kk)
        probs = []
        dens = []
        for hh in range(SWA_GQ):
            sink = sink_ref[kvh * SWA_GQ + hh]
            s = jnp.where(valid, s4[wb * hh:wb * (hh + 1)], NEG_BIG)
            m = jnp.maximum(jnp.max(s, axis=-1, keepdims=True), sink)
            p = jnp.exp(s - m)
            probs.append(p.astype(BF16))
            dens.append(jnp.sum(p, axis=-1, keepdims=True) + jnp.exp(sink - m))
        o4 = jnp.dot(jnp.concatenate(probs, axis=0), vv, preferred_element_type=F32)
        for hh in range(SWA_GQ):
            o = o4[wb * hh:wb * (hh + 1)] / dens[hh]
            o_ref[0, rows, SWA_DH * hh:SWA_DH * (hh + 1)] = o.astype(o_ref.dtype)
        return carry

    lax.fori_loop(0, nb, blk, 0)


def _swa(proj3, sink, cos, sin, col0):
    bn, seq, _ = proj3.shape
    qw = SWA_GQ * SWA_DH
    qoff = col0 // qw
    koff = (col0 + SWA_HEADS * SWA_DH) // SWA_DH
    voff = koff + SWA_KV
    return pl.pallas_call(
        functools.partial(_swa_body, seq=seq),
        grid=(bn, SWA_KV),
        in_specs=[
            pl.BlockSpec(memory_space=pltpu.SMEM),
            pl.BlockSpec((1, seq, qw), lambda b, h: (b, 0, qoff + h)),
            pl.BlockSpec((1, seq, SWA_DH), lambda b, h: (b, 0, koff + h)),
            pl.BlockSpec((1, seq, SWA_DH), lambda b, h: (b, 0, voff + h)),
            pl.BlockSpec((seq, SWA_DH), lambda b, h: (0, 0)),
            pl.BlockSpec((seq, SWA_DH), lambda b, h: (0, 0)),
        ],
        out_specs=pl.BlockSpec((1, seq, qw), lambda b, h: (b, 0, h)),
        out_shape=jax.ShapeDtypeStruct((bn, seq, SWA_HEADS * SWA_DH), BF16),
        scratch_shapes=[pltpu.VMEM((seq, SWA_DH), BF16)],
        compiler_params=_cparams(("parallel", "parallel")),
        name="swa",
    )(sink, proj3, proj3, proj3, cos, sin)


def _pad_cols(w, width):
    return jnp.pad(w, ((0, 0), (0, width - w.shape[1])))


def _even_mixers(x2, bn, seq, g0, w_in, w_decay, b_decay, gla_norm, rpb):
    d = x2.shape[1]
    n_gla = 2 * GLA_HEADS * GLA_DK + 2 * GLA_HEADS * GLA_DV
    n_lr = 2 * GLA_LOWRANK
    w_main = jnp.concatenate([w_in[:, :n_gla], w_in[:, n_gla + n_lr:]], axis=1).astype(BF16)
    w_small = _pad_cols(w_in[:, n_gla:n_gla + n_lr], LANES).astype(BF16)
    proj, small = _inproj(x2, g0, w_main, w_small)
    proj3 = proj.reshape(bn, seq, -1)
    small3 = small.reshape(bn, seq, -1)
    wdec = jnp.zeros((2, LANES, w_decay.shape[-1]), F32)
    for dd in range(2):
        wdec = wdec.at[dd, dd * GLA_LOWRANK:(dd + 1) * GLA_LOWRANK].set(w_decay[dd])
    o_a = _gla(proj3, small3, wdec, b_decay, gla_norm[None, :])
    o_b = _na(proj3, _na_bias_table(rpb), n_gla)
    return o_a.reshape(-1, o_a.shape[-1]), o_b.reshape(-1, o_b.shape[-1])


def _per_group(v):
    return v.reshape(2, SSD_GROUPS, SSD_JH).transpose(1, 0, 2).reshape(SSD_GROUPS, 2 * SSD_JH)


def _odd_mixers(x2, bn, seq, g0, w_in, conv_w, conv_b, dt_bias, a_log, d_skip, sink, cos, sin):
    conv_dim = SSD_INNER + 2 * SSD_GROUPS * SSD_STATE
    n_z = SSD_INNER
    n_dt = 2 * SSD_HEADS
    c_dt = n_z + conv_dim
    w_main = jnp.concatenate([w_in[:, :c_dt], w_in[:, c_dt + n_dt:]], axis=1).astype(BF16)
    w_dt = w_in[:, c_dt:c_dt + n_dt].reshape(-1, 2, SSD_GROUPS, SSD_JH).transpose(0, 2, 1, 3)
    w_dt = w_dt.reshape(-1, SSD_GROUPS, 2 * SSD_JH)
    w_small = jnp.pad(w_dt, ((0, 0), (0, 0), (0, LANES - 2 * SSD_JH))).reshape(-1, SSD_GROUPS * LANES)
    proj, small = _inproj(x2, g0, w_main, w_small.astype(BF16))
    proj3 = proj.reshape(bn, seq, -1)
    dt3 = small.reshape(bn, seq, SSD_GROUPS * LANES)
    xbc_act = _conv(proj3, conv_w, conv_b[None, :], n_z)
    pbg = _per_group(dt_bias)
    alg = _per_group(a_log)
    head_of_ch = np.arange(SSD_JH * SSD_HEADDIM) // SSD_HEADDIM
    expand = jnp.asarray(np.arange(2 * SSD_JH)[None, :, None]
                         == (np.arange(2)[:, None, None] * SSD_JH + head_of_ch[None, None, :]), F32)
    dskip = jnp.repeat(d_skip.astype(F32), SSD_HEADDIM)[None, :]
    y_pre = _ssd(xbc_act, dt3, pbg[:, :, None], alg[:, :, None], dskip, expand)
    o_d = _swa(proj3, sink, cos, sin, c_dt)
    return y_pre.reshape(-1, SSD_INNER), proj, o_d.reshape(-1, o_d.shape[-1])


def kernel(x, norm_gains, ffn_w_gate, ffn_w_up, ffn_w_down, even_w_in, even_w_out, gla_w_decay,
           gla_b_decay, gla_norm, na_rpb, odd_w_in, odd_w_out, ssd_conv_w, ssd_conv_b, ssd_dt_bias,
           ssd_a_log, ssd_d, ssd_norm, swa_sink):
    bn, seq, d = x.shape
    depth = norm_gains.shape[0]
    x2 = x.reshape(bn * seq, d)
    cos, sin = _rope_tables(seq)
    wg = ffn_w_gate.astype(BF16)
    wu = ffn_w_up.astype(BF16)
    wd = ffn_w_down.astype(BF16)
    even_w_out_b = even_w_out.astype(BF16)
    odd_w_out_b = odd_w_out.astype(BF16)
    for layer in range(depth):
        g = norm_gains[layer][:, None, :]
        i = layer // 2
        if layer % 2 == 0:
            o_a, o_b = _even_mixers(x2, bn, seq, g[0], even_w_in[i], gla_w_decay[i], gla_b_decay[i],
                                    gla_norm[i], na_rpb[i])
            x2 = _outproj(x2, o_a, o_b, even_w_out_b, i, g[1])
        else:
            y_pre, proj, o_d = _odd_mixers(x2, bn, seq, g[0], odd_w_in[i], ssd_conv_w[i], ssd_conv_b[i],
                                           ssd_dt_bias[i], ssd_a_log[i], ssd_d[i], swa_sink[i], cos, sin)
            x2 = _outproj(x2, y_pre, o_d, odd_w_out_b, i, g[1],
                          z_src=proj, ssd_gain=ssd_norm[i][None, :])
        x2 = _ffn(x2, g[2], wg, wu, wd, g[3], layer)
    return x2.reshape(bn, seq, d)
```

```python
import functools

import jax
import jax.numpy as jnp
import numpy as np
from jax import lax
from jax.experimental import pallas as pl
from jax.experimental.pallas import tpu as pltpu

F32 = jnp.float32
BF16 = jnp.bfloat16
HIGHEST = lax.Precision.HIGHEST

EPS = 1e-6
ROPE_THETA = 10000.0
GRID_W = 64
LANES = 128

GLA_DK = 64
GLA_DV = 128
GLA_HEADS = 8
GLA_LOWRANK = 16
GLA_TAU = 16.0
GLA_CHUNK = 64
GLA_SUB = 16
GLA_EXP_CLAMP = 60.0
GLA_TRI_BLOCK = 256

NA_DH = 128
NA_HEADS = 8
NA_KH = 8
NA_KW = 16
NA_ROWS_PER_ITER = 16
NEG_BIG = -1e30

SSD_INNER = 1024
SSD_HEADDIM = 64
SSD_HEADS = 16
SSD_GROUPS = 2
SSD_STATE = 128
SSD_CONV = 5
SSD_CHUNK = 128
SSD_JH = SSD_HEADS // SSD_GROUPS
SSD_ROW_SLAB = 64

SWA_DH = 128
SWA_HEADS = 8
SWA_KV = 2
SWA_GQ = SWA_HEADS // SWA_KV
SWA_WINDOW = 128
SWA_BLOCK = 128

VMEM_LIMIT = 52 * 1024 * 1024


FFN_NORM_ROWS = 256
FFN_VMEM_LIMIT = 58 * 1024 * 1024


def _cparams(sem, vmem_limit=VMEM_LIMIT):
    return pltpu.CompilerParams(dimension_semantics=sem, vmem_limit_bytes=vmem_limit)


def _rms(x, g):
    return x * lax.rsqrt(jnp.mean(x * x, axis=-1, keepdims=True) + EPS) * g


def _silu(x):
    return x * (1.0 / (1.0 + jnp.exp(-x)))


def _dot_t(a, b):
    return lax.dot_general(a, b, (((1,), (1,)), ((), ())), preferred_element_type=F32)


def _inproj_body(x_ref, g_ref, w_ref, ws_ref, o_ref, os_ref, h_scr):
    j = pl.program_id(1)
    half = x_ref.shape[0] // 2

    @pl.when(j == 0)
    def _():
        for r0 in (0, half):
            rs = pl.ds(r0, half)
            h = _rms(x_ref[rs, :], g_ref[...]).astype(BF16)
            h_scr[rs, :] = h
            os_ref[rs, :] = jnp.dot(h, ws_ref[...], preferred_element_type=F32)
            o_ref[rs, :] = jnp.dot(h, w_ref[...], preferred_element_type=F32).astype(o_ref.dtype)

    @pl.when(j > 0)
    def _():
        o_ref[...] = jnp.dot(h_scr[...], w_ref[...], preferred_element_type=F32).astype(o_ref.dtype)


def _inproj(x2, g, w_main, w_small, tm=1024, tn=2048):
    n, d = x2.shape
    nm = w_main.shape[1]
    ns = w_small.shape[1]
    return pl.pallas_call(
        _inproj_body,
        grid=(n // tm, nm // tn),
        in_specs=[
            pl.BlockSpec((tm, d), lambda i, j: (i, 0)),
            pl.BlockSpec((1, d), lambda i, j: (0, 0)),
            pl.BlockSpec((d, tn), lambda i, j: (0, j)),
            pl.BlockSpec((d, ns), lambda i, j: (0, 0)),
        ],
        out_specs=[
            pl.BlockSpec((tm, tn), lambda i, j: (i, j)),
            pl.BlockSpec((tm, ns), lambda i, j: (i, 0)),
        ],
        out_shape=[jax.ShapeDtypeStruct((n, nm), BF16), jax.ShapeDtypeStruct((n, ns), F32)],
        scratch_shapes=[pltpu.VMEM((tm, d), BF16)],
        compiler_params=_cparams(("parallel", "arbitrary")),
        name="inproj",
    )(x2, g, w_main, w_small)


def _cast_weight_slices(w_refs):
    for src, dst in zip(w_refs[:3], w_refs[3:]):
        dst[...] = src[...].astype(BF16)


def _outproj_even_body(a_ref, b_ref, wa_ref, wb_ref, g_ref, x_ref, *rest):
    o_ref = rest[3]
    _cast_weight_slices(rest[:3] + rest[4:])
    mix = (jnp.dot(a_ref[...], wa_ref[...], preferred_element_type=F32)
           + jnp.dot(b_ref[...], wb_ref[...], preferred_element_type=F32))
    o_ref[...] = x_ref[...] + _rms(mix, g_ref[...])


def _outproj_odd_body(y_ref, z_ref, ng_ref, b_ref, wa_ref, wb_ref, g_ref, x_ref, *rest):
    o_ref = rest[3]
    _cast_weight_slices(rest[:3] + rest[4:])
    y = y_ref[...].astype(F32) * _silu(z_ref[...].astype(F32))
    a = _rms(y, ng_ref[...]).astype(BF16)
    mix = (jnp.dot(a, wa_ref[...], preferred_element_type=F32)
           + jnp.dot(b_ref[...], wb_ref[...], preferred_element_type=F32))
    o_ref[...] = x_ref[...] + _rms(mix, g_ref[...])


def _outproj(x2, a, b, w_out, li, g, ffn_w, layer, z_src=None, ssd_gain=None, tm=512):
    n, d = x2.shape
    half = a.shape[1]
    steps = n // tm
    row = lambda i: (i, 0)
    fixed = lambda i: (0, 0)
    w_specs = [pl.BlockSpec((None, half, d), lambda i: (li, 0, 0)),
               pl.BlockSpec((None, half, d), lambda i: (li, 1, 0))]
    tail_specs = [pl.BlockSpec((1, d), fixed), pl.BlockSpec((tm, d), row)]
    cast_in = [pl.BlockSpec((None, w.shape[1] // steps, w.shape[2]), lambda i: (layer, i, 0)) for w in ffn_w]
    cast_out = [pl.BlockSpec((w.shape[1] // steps, w.shape[2]), row) for w in ffn_w]
    cast_shapes = [jax.ShapeDtypeStruct(w.shape[1:], BF16) for w in ffn_w]
    if z_src is None:
        body = _outproj_even_body
        in_specs = [pl.BlockSpec((tm, half), row), pl.BlockSpec((tm, half), row)] + w_specs + tail_specs
        args = (a, b, w_out, w_out, g, x2)
    else:
        body = _outproj_odd_body
        in_specs = ([pl.BlockSpec((tm, half), row), pl.BlockSpec((tm, half), row),
                     pl.BlockSpec((1, half), fixed), pl.BlockSpec((tm, half), row)]
                    + w_specs + tail_specs)
        args = (a, z_src, ssd_gain, b, w_out, w_out, g, x2)
    return pl.pallas_call(
        body,
        grid=(steps,),
        in_specs=in_specs + cast_in,
        out_specs=[pl.BlockSpec((tm, d), row)] + cast_out,
        out_shape=[jax.ShapeDtypeStruct((n, d), F32)] + cast_shapes,
        compiler_params=_cparams(("parallel",)),
        name="outproj",
    )(*args, *ffn_w)


def _ffn_body(x_ref, g2_ref, wg_ref, wu_ref, wd_ref, g3_ref, o_ref, h_scr):
    k = pl.program_id(1)
    last = pl.num_programs(1) - 1
    half = x_ref.shape[0] // 2

    def pieces(r0):
        return [pl.ds(r0 + p, FFN_NORM_ROWS) for p in range(0, half, FFN_NORM_ROWS)]

    def partial_down(rs):
        h = h_scr[rs, :]
        gate = jnp.dot(h, wg_ref[...], preferred_element_type=F32)
        up = jnp.dot(h, wu_ref[...], preferred_element_type=F32)
        return jnp.dot((_silu(gate) * up).astype(BF16), wd_ref[...], preferred_element_type=F32)

    @pl.when(k == 0)
    def _():
        for r0 in (0, half):
            for ps in pieces(r0):
                h_scr[ps, :] = _rms(x_ref[ps, :], g2_ref[...]).astype(BF16)
            o_ref[pl.ds(r0, half), :] = partial_down(pl.ds(r0, half))

    @pl.when(jnp.logical_and(k > 0, k < last))
    def _():
        for r0 in (0, half):
            o_ref[pl.ds(r0, half), :] += partial_down(pl.ds(r0, half))

    @pl.when(k == last)
    def _():
        for r0 in (0, half):
            o_ref[pl.ds(r0, half), :] += partial_down(pl.ds(r0, half))
            for ps in pieces(r0):
                o_ref[ps, :] = x_ref[ps, :] + _rms(o_ref[ps, :], g3_ref[...])


def _ffn(x2, g2, wg, wu, wd, g3, tm=1024, th=512):
    n, d = x2.shape
    hid = wd.shape[0]
    return pl.pallas_call(
        _ffn_body,
        grid=(n // tm, hid // th),
        in_specs=[
            pl.BlockSpec((tm, d), lambda i, k: (i, 0)),
            pl.BlockSpec((1, d), lambda i, k: (0, 0)),
            pl.BlockSpec((d, th), lambda i, k: (0, k)),
            pl.BlockSpec((d, th), lambda i, k: (0, k)),
            pl.BlockSpec((th, d), lambda i, k: (k, 0)),
            pl.BlockSpec((1, d), lambda i, k: (0, 0)),
        ],
        out_specs=pl.BlockSpec((tm, d), lambda i, k: (i, 0)),
        out_shape=jax.ShapeDtypeStruct((n, d), F32),
        scratch_shapes=[pltpu.VMEM((tm, d), BF16)],
        compiler_params=_cparams(("parallel", "arbitrary"), FFN_VMEM_LIMIT),
        name="ffn",
    )(x2, g2, wg, wu, wd, g3)


def _gla_body(q_ref, k_ref, v_ref, g_ref, lr_ref, wd_ref, bd_ref, gn_ref, o_ref,
              bc_scr, o_scr, st_scr, *, seq):
    c = GLA_CHUNK
    nc = seq // c
    half = nc // 2
    nsub = c // GLA_SUB
    rb = min(512, seq)
    row_i = lax.broadcasted_iota(jnp.int32, (c, c), 0)
    col_i = lax.broadcasted_iota(jnp.int32, (c, c), 1)
    tri = (row_i >= col_i, row_i <= col_i)
    lane = lax.broadcasted_iota(jnp.int32, (1, LANES), 1)
    hmask = (lane < GLA_DK, lane >= GLA_DK)
    own_head = ((lax.broadcasted_iota(jnp.int32, (2 * GLA_DV, LANES), 0) < GLA_DV)
                == (lax.broadcasted_iota(jnp.int32, (2 * GLA_DV, LANES), 1) < GLA_DK))

    w2 = jnp.concatenate([wd_ref[0], wd_ref[1]], axis=1)
    w_hi = w2.astype(BF16)
    w_lo = (w2 - w_hi.astype(F32)).astype(BF16)
    w_cat = jnp.concatenate([w_hi, w_lo, w_hi], axis=0)
    tb = GLA_TRI_BLOCK
    brow = lax.broadcasted_iota(jnp.int32, (tb, tb), 0)
    bcol = lax.broadcasted_iota(jnp.int32, (tb, tb), 1)
    same_chunk = (brow // c) == (bcol // c)
    blk_tri = ((same_chunk & (brow >= bcol)).astype(BF16), (same_chunk & (brow <= bcol)).astype(BF16))

    def pre(i, carry):
        r0 = pl.multiple_of(i * rb, rb)
        lr = lr_ref[0, pl.ds(r0, rb), :]
        lr_hi = lr.astype(BF16)
        lr_lo = (lr - lr_hi.astype(F32)).astype(BF16)
        z2 = jnp.dot(jnp.concatenate([lr_hi, lr_hi, lr_lo], axis=1), w_cat, preferred_element_type=F32)
        las = []
        for d in range(2):
            z = z2[:, LANES * d:LANES * (d + 1)] + bd_ref[d:d + 1, :]
            la = (jnp.minimum(z, 0.0) - jnp.log(1.0 + jnp.exp(-jnp.abs(z)))) * (1.0 / GLA_TAU)
            la_hi = la.astype(BF16)
            las.append(jnp.concatenate([la_hi, (la - la_hi.astype(F32)).astype(BF16)], axis=1))
        for d in range(2):
            for t in range(rb // tb):
                s2 = jnp.dot(blk_tri[d], las[d][t * tb:(t + 1) * tb], preferred_element_type=F32)
                bc_scr[d, pl.ds(r0 + t * tb, tb), :] = s2[:, :LANES] + s2[:, LANES:]
        return carry

    lax.fori_loop(0, seq // rb, pre, 0)

    def stage1(n, d):
        rows = pl.ds(pl.multiple_of(n * c, c), c)
        bc = bc_scr[d, rows, :]
        q = q_ref[0, rows, :].astype(F32) * (GLA_DK ** -0.5)
        k = k_ref[0, rows, :].astype(F32)
        v = v_ref[0, rows, :]
        b_edge = bc[c - 1:c] if d == 0 else bc[0:1]
        qhat = q * jnp.exp(bc)
        khat = (k * jnp.exp(b_edge - bc)).astype(BF16)
        att_rows = ([], [])
        for i in range(nsub):
            lo = GLA_SUB * i
            hi = lo + GLA_SUB
            if d == 0:
                ref = bc[lo - 1:lo] if i > 0 else jnp.zeros((1, LANES), F32)
            else:
                ref = bc[hi:hi + 1] if i < nsub - 1 else jnp.zeros((1, LANES), F32)
            qi = q[lo:hi] * jnp.exp(bc[lo:hi] - ref)
            ki = (k * jnp.exp(jnp.minimum(ref - bc, GLA_EXP_CLAMP))).astype(BF16)
            q2 = jnp.concatenate([jnp.where(hmask[0], qi, 0.0), jnp.where(hmask[1], qi, 0.0)], axis=0)
            a2 = _dot_t(q2.astype(BF16), ki)
            att_rows[0].append(a2[:GLA_SUB])
            att_rows[1].append(a2[GLA_SUB:])
        kv2 = lax.dot_general(v, khat, (((0,), (0,)), ((), ())), preferred_element_type=F32)
        kv2 = jnp.where(own_head, kv2, 0.0)
        att2 = jnp.concatenate(
            [jnp.where(tri[d], jnp.concatenate(att_rows[hh], axis=0), 0.0).astype(BF16)
             for hh in range(2)], axis=1)
        zero_v = jnp.zeros((c, GLA_DV), BF16)
        v_blockdiag = jnp.concatenate(
            [jnp.concatenate([v[:, :GLA_DV], zero_v], axis=1),
             jnp.concatenate([zero_v, v[:, GLA_DV:]], axis=1)], axis=0)
        return d, att2, v_blockdiag, kv2, qhat.astype(BF16), jnp.exp(b_edge)

    def stage2(s1, states):
        d, att2, v_blockdiag, kv2, qhat, decay = s1
        st = states[d]
        o = jnp.dot(att2, v_blockdiag, preferred_element_type=F32) + _dot_t(qhat, st.astype(BF16))
        states[d] = st * decay + kv2
        return o

    def finish(n, o):
        rows = pl.ds(pl.multiple_of(n * c, c), c)
        parts = []
        for hh in range(2):
            oh = o[:, GLA_DV * hh:GLA_DV * (hh + 1)]
            parts.append(oh * lax.rsqrt(jnp.mean(oh * oh, axis=-1, keepdims=True) + EPS))
        o = jnp.concatenate(parts, axis=1) * gn_ref[...]
        o_ref[0, rows, :] = (o * _silu(g_ref[0, rows, :].astype(F32))).astype(o_ref.dtype)

    st_scr[...] = jnp.zeros_like(st_scr)

    cpi = next(u for u in (4, 2, 1) if half % u == 0)

    def walk(i, base_f, base_b):
        work = []
        for u in range(cpi):
            work.append((base_f + i * cpi + u, 0))
            work.append((base_b - i * cpi - u, 1))
        prepared = [stage1(n, d) for n, d in work]
        states = [st_scr[d] for d in range(2)]
        outs = [(n, stage2(s1, states)) for (n, _), s1 in zip(work, prepared)]
        for d in range(2):
            st_scr[d] = states[d]
        return outs

    def first(i, carry):
        for n, o in walk(i, 0, nc - 1):
            o_scr[pl.ds(pl.multiple_of(n * c, c), c), :] = o
        return carry

    lax.fori_loop(0, half // cpi, first, 0)

    def second(i, carry):
        for n, o in walk(i, half, half - 1):
            finish(n, o_scr[pl.ds(pl.multiple_of(n * c, c), c), :] + o)
        return carry

    lax.fori_loop(0, half // cpi, second, 0)


def _gla(proj3, small3, wdec_pad, bdec, gnorm):
    bn, seq, _ = proj3.shape
    npair = GLA_HEADS // 2
    qoff = 0
    koff = (GLA_HEADS * GLA_DK) // LANES
    voff = (2 * GLA_HEADS * GLA_DK) // (2 * GLA_DV)
    goff = voff + npair
    return pl.pallas_call(
        functools.partial(_gla_body, seq=seq),
        grid=(bn, npair),
        in_specs=[
            pl.BlockSpec((1, seq, LANES), lambda b, p: (b, 0, qoff + p)),
            pl.BlockSpec((1, seq, LANES), lambda b, p: (b, 0, koff + p)),
            pl.BlockSpec((1, seq, 2 * GLA_DV), lambda b, p: (b, 0, voff + p)),
            pl.BlockSpec((1, seq, 2 * GLA_DV), lambda b, p: (b, 0, goff + p)),
            pl.BlockSpec((1, seq, LANES), lambda b, p: (b, 0, 0)),
            pl.BlockSpec((2, LANES, LANES), lambda b, p: (0, 0, p)),
            pl.BlockSpec((2, LANES), lambda b, p: (0, p)),
            pl.BlockSpec((1, 2 * GLA_DV), lambda b, p: (0, p)),
        ],
        out_specs=pl.BlockSpec((1, seq, 2 * GLA_DV), lambda b, p: (b, 0, p)),
        out_shape=jax.ShapeDtypeStruct((bn, seq, GLA_HEADS * GLA_DV), BF16),
        scratch_shapes=[
            pltpu.VMEM((2, seq, LANES), F32),
            pltpu.VMEM((seq, 2 * GLA_DV), F32),
            pltpu.VMEM((2, 2 * GLA_DV, LANES), F32),
        ],
        compiler_params=_cparams(("parallel", "parallel")),
        name="gla",
    )(proj3, proj3, proj3, proj3, small3, wdec_pad, bdec, gnorm)


def _na_bias_table(rpb):
    nh, ndr, ndc = rpb.shape
    cols = np.arange(GRID_W)
    cs = np.clip(cols - NA_KW // 2, 0, GRID_W - NA_KW)
    valid = (cols[None, :] >= cs[:, None]) & (cols[None, :] < cs[:, None] + NA_KW)
    dc = cols[None, :] - cols[:, None] + NA_KW - 1
    onehot = (np.arange(ndc)[:, None, None] == dc[None]) & valid[None]
    toep = jnp.dot(rpb.reshape(nh * ndr, ndc).astype(F32),
                   jnp.asarray(onehot.reshape(ndc, -1), F32), precision=HIGHEST)
    toep = jnp.where(jnp.asarray(valid.reshape(1, -1)), toep, NEG_BIG)
    toep = toep.reshape(nh, ndr, GRID_W, GRID_W)
    t = jnp.stack([toep[:, off:off + NA_KH] for off in range(NA_KH)], axis=1)
    return t.transpose(0, 1, 3, 2, 4).reshape(nh, NA_KH, GRID_W, NA_KH * GRID_W)


def _na_body(q_ref, k_ref, v_ref, bias_ref, o_ref, *, nrows):
    w = GRID_W
    nk = NA_KH * w

    def group(i, carry):
        idx = []
        scores = []
        for u in range(NA_ROWS_PER_ITER):
            r = i * NA_ROWS_PER_ITER + u
            rs = jnp.clip(r - NA_KH // 2, 0, nrows - NA_KH)
            qrows = pl.ds(pl.multiple_of(r * w, w), w)
            krows = pl.ds(pl.multiple_of(rs * w, w), nk)
            idx.append((qrows, krows, rs - r + (NA_KH - 1)))
            scores.append(_dot_t(q_ref[0, qrows, :], k_ref[0, krows, :]))
        probs = []
        for (qrows, krows, off), s in zip(idx, scores):
            s = s * (NA_DH ** -0.5) + bias_ref[0, off]
            p = jnp.exp(s - jnp.max(s, axis=-1, keepdims=True))
            probs.append((p.astype(BF16), jnp.sum(p, axis=-1, keepdims=True)))
        for (qrows, krows, off), (p, l) in zip(idx, probs):
            o = jnp.dot(p, v_ref[0, krows, :], preferred_element_type=F32) / l
            o_ref[0, qrows, :] = o.astype(o_ref.dtype)
        return carry

    lax.fori_loop(0, nrows // NA_ROWS_PER_ITER, group, 0)


def _na(proj3, bias_tab, col0):
    bn, seq, _ = proj3.shape
    nrows = seq // GRID_W
    qoff = col0 // NA_DH
    koff = qoff + NA_HEADS
    voff = koff + NA_HEADS
    return pl.pallas_call(
        functools.partial(_na_body, nrows=nrows),
        grid=(bn, NA_HEADS),
        in_specs=[
            pl.BlockSpec((1, seq, NA_DH), lambda b, h: (b, 0, qoff + h)),
            pl.BlockSpec((1, seq, NA_DH), lambda b, h: (b, 0, koff + h)),
            pl.BlockSpec((1, seq, NA_DH), lambda b, h: (b, 0, voff + h)),
            pl.BlockSpec((1, NA_KH, GRID_W, NA_KH * GRID_W), lambda b, h: (h, 0, 0, 0)),
        ],
        out_specs=pl.BlockSpec((1, seq, NA_DH), lambda b, h: (b, 0, h)),
        out_shape=jax.ShapeDtypeStruct((bn, seq, NA_HEADS * NA_DH), BF16),
        compiler_params=_cparams(("parallel", "parallel")),
        name="na",
    )(proj3, proj3, proj3, bias_tab)


CONV_PAD = 16


def _conv_body(x_ref, w_ref, b_ref, o_ref, xp_scr, *, seq):
    rb = min(256, seq)
    ch = x_ref.shape[-1]
    zeros = jnp.zeros((CONV_PAD, ch), xp_scr.dtype)
    xp_scr[0:CONV_PAD, :] = zeros
    xp_scr[seq + CONV_PAD:seq + 2 * CONV_PAD, :] = zeros
    xp_scr[CONV_PAD:seq + CONV_PAD, :] = x_ref[0]

    def blk(i, carry):
        r0 = pl.multiple_of(i * rb, rb)
        xw = xp_scr[pl.ds(r0, rb + 2 * CONV_PAD), :].astype(F32)
        acc = jnp.zeros((rb, ch), F32) + b_ref[...]
        for j in range(SSD_CONV):
            s0 = CONV_PAD - SSD_CONV // 2 + j
            acc = acc + xw[s0:s0 + rb] * w_ref[j:j + 1, :]
        o_ref[0, pl.ds(r0, rb), :] = _silu(acc).astype(o_ref.dtype)
        return carry

    lax.fori_loop(0, seq // rb, blk, 0)


def _conv(proj3, conv_w, conv_b, col0, tc=512):
    bn, seq, _ = proj3.shape
    cdim = conv_w.shape[1]
    c0 = col0 // tc
    return pl.pallas_call(
        functools.partial(_conv_body, seq=seq),
        grid=(bn, cdim // tc),
        in_specs=[
            pl.BlockSpec((1, seq, tc), lambda b, j: (b, 0, c0 + j)),
            pl.BlockSpec((SSD_CONV, tc), lambda b, j: (0, j)),
            pl.BlockSpec((1, tc), lambda b, j: (0, j)),
        ],
        out_specs=pl.BlockSpec((1, seq, tc), lambda b, j: (b, 0, j)),
        out_shape=jax.ShapeDtypeStruct((bn, seq, cdim), BF16),
        scratch_shapes=[pltpu.VMEM((seq + 2 * CONV_PAD, tc), BF16)],
        compiler_params=_cparams(("parallel", "parallel")),
        name="ssd_conv",
    )(proj3, conv_w, conv_b)


def _softplus(x):
    return jnp.maximum(x, 0.0) + jnp.log(1.0 + jnp.exp(-jnp.abs(x)))


def _ssd_body(xs_ref, bm_ref, cm_ref, dt_ref, pb_ref, al_ref, dsk_ref, ex_ref, o_ref,
              st_scr, run_scr, dec_scr, *, seq):
    l = SSD_CHUNK
    nc = seq // l
    nh2 = 2 * SSD_JH
    row_i = lax.broadcasted_iota(jnp.int32, (l, l), 0)
    col_i = lax.broadcasted_iota(jnp.int32, (l, l), 1)
    lane = lax.broadcasted_iota(jnp.int32, (1, LANES), 1)
    first_head = lane < SSD_HEADDIM
    bwd_row = lax.broadcasted_iota(jnp.int32, (nh2, 1), 0) >= SSD_JH
    a_col = -jnp.exp(al_ref[0])

    lower = row_i > col_i
    diag = row_i == col_i
    triu_b = (row_i <= col_i).astype(BF16)

    def decays(n):
        rows = pl.ds(pl.multiple_of(n * l, l), l)
        dtv = _softplus(dt_ref[0, rows, :].T[0:nh2] + pb_ref[0])
        da = dtv * a_col
        hi = da.astype(BF16)
        r1 = da - hi.astype(F32)
        mid = r1.astype(BF16)
        lo = (r1 - mid.astype(F32)).astype(BF16)
        p3 = jnp.dot(jnp.concatenate([hi, mid, lo], axis=0), triu_b, preferred_element_type=F32)
        pre = p3[0:nh2] + p3[nh2:2 * nh2] + p3[2 * nh2:]
        total = pre[:, l - 1:l]
        acum = jnp.where(bwd_row, total - pre + da, pre)
        dec_scr[0, n] = dtv
        dec_scr[1, n] = acum
        dec_scr[2, n] = dtv * jnp.exp(total - acum)
        dec_scr[3, n] = jnp.broadcast_to(jnp.exp(total), (nh2, l))
        dec_scr[4, n] = acum - jnp.log(dtv)

    def chunk_state(n, d):
        rows = pl.ds(pl.multiple_of(n * l, l), l)
        w_state_t = dec_scr[2, n]
        x = xs_ref[0, rows, :]
        bmt = bm_ref[0, rows, :].astype(F32).T
        edge_ch = jnp.sum(dec_scr[3, n][:, 0:1] * ex_ref[d], axis=0, keepdims=True)
        new_st = []
        ns = SSD_STATE
        for jj in range(SSD_JH // 2):
            xp = x[:, LANES * jj:LANES * (jj + 1)]
            bw2 = jnp.concatenate(
                [(bmt * w_state_t[d * SSD_JH + 2 * jj + hh:d * SSD_JH + 2 * jj + hh + 1, :]).astype(BF16)
                 for hh in range(2)], axis=0)
            s2 = jnp.dot(bw2, xp, preferred_element_type=F32)
            new_st.append(jnp.where(first_head, s2[:ns], s2[ns:]))
        st = run_scr[d]
        st_scr[d, n] = st.astype(BF16)
        run_scr[d] = st * edge_ch + jnp.concatenate(new_st, axis=1)

    def chunk_out(n):
        rows = pl.ds(pl.multiple_of(n * l, l), l)
        dtv_t = dec_scr[0, n]
        acum = dec_scr[1, n].T
        key_t = dec_scr[4, n]
        x = xs_ref[0, rows, :]
        cm = cm_ref[0, rows, :]
        cmf = cm.astype(F32)
        cb = _dot_t(cm, bm_ref[0, rows, :])
        st_f = st_scr[0, n]
        st_b = st_scr[1, n]
        rhs = [jnp.concatenate([x[:, LANES * jj:LANES * (jj + 1)],
                                st_f[:, LANES * jj:LANES * (jj + 1)],
                                st_b[:, LANES * jj:LANES * (jj + 1)]], axis=0)
               for jj in range(SSD_JH // 2)]
        slabs = [slice(r0, r0 + SSD_ROW_SLAB) for r0 in range(0, l, SSD_ROW_SLAB)]
        ys = [[] for _ in slabs]
        for jj in range(SSD_JH // 2):
            lhs = []
            for rs in slabs:
                for hh in range(2):
                    cf = 2 * jj + hh
                    cr = SSD_JH + cf
                    col_f = jnp.broadcast_to(acum[rs, cf:cf + 1], (SSD_ROW_SLAB, l))
                    col_r = jnp.broadcast_to(acum[rs, cr:cr + 1], (SSD_ROW_SLAB, l))
                    seg = jnp.where(lower[rs], col_f - key_t[cf:cf + 1, :], col_r - key_t[cr:cr + 1, :])
                    m = cb[rs] * (jnp.exp(seg) + jnp.where(diag[rs], dtv_t[cf:cf + 1, :], 0.0))
                    lhs.append(jnp.concatenate([m.astype(BF16),
                                                (cmf[rs] * jnp.exp(col_f)).astype(BF16),
                                                (cmf[rs] * jnp.exp(col_r)).astype(BF16)], axis=1))
            y4 = jnp.dot(jnp.concatenate(lhs, axis=0), rhs[jj], preferred_element_type=F32)
            for si in range(len(slabs)):
                r0 = 2 * si * SSD_ROW_SLAB
                ys[si].append(jnp.where(first_head, y4[r0:r0 + SSD_ROW_SLAB],
                                        y4[r0 + SSD_ROW_SLAB:r0 + 2 * SSD_ROW_SLAB]))
        for rs, y_parts in zip(slabs, ys):
            y = jnp.concatenate(y_parts, axis=1) + x[rs].astype(F32) * dsk_ref[...]
            o_ref[0, pl.ds(pl.multiple_of(n * l, l) + rs.start, SSD_ROW_SLAB), :] = y.astype(o_ref.dtype)

    def prep(i, carry):
        decays(i)
        return carry

    lax.fori_loop(0, nc, prep, 0, unroll=2)
    run_scr[...] = jnp.zeros_like(run_scr)

    def states(i, carry):
        chunk_state(i, 0)
        chunk_state(nc - 1 - i, 1)
        return carry

    lax.fori_loop(0, nc, states, 0)

    def outputs(n, carry):
        chunk_out(n)
        return carry

    lax.fori_loop(0, nc, outputs, 0)


def _ssd(xbc_act, dt3, pb, al, dskip, expand):
    bn, seq, _ = xbc_act.shape
    gw = SSD_JH * SSD_HEADDIM
    boff = SSD_INNER // SSD_STATE
    coff = boff + SSD_GROUPS
    return pl.pallas_call(
        functools.partial(_ssd_body, seq=seq),
        grid=(bn, SSD_GROUPS),
        in_specs=[
            pl.BlockSpec((1, seq, gw), lambda b, g: (b, 0, g)),
            pl.BlockSpec((1, seq, SSD_STATE), lambda b, g: (b, 0, boff + g)),
            pl.BlockSpec((1, seq, SSD_STATE), lambda b, g: (b, 0, coff + g)),
            pl.BlockSpec((1, seq, LANES), lambda b, g: (b, 0, g)),
            pl.BlockSpec((1, 2 * SSD_JH, 1), lambda b, g: (g, 0, 0)),
            pl.BlockSpec((1, 2 * SSD_JH, 1), lambda b, g: (g, 0, 0)),
            pl.BlockSpec((1, gw), lambda b, g: (0, g)),
            pl.BlockSpec((2, 2 * SSD_JH, gw), lambda b, g: (0, 0, 0)),
        ],
        out_specs=pl.BlockSpec((1, seq, gw), lambda b, g: (b, 0, g)),
        out_shape=jax.ShapeDtypeStruct((bn, seq, SSD_INNER), BF16),
        scratch_shapes=[pltpu.VMEM((2, seq // SSD_CHUNK, SSD_STATE, gw), BF16),
                        pltpu.VMEM((2, SSD_STATE, gw), F32),
                        pltpu.VMEM((5, seq // SSD_CHUNK, 2 * SSD_JH, SSD_CHUNK), F32)],
        compiler_params=_cparams(("parallel", "parallel")),
        name="ssd_scan",
    )(xbc_act, xbc_act, xbc_act, dt3, pb, al, dskip, expand)


def _rope_tables(seq):
    half = SWA_DH // 2
    inv = ROPE_THETA ** (-np.arange(half, dtype=np.float64) / half)
    ang = np.arange(seq, dtype=np.float64)[:, None] * inv[None, :]
    cos, sin = np.cos(ang), np.sin(ang)
    return (jnp.asarray(np.concatenate([cos, cos], axis=1), F32),
            jnp.asarray(np.concatenate([-sin, sin], axis=1), F32))


def _rope(x, cos, sin_signed):
    return x * cos + pltpu.roll(x, SWA_DH // 2, 1) * sin_signed


def _swa_body(sink_ref, q_ref, k_ref, v_ref, cos_ref, sin_ref, o_ref, kr_scr, *, seq):
    wb = SWA_BLOCK
    nb = seq // wb
    nkeys = 3 * wb
    kvh = pl.program_id(1)
    rb = min(512, seq)

    def krope(i, carry):
        rows = pl.ds(pl.multiple_of(i * rb, rb), rb)
        kr_scr[rows, :] = _rope(k_ref[0, rows, :].astype(F32), cos_ref[rows, :],
                                sin_ref[rows, :]).astype(BF16)
        return carry

    lax.fori_loop(0, seq // rb, krope, 0)

    qpos_l = lax.broadcasted_iota(jnp.int32, (wb, nkeys), 0)
    kpos_l = lax.broadcasted_iota(jnp.int32, (wb, nkeys), 1)

    def blk(n, carry):
        rows = pl.ds(pl.multiple_of(n * wb, wb), wb)
        ks = pl.multiple_of(jnp.clip(n - 1, 0, nb - 3) * wb, wb)
        krows = pl.ds(ks, nkeys)
        cos = cos_ref[rows, :]
        sin = sin_ref[rows, :]
        kk = kr_scr[krows, :]
        vv = v_ref[0, krows, :]
        valid = jnp.abs((qpos_l + n * wb) - (kpos_l + ks)) <= SWA_WINDOW
        qb = q_ref[0, rows, :].astype(F32)
        q4 = jnp.concatenate(
            [(_rope(qb[:, SWA_DH * hh:SWA_DH * (hh + 1)], cos, sin) * (SWA_DH ** -0.5)).astype(BF16)
             for hh in range(SWA_GQ)], axis=0)
        s4 = _dot_t(q4, kk)
        probs = []
        dens = []
        for hh in range(SWA_GQ):
            sink = sink_ref[kvh * SWA_GQ + hh]
            s = jnp.where(valid, s4[wb * hh:wb * (hh + 1)], NEG_BIG)
            m = jnp.maximum(jnp.max(s, axis=-1, keepdims=True), sink)
            p = jnp.exp(s - m)
            probs.append(p.astype(BF16))
            dens.append(jnp.sum(p, axis=-1, keepdims=True) + jnp.exp(sink - m))
        o4 = jnp.dot(jnp.concatenate(probs, axis=0), vv, preferred_element_type=F32)
        for hh in range(SWA_GQ):
            o = o4[wb * hh:wb * (hh + 1)] / dens[hh]
            o_ref[0, rows, SWA_DH * hh:SWA_DH * (hh + 1)] = o.astype(o_ref.dtype)
        return carry

    lax.fori_loop(0, nb, blk, 0)


def _swa(proj3, sink, cos, sin, col0):
    bn, seq, _ = proj3.shape
    qw = SWA_GQ * SWA_DH
    qoff = col0 // qw
    koff = (col0 + SWA_HEADS * SWA_DH) // SWA_DH
    voff = koff + SWA_KV
    return pl.pallas_call(
        functools.partial(_swa_body, seq=seq),
        grid=(bn, SWA_KV),
        in_specs=[
            pl.BlockSpec(memory_space=pltpu.SMEM),
            pl.BlockSpec((1, seq, qw), lambda b, h: (b, 0, qoff + h)),
            pl.BlockSpec((1, seq, SWA_DH), lambda b, h: (b, 0, koff + h)),
            pl.BlockSpec((1, seq, SWA_DH), lambda b, h: (b, 0, voff + h)),
            pl.BlockSpec((seq, SWA_DH), lambda b, h: (0, 0)),
            pl.BlockSpec((seq, SWA_DH), lambda b, h: (0, 0)),
        ],
        out_specs=pl.BlockSpec((1, seq, qw), lambda b, h: (b, 0, h)),
        out_shape=jax.ShapeDtypeStruct((bn, seq, SWA_HEADS * SWA_DH), BF16),
        scratch_shapes=[pltpu.VMEM((seq, SWA_DH), BF16)],
        compiler_params=_cparams(("parallel", "parallel")),
        name="swa",
    )(sink, proj3, proj3, proj3, cos, sin)


def _pad_cols(w, width):
    return jnp.pad(w, ((0, 0), (0, width - w.shape[1])))


def _even_mixers(x2, bn, seq, g0, w_in, w_decay, b_decay, gla_norm, rpb):
    d = x2.shape[1]
    n_gla = 2 * GLA_HEADS * GLA_DK + 2 * GLA_HEADS * GLA_DV
    n_lr = 2 * GLA_LOWRANK
    w_main = jnp.concatenate([w_in[:, :n_gla], w_in[:, n_gla + n_lr:]], axis=1).astype(BF16)
    w_small = _pad_cols(w_in[:, n_gla:n_gla + n_lr], LANES).astype(BF16)
    proj, small = _inproj(x2, g0, w_main, w_small)
    proj3 = proj.reshape(bn, seq, -1)
    small3 = small.reshape(bn, seq, -1)
    wdec = jnp.zeros((2, LANES, w_decay.shape[-1]), F32)
    for dd in range(2):
        wdec = wdec.at[dd, dd * GLA_LOWRANK:(dd + 1) * GLA_LOWRANK].set(w_decay[dd])
    o_a = _gla(proj3, small3, wdec, b_decay, gla_norm[None, :])
    o_b = _na(proj3, _na_bias_table(rpb), n_gla)
    return o_a.reshape(-1, o_a.shape[-1]), o_b.reshape(-1, o_b.shape[-1])


def _per_group(v):
    return v.reshape(2, SSD_GROUPS, SSD_JH).transpose(1, 0, 2).reshape(SSD_GROUPS, 2 * SSD_JH)


def _odd_mixers(x2, bn, seq, g0, w_in, conv_w, conv_b, dt_bias, a_log, d_skip, sink, cos, sin):
    conv_dim = SSD_INNER + 2 * SSD_GROUPS * SSD_STATE
    n_z = SSD_INNER
    n_dt = 2 * SSD_HEADS
    c_dt = n_z + conv_dim
    w_main = jnp.concatenate([w_in[:, :c_dt], w_in[:, c_dt + n_dt:]], axis=1).astype(BF16)
    w_dt = w_in[:, c_dt:c_dt + n_dt].reshape(-1, 2, SSD_GROUPS, SSD_JH).transpose(0, 2, 1, 3)
    w_dt = w_dt.reshape(-1, SSD_GROUPS, 2 * SSD_JH)
    w_small = jnp.pad(w_dt, ((0, 0), (0, 0), (0, LANES - 2 * SSD_JH))).reshape(-1, SSD_GROUPS * LANES)
    proj, small = _inproj(x2, g0, w_main, w_small.astype(BF16))
    proj3 = proj.reshape(bn, seq, -1)
    dt3 = small.reshape(bn, seq, SSD_GROUPS * LANES)
    xbc_act = _conv(proj3, conv_w, conv_b[None, :], n_z)
    pbg = _per_group(dt_bias)
    alg = _per_group(a_log)
    head_of_ch = np.arange(SSD_JH * SSD_HEADDIM) // SSD_HEADDIM
    expand = jnp.asarray(np.arange(2 * SSD_JH)[None, :, None]
                         == (np.arange(2)[:, None, None] * SSD_JH + head_of_ch[None, None, :]), F32)
    dskip = jnp.repeat(d_skip.astype(F32), SSD_HEADDIM)[None, :]
    y_pre = _ssd(xbc_act, dt3, pbg[:, :, None], alg[:, :, None], dskip, expand)
    o_d = _swa(proj3, sink, cos, sin, c_dt)
    return y_pre.reshape(-1, SSD_INNER), proj, o_d.reshape(-1, o_d.shape[-1])


def kernel(x, norm_gains, ffn_w_gate, ffn_w_up, ffn_w_down, even_w_in, even_w_out, gla_w_decay,
           gla_b_decay, gla_norm, na_rpb, odd_w_in, odd_w_out, ssd_conv_w, ssd_conv_b, ssd_dt_bias,
           ssd_a_log, ssd_d, ssd_norm, swa_sink):
    bn, seq, d = x.shape
    depth = norm_gains.shape[0]
    x2 = x.reshape(bn * seq, d)
    cos, sin = _rope_tables(seq)
    ffn_w = (ffn_w_gate, ffn_w_up, ffn_w_down)
    even_w_out_b = even_w_out.astype(BF16)
    odd_w_out_b = odd_w_out.astype(BF16)
    for layer in range(depth):
        g = norm_gains[layer][:, None, :]
        i = layer // 2
        if layer % 2 == 0:
            o_a, o_b = _even_mixers(x2, bn, seq, g[0], even_w_in[i], gla_w_decay[i], gla_b_decay[i],
                                    gla_norm[i], na_rpb[i])
            x2, wg, wu, wd = _outproj(x2, o_a, o_b, even_w_out_b, i, g[1], ffn_w, layer)
        else:
            y_pre, proj, o_d = _odd_mixers(x2, bn, seq, g[0], odd_w_in[i], ssd_conv_w[i], ssd_conv_b[i],
                                           ssd_dt_bias[i], ssd_a_log[i], ssd_d[i], swa_sink[i], cos, sin)
            x2, wg, wu, wd = _outproj(x2, y_pre, o_d, odd_w_out_b, i, g[1], ffn_w, layer,
                                      z_src=proj, ssd_gain=ssd_norm[i][None, :])
        x2 = _ffn(x2, g[2], wg, wu, wd, g[3])
    return x2.reshape(bn, seq, d)
```

```python
import functools

import jax
import jax.numpy as jnp
import numpy as np
from jax import lax
from jax.experimental import pallas as pl
from jax.experimental.pallas import tpu as pltpu

F32 = jnp.float32
BF16 = jnp.bfloat16
HIGHEST = lax.Precision.HIGHEST

EPS = 1e-6
ROPE_THETA = 10000.0
GRID_W = 64
LANES = 128

GLA_DK = 64
GLA_DV = 128
GLA_HEADS = 8
GLA_LOWRANK = 16
GLA_TAU = 16.0
GLA_CHUNK = 64
GLA_SUB = 16
GLA_EXP_CLAMP = 60.0
GLA_TRI_BLOCK = 256

NA_DH = 128
NA_HEADS = 8
NA_KH = 8
NA_KW = 16
NA_ROWS_PER_ITER = 16
NEG_BIG = -1e30

SSD_INNER = 1024
SSD_HEADDIM = 64
SSD_HEADS = 16
SSD_GROUPS = 2
SSD_STATE = 128
SSD_CONV = 5
SSD_CHUNK = 128
SSD_JH = SSD_HEADS // SSD_GROUPS
SSD_ROW_SLAB = 64

SWA_DH = 128
SWA_HEADS = 8
SWA_KV = 2
SWA_GQ = SWA_HEADS // SWA_KV
SWA_WINDOW = 128
SWA_BLOCK = 128

VMEM_LIMIT = 52 * 1024 * 1024


FFN_NORM_ROWS = 256
FFN_VMEM_LIMIT = 58 * 1024 * 1024


def _cparams(sem, vmem_limit=VMEM_LIMIT):
    return pltpu.CompilerParams(dimension_semantics=sem, vmem_limit_bytes=vmem_limit)


def _rms(x, g):
    return x * lax.rsqrt(jnp.mean(x * x, axis=-1, keepdims=True) + EPS) * g


def _silu(x):
    return x * (1.0 / (1.0 + jnp.exp(-x)))


def _dot_t(a, b):
    return lax.dot_general(a, b, (((1,), (1,)), ((), ())), preferred_element_type=F32)


def _inproj_body(x_ref, g_ref, w_ref, ws_ref, o_ref, os_ref, h_scr):
    j = pl.program_id(1)
    half = x_ref.shape[0] // 2

    @pl.when(j == 0)
    def _():
        for r0 in (0, half):
            rs = pl.ds(r0, half)
            h = _rms(x_ref[rs, :], g_ref[...]).astype(BF16)
            h_scr[rs, :] = h
            os_ref[rs, :] = jnp.dot(h, ws_ref[...], preferred_element_type=F32)
            o_ref[rs, :] = jnp.dot(h, w_ref[...], preferred_element_type=F32).astype(o_ref.dtype)

    @pl.when(j > 0)
    def _():
        o_ref[...] = jnp.dot(h_scr[...], w_ref[...], preferred_element_type=F32).astype(o_ref.dtype)


def _inproj(x2, g, w_main, w_small, tm=1024, tn=2048):
    n, d = x2.shape
    nm = w_main.shape[1]
    ns = w_small.shape[1]
    return pl.pallas_call(
        _inproj_body,
        grid=(n // tm, nm // tn),
        in_specs=[
            pl.BlockSpec((tm, d), lambda i, j: (i, 0)),
            pl.BlockSpec((1, d), lambda i, j: (0, 0)),
            pl.BlockSpec((d, tn), lambda i, j: (0, j)),
            pl.BlockSpec((d, ns), lambda i, j: (0, 0)),
        ],
        out_specs=[
            pl.BlockSpec((tm, tn), lambda i, j: (i, j)),
            pl.BlockSpec((tm, ns), lambda i, j: (i, 0)),
        ],
        out_shape=[jax.ShapeDtypeStruct((n, nm), BF16), jax.ShapeDtypeStruct((n, ns), F32)],
        scratch_shapes=[pltpu.VMEM((tm, d), BF16)],
        compiler_params=_cparams(("parallel", "arbitrary")),
        name="inproj",
    )(x2, g, w_main, w_small)


def _cast_weight_slices(w_refs):
    for src, dst in zip(w_refs[:3], w_refs[3:]):
        dst[...] = src[...].astype(BF16)


def _outproj_out(rest):
    if len(rest) == 1:
        return rest[0]
    _cast_weight_slices(rest[:3] + rest[4:])
    return rest[3]


def _outproj_even_body(a_ref, b_ref, wa_ref, wb_ref, g_ref, x_ref, *rest):
    o_ref = _outproj_out(rest)
    mix = (jnp.dot(a_ref[...], wa_ref[...], preferred_element_type=F32)
           + jnp.dot(b_ref[...], wb_ref[...], preferred_element_type=F32))
    o_ref[...] = x_ref[...] + _rms(mix, g_ref[...])


def _outproj_odd_body(y_ref, z_ref, ng_ref, b_ref, wa_ref, wb_ref, g_ref, x_ref, *rest):
    o_ref = _outproj_out(rest)
    y = y_ref[...].astype(F32) * _silu(z_ref[...].astype(F32))
    a = _rms(y, ng_ref[...]).astype(BF16)
    mix = (jnp.dot(a, wa_ref[...], preferred_element_type=F32)
           + jnp.dot(b_ref[...], wb_ref[...], preferred_element_type=F32))
    o_ref[...] = x_ref[...] + _rms(mix, g_ref[...])


def _outproj(x2, a, b, w_out, li, g, ffn_w=None, layer=None, z_src=None, ssd_gain=None, tm=512):
    n, d = x2.shape
    half = a.shape[1]
    steps = n // tm
    row = lambda i: (i, 0)
    fixed = lambda i: (0, 0)
    w_specs = [pl.BlockSpec((None, half, d), lambda i: (li, 0, 0)),
               pl.BlockSpec((None, half, d), lambda i: (li, 1, 0))]
    tail_specs = [pl.BlockSpec((1, d), fixed), pl.BlockSpec((tm, d), row)]
    ffn_w = ffn_w or ()
    cast_in = [pl.BlockSpec((None, w.shape[1] // steps, w.shape[2]), lambda i: (layer, i, 0)) for w in ffn_w]
    cast_out = [pl.BlockSpec((w.shape[1] // steps, w.shape[2]), row) for w in ffn_w]
    cast_shapes = [jax.ShapeDtypeStruct(w.shape[1:], BF16) for w in ffn_w]
    if z_src is None:
        body = _outproj_even_body
        in_specs = [pl.BlockSpec((tm, half), row), pl.BlockSpec((tm, half), row)] + w_specs + tail_specs
        args = (a, b, w_out, w_out, g, x2)
    else:
        body = _outproj_odd_body
        in_specs = ([pl.BlockSpec((tm, half), row), pl.BlockSpec((tm, half), row),
                     pl.BlockSpec((1, half), fixed), pl.BlockSpec((tm, half), row)]
                    + w_specs + tail_specs)
        args = (a, z_src, ssd_gain, b, w_out, w_out, g, x2)
    return pl.pallas_call(
        body,
        grid=(steps,),
        in_specs=in_specs + cast_in,
        out_specs=[pl.BlockSpec((tm, d), row)] + cast_out,
        out_shape=[jax.ShapeDtypeStruct((n, d), F32)] + cast_shapes,
        compiler_params=_cparams(("parallel",)),
        name="outproj",
    )(*args, *ffn_w)


def _ffn_body(x_ref, g2_ref, wg_ref, wu_ref, wd_ref, g3_ref, *rest):
    if len(rest) == 2:
        o_ref, h_scr = rest
    else:
        o_ref, h_scr = rest[3], rest[7]
        _cast_weight_slices(rest[:3] + rest[4:7])
    k = pl.program_id(1)
    last = pl.num_programs(1) - 1
    half = x_ref.shape[0] // 2

    def pieces(r0):
        return [pl.ds(r0 + p, FFN_NORM_ROWS) for p in range(0, half, FFN_NORM_ROWS)]

    def partial_down(rs):
        h = h_scr[rs, :]
        gate = jnp.dot(h, wg_ref[...], preferred_element_type=F32)
        up = jnp.dot(h, wu_ref[...], preferred_element_type=F32)
        return jnp.dot((_silu(gate) * up).astype(BF16), wd_ref[...], preferred_element_type=F32)

    @pl.when(k == 0)
    def _():
        for r0 in (0, half):
            for ps in pieces(r0):
                h_scr[ps, :] = _rms(x_ref[ps, :], g2_ref[...]).astype(BF16)
            o_ref[pl.ds(r0, half), :] = partial_down(pl.ds(r0, half))

    @pl.when(jnp.logical_and(k > 0, k < last))
    def _():
        for r0 in (0, half):
            o_ref[pl.ds(r0, half), :] += partial_down(pl.ds(r0, half))

    @pl.when(k == last)
    def _():
        for r0 in (0, half):
            o_ref[pl.ds(r0, half), :] += partial_down(pl.ds(r0, half))
            for ps in pieces(r0):
                o_ref[ps, :] = x_ref[ps, :] + _rms(o_ref[ps, :], g3_ref[...])


def _ffn(x2, g2, wg, wu, wd, g3, ffn_w=None, next_layer=None, tm=1024, th=512):
    n, d = x2.shape
    hid = wd.shape[0]
    ni, nk = n // tm, hid // th
    in_specs = [
        pl.BlockSpec((tm, d), lambda i, k: (i, 0)),
        pl.BlockSpec((1, d), lambda i, k: (0, 0)),
        pl.BlockSpec((d, th), lambda i, k: (0, k)),
        pl.BlockSpec((d, th), lambda i, k: (0, k)),
        pl.BlockSpec((th, d), lambda i, k: (k, 0)),
        pl.BlockSpec((1, d), lambda i, k: (0, 0)),
    ]
    out_specs = [pl.BlockSpec((tm, d), lambda i, k: (i, 0))]
    out_shape = [jax.ShapeDtypeStruct((n, d), F32)]
    args = [x2, g2, wg, wu, wd, g3]
    if ffn_w is not None:
        up_blk, down_blk = (d // ni, hid // nk), (hid // nk, d // ni)
        in_specs += [pl.BlockSpec((None,) + up_blk, lambda i, k: (next_layer, i, k)),
                     pl.BlockSpec((None,) + up_blk, lambda i, k: (next_layer, i, k)),
                     pl.BlockSpec((None,) + down_blk, lambda i, k: (next_layer, k, i))]
        out_specs += [pl.BlockSpec(up_blk, lambda i, k: (i, k)),
                      pl.BlockSpec(up_blk, lambda i, k: (i, k)),
                      pl.BlockSpec(down_blk, lambda i, k: (k, i))]
        out_shape += [jax.ShapeDtypeStruct(w.shape[1:], BF16) for w in ffn_w]
        args += list(ffn_w)
    return pl.pallas_call(
        _ffn_body,
        grid=(ni, nk),
        in_specs=in_specs,
        out_specs=out_specs,
        out_shape=out_shape,
        scratch_shapes=[pltpu.VMEM((tm, d), BF16)],
        compiler_params=_cparams(("parallel", "arbitrary"), FFN_VMEM_LIMIT),
        name="ffn",
    )(*args)


def _gla_body(q_ref, k_ref, v_ref, g_ref, lr_ref, wd_ref, bd_ref, gn_ref, o_ref,
              bc_scr, o_scr, st_scr, *, seq):
    c = GLA_CHUNK
    nc = seq // c
    half = nc // 2
    nsub = c // GLA_SUB
    rb = min(512, seq)
    row_i = lax.broadcasted_iota(jnp.int32, (c, c), 0)
    col_i = lax.broadcasted_iota(jnp.int32, (c, c), 1)
    tri = (row_i >= col_i, row_i <= col_i)
    lane = lax.broadcasted_iota(jnp.int32, (1, LANES), 1)
    hmask = (lane < GLA_DK, lane >= GLA_DK)
    own_head = ((lax.broadcasted_iota(jnp.int32, (2 * GLA_DV, LANES), 0) < GLA_DV)
                == (lax.broadcasted_iota(jnp.int32, (2 * GLA_DV, LANES), 1) < GLA_DK))

    w2 = jnp.concatenate([wd_ref[0], wd_ref[1]], axis=1)
    w_hi = w2.astype(BF16)
    w_lo = (w2 - w_hi.astype(F32)).astype(BF16)
    w_cat = jnp.concatenate([w_hi, w_lo, w_hi], axis=0)
    tb = GLA_TRI_BLOCK
    brow = lax.broadcasted_iota(jnp.int32, (tb, tb), 0)
    bcol = lax.broadcasted_iota(jnp.int32, (tb, tb), 1)
    same_chunk = (brow // c) == (bcol // c)
    blk_tri = ((same_chunk & (brow >= bcol)).astype(BF16), (same_chunk & (brow <= bcol)).astype(BF16))

    def pre(i, carry):
        r0 = pl.multiple_of(i * rb, rb)
        lr = lr_ref[0, pl.ds(r0, rb), :]
        lr_hi = lr.astype(BF16)
        lr_lo = (lr - lr_hi.astype(F32)).astype(BF16)
        z2 = jnp.dot(jnp.concatenate([lr_hi, lr_hi, lr_lo], axis=1), w_cat, preferred_element_type=F32)
        las = []
        for d in range(2):
            z = z2[:, LANES * d:LANES * (d + 1)] + bd_ref[d:d + 1, :]
            la = (jnp.minimum(z, 0.0) - jnp.log(1.0 + jnp.exp(-jnp.abs(z)))) * (1.0 / GLA_TAU)
            la_hi = la.astype(BF16)
            las.append(jnp.concatenate([la_hi, (la - la_hi.astype(F32)).astype(BF16)], axis=1))
        for d in range(2):
            for t in range(rb // tb):
                s2 = jnp.dot(blk_tri[d], las[d][t * tb:(t + 1) * tb], preferred_element_type=F32)
                bc_scr[d, pl.ds(r0 + t * tb, tb), :] = s2[:, :LANES] + s2[:, LANES:]
        return carry

    lax.fori_loop(0, seq // rb, pre, 0)

    def stage1(n, d):
        rows = pl.ds(pl.multiple_of(n * c, c), c)
        bc = bc_scr[d, rows, :]
        q = q_ref[0, rows, :].astype(F32) * (GLA_DK ** -0.5)
        k = k_ref[0, rows, :].astype(F32)
        v = v_ref[0, rows, :]
        b_edge = bc[c - 1:c] if d == 0 else bc[0:1]
        qhat = q * jnp.exp(bc)
        khat = (k * jnp.exp(b_edge - bc)).astype(BF16)
        att_rows = ([], [])
        for i in range(nsub):
            lo = GLA_SUB * i
            hi = lo + GLA_SUB
            if d == 0:
                ref = bc[lo - 1:lo] if i > 0 else jnp.zeros((1, LANES), F32)
            else:
                ref = bc[hi:hi + 1] if i < nsub - 1 else jnp.zeros((1, LANES), F32)
            qi = q[lo:hi] * jnp.exp(bc[lo:hi] - ref)
            ki = (k * jnp.exp(jnp.minimum(ref - bc, GLA_EXP_CLAMP))).astype(BF16)
            q2 = jnp.concatenate([jnp.where(hmask[0], qi, 0.0), jnp.where(hmask[1], qi, 0.0)], axis=0)
            a2 = _dot_t(q2.astype(BF16), ki)
            att_rows[0].append(a2[:GLA_SUB])
            att_rows[1].append(a2[GLA_SUB:])
        kv2 = lax.dot_general(v, khat, (((0,), (0,)), ((), ())), preferred_element_type=F32)
        kv2 = jnp.where(own_head, kv2, 0.0)
        att2 = jnp.concatenate(
            [jnp.where(tri[d], jnp.concatenate(att_rows[hh], axis=0), 0.0).astype(BF16)
             for hh in range(2)], axis=1)
        zero_v = jnp.zeros((c, GLA_DV), BF16)
        v_blockdiag = jnp.concatenate(
            [jnp.concatenate([v[:, :GLA_DV], zero_v], axis=1),
             jnp.concatenate([zero_v, v[:, GLA_DV:]], axis=1)], axis=0)
        return d, att2, v_blockdiag, kv2, qhat.astype(BF16), jnp.exp(b_edge)

    def stage2(s1, states):
        d, att2, v_blockdiag, kv2, qhat, decay = s1
        st = states[d]
        o = jnp.dot(att2, v_blockdiag, preferred_element_type=F32) + _dot_t(qhat, st.astype(BF16))
        states[d] = st * decay + kv2
        return o

    def finish(n, o):
        rows = pl.ds(pl.multiple_of(n * c, c), c)
        parts = []
        for hh in range(2):
            oh = o[:, GLA_DV * hh:GLA_DV * (hh + 1)]
            parts.append(oh * lax.rsqrt(jnp.mean(oh * oh, axis=-1, keepdims=True) + EPS))
        o = jnp.concatenate(parts, axis=1) * gn_ref[...]
        o_ref[0, rows, :] = (o * _silu(g_ref[0, rows, :].astype(F32))).astype(o_ref.dtype)

    st_scr[...] = jnp.zeros_like(st_scr)

    cpi = next(u for u in (4, 2, 1) if half % u == 0)

    def walk(i, base_f, base_b):
        work = []
        for u in range(cpi):
            work.append((base_f + i * cpi + u, 0))
            work.append((base_b - i * cpi - u, 1))
        prepared = [stage1(n, d) for n, d in work]
        states = [st_scr[d] for d in range(2)]
        outs = [(n, stage2(s1, states)) for (n, _), s1 in zip(work, prepared)]
        for d in range(2):
            st_scr[d] = states[d]
        return outs

    def first(i, carry):
        for n, o in walk(i, 0, nc - 1):
            o_scr[pl.ds(pl.multiple_of(n * c, c), c), :] = o
        return carry

    lax.fori_loop(0, half // cpi, first, 0)

    def second(i, carry):
        for n, o in walk(i, half, half - 1):
            finish(n, o_scr[pl.ds(pl.multiple_of(n * c, c), c), :] + o)
        return carry

    lax.fori_loop(0, half // cpi, second, 0)


def _gla(proj3, small3, wdec_pad, bdec, gnorm):
    bn, seq, _ = proj3.shape
    npair = GLA_HEADS // 2
    qoff = 0
    koff = (GLA_HEADS * GLA_DK) // LANES
    voff = (2 * GLA_HEADS * GLA_DK) // (2 * GLA_DV)
    goff = voff + npair
    return pl.pallas_call(
        functools.partial(_gla_body, seq=seq),
        grid=(bn, npair),
        in_specs=[
            pl.BlockSpec((1, seq, LANES), lambda b, p: (b, 0, qoff + p)),
            pl.BlockSpec((1, seq, LANES), lambda b, p: (b, 0, koff + p)),
            pl.BlockSpec((1, seq, 2 * GLA_DV), lambda b, p: (b, 0, voff + p)),
            pl.BlockSpec((1, seq, 2 * GLA_DV), lambda b, p: (b, 0, goff + p)),
            pl.BlockSpec((1, seq, LANES), lambda b, p: (b, 0, 0)),
            pl.BlockSpec((2, LANES, LANES), lambda b, p: (0, 0, p)),
            pl.BlockSpec((2, LANES), lambda b, p: (0, p)),
            pl.BlockSpec((1, 2 * GLA_DV), lambda b, p: (0, p)),
        ],
        out_specs=pl.BlockSpec((1, seq, 2 * GLA_DV), lambda b, p: (b, 0, p)),
        out_shape=jax.ShapeDtypeStruct((bn, seq, GLA_HEADS * GLA_DV), BF16),
        scratch_shapes=[
            pltpu.VMEM((2, seq, LANES), F32),
            pltpu.VMEM((seq, 2 * GLA_DV), F32),
            pltpu.VMEM((2, 2 * GLA_DV, LANES), F32),
        ],
        compiler_params=_cparams(("parallel", "parallel")),
        name="gla",
    )(proj3, proj3, proj3, proj3, small3, wdec_pad, bdec, gnorm)


def _na_bias_table(rpb):
    nh, ndr, ndc = rpb.shape
    cols = np.arange(GRID_W)
    cs = np.clip(cols - NA_KW // 2, 0, GRID_W - NA_KW)
    valid = (cols[None, :] >= cs[:, None]) & (cols[None, :] < cs[:, None] + NA_KW)
    dc = cols[None, :] - cols[:, None] + NA_KW - 1
    onehot = (np.arange(ndc)[:, None, None] == dc[None]) & valid[None]
    toep = jnp.dot(rpb.reshape(nh * ndr, ndc).astype(F32),
                   jnp.asarray(onehot.reshape(ndc, -1), F32), precision=HIGHEST)
    toep = jnp.where(jnp.asarray(valid.reshape(1, -1)), toep, NEG_BIG)
    toep = toep.reshape(nh, ndr, GRID_W, GRID_W)
    t = jnp.stack([toep[:, off:off + NA_KH] for off in range(NA_KH)], axis=1)
    return t.transpose(0, 1, 3, 2, 4).reshape(nh, NA_KH, GRID_W, NA_KH * GRID_W)


def _na_body(q_ref, k_ref, v_ref, bias_ref, o_ref, *, nrows):
    w = GRID_W
    nk = NA_KH * w

    def group(i, carry):
        idx = []
        scores = []
        for u in range(NA_ROWS_PER_ITER):
            r = i * NA_ROWS_PER_ITER + u
            rs = jnp.clip(r - NA_KH // 2, 0, nrows - NA_KH)
            qrows = pl.ds(pl.multiple_of(r * w, w), w)
            krows = pl.ds(pl.multiple_of(rs * w, w), nk)
            idx.append((qrows, krows, rs - r + (NA_KH - 1)))
            scores.append(_dot_t(q_ref[0, qrows, :], k_ref[0, krows, :]))
        probs = []
        for (qrows, krows, off), s in zip(idx, scores):
            s = s * (NA_DH ** -0.5) + bias_ref[0, off]
            p = jnp.exp(s - jnp.max(s, axis=-1, keepdims=True))
            probs.append((p.astype(BF16), jnp.sum(p, axis=-1, keepdims=True)))
        for (qrows, krows, off), (p, l) in zip(idx, probs):
            o = jnp.dot(p, v_ref[0, krows, :], preferred_element_type=F32) / l
            o_ref[0, qrows, :] = o.astype(o_ref.dtype)
        return carry

    lax.fori_loop(0, nrows // NA_ROWS_PER_ITER, group, 0)


def _na(proj3, bias_tab, col0):
    bn, seq, _ = proj3.shape
    nrows = seq // GRID_W
    qoff = col0 // NA_DH
    koff = qoff + NA_HEADS
    voff = koff + NA_HEADS
    return pl.pallas_call(
        functools.partial(_na_body, nrows=nrows),
        grid=(bn, NA_HEADS),
        in_specs=[
            pl.BlockSpec((1, seq, NA_DH), lambda b, h: (b, 0, qoff + h)),
            pl.BlockSpec((1, seq, NA_DH), lambda b, h: (b, 0, koff + h)),
            pl.BlockSpec((1, seq, NA_DH), lambda b, h: (b, 0, voff + h)),
            pl.BlockSpec((1, NA_KH, GRID_W, NA_KH * GRID_W), lambda b, h: (h, 0, 0, 0)),
        ],
        out_specs=pl.BlockSpec((1, seq, NA_DH), lambda b, h: (b, 0, h)),
        out_shape=jax.ShapeDtypeStruct((bn, seq, NA_HEADS * NA_DH), BF16),
        compiler_params=_cparams(("parallel", "parallel")),
        name="na",
    )(proj3, proj3, proj3, bias_tab)


CONV_PAD = 16


def _conv_body(x_ref, w_ref, b_ref, o_ref, xp_scr, *, seq):
    rb = min(256, seq)
    ch = x_ref.shape[-1]
    zeros = jnp.zeros((CONV_PAD, ch), xp_scr.dtype)
    xp_scr[0:CONV_PAD, :] = zeros
    xp_scr[seq + CONV_PAD:seq + 2 * CONV_PAD, :] = zeros
    xp_scr[CONV_PAD:seq + CONV_PAD, :] = x_ref[0]

    def blk(i, carry):
        r0 = pl.multiple_of(i * rb, rb)
        xw = xp_scr[pl.ds(r0, rb + 2 * CONV_PAD), :].astype(F32)
        acc = jnp.zeros((rb, ch), F32) + b_ref[...]
        for j in range(SSD_CONV):
            s0 = CONV_PAD - SSD_CONV // 2 + j
            acc = acc + xw[s0:s0 + rb] * w_ref[j:j + 1, :]
        o_ref[0, pl.ds(r0, rb), :] = _silu(acc).astype(o_ref.dtype)
        return carry

    lax.fori_loop(0, seq // rb, blk, 0)


def _conv(proj3, conv_w, conv_b, col0, tc=512):
    bn, seq, _ = proj3.shape
    cdim = conv_w.shape[1]
    c0 = col0 // tc
    return pl.pallas_call(
        functools.partial(_conv_body, seq=seq),
        grid=(bn, cdim // tc),
        in_specs=[
            pl.BlockSpec((1, seq, tc), lambda b, j: (b, 0, c0 + j)),
            pl.BlockSpec((SSD_CONV, tc), lambda b, j: (0, j)),
            pl.BlockSpec((1, tc), lambda b, j: (0, j)),
        ],
        out_specs=pl.BlockSpec((1, seq, tc), lambda b, j: (b, 0, j)),
        out_shape=jax.ShapeDtypeStruct((bn, seq, cdim), BF16),
        scratch_shapes=[pltpu.VMEM((seq + 2 * CONV_PAD, tc), BF16)],
        compiler_params=_cparams(("parallel", "parallel")),
        name="ssd_conv",
    )(proj3, conv_w, conv_b)


def _softplus(x):
    return jnp.maximum(x, 0.0) + jnp.log(1.0 + jnp.exp(-jnp.abs(x)))


def _ssd_body(xs_ref, bm_ref, cm_ref, dt_ref, pb_ref, al_ref, dsk_ref, ex_ref, o_ref,
              st_scr, run_scr, dec_scr, *, seq):
    l = SSD_CHUNK
    nc = seq // l
    nh2 = 2 * SSD_JH
    row_i = lax.broadcasted_iota(jnp.int32, (l, l), 0)
    col_i = lax.broadcasted_iota(jnp.int32, (l, l), 1)
    lane = lax.broadcasted_iota(jnp.int32, (1, LANES), 1)
    first_head = lane < SSD_HEADDIM
    bwd_row = lax.broadcasted_iota(jnp.int32, (nh2, 1), 0) >= SSD_JH
    a_col = -jnp.exp(al_ref[0])

    lower = row_i > col_i
    diag = row_i == col_i
    triu_b = (row_i <= col_i).astype(BF16)

    def decays(n):
        rows = pl.ds(pl.multiple_of(n * l, l), l)
        dtv = _softplus(dt_ref[0, rows, :].T[0:nh2] + pb_ref[0])
        da = dtv * a_col
        hi = da.astype(BF16)
        r1 = da - hi.astype(F32)
        mid = r1.astype(BF16)
        lo = (r1 - mid.astype(F32)).astype(BF16)
        p3 = jnp.dot(jnp.concatenate([hi, mid, lo], axis=0), triu_b, preferred_element_type=F32)
        pre = p3[0:nh2] + p3[nh2:2 * nh2] + p3[2 * nh2:]
        total = pre[:, l - 1:l]
        acum = jnp.where(bwd_row, total - pre + da, pre)
        dec_scr[0, n] = dtv
        dec_scr[1, n] = acum
        dec_scr[2, n] = dtv * jnp.exp(total - acum)
        dec_scr[3, n] = jnp.broadcast_to(jnp.exp(total), (nh2, l))
        dec_scr[4, n] = acum - jnp.log(dtv)

    def chunk_state(n, d):
        rows = pl.ds(pl.multiple_of(n * l, l), l)
        w_state_t = dec_scr[2, n]
        x = xs_ref[0, rows, :]
        bmt = bm_ref[0, rows, :].astype(F32).T
        edge_ch = jnp.sum(dec_scr[3, n][:, 0:1] * ex_ref[d], axis=0, keepdims=True)
        new_st = []
        ns = SSD_STATE
        for jj in range(SSD_JH // 2):
            xp = x[:, LANES * jj:LANES * (jj + 1)]
            bw2 = jnp.concatenate(
                [(bmt * w_state_t[d * SSD_JH + 2 * jj + hh:d * SSD_JH + 2 * jj + hh + 1, :]).astype(BF16)
                 for hh in range(2)], axis=0)
            s2 = jnp.dot(bw2, xp, preferred_element_type=F32)
            new_st.append(jnp.where(first_head, s2[:ns], s2[ns:]))
        st = run_scr[d]
        st_scr[d, n] = st.astype(BF16)
        run_scr[d] = st * edge_ch + jnp.concatenate(new_st, axis=1)

    def chunk_out(n):
        rows = pl.ds(pl.multiple_of(n * l, l), l)
        dtv_t = dec_scr[0, n]
        acum = dec_scr[1, n].T
        key_t = dec_scr[4, n]
        x = xs_ref[0, rows, :]
        cm = cm_ref[0, rows, :]
        cmf = cm.astype(F32)
        cb = _dot_t(cm, bm_ref[0, rows, :])
        st_f = st_scr[0, n]
        st_b = st_scr[1, n]
        rhs = [jnp.concatenate([x[:, LANES * jj:LANES * (jj + 1)],
                                st_f[:, LANES * jj:LANES * (jj + 1)],
                                st_b[:, LANES * jj:LANES * (jj + 1)]], axis=0)
               for jj in range(SSD_JH // 2)]
        slabs = [slice(r0, r0 + SSD_ROW_SLAB) for r0 in range(0, l, SSD_ROW_SLAB)]
        ys = [[] for _ in slabs]
        for jj in range(SSD_JH // 2):
            lhs = []
            for rs in slabs:
                for hh in range(2):
                    cf = 2 * jj + hh
                    cr = SSD_JH + cf
                    col_f = jnp.broadcast_to(acum[rs, cf:cf + 1], (SSD_ROW_SLAB, l))
                    col_r = jnp.broadcast_to(acum[rs, cr:cr + 1], (SSD_ROW_SLAB, l))
                    seg = jnp.where(lower[rs], col_f - key_t[cf:cf + 1, :], col_r - key_t[cr:cr + 1, :])
                    m = cb[rs] * (jnp.exp(seg) + jnp.where(diag[rs], dtv_t[cf:cf + 1, :], 0.0))
                    lhs.append(jnp.concatenate([m.astype(BF16),
                                                (cmf[rs] * jnp.exp(col_f)).astype(BF16),
                                                (cmf[rs] * jnp.exp(col_r)).astype(BF16)], axis=1))
            y4 = jnp.dot(jnp.concatenate(lhs, axis=0), rhs[jj], preferred_element_type=F32)
            for si in range(len(slabs)):
                r0 = 2 * si * SSD_ROW_SLAB
                ys[si].append(jnp.where(first_head, y4[r0:r0 + SSD_ROW_SLAB],
                                        y4[r0 + SSD_ROW_SLAB:r0 + 2 * SSD_ROW_SLAB]))
        for rs, y_parts in zip(slabs, ys):
            y = jnp.concatenate(y_parts, axis=1) + x[rs].astype(F32) * dsk_ref[...]
            o_ref[0, pl.ds(pl.multiple_of(n * l, l) + rs.start, SSD_ROW_SLAB), :] = y.astype(o_ref.dtype)

    def prep(i, carry):
        decays(i)
        return carry

    lax.fori_loop(0, nc, prep, 0, unroll=2)
    run_scr[...] = jnp.zeros_like(run_scr)

    def states(i, carry):
        chunk_state(i, 0)
        chunk_state(nc - 1 - i, 1)
        return carry

    lax.fori_loop(0, nc, states, 0)

    def outputs(n, carry):
        chunk_out(n)
        return carry

    lax.fori_loop(0, nc, outputs, 0)


def _ssd(xbc_act, dt3, pb, al, dskip, expand):
    bn, seq, _ = xbc_act.shape
    gw = SSD_JH * SSD_HEADDIM
    boff = SSD_INNER // SSD_STATE
    coff = boff + SSD_GROUPS
    return pl.pallas_call(
        functools.partial(_ssd_body, seq=seq),
        grid=(bn, SSD_GROUPS),
        in_specs=[
            pl.BlockSpec((1, seq, gw), lambda b, g: (b, 0, g)),
            pl.BlockSpec((1, seq, SSD_STATE), lambda b, g: (b, 0, boff + g)),
            pl.BlockSpec((1, seq, SSD_STATE), lambda b, g: (b, 0, coff + g)),
            pl.BlockSpec((1, seq, LANES), lambda b, g: (b, 0, g)),
            pl.BlockSpec((1, 2 * SSD_JH, 1), lambda b, g: (g, 0, 0)),
            pl.BlockSpec((1, 2 * SSD_JH, 1), lambda b, g: (g, 0, 0)),
            pl.BlockSpec((1, gw), lambda b, g: (0, g)),
            pl.BlockSpec((2, 2 * SSD_JH, gw), lambda b, g: (0, 0, 0)),
        ],
        out_specs=pl.BlockSpec((1, seq, gw), lambda b, g: (b, 0, g)),
        out_shape=jax.ShapeDtypeStruct((bn, seq, SSD_INNER), BF16),
        scratch_shapes=[pltpu.VMEM((2, seq // SSD_CHUNK, SSD_STATE, gw), BF16),
                        pltpu.VMEM((2, SSD_STATE, gw), F32),
                        pltpu.VMEM((5, seq // SSD_CHUNK, 2 * SSD_JH, SSD_CHUNK), F32)],
        compiler_params=_cparams(("parallel", "parallel")),
        name="ssd_scan",
    )(xbc_act, xbc_act, xbc_act, dt3, pb, al, dskip, expand)


def _rope_tables(seq):
    half = SWA_DH // 2
    inv = ROPE_THETA ** (-np.arange(half, dtype=np.float64) / half)
    ang = np.arange(seq, dtype=np.float64)[:, None] * inv[None, :]
    cos, sin = np.cos(ang), np.sin(ang)
    return (jnp.asarray(np.concatenate([cos, cos], axis=1), F32),
            jnp.asarray(np.concatenate([-sin, sin], axis=1), F32))


def _rope(x, cos, sin_signed):
    return x * cos + pltpu.roll(x, SWA_DH // 2, 1) * sin_signed


def _swa_body(sink_ref, q_ref, k_ref, v_ref, cos_ref, sin_ref, o_ref, kr_scr, *, seq):
    wb = SWA_BLOCK
    nb = seq // wb
    nkeys = 3 * wb
    kvh = pl.program_id(1)
    rb = min(512, seq)

    def krope(i, carry):
        rows = pl.ds(pl.multiple_of(i * rb, rb), rb)
        kr_scr[rows, :] = _rope(k_ref[0, rows, :].astype(F32), cos_ref[rows, :],
                                sin_ref[rows, :]).astype(BF16)
        return carry

    lax.fori_loop(0, seq // rb, krope, 0)

    qpos_l = lax.broadcasted_iota(jnp.int32, (wb, nkeys), 0)
    kpos_l = lax.broadcasted_iota(jnp.int32, (wb, nkeys), 1)

    def blk(n, carry):
        rows = pl.ds(pl.multiple_of(n * wb, wb), wb)
        ks = pl.multiple_of(jnp.clip(n - 1, 0, nb - 3) * wb, wb)
        krows = pl.ds(ks, nkeys)
        cos = cos_ref[rows, :]
        sin = sin_ref[rows, :]
        kk = kr_scr[krows, :]
        vv = v_ref[0, krows, :]
        valid = jnp.abs((qpos_l + n * wb) - (kpos_l + ks)) <= SWA_WINDOW
        qb = q_ref[0, rows, :].astype(F32)
        q4 = jnp.concatenate(
            [(_rope(qb[:, SWA_DH * hh:SWA_DH * (hh + 1)], cos, sin) * (SWA_DH ** -0.5)).astype(BF16)
             for hh in range(SWA_GQ)], axis=0)
        s4 = _dot_t(q4, kk)
        probs = []
        dens = []
        for hh in range(SWA_GQ):
            sink = sink_ref[kvh * SWA_GQ + hh]
            s = jnp.where(valid, s4[wb * hh:wb * (hh + 1)], NEG_BIG)
            m = jnp.maximum(jnp.max(s, axis=-1, keepdims=True), sink)
            p = jnp.exp(s - m)
            probs.append(p.astype(BF16))
            dens.append(jnp.sum(p, axis=-1, keepdims=True) + jnp.exp(sink - m))
        o4 = jnp.dot(jnp.concatenate(probs, axis=0), vv, preferred_element_type=F32)
        for hh in range(SWA_GQ):
            o = o4[wb * hh:wb * (hh + 1)] / dens[hh]
            o_ref[0, rows, SWA_DH * hh:SWA_DH * (hh + 1)] = o.astype(o_ref.dtype)
        return carry

    lax.fori_loop(0, nb, blk, 0)


def _swa(proj3, sink, cos, sin, col0):
    bn, seq, _ = proj3.shape
    qw = SWA_GQ * SWA_DH
    qoff = col0 // qw
    koff = (col0 + SWA_HEADS * SWA_DH) // SWA_DH
    voff = koff + SWA_KV
    return pl.pallas_call(
        functools.partial(_swa_body, seq=seq),
        grid=(bn, SWA_KV),
        in_specs=[
            pl.BlockSpec(memory_space=pltpu.SMEM),
            pl.BlockSpec((1, seq, qw), lambda b, h: (b, 0, qoff + h)),
            pl.BlockSpec((1, seq, SWA_DH), lambda b, h: (b, 0, koff + h)),
            pl.BlockSpec((1, seq, SWA_DH), lambda b, h: (b, 0, voff + h)),
            pl.BlockSpec((seq, SWA_DH), lambda b, h: (0, 0)),
            pl.BlockSpec((seq, SWA_DH), lambda b, h: (0, 0)),
        ],
        out_specs=pl.BlockSpec((1, seq, qw), lambda b, h: (b, 0, h)),
        out_shape=jax.ShapeDtypeStruct((bn, seq, SWA_HEADS * SWA_DH), BF16),
        scratch_shapes=[pltpu.VMEM((seq, SWA_DH), BF16)],
        compiler_params=_cparams(("parallel", "parallel")),
        name="swa",
    )(sink, proj3, proj3, proj3, cos, sin)


def _pad_cols(w, width):
    return jnp.pad(w, ((0, 0), (0, width - w.shape[1])))


def _even_mixers(x2, bn, seq, g0, w_in, w_decay, b_decay, gla_norm, rpb):
    d = x2.shape[1]
    n_gla = 2 * GLA_HEADS * GLA_DK + 2 * GLA_HEADS * GLA_DV
    n_lr = 2 * GLA_LOWRANK
    w_main = jnp.concatenate([w_in[:, :n_gla], w_in[:, n_gla + n_lr:]], axis=1).astype(BF16)
    w_small = _pad_cols(w_in[:, n_gla:n_gla + n_lr], LANES).astype(BF16)
    proj, small = _inproj(x2, g0, w_main, w_small)
    proj3 = proj.reshape(bn, seq, -1)
    small3 = small.reshape(bn, seq, -1)
    wdec = jnp.zeros((2, LANES, w_decay.shape[-1]), F32)
    for dd in range(2):
        wdec = wdec.at[dd, dd * GLA_LOWRANK:(dd + 1) * GLA_LOWRANK].set(w_decay[dd])
    o_a = _gla(proj3, small3, wdec, b_decay, gla_norm[None, :])
    o_b = _na(proj3, _na_bias_table(rpb), n_gla)
    return o_a.reshape(-1, o_a.shape[-1]), o_b.reshape(-1, o_b.shape[-1])


def _per_group(v):
    return v.reshape(2, SSD_GROUPS, SSD_JH).transpose(1, 0, 2).reshape(SSD_GROUPS, 2 * SSD_JH)


def _odd_mixers(x2, bn, seq, g0, w_in, conv_w, conv_b, dt_bias, a_log, d_skip, sink, cos, sin):
    conv_dim = SSD_INNER + 2 * SSD_GROUPS * SSD_STATE
    n_z = SSD_INNER
    n_dt = 2 * SSD_HEADS
    c_dt = n_z + conv_dim
    w_main = jnp.concatenate([w_in[:, :c_dt], w_in[:, c_dt + n_dt:]], axis=1).astype(BF16)
    w_dt = w_in[:, c_dt:c_dt + n_dt].reshape(-1, 2, SSD_GROUPS, SSD_JH).transpose(0, 2, 1, 3)
    w_dt = w_dt.reshape(-1, SSD_GROUPS, 2 * SSD_JH)
    w_small = jnp.pad(w_dt, ((0, 0), (0, 0), (0, LANES - 2 * SSD_JH))).reshape(-1, SSD_GROUPS * LANES)
    proj, small = _inproj(x2, g0, w_main, w_small.astype(BF16))
    proj3 = proj.reshape(bn, seq, -1)
    dt3 = small.reshape(bn, seq, SSD_GROUPS * LANES)
    xbc_act = _conv(proj3, conv_w, conv_b[None, :], n_z)
    pbg = _per_group(dt_bias)
    alg = _per_group(a_log)
    head_of_ch = np.arange(SSD_JH * SSD_HEADDIM) // SSD_HEADDIM
    expand = jnp.asarray(np.arange(2 * SSD_JH)[None, :, None]
                         == (np.arange(2)[:, None, None] * SSD_JH + head_of_ch[None, None, :]), F32)
    dskip = jnp.repeat(d_skip.astype(F32), SSD_HEADDIM)[None, :]
    y_pre = _ssd(xbc_act, dt3, pbg[:, :, None], alg[:, :, None], dskip, expand)
    o_d = _swa(proj3, sink, cos, sin, c_dt)
    return y_pre.reshape(-1, SSD_INNER), proj, o_d.reshape(-1, o_d.shape[-1])


def kernel(x, norm_gains, ffn_w_gate, ffn_w_up, ffn_w_down, even_w_in, even_w_out, gla_w_decay,
           gla_b_decay, gla_norm, na_rpb, odd_w_in, odd_w_out, ssd_conv_w, ssd_conv_b, ssd_dt_bias,
           ssd_a_log, ssd_d, ssd_norm, swa_sink):
    bn, seq, d = x.shape
    depth = norm_gains.shape[0]
    x2 = x.reshape(bn * seq, d)
    cos, sin = _rope_tables(seq)
    ffn_w = (ffn_w_gate, ffn_w_up, ffn_w_down)
    even_w_out_b = even_w_out.astype(BF16)
    odd_w_out_b = odd_w_out.astype(BF16)
    w_bf16 = None
    for layer in range(depth):
        g = norm_gains[layer][:, None, :]
        i = layer // 2
        cast_here = dict(ffn_w=ffn_w, layer=0) if layer == 0 else {}
        cast_next = dict(ffn_w=ffn_w, next_layer=layer + 1) if layer + 1 < depth else {}
        if layer % 2 == 0:
            o_a, o_b = _even_mixers(x2, bn, seq, g[0], even_w_in[i], gla_w_decay[i], gla_b_decay[i],
                                    gla_norm[i], na_rpb[i])
            x2, *cast = _outproj(x2, o_a, o_b, even_w_out_b, i, g[1], **cast_here)
        else:
            y_pre, proj, o_d = _odd_mixers(x2, bn, seq, g[0], odd_w_in[i], ssd_conv_w[i], ssd_conv_b[i],
                                           ssd_dt_bias[i], ssd_a_log[i], ssd_d[i], swa_sink[i], cos, sin)
            x2, *cast = _outproj(x2, y_pre, o_d, odd_w_out_b, i, g[1], z_src=proj,
                                 ssd_gain=ssd_norm[i][None, :], **cast_here)
        w_bf16 = cast or w_bf16
        x2, *w_bf16 = _ffn(x2, g[2], *w_bf16, g[3], **cast_next)
    return x2.reshape(bn, seq, d)
```

```python
import functools

import jax
import jax.numpy as jnp
import numpy as np
from jax import lax
from jax.experimental import pallas as pl
from jax.experimental.pallas import tpu as pltpu

F32 = jnp.float32
BF16 = jnp.bfloat16
HIGHEST = lax.Precision.HIGHEST

EPS = 1e-6
ROPE_THETA = 10000.0
GRID_W = 64
LANES = 128

GLA_DK = 64
GLA_DV = 128
GLA_HEADS = 8
GLA_LOWRANK = 16
GLA_TAU = 16.0
GLA_CHUNK = 64
GLA_SUB = 16
GLA_EXP_CLAMP = 60.0
GLA_TRI_BLOCK = 256

NA_DH = 128
NA_HEADS = 8
NA_KH = 8
NA_KW = 16
NA_ROWS_PER_ITER = 16
NEG_BIG = -1e30

SSD_INNER = 1024
SSD_HEADDIM = 64
SSD_HEADS = 16
SSD_GROUPS = 2
SSD_STATE = 128
SSD_CONV = 5
SSD_CHUNK = 128
SSD_JH = SSD_HEADS // SSD_GROUPS
SSD_ROW_SLAB = 64

SWA_DH = 128
SWA_HEADS = 8
SWA_KV = 2
SWA_GQ = SWA_HEADS // SWA_KV
SWA_WINDOW = 128
SWA_BLOCK = 128

VMEM_LIMIT = 52 * 1024 * 1024


FFN_NORM_ROWS = 256
FFN_VMEM_LIMIT = 58 * 1024 * 1024


def _cparams(sem, vmem_limit=VMEM_LIMIT):
    return pltpu.CompilerParams(dimension_semantics=sem, vmem_limit_bytes=vmem_limit)


def _rms(x, g):
    return x * lax.rsqrt(jnp.mean(x * x, axis=-1, keepdims=True) + EPS) * g


def _silu(x):
    return x * (1.0 / (1.0 + jnp.exp(-x)))


def _dot_t(a, b):
    return lax.dot_general(a, b, (((1,), (1,)), ((), ())), preferred_element_type=F32)


def _inproj_body(x_ref, g_ref, w_ref, ws_ref, o_ref, os_ref, h_scr):
    j = pl.program_id(1)
    half = x_ref.shape[0] // 2

    @pl.when(j == 0)
    def _():
        for r0 in (0, half):
            rs = pl.ds(r0, half)
            h = _rms(x_ref[rs, :], g_ref[...]).astype(BF16)
            h_scr[rs, :] = h
            os_ref[rs, :] = jnp.dot(h, ws_ref[...], preferred_element_type=F32)
            o_ref[rs, :] = jnp.dot(h, w_ref[...], preferred_element_type=F32).astype(o_ref.dtype)

    @pl.when(j > 0)
    def _():
        o_ref[...] = jnp.dot(h_scr[...], w_ref[...], preferred_element_type=F32).astype(o_ref.dtype)


def _inproj(x2, g, w_main, w_small, tm=1024, tn=2048):
    n, d = x2.shape
    nm = w_main.shape[1]
    ns = w_small.shape[1]
    return pl.pallas_call(
        _inproj_body,
        grid=(n // tm, nm // tn),
        in_specs=[
            pl.BlockSpec((tm, d), lambda i, j: (i, 0)),
            pl.BlockSpec((1, d), lambda i, j: (0, 0)),
            pl.BlockSpec((d, tn), lambda i, j: (0, j)),
            pl.BlockSpec((d, ns), lambda i, j: (0, 0)),
        ],
        out_specs=[
            pl.BlockSpec((tm, tn), lambda i, j: (i, j)),
            pl.BlockSpec((tm, ns), lambda i, j: (i, 0)),
        ],
        out_shape=[jax.ShapeDtypeStruct((n, nm), BF16), jax.ShapeDtypeStruct((n, ns), F32)],
        scratch_shapes=[pltpu.VMEM((tm, d), BF16)],
        compiler_params=_cparams(("parallel", "arbitrary")),
        name="inproj",
    )(x2, g, w_main, w_small)


def _cast_weight_slices(w_refs):
    for src, dst in zip(w_refs[:3], w_refs[3:]):
        dst[...] = src[...].astype(BF16)


def _outproj_out(rest):
    if len(rest) == 1:
        return rest[0]
    _cast_weight_slices(rest[:3] + rest[4:])
    return rest[3]


def _outproj_even_body(a_ref, b_ref, wa_ref, wb_ref, g_ref, x_ref, *rest):
    o_ref = _outproj_out(rest)
    mix = (jnp.dot(a_ref[...], wa_ref[...], preferred_element_type=F32)
           + jnp.dot(b_ref[...], wb_ref[...], preferred_element_type=F32))
    o_ref[...] = x_ref[...] + _rms(mix, g_ref[...])


def _outproj_odd_body(y_ref, z_ref, ng_ref, b_ref, wa_ref, wb_ref, g_ref, x_ref, *rest):
    o_ref = _outproj_out(rest)
    y = y_ref[...].astype(F32) * _silu(z_ref[...].astype(F32))
    a = _rms(y, ng_ref[...]).astype(BF16)
    mix = (jnp.dot(a, wa_ref[...], preferred_element_type=F32)
           + jnp.dot(b_ref[...], wb_ref[...], preferred_element_type=F32))
    o_ref[...] = x_ref[...] + _rms(mix, g_ref[...])


def _outproj(x2, a, b, w_out, li, g, ffn_w=None, layer=None, z_src=None, ssd_gain=None, tm=512):
    n, d = x2.shape
    half = a.shape[1]
    steps = n // tm
    row = lambda i: (i, 0)
    fixed = lambda i: (0, 0)
    w_specs = [pl.BlockSpec((None, half, d), lambda i: (li, 0, 0)),
               pl.BlockSpec((None, half, d), lambda i: (li, 1, 0))]
    tail_specs = [pl.BlockSpec((1, d), fixed), pl.BlockSpec((tm, d), row)]
    ffn_w = ffn_w or ()
    cast_in = [pl.BlockSpec((None, w.shape[1] // steps, w.shape[2]), lambda i: (layer, i, 0)) for w in ffn_w]
    cast_out = [pl.BlockSpec((w.shape[1] // steps, w.shape[2]), row) for w in ffn_w]
    cast_shapes = [jax.ShapeDtypeStruct(w.shape[1:], BF16) for w in ffn_w]
    if z_src is None:
        body = _outproj_even_body
        in_specs = [pl.BlockSpec((tm, half), row), pl.BlockSpec((tm, half), row)] + w_specs + tail_specs
        args = (a, b, w_out, w_out, g, x2)
    else:
        body = _outproj_odd_body
        in_specs = ([pl.BlockSpec((tm, half), row), pl.BlockSpec((tm, half), row),
                     pl.BlockSpec((1, half), fixed), pl.BlockSpec((tm, half), row)]
                    + w_specs + tail_specs)
        args = (a, z_src, ssd_gain, b, w_out, w_out, g, x2)
    return pl.pallas_call(
        body,
        grid=(steps,),
        in_specs=in_specs + cast_in,
        out_specs=[pl.BlockSpec((tm, d), row)] + cast_out,
        out_shape=[jax.ShapeDtypeStruct((n, d), F32)] + cast_shapes,
        compiler_params=_cparams(("parallel",)),
        name="outproj",
    )(*args, *ffn_w)


def _ffn_body(x_ref, g2_ref, wg_ref, wu_ref, wd_ref, g3_ref, *rest):
    if len(rest) == 2:
        o_ref, h_scr = rest
    else:
        o_ref, h_scr = rest[3], rest[7]
        _cast_weight_slices(rest[:3] + rest[4:7])
    k = pl.program_id(1)
    last = pl.num_programs(1) - 1
    half = x_ref.shape[0] // 2

    def pieces(r0):
        return [pl.ds(r0 + p, FFN_NORM_ROWS) for p in range(0, half, FFN_NORM_ROWS)]

    def partial_down(rs):
        h = h_scr[rs, :]
        gate = jnp.dot(h, wg_ref[...], preferred_element_type=F32)
        up = jnp.dot(h, wu_ref[...], preferred_element_type=F32)
        return jnp.dot((_silu(gate) * up).astype(BF16), wd_ref[...], preferred_element_type=F32)

    @pl.when(k == 0)
    def _():
        for r0 in (0, half):
            for ps in pieces(r0):
                h_scr[ps, :] = _rms(x_ref[ps, :], g2_ref[...]).astype(BF16)
            o_ref[pl.ds(r0, half), :] = partial_down(pl.ds(r0, half))

    @pl.when(jnp.logical_and(k > 0, k < last))
    def _():
        for r0 in (0, half):
            o_ref[pl.ds(r0, half), :] += partial_down(pl.ds(r0, half))

    @pl.when(k == last)
    def _():
        for r0 in (0, half):
            o_ref[pl.ds(r0, half), :] += partial_down(pl.ds(r0, half))
            for ps in pieces(r0):
                o_ref[ps, :] = x_ref[ps, :] + _rms(o_ref[ps, :], g3_ref[...])


def _ffn(x2, g2, wg, wu, wd, g3, ffn_w=None, next_layer=None, tm=1024, th=512):
    n, d = x2.shape
    hid = wd.shape[0]
    ni, nk = n // tm, hid // th
    in_specs = [
        pl.BlockSpec((tm, d), lambda i, k: (i, 0)),
        pl.BlockSpec((1, d), lambda i, k: (0, 0)),
        pl.BlockSpec((d, th), lambda i, k: (0, k)),
        pl.BlockSpec((d, th), lambda i, k: (0, k)),
        pl.BlockSpec((th, d), lambda i, k: (k, 0)),
        pl.BlockSpec((1, d), lambda i, k: (0, 0)),
    ]
    out_specs = [pl.BlockSpec((tm, d), lambda i, k: (i, 0))]
    out_shape = [jax.ShapeDtypeStruct((n, d), F32)]
    args = [x2, g2, wg, wu, wd, g3]
    if ffn_w is not None:
        up_blk, down_blk = (d // ni, hid // nk), (hid // nk, d // ni)
        in_specs += [pl.BlockSpec((None,) + up_blk, lambda i, k: (next_layer, i, k)),
                     pl.BlockSpec((None,) + up_blk, lambda i, k: (next_layer, i, k)),
                     pl.BlockSpec((None,) + down_blk, lambda i, k: (next_layer, k, i))]
        out_specs += [pl.BlockSpec(up_blk, lambda i, k: (i, k)),
                      pl.BlockSpec(up_blk, lambda i, k: (i, k)),
                      pl.BlockSpec(down_blk, lambda i, k: (k, i))]
        out_shape += [jax.ShapeDtypeStruct(w.shape[1:], BF16) for w in ffn_w]
        args += list(ffn_w)
    return pl.pallas_call(
        _ffn_body,
        grid=(ni, nk),
        in_specs=in_specs,
        out_specs=out_specs,
        out_shape=out_shape,
        scratch_shapes=[pltpu.VMEM((tm, d), BF16)],
        compiler_params=_cparams(("parallel", "arbitrary"), FFN_VMEM_LIMIT),
        name="ffn",
    )(*args)


def _gla_body(q_ref, k_ref, v_ref, g_ref, lr_ref, wd_ref, bd_ref, gn_ref, o_ref,
              bc_scr, o_scr, st_scr, *, seq):
    c = GLA_CHUNK
    nc = seq // c
    half = nc // 2
    nsub = c // GLA_SUB
    rb = min(512, seq)
    row_i = lax.broadcasted_iota(jnp.int32, (c, c), 0)
    col_i = lax.broadcasted_iota(jnp.int32, (c, c), 1)
    tri = (row_i >= col_i, row_i <= col_i)
    lane = lax.broadcasted_iota(jnp.int32, (1, LANES), 1)
    hmask = (lane < GLA_DK, lane >= GLA_DK)
    own_head = ((lax.broadcasted_iota(jnp.int32, (2 * GLA_DV, LANES), 0) < GLA_DV)
                == (lax.broadcasted_iota(jnp.int32, (2 * GLA_DV, LANES), 1) < GLA_DK))

    w2 = jnp.concatenate([wd_ref[0], wd_ref[1]], axis=1)
    w_hi = w2.astype(BF16)
    w_lo = (w2 - w_hi.astype(F32)).astype(BF16)
    w_cat = jnp.concatenate([w_hi, w_lo, w_hi], axis=0)
    tb = GLA_TRI_BLOCK
    brow = lax.broadcasted_iota(jnp.int32, (tb, tb), 0)
    bcol = lax.broadcasted_iota(jnp.int32, (tb, tb), 1)
    same_chunk = (brow // c) == (bcol // c)
    blk_tri = ((same_chunk & (brow >= bcol)).astype(BF16), (same_chunk & (brow <= bcol)).astype(BF16))

    def pre(i, carry):
        r0 = pl.multiple_of(i * rb, rb)
        lr = lr_ref[0, pl.ds(r0, rb), :]
        lr_hi = lr.astype(BF16)
        lr_lo = (lr - lr_hi.astype(F32)).astype(BF16)
        z2 = jnp.dot(jnp.concatenate([lr_hi, lr_hi, lr_lo], axis=1), w_cat, preferred_element_type=F32)
        las = []
        for d in range(2):
            z = z2[:, LANES * d:LANES * (d + 1)] + bd_ref[d:d + 1, :]
            la = (jnp.minimum(z, 0.0) - jnp.log(1.0 + jnp.exp(-jnp.abs(z)))) * (1.0 / GLA_TAU)
            la_hi = la.astype(BF16)
            las.append(jnp.concatenate([la_hi, (la - la_hi.astype(F32)).astype(BF16)], axis=1))
        for d in range(2):
            for t in range(rb // tb):
                s2 = jnp.dot(blk_tri[d], las[d][t * tb:(t + 1) * tb], preferred_element_type=F32)
                bc_scr[d, pl.ds(r0 + t * tb, tb), :] = s2[:, :LANES] + s2[:, LANES:]
        return carry

    lax.fori_loop(0, seq // rb, pre, 0, unroll=2)

    def stage1(n, d):
        rows = pl.ds(pl.multiple_of(n * c, c), c)
        bc = bc_scr[d, rows, :]
        q = q_ref[0, rows, :].astype(F32) * (GLA_DK ** -0.5)
        k = k_ref[0, rows, :].astype(F32)
        v = v_ref[0, rows, :]
        b_edge = bc[c - 1:c] if d == 0 else bc[0:1]
        qhat = q * jnp.exp(bc)
        khat = (k * jnp.exp(b_edge - bc)).astype(BF16)
        att_rows = ([], [])
        for i in range(nsub):
            lo = GLA_SUB * i
            hi = lo + GLA_SUB
            if d == 0:
                ref = bc[lo - 1:lo] if i > 0 else jnp.zeros((1, LANES), F32)
            else:
                ref = bc[hi:hi + 1] if i < nsub - 1 else jnp.zeros((1, LANES), F32)
            qi = q[lo:hi] * jnp.exp(bc[lo:hi] - ref)
            ki = (k * jnp.exp(jnp.minimum(ref - bc, GLA_EXP_CLAMP))).astype(BF16)
            q2 = jnp.concatenate([jnp.where(hmask[0], qi, 0.0), jnp.where(hmask[1], qi, 0.0)], axis=0)
            a2 = _dot_t(q2.astype(BF16), ki)
            att_rows[0].append(a2[:GLA_SUB])
            att_rows[1].append(a2[GLA_SUB:])
        kv2 = lax.dot_general(v, khat, (((0,), (0,)), ((), ())), preferred_element_type=F32)
        kv2 = jnp.where(own_head, kv2, 0.0)
        att2 = jnp.concatenate(
            [jnp.where(tri[d], jnp.concatenate(att_rows[hh], axis=0), 0.0).astype(BF16)
             for hh in range(2)], axis=1)
        zero_v = jnp.zeros((c, GLA_DV), BF16)
        v_blockdiag = jnp.concatenate(
            [jnp.concatenate([v[:, :GLA_DV], zero_v], axis=1),
             jnp.concatenate([zero_v, v[:, GLA_DV:]], axis=1)], axis=0)
        return d, att2, v_blockdiag, kv2, qhat.astype(BF16), jnp.exp(b_edge)

    def stage2(s1, states):
        d, att2, v_blockdiag, kv2, qhat, decay = s1
        st = states[d]
        o = jnp.dot(att2, v_blockdiag, preferred_element_type=F32) + _dot_t(qhat, st.astype(BF16))
        states[d] = st * decay + kv2
        return o

    def finish(n, o):
        rows = pl.ds(pl.multiple_of(n * c, c), c)
        parts = []
        for hh in range(2):
            oh = o[:, GLA_DV * hh:GLA_DV * (hh + 1)]
            parts.append(oh * lax.rsqrt(jnp.mean(oh * oh, axis=-1, keepdims=True) + EPS))
        o = jnp.concatenate(parts, axis=1) * gn_ref[...]
        o_ref[0, rows, :] = (o * _silu(g_ref[0, rows, :].astype(F32))).astype(o_ref.dtype)

    st_scr[...] = jnp.zeros_like(st_scr)

    cpi = next(u for u in (4, 2, 1) if half % u == 0)

    def walk(i, base_f, base_b):
        work = []
        for u in range(cpi):
            work.append((base_f + i * cpi + u, 0))
            work.append((base_b - i * cpi - u, 1))
        prepared = [stage1(n, d) for n, d in work]
        states = [st_scr[d] for d in range(2)]
        outs = [(n, stage2(s1, states)) for (n, _), s1 in zip(work, prepared)]
        for d in range(2):
            st_scr[d] = states[d]
        return outs

    def first(i, carry):
        for n, o in walk(i, 0, nc - 1):
            o_scr[pl.ds(pl.multiple_of(n * c, c), c), :] = o
        return carry

    lax.fori_loop(0, half // cpi, first, 0)

    def second(i, carry):
        for n, o in walk(i, half, half - 1):
            finish(n, o_scr[pl.ds(pl.multiple_of(n * c, c), c), :] + o)
        return carry

    lax.fori_loop(0, half // cpi, second, 0)


def _gla(proj3, small3, wdec_pad, bdec, gnorm):
    bn, seq, _ = proj3.shape
    npair = GLA_HEADS // 2
    qoff = 0
    koff = (GLA_HEADS * GLA_DK) // LANES
    voff = (2 * GLA_HEADS * GLA_DK) // (2 * GLA_DV)
    goff = voff + npair
    return pl.pallas_call(
        functools.partial(_gla_body, seq=seq),
        grid=(bn, npair),
        in_specs=[
            pl.BlockSpec((1, seq, LANES), lambda b, p: (b, 0, qoff + p)),
            pl.BlockSpec((1, seq, LANES), lambda b, p: (b, 0, koff + p)),
            pl.BlockSpec((1, seq, 2 * GLA_DV), lambda b, p: (b, 0, voff + p)),
            pl.BlockSpec((1, seq, 2 * GLA_DV), lambda b, p: (b, 0, goff + p)),
            pl.BlockSpec((1, seq, LANES), lambda b, p: (b, 0, 0)),
            pl.BlockSpec((2, LANES, LANES), lambda b, p: (0, 0, p)),
            pl.BlockSpec((2, LANES), lambda b, p: (0, p)),
            pl.BlockSpec((1, 2 * GLA_DV), lambda b, p: (0, p)),
        ],
        out_specs=pl.BlockSpec((1, seq, 2 * GLA_DV), lambda b, p: (b, 0, p)),
        out_shape=jax.ShapeDtypeStruct((bn, seq, GLA_HEADS * GLA_DV), BF16),
        scratch_shapes=[
            pltpu.VMEM((2, seq, LANES), F32),
            pltpu.VMEM((seq, 2 * GLA_DV), F32),
            pltpu.VMEM((2, 2 * GLA_DV, LANES), F32),
        ],
        compiler_params=_cparams(("parallel", "parallel")),
        name="gla",
    )(proj3, proj3, proj3, proj3, small3, wdec_pad, bdec, gnorm)


def _na_bias_table(rpb):
    nh, ndr, ndc = rpb.shape
    cols = np.arange(GRID_W)
    cs = np.clip(cols - NA_KW // 2, 0, GRID_W - NA_KW)
    valid = (cols[None, :] >= cs[:, None]) & (cols[None, :] < cs[:, None] + NA_KW)
    dc = cols[None, :] - cols[:, None] + NA_KW - 1
    onehot = (np.arange(ndc)[:, None, None] == dc[None]) & valid[None]
    toep = jnp.dot(rpb.reshape(nh * ndr, ndc).astype(F32),
                   jnp.asarray(onehot.reshape(ndc, -1), F32), precision=HIGHEST)
    toep = jnp.where(jnp.asarray(valid.reshape(1, -1)), toep, NEG_BIG)
    toep = toep.reshape(nh, ndr, GRID_W, GRID_W)
    t = jnp.stack([toep[:, off:off + NA_KH] for off in range(NA_KH)], axis=1)
    return t.transpose(0, 1, 3, 2, 4).reshape(nh, NA_KH, GRID_W, NA_KH * GRID_W)


def _na_body(q_ref, k_ref, v_ref, bias_ref, o_ref, *, nrows):
    w = GRID_W
    nk = NA_KH * w

    def group(i, carry):
        idx = []
        scores = []
        for u in range(NA_ROWS_PER_ITER):
            r = i * NA_ROWS_PER_ITER + u
            rs = jnp.clip(r - NA_KH // 2, 0, nrows - NA_KH)
            qrows = pl.ds(pl.multiple_of(r * w, w), w)
            krows = pl.ds(pl.multiple_of(rs * w, w), nk)
            idx.append((qrows, krows, rs - r + (NA_KH - 1)))
            scores.append(_dot_t(q_ref[0, qrows, :], k_ref[0, krows, :]))
        probs = []
        for (qrows, krows, off), s in zip(idx, scores):
            s = s * (NA_DH ** -0.5) + bias_ref[0, off]
            p = jnp.exp(s - jnp.max(s, axis=-1, keepdims=True))
            probs.append((p.astype(BF16), jnp.sum(p, axis=-1, keepdims=True)))
        for (qrows, krows, off), (p, l) in zip(idx, probs):
            o = jnp.dot(p, v_ref[0, krows, :], preferred_element_type=F32) / l
            o_ref[0, qrows, :] = o.astype(o_ref.dtype)
        return carry

    lax.fori_loop(0, nrows // NA_ROWS_PER_ITER, group, 0)


def _na(proj3, bias_tab, col0):
    bn, seq, _ = proj3.shape
    nrows = seq // GRID_W
    qoff = col0 // NA_DH
    koff = qoff + NA_HEADS
    voff = koff + NA_HEADS
    return pl.pallas_call(
        functools.partial(_na_body, nrows=nrows),
        grid=(bn, NA_HEADS),
        in_specs=[
            pl.BlockSpec((1, seq, NA_DH), lambda b, h: (b, 0, qoff + h)),
            pl.BlockSpec((1, seq, NA_DH), lambda b, h: (b, 0, koff + h)),
            pl.BlockSpec((1, seq, NA_DH), lambda b, h: (b, 0, voff + h)),
            pl.BlockSpec((1, NA_KH, GRID_W, NA_KH * GRID_W), lambda b, h: (h, 0, 0, 0)),
        ],
        out_specs=pl.BlockSpec((1, seq, NA_DH), lambda b, h: (b, 0, h)),
        out_shape=jax.ShapeDtypeStruct((bn, seq, NA_HEADS * NA_DH), BF16),
        compiler_params=_cparams(("parallel", "parallel")),
        name="na",
    )(proj3, proj3, proj3, bias_tab)


CONV_PAD = 16


def _conv_body(x_ref, w_ref, b_ref, o_ref, xp_scr, *, seq):
    rb = min(256, seq)
    ch = x_ref.shape[-1]
    zeros = jnp.zeros((CONV_PAD, ch), xp_scr.dtype)
    xp_scr[0:CONV_PAD, :] = zeros
    xp_scr[seq + CONV_PAD:seq + 2 * CONV_PAD, :] = zeros
    xp_scr[CONV_PAD:seq + CONV_PAD, :] = x_ref[0]

    def blk(i, carry):
        r0 = pl.multiple_of(i * rb, rb)
        xw = xp_scr[pl.ds(r0, rb + 2 * CONV_PAD), :].astype(F32)
        acc = jnp.zeros((rb, ch), F32) + b_ref[...]
        for j in range(SSD_CONV):
            s0 = CONV_PAD - SSD_CONV // 2 + j
            acc = acc + xw[s0:s0 + rb] * w_ref[j:j + 1, :]
        o_ref[0, pl.ds(r0, rb), :] = _silu(acc).astype(o_ref.dtype)
        return carry

    lax.fori_loop(0, seq // rb, blk, 0)


def _conv(proj3, conv_w, conv_b, col0, tc=512):
    bn, seq, _ = proj3.shape
    cdim = conv_w.shape[1]
    c0 = col0 // tc
    return pl.pallas_call(
        functools.partial(_conv_body, seq=seq),
        grid=(bn, cdim // tc),
        in_specs=[
            pl.BlockSpec((1, seq, tc), lambda b, j: (b, 0, c0 + j)),
            pl.BlockSpec((SSD_CONV, tc), lambda b, j: (0, j)),
            pl.BlockSpec((1, tc), lambda b, j: (0, j)),
        ],
        out_specs=pl.BlockSpec((1, seq, tc), lambda b, j: (b, 0, j)),
        out_shape=jax.ShapeDtypeStruct((bn, seq, cdim), BF16),
        scratch_shapes=[pltpu.VMEM((seq + 2 * CONV_PAD, tc), BF16)],
        compiler_params=_cparams(("parallel", "parallel")),
        name="ssd_conv",
    )(proj3, conv_w, conv_b)


def _softplus(x):
    return jnp.maximum(x, 0.0) + jnp.log(1.0 + jnp.exp(-jnp.abs(x)))


def _ssd_body(xs_ref, bm_ref, cm_ref, dt_ref, pb_ref, al_ref, dsk_ref, ex_ref, o_ref,
              st_scr, run_scr, dec_scr, *, seq):
    l = SSD_CHUNK
    nc = seq // l
    nh2 = 2 * SSD_JH
    row_i = lax.broadcasted_iota(jnp.int32, (l, l), 0)
    col_i = lax.broadcasted_iota(jnp.int32, (l, l), 1)
    lane = lax.broadcasted_iota(jnp.int32, (1, LANES), 1)
    first_head = lane < SSD_HEADDIM
    bwd_row = lax.broadcasted_iota(jnp.int32, (nh2, 1), 0) >= SSD_JH
    a_col = -jnp.exp(al_ref[0])

    lower = row_i > col_i
    diag = row_i == col_i
    triu_b = (row_i <= col_i).astype(BF16)

    def decays(n):
        rows = pl.ds(pl.multiple_of(n * l, l), l)
        dtv = _softplus(dt_ref[0, rows, :].T[0:nh2] + pb_ref[0])
        da = dtv * a_col
        hi = da.astype(BF16)
        r1 = da - hi.astype(F32)
        mid = r1.astype(BF16)
        lo = (r1 - mid.astype(F32)).astype(BF16)
        p3 = jnp.dot(jnp.concatenate([hi, mid, lo], axis=0), triu_b, preferred_element_type=F32)
        pre = p3[0:nh2] + p3[nh2:2 * nh2] + p3[2 * nh2:]
        total = pre[:, l - 1:l]
        acum = jnp.where(bwd_row, total - pre + da, pre)
        dec_scr[0, n] = dtv
        dec_scr[1, n] = acum
        dec_scr[2, n] = dtv * jnp.exp(total - acum)
        dec_scr[3, n] = jnp.broadcast_to(jnp.exp(total), (nh2, l))
        dec_scr[4, n] = acum - jnp.log(dtv)

    def chunk_state(n, d):
        rows = pl.ds(pl.multiple_of(n * l, l), l)
        w_state_t = dec_scr[2, n]
        x = xs_ref[0, rows, :]
        bmt = bm_ref[0, rows, :].astype(F32).T
        edge_ch = jnp.sum(dec_scr[3, n][:, 0:1] * ex_ref[d], axis=0, keepdims=True)
        new_st = []
        ns = SSD_STATE
        for jj in range(SSD_JH // 2):
            xp = x[:, LANES * jj:LANES * (jj + 1)]
            bw2 = jnp.concatenate(
                [(bmt * w_state_t[d * SSD_JH + 2 * jj + hh:d * SSD_JH + 2 * jj + hh + 1, :]).astype(BF16)
                 for hh in range(2)], axis=0)
            s2 = jnp.dot(bw2, xp, preferred_element_type=F32)
            new_st.append(jnp.where(first_head, s2[:ns], s2[ns:]))
        st = run_scr[d]
        st_scr[d, n] = st.astype(BF16)
        run_scr[d] = st * edge_ch + jnp.concatenate(new_st, axis=1)

    def chunk_out(n):
        rows = pl.ds(pl.multiple_of(n * l, l), l)
        dtv_t = dec_scr[0, n]
        acum = dec_scr[1, n].T
        key_t = dec_scr[4, n]
        x = xs_ref[0, rows, :]
        cm = cm_ref[0, rows, :]
        cmf = cm.astype(F32)
        cb = _dot_t(cm, bm_ref[0, rows, :])
        st_f = st_scr[0, n]
        st_b = st_scr[1, n]
        rhs = [jnp.concatenate([x[:, LANES * jj:LANES * (jj + 1)],
                                st_f[:, LANES * jj:LANES * (jj + 1)],
                                st_b[:, LANES * jj:LANES * (jj + 1)]], axis=0)
               for jj in range(SSD_JH // 2)]
        slabs = [slice(r0, r0 + SSD_ROW_SLAB) for r0 in range(0, l, SSD_ROW_SLAB)]
        ys = [[] for _ in slabs]
        for jj in range(SSD_JH // 2):
            lhs = []
            for rs in slabs:
                for hh in range(2):
                    cf = 2 * jj + hh
                    cr = SSD_JH + cf
                    col_f = jnp.broadcast_to(acum[rs, cf:cf + 1], (SSD_ROW_SLAB, l))
                    col_r = jnp.broadcast_to(acum[rs, cr:cr + 1], (SSD_ROW_SLAB, l))
                    seg = jnp.where(lower[rs], col_f - key_t[cf:cf + 1, :], col_r - key_t[cr:cr + 1, :])
                    m = cb[rs] * (jnp.exp(seg) + jnp.where(diag[rs], dtv_t[cf:cf + 1, :], 0.0))
                    lhs.append(jnp.concatenate([m.astype(BF16),
                                                (cmf[rs] * jnp.exp(col_f)).astype(BF16),
                                                (cmf[rs] * jnp.exp(col_r)).astype(BF16)], axis=1))
            y4 = jnp.dot(jnp.concatenate(lhs, axis=0), rhs[jj], preferred_element_type=F32)
            for si in range(len(slabs)):
                r0 = 2 * si * SSD_ROW_SLAB
                ys[si].append(jnp.where(first_head, y4[r0:r0 + SSD_ROW_SLAB],
                                        y4[r0 + SSD_ROW_SLAB:r0 + 2 * SSD_ROW_SLAB]))
        for rs, y_parts in zip(slabs, ys):
            y = jnp.concatenate(y_parts, axis=1) + x[rs].astype(F32) * dsk_ref[...]
            o_ref[0, pl.ds(pl.multiple_of(n * l, l) + rs.start, SSD_ROW_SLAB), :] = y.astype(o_ref.dtype)

    def prep(i, carry):
        decays(i)
        return carry

    lax.fori_loop(0, nc, prep, 0, unroll=2)
    run_scr[...] = jnp.zeros_like(run_scr)

    def states(i, carry):
        chunk_state(i, 0)
        chunk_state(nc - 1 - i, 1)
        return carry

    lax.fori_loop(0, nc, states, 0, unroll=2)

    def outputs(n, carry):
        chunk_out(n)
        return carry

    lax.fori_loop(0, nc, outputs, 0, unroll=2)


def _ssd(xbc_act, dt3, pb, al, dskip, expand):
    bn, seq, _ = xbc_act.shape
    gw = SSD_JH * SSD_HEADDIM
    boff = SSD_INNER // SSD_STATE
    coff = boff + SSD_GROUPS
    return pl.pallas_call(
        functools.partial(_ssd_body, seq=seq),
        grid=(bn, SSD_GROUPS),
        in_specs=[
            pl.BlockSpec((1, seq, gw), lambda b, g: (b, 0, g)),
            pl.BlockSpec((1, seq, SSD_STATE), lambda b, g: (b, 0, boff + g)),
            pl.BlockSpec((1, seq, SSD_STATE), lambda b, g: (b, 0, coff + g)),
            pl.BlockSpec((1, seq, LANES), lambda b, g: (b, 0, g)),
            pl.BlockSpec((1, 2 * SSD_JH, 1), lambda b, g: (g, 0, 0)),
            pl.BlockSpec((1, 2 * SSD_JH, 1), lambda b, g: (g, 0, 0)),
            pl.BlockSpec((1, gw), lambda b, g: (0, g)),
            pl.BlockSpec((2, 2 * SSD_JH, gw), lambda b, g: (0, 0, 0)),
        ],
        out_specs=pl.BlockSpec((1, seq, gw), lambda b, g: (b, 0, g)),
        out_shape=jax.ShapeDtypeStruct((bn, seq, SSD_INNER), BF16),
        scratch_shapes=[pltpu.VMEM((2, seq // SSD_CHUNK, SSD_STATE, gw), BF16),
                        pltpu.VMEM((2, SSD_STATE, gw), F32),
                        pltpu.VMEM((5, seq // SSD_CHUNK, 2 * SSD_JH, SSD_CHUNK), F32)],
        compiler_params=_cparams(("parallel", "parallel")),
        name="ssd_scan",
    )(xbc_act, xbc_act, xbc_act, dt3, pb, al, dskip, expand)


def _rope_tables(seq):
    half = SWA_DH // 2
    inv = ROPE_THETA ** (-np.arange(half, dtype=np.float64) / half)
    ang = np.arange(seq, dtype=np.float64)[:, None] * inv[None, :]
    cos, sin = np.cos(ang), np.sin(ang)
    return (jnp.asarray(np.concatenate([cos, cos], axis=1), F32),
            jnp.asarray(np.concatenate([-sin, sin], axis=1), F32))


def _rope(x, cos, sin_signed):
    return x * cos + pltpu.roll(x, SWA_DH // 2, 1) * sin_signed


def _swa_body(sink_ref, q_ref, k_ref, v_ref, cos_ref, sin_ref, o_ref, kr_scr, *, seq):
    wb = SWA_BLOCK
    nb = seq // wb
    nkeys = 3 * wb
    kvh = pl.program_id(1)
    rb = min(512, seq)

    def krope(i, carry):
        rows = pl.ds(pl.multiple_of(i * rb, rb), rb)
        kr_scr[rows, :] = _rope(k_ref[0, rows, :].astype(F32), cos_ref[rows, :],
                                sin_ref[rows, :]).astype(BF16)
        return carry

    lax.fori_loop(0, seq // rb, krope, 0)

    qpos_l = lax.broadcasted_iota(jnp.int32, (wb, nkeys), 0)
    kpos_l = lax.broadcasted_iota(jnp.int32, (wb, nkeys), 1)

    def blk(n, carry):
        rows = pl.ds(pl.multiple_of(n * wb, wb), wb)
        ks = pl.multiple_of(jnp.clip(n - 1, 0, nb - 3) * wb, wb)
        krows = pl.ds(ks, nkeys)
        cos = cos_ref[rows, :]
        sin = sin_ref[rows, :]
        kk = kr_scr[krows, :]
        vv = v_ref[0, krows, :]
        valid = jnp.abs((qpos_l + n * wb) - (kpos_l + ks)) <= SWA_WINDOW
        qb = q_ref[0, rows, :].astype(F32)
        q4 = jnp.concatenate(
            [(_rope(qb[:, SWA_DH * hh:SWA_DH * (hh + 1)], cos, sin) * (SWA_DH ** -0.5)).astype(BF16)
             for hh in range(SWA_GQ)], axis=0)
        s4 = _dot_t(q4, kk)
        probs = []
        dens = []
        for hh in range(SWA_GQ):
            sink = sink_ref[kvh * SWA_GQ + hh]
            s = jnp.where(valid, s4[wb * hh:wb * (hh + 1)], NEG_BIG)
            m = jnp.maximum(jnp.max(s, axis=-1, keepdims=True), sink)
            p = jnp.exp(s - m)
            probs.append(p.astype(BF16))
            dens.append(jnp.sum(p, axis=-1, keepdims=True) + jnp.exp(sink - m))
        o4 = jnp.dot(jnp.concatenate(probs, axis=0), vv, preferred_element_type=F32)
        for hh in range(SWA_GQ):
            o = o4[wb * hh:wb * (hh + 1)] / dens[hh]
            o_ref[0, rows, SWA_DH * hh:SWA_DH * (hh + 1)] = o.astype(o_ref.dtype)
        return carry

    lax.fori_loop(0, nb, blk, 0, unroll=2)


def _swa(proj3, sink, cos, sin, col0):
    bn, seq, _ = proj3.shape
    qw = SWA_GQ * SWA_DH
    qoff = col0 // qw
    koff = (col0 + SWA_HEADS * SWA_DH) // SWA_DH
    voff = koff + SWA_KV
    return pl.pallas_call(
        functools.partial(_swa_body, seq=seq),
        grid=(bn, SWA_KV),
        in_specs=[
            pl.BlockSpec(memory_space=pltpu.SMEM),
            pl.BlockSpec((1, seq, qw), lambda b, h: (b, 0, qoff + h)),
            pl.BlockSpec((1, seq, SWA_DH), lambda b, h: (b, 0, koff + h)),
            pl.BlockSpec((1, seq, SWA_DH), lambda b, h: (b, 0, voff + h)),
            pl.BlockSpec((seq, SWA_DH), lambda b, h: (0, 0)),
            pl.BlockSpec((seq, SWA_DH), lambda b, h: (0, 0)),
        ],
        out_specs=pl.BlockSpec((1, seq, qw), lambda b, h: (b, 0, h)),
        out_shape=jax.ShapeDtypeStruct((bn, seq, SWA_HEADS * SWA_DH), BF16),
        scratch_shapes=[pltpu.VMEM((seq, SWA_DH), BF16)],
        compiler_params=_cparams(("parallel", "parallel")),
        name="swa",
    )(sink, proj3, proj3, proj3, cos, sin)


def _pad_cols(w, width):
    return jnp.pad(w, ((0, 0), (0, width - w.shape[1])))


def _even_mixers(x2, bn, seq, g0, w_in, w_decay, b_decay, gla_norm, rpb):
    d = x2.shape[1]
    n_gla = 2 * GLA_HEADS * GLA_DK + 2 * GLA_HEADS * GLA_DV
    n_lr = 2 * GLA_LOWRANK
    w_main = jnp.concatenate([w_in[:, :n_gla], w_in[:, n_gla + n_lr:]], axis=1).astype(BF16)
    w_small = _pad_cols(w_in[:, n_gla:n_gla + n_lr], LANES).astype(BF16)
    proj, small = _inproj(x2, g0, w_main, w_small)
    proj3 = proj.reshape(bn, seq, -1)
    small3 = small.reshape(bn, seq, -1)
    wdec = jnp.zeros((2, LANES, w_decay.shape[-1]), F32)
    for dd in range(2):
        wdec = wdec.at[dd, dd * GLA_LOWRANK:(dd + 1) * GLA_LOWRANK].set(w_decay[dd])
    o_a = _gla(proj3, small3, wdec, b_decay, gla_norm[None, :])
    o_b = _na(proj3, _na_bias_table(rpb), n_gla)
    return o_a.reshape(-1, o_a.shape[-1]), o_b.reshape(-1, o_b.shape[-1])


def _per_group(v):
    return v.reshape(2, SSD_GROUPS, SSD_JH).transpose(1, 0, 2).reshape(SSD_GROUPS, 2 * SSD_JH)


def _odd_mixers(x2, bn, seq, g0, w_in, conv_w, conv_b, dt_bias, a_log, d_skip, sink, cos, sin):
    conv_dim = SSD_INNER + 2 * SSD_GROUPS * SSD_STATE
    n_z = SSD_INNER
    n_dt = 2 * SSD_HEADS
    c_dt = n_z + conv_dim
    w_main = jnp.concatenate([w_in[:, :c_dt], w_in[:, c_dt + n_dt:]], axis=1).astype(BF16)
    w_dt = w_in[:, c_dt:c_dt + n_dt].reshape(-1, 2, SSD_GROUPS, SSD_JH).transpose(0, 2, 1, 3)
    w_dt = w_dt.reshape(-1, SSD_GROUPS, 2 * SSD_JH)
    w_small = jnp.pad(w_dt, ((0, 0), (0, 0), (0, LANES - 2 * SSD_JH))).reshape(-1, SSD_GROUPS * LANES)
    proj, small = _inproj(x2, g0, w_main, w_small.astype(BF16))
    proj3 = proj.reshape(bn, seq, -1)
    dt3 = small.reshape(bn, seq, SSD_GROUPS * LANES)
    xbc_act = _conv(proj3, conv_w, conv_b[None, :], n_z)
    pbg = _per_group(dt_bias)
    alg = _per_group(a_log)
    head_of_ch = np.arange(SSD_JH * SSD_HEADDIM) // SSD_HEADDIM
    expand = jnp.asarray(np.arange(2 * SSD_JH)[None, :, None]
                         == (np.arange(2)[:, None, None] * SSD_JH + head_of_ch[None, None, :]), F32)
    dskip = jnp.repeat(d_skip.astype(F32), SSD_HEADDIM)[None, :]
    y_pre = _ssd(xbc_act, dt3, pbg[:, :, None], alg[:, :, None], dskip, expand)
    o_d = _swa(proj3, sink, cos, sin, c_dt)
    return y_pre.reshape(-1, SSD_INNER), proj, o_d.reshape(-1, o_d.shape[-1])


def kernel(x, norm_gains, ffn_w_gate, ffn_w_up, ffn_w_down, even_w_in, even_w_out, gla_w_decay,
           gla_b_decay, gla_norm, na_rpb, odd_w_in, odd_w_out, ssd_conv_w, ssd_conv_b, ssd_dt_bias,
           ssd_a_log, ssd_d, ssd_norm, swa_sink):
    bn, seq, d = x.shape
    depth = norm_gains.shape[0]
    x2 = x.reshape(bn * seq, d)
    cos, sin = _rope_tables(seq)
    ffn_w = (ffn_w_gate, ffn_w_up, ffn_w_down)
    even_w_out_b = even_w_out.astype(BF16)
    odd_w_out_b = odd_w_out.astype(BF16)
    w_bf16 = None
    for layer in range(depth):
        g = norm_gains[layer][:, None, :]
        i = layer // 2
        cast_here = dict(ffn_w=ffn_w, layer=0) if layer == 0 else {}
        cast_next = dict(ffn_w=ffn_w, next_layer=layer + 1) if layer + 1 < depth else {}
        if layer % 2 == 0:
            o_a, o_b = _even_mixers(x2, bn, seq, g[0], even_w_in[i], gla_w_decay[i], gla_b_decay[i],
                                    gla_norm[i], na_rpb[i])
            x2, *cast = _outproj(x2, o_a, o_b, even_w_out_b, i, g[1], **cast_here)
        else:
            y_pre, proj, o_d = _odd_mixers(x2, bn, seq, g[0], odd_w_in[i], ssd_conv_w[i], ssd_conv_b[i],
                                           ssd_dt_bias[i], ssd_a_log[i], ssd_d[i], swa_sink[i], cos, sin)
            x2, *cast = _outproj(x2, y_pre, o_d, odd_w_out_b, i, g[1], z_src=proj,
                                 ssd_gain=ssd_norm[i][None, :], **cast_here)
        w_bf16 = cast or w_bf16
        x2, *w_bf16 = _ffn(x2, g[2], *w_bf16, g[3], **cast_next)
    return x2.reshape(bn, seq, d)
```

```python
import functools

import jax
import jax.numpy as jnp
import numpy as np
from jax import lax
from jax.experimental import pallas as pl
from jax.experimental.pallas import tpu as pltpu

F32 = jnp.float32
BF16 = jnp.bfloat16
HIGHEST = lax.Precision.HIGHEST

EPS = 1e-6
ROPE_THETA = 10000.0
GRID_W = 64
LANES = 128

GLA_DK = 64
GLA_DV = 128
GLA_HEADS = 8
GLA_LOWRANK = 16
GLA_TAU = 16.0
GLA_CHUNK = 64
GLA_SUB = 16
GLA_EXP_CLAMP = 60.0
GLA_TRI_BLOCK = 256

NA_DH = 128
NA_HEADS = 8
NA_KH = 8
NA_KW = 16
NA_ROWS_PER_ITER = 16
NEG_BIG = -1e30

SSD_INNER = 1024
SSD_HEADDIM = 64
SSD_HEADS = 16
SSD_GROUPS = 2
SSD_STATE = 128
SSD_CONV = 5
SSD_CHUNK = 128
SSD_JH = SSD_HEADS // SSD_GROUPS
SSD_ROW_SLAB = 64

SWA_DH = 128
SWA_HEADS = 8
SWA_KV = 2
SWA_GQ = SWA_HEADS // SWA_KV
SWA_WINDOW = 128
SWA_BLOCK = 128

VMEM_LIMIT = 52 * 1024 * 1024


FFN_NORM_ROWS = 256
FFN_VMEM_LIMIT = 58 * 1024 * 1024


def _cparams(sem, vmem_limit=VMEM_LIMIT):
    return pltpu.CompilerParams(dimension_semantics=sem, vmem_limit_bytes=vmem_limit)


def _rms(x, g):
    return x * lax.rsqrt(jnp.mean(x * x, axis=-1, keepdims=True) + EPS) * g


def _silu(x):
    return x * (1.0 / (1.0 + jnp.exp(-x)))


def _dot_t(a, b):
    return lax.dot_general(a, b, (((1,), (1,)), ((), ())), preferred_element_type=F32)


def _inproj_body(x_ref, g_ref, w_ref, ws_ref, o_ref, os_ref, h_scr):
    j = pl.program_id(1)
    half = x_ref.shape[0] // 2

    @pl.when(j == 0)
    def _():
        for r0 in (0, half):
            rs = pl.ds(r0, half)
            h = _rms(x_ref[rs, :], g_ref[...]).astype(BF16)
            h_scr[rs, :] = h
            os_ref[rs, :] = jnp.dot(h, ws_ref[...], preferred_element_type=F32)
            o_ref[rs, :] = jnp.dot(h, w_ref[...], preferred_element_type=F32).astype(o_ref.dtype)

    @pl.when(j > 0)
    def _():
        o_ref[...] = jnp.dot(h_scr[...], w_ref[...], preferred_element_type=F32).astype(o_ref.dtype)


def _inproj(x2, g, w_main, w_small, tm=1024, tn=2048):
    n, d = x2.shape
    nm = w_main.shape[1]
    ns = w_small.shape[1]
    return pl.pallas_call(
        _inproj_body,
        grid=(n // tm, nm // tn),
        in_specs=[
            pl.BlockSpec((tm, d), lambda i, j: (i, 0)),
            pl.BlockSpec((1, d), lambda i, j: (0, 0)),
            pl.BlockSpec((d, tn), lambda i, j: (0, j)),
            pl.BlockSpec((d, ns), lambda i, j: (0, 0)),
        ],
        out_specs=[
            pl.BlockSpec((tm, tn), lambda i, j: (i, j)),
            pl.BlockSpec((tm, ns), lambda i, j: (i, 0)),
        ],
        out_shape=[jax.ShapeDtypeStruct((n, nm), BF16), jax.ShapeDtypeStruct((n, ns), F32)],
        scratch_shapes=[pltpu.VMEM((tm, d), BF16)],
        compiler_params=_cparams(("parallel", "arbitrary")),
        name="inproj",
    )(x2, g, w_main, w_small)


def _cast_weight_slices(w_refs):
    for src, dst in zip(w_refs[:3], w_refs[3:]):
        dst[...] = src[...].astype(BF16)


def _outproj_out(rest):
    if len(rest) == 1:
        return rest[0]
    _cast_weight_slices(rest[:3] + rest[4:])
    return rest[3]


def _outproj_even_body(a_ref, b_ref, wa_ref, wb_ref, g_ref, x_ref, *rest):
    o_ref = _outproj_out(rest)
    mix = (jnp.dot(a_ref[...], wa_ref[...], preferred_element_type=F32)
           + jnp.dot(b_ref[...], wb_ref[...], preferred_element_type=F32))
    o_ref[...] = x_ref[...] + _rms(mix, g_ref[...])


def _outproj_odd_body(y_ref, z_ref, ng_ref, b_ref, wa_ref, wb_ref, g_ref, x_ref, *rest):
    o_ref = _outproj_out(rest)
    y = y_ref[...].astype(F32) * _silu(z_ref[...].astype(F32))
    a = _rms(y, ng_ref[...]).astype(BF16)
    mix = (jnp.dot(a, wa_ref[...], preferred_element_type=F32)
           + jnp.dot(b_ref[...], wb_ref[...], preferred_element_type=F32))
    o_ref[...] = x_ref[...] + _rms(mix, g_ref[...])


def _outproj(x2, a, b, w_out, li, g, ffn_w=None, layer=None, z_src=None, ssd_gain=None, tm=512):
    n, d = x2.shape
    half = a.shape[1]
    steps = n // tm
    row = lambda i: (i, 0)
    fixed = lambda i: (0, 0)
    w_specs = [pl.BlockSpec((None, half, d), lambda i: (li, 0, 0)),
               pl.BlockSpec((None, half, d), lambda i: (li, 1, 0))]
    tail_specs = [pl.BlockSpec((1, d), fixed), pl.BlockSpec((tm, d), row)]
    ffn_w = ffn_w or ()
    cast_in = [pl.BlockSpec((None, w.shape[1] // steps, w.shape[2]), lambda i: (layer, i, 0)) for w in ffn_w]
    cast_out = [pl.BlockSpec((w.shape[1] // steps, w.shape[2]), row) for w in ffn_w]
    cast_shapes = [jax.ShapeDtypeStruct(w.shape[1:], BF16) for w in ffn_w]
    if z_src is None:
        body = _outproj_even_body
        in_specs = [pl.BlockSpec((tm, half), row), pl.BlockSpec((tm, half), row)] + w_specs + tail_specs
        args = (a, b, w_out, w_out, g, x2)
    else:
        body = _outproj_odd_body
        in_specs = ([pl.BlockSpec((tm, half), row), pl.BlockSpec((tm, half), row),
                     pl.BlockSpec((1, half), fixed), pl.BlockSpec((tm, half), row)]
                    + w_specs + tail_specs)
        args = (a, z_src, ssd_gain, b, w_out, w_out, g, x2)
    return pl.pallas_call(
        body,
        grid=(steps,),
        in_specs=in_specs + cast_in,
        out_specs=[pl.BlockSpec((tm, d), row)] + cast_out,
        out_shape=[jax.ShapeDtypeStruct((n, d), F32)] + cast_shapes,
        compiler_params=_cparams(("parallel",)),
        name="outproj",
    )(*args, *ffn_w)


def _ffn_body(x_ref, g2_ref, wg_ref, wu_ref, wd_ref, g3_ref, *rest):
    if len(rest) == 2:
        o_ref, h_scr = rest
    else:
        o_ref, h_scr = rest[3], rest[7]
        _cast_weight_slices(rest[:3] + rest[4:7])
    k = pl.program_id(1)
    last = pl.num_programs(1) - 1
    half = x_ref.shape[0] // 2

    def pieces(r0):
        return [pl.ds(r0 + p, FFN_NORM_ROWS) for p in range(0, half, FFN_NORM_ROWS)]

    def partial_down(rs):
        h = h_scr[rs, :]
        gate = jnp.dot(h, wg_ref[...], preferred_element_type=F32)
        up = jnp.dot(h, wu_ref[...], preferred_element_type=F32)
        return jnp.dot((_silu(gate) * up).astype(BF16), wd_ref[...], preferred_element_type=F32)

    @pl.when(k == 0)
    def _():
        for r0 in (0, half):
            for ps in pieces(r0):
                h_scr[ps, :] = _rms(x_ref[ps, :], g2_ref[...]).astype(BF16)
            o_ref[pl.ds(r0, half), :] = partial_down(pl.ds(r0, half))

    @pl.when(jnp.logical_and(k > 0, k < last))
    def _():
        for r0 in (0, half):
            o_ref[pl.ds(r0, half), :] += partial_down(pl.ds(r0, half))

    @pl.when(k == last)
    def _():
        for r0 in (0, half):
            o_ref[pl.ds(r0, half), :] += partial_down(pl.ds(r0, half))
            for ps in pieces(r0):
                o_ref[ps, :] = x_ref[ps, :] + _rms(o_ref[ps, :], g3_ref[...])


def _ffn(x2, g2, wg, wu, wd, g3, ffn_w=None, next_layer=None, tm=1024, th=512):
    n, d = x2.shape
    hid = wd.shape[0]
    ni, nk = n // tm, hid // th
    in_specs = [
        pl.BlockSpec((tm, d), lambda i, k: (i, 0)),
        pl.BlockSpec((1, d), lambda i, k: (0, 0)),
        pl.BlockSpec((d, th), lambda i, k: (0, k)),
        pl.BlockSpec((d, th), lambda i, k: (0, k)),
        pl.BlockSpec((th, d), lambda i, k: (k, 0)),
        pl.BlockSpec((1, d), lambda i, k: (0, 0)),
    ]
    out_specs = [pl.BlockSpec((tm, d), lambda i, k: (i, 0))]
    out_shape = [jax.ShapeDtypeStruct((n, d), F32)]
    args = [x2, g2, wg, wu, wd, g3]
    if ffn_w is not None:
        up_blk, down_blk = (d // ni, hid // nk), (hid // nk, d // ni)
        in_specs += [pl.BlockSpec((None,) + up_blk, lambda i, k: (next_layer, i, k)),
                     pl.BlockSpec((None,) + up_blk, lambda i, k: (next_layer, i, k)),
                     pl.BlockSpec((None,) + down_blk, lambda i, k: (next_layer, k, i))]
        out_specs += [pl.BlockSpec(up_blk, lambda i, k: (i, k)),
                      pl.BlockSpec(up_blk, lambda i, k: (i, k)),
                      pl.BlockSpec(down_blk, lambda i, k: (k, i))]
        out_shape += [jax.ShapeDtypeStruct(w.shape[1:], BF16) for w in ffn_w]
        args += list(ffn_w)
    return pl.pallas_call(
        _ffn_body,
        grid=(ni, nk),
        in_specs=in_specs,
        out_specs=out_specs,
        out_shape=out_shape,
        scratch_shapes=[pltpu.VMEM((tm, d), BF16)],
        compiler_params=_cparams(("parallel", "arbitrary"), FFN_VMEM_LIMIT),
        name="ffn",
    )(*args)


def _gla_body(q_ref, k_ref, v_ref, g_ref, lr_ref, wd_ref, bd_ref, gn_ref, o_ref,
              bc_scr, o_scr, st_scr, *, seq):
    c = GLA_CHUNK
    nc = seq // c
    half = nc // 2
    nsub = c // GLA_SUB
    rb = min(512, seq)
    row_i = lax.broadcasted_iota(jnp.int32, (c, c), 0)
    col_i = lax.broadcasted_iota(jnp.int32, (c, c), 1)
    tri = (row_i >= col_i, row_i <= col_i)
    lane = lax.broadcasted_iota(jnp.int32, (1, LANES), 1)
    hmask = (lane < GLA_DK, lane >= GLA_DK)
    own_head = ((lax.broadcasted_iota(jnp.int32, (2 * GLA_DV, LANES), 0) < GLA_DV)
                == (lax.broadcasted_iota(jnp.int32, (2 * GLA_DV, LANES), 1) < GLA_DK))

    w2 = jnp.concatenate([wd_ref[0], wd_ref[1]], axis=1)
    w_hi = w2.astype(BF16)
    w_lo = (w2 - w_hi.astype(F32)).astype(BF16)
    w_cat = jnp.concatenate([w_hi, w_lo, w_hi], axis=0)
    tb = GLA_TRI_BLOCK
    brow = lax.broadcasted_iota(jnp.int32, (tb, tb), 0)
    bcol = lax.broadcasted_iota(jnp.int32, (tb, tb), 1)
    same_chunk = (brow // c) == (bcol // c)
    blk_tri = ((same_chunk & (brow >= bcol)).astype(BF16), (same_chunk & (brow <= bcol)).astype(BF16))

    def pre(i, carry):
        r0 = pl.multiple_of(i * rb, rb)
        lr = lr_ref[0, pl.ds(r0, rb), :]
        lr_hi = lr.astype(BF16)
        lr_lo = (lr - lr_hi.astype(F32)).astype(BF16)
        z2 = jnp.dot(jnp.concatenate([lr_hi, lr_hi, lr_lo], axis=1), w_cat, preferred_element_type=F32)
        las = []
        for d in range(2):
            z = z2[:, LANES * d:LANES * (d + 1)] + bd_ref[d:d + 1, :]
            la = (jnp.minimum(z, 0.0) - jnp.log(1.0 + jnp.exp(-jnp.abs(z)))) * (1.0 / GLA_TAU)
            la_hi = la.astype(BF16)
            las.append(jnp.concatenate([la_hi, (la - la_hi.astype(F32)).astype(BF16)], axis=1))
        for d in range(2):
            for t in range(rb // tb):
                s2 = jnp.dot(blk_tri[d], las[d][t * tb:(t + 1) * tb], preferred_element_type=F32)
                bc_scr[d, pl.ds(r0 + t * tb, tb), :] = s2[:, :LANES] + s2[:, LANES:]
        return carry

    lax.fori_loop(0, seq // rb, pre, 0, unroll=4)

    def stage1(n, d):
        rows = pl.ds(pl.multiple_of(n * c, c), c)
        bc = bc_scr[d, rows, :]
        q = q_ref[0, rows, :].astype(F32) * (GLA_DK ** -0.5)
        k = k_ref[0, rows, :].astype(F32)
        v = v_ref[0, rows, :]
        b_edge = bc[c - 1:c] if d == 0 else bc[0:1]
        qhat = q * jnp.exp(bc)
        khat = (k * jnp.exp(b_edge - bc)).astype(BF16)
        att_rows = ([], [])
        for i in range(nsub):
            lo = GLA_SUB * i
            hi = lo + GLA_SUB
            if d == 0:
                ref = bc[lo - 1:lo] if i > 0 else jnp.zeros((1, LANES), F32)
            else:
                ref = bc[hi:hi + 1] if i < nsub - 1 else jnp.zeros((1, LANES), F32)
            qi = q[lo:hi] * jnp.exp(bc[lo:hi] - ref)
            ki = (k * jnp.exp(jnp.minimum(ref - bc, GLA_EXP_CLAMP))).astype(BF16)
            q2 = jnp.concatenate([jnp.where(hmask[0], qi, 0.0), jnp.where(hmask[1], qi, 0.0)], axis=0)
            a2 = _dot_t(q2.astype(BF16), ki)
            att_rows[0].append(a2[:GLA_SUB])
            att_rows[1].append(a2[GLA_SUB:])
        kv2 = lax.dot_general(v, khat, (((0,), (0,)), ((), ())), preferred_element_type=F32)
        kv2 = jnp.where(own_head, kv2, 0.0)
        att2 = jnp.concatenate(
            [jnp.where(tri[d], jnp.concatenate(att_rows[hh], axis=0), 0.0).astype(BF16)
             for hh in range(2)], axis=1)
        zero_v = jnp.zeros((c, GLA_DV), BF16)
        v_blockdiag = jnp.concatenate(
            [jnp.concatenate([v[:, :GLA_DV], zero_v], axis=1),
             jnp.concatenate([zero_v, v[:, GLA_DV:]], axis=1)], axis=0)
        return d, att2, v_blockdiag, kv2, qhat.astype(BF16), jnp.exp(b_edge)

    def stage2(s1, states):
        d, att2, v_blockdiag, kv2, qhat, decay = s1
        st = states[d]
        o = jnp.dot(att2, v_blockdiag, preferred_element_type=F32) + _dot_t(qhat, st.astype(BF16))
        states[d] = st * decay + kv2
        return o

    def finish(n, o):
        rows = pl.ds(pl.multiple_of(n * c, c), c)
        parts = []
        for hh in range(2):
            oh = o[:, GLA_DV * hh:GLA_DV * (hh + 1)]
            parts.append(oh * lax.rsqrt(jnp.mean(oh * oh, axis=-1, keepdims=True) + EPS))
        o = jnp.concatenate(parts, axis=1) * gn_ref[...]
        o_ref[0, rows, :] = (o * _silu(g_ref[0, rows, :].astype(F32))).astype(o_ref.dtype)

    st_scr[...] = jnp.zeros_like(st_scr)

    cpi = next(u for u in (4, 2, 1) if half % u == 0)

    def walk(i, base_f, base_b):
        work = []
        for u in range(cpi):
            work.append((base_f + i * cpi + u, 0))
            work.append((base_b - i * cpi - u, 1))
        prepared = [stage1(n, d) for n, d in work]
        states = [st_scr[d] for d in range(2)]
        outs = [(n, stage2(s1, states)) for (n, _), s1 in zip(work, prepared)]
        for d in range(2):
            st_scr[d] = states[d]
        return outs

    def first(i, carry):
        for n, o in walk(i, 0, nc - 1):
            o_scr[pl.ds(pl.multiple_of(n * c, c), c), :] = o
        return carry

    lax.fori_loop(0, half // cpi, first, 0)

    def second(i, carry):
        for n, o in walk(i, half, half - 1):
            finish(n, o_scr[pl.ds(pl.multiple_of(n * c, c), c), :] + o)
        return carry

    lax.fori_loop(0, half // cpi, second, 0)


def _gla(proj3, small3, wdec_pad, bdec, gnorm):
    bn, seq, _ = proj3.shape
    npair = GLA_HEADS // 2
    qoff = 0
    koff = (GLA_HEADS * GLA_DK) // LANES
    voff = (2 * GLA_HEADS * GLA_DK) // (2 * GLA_DV)
    goff = voff + npair
    return pl.pallas_call(
        functools.partial(_gla_body, seq=seq),
        grid=(bn, npair),
        in_specs=[
            pl.BlockSpec((1, seq, LANES), lambda b, p: (b, 0, qoff + p)),
            pl.BlockSpec((1, seq, LANES), lambda b, p: (b, 0, koff + p)),
            pl.BlockSpec((1, seq, 2 * GLA_DV), lambda b, p: (b, 0, voff + p)),
            pl.BlockSpec((1, seq, 2 * GLA_DV), lambda b, p: (b, 0, goff + p)),
            pl.BlockSpec((1, seq, LANES), lambda b, p: (b, 0, 0)),
            pl.BlockSpec((2, LANES, LANES), lambda b, p: (0, 0, p)),
            pl.BlockSpec((2, LANES), lambda b, p: (0, p)),
            pl.BlockSpec((1, 2 * GLA_DV), lambda b, p: (0, p)),
        ],
        out_specs=pl.BlockSpec((1, seq, 2 * GLA_DV), lambda b, p: (b, 0, p)),
        out_shape=jax.ShapeDtypeStruct((bn, seq, GLA_HEADS * GLA_DV), BF16),
        scratch_shapes=[
            pltpu.VMEM((2, seq, LANES), F32),
            pltpu.VMEM((seq, 2 * GLA_DV), F32),
            pltpu.VMEM((2, 2 * GLA_DV, LANES), F32),
        ],
        compiler_params=_cparams(("parallel", "parallel")),
        name="gla",
    )(proj3, proj3, proj3, proj3, small3, wdec_pad, bdec, gnorm)


def _na_bias_table(rpb):
    nh, ndr, ndc = rpb.shape
    cols = np.arange(GRID_W)
    cs = np.clip(cols - NA_KW // 2, 0, GRID_W - NA_KW)
    valid = (cols[None, :] >= cs[:, None]) & (cols[None, :] < cs[:, None] + NA_KW)
    dc = cols[None, :] - cols[:, None] + NA_KW - 1
    onehot = (np.arange(ndc)[:, None, None] == dc[None]) & valid[None]
    toep = jnp.dot(rpb.reshape(nh * ndr, ndc).astype(F32),
                   jnp.asarray(onehot.reshape(ndc, -1), F32), precision=HIGHEST)
    toep = jnp.where(jnp.asarray(valid.reshape(1, -1)), toep, NEG_BIG)
    toep = toep.reshape(nh, ndr, GRID_W, GRID_W)
    t = jnp.stack([toep[:, off:off + NA_KH] for off in range(NA_KH)], axis=1)
    return t.transpose(0, 1, 3, 2, 4).reshape(nh, NA_KH, GRID_W, NA_KH * GRID_W)


def _na_body(q_ref, k_ref, v_ref, bias_ref, o_ref, *, nrows):
    w = GRID_W
    nk = NA_KH * w

    def group(i, carry):
        idx = []
        scores = []
        for u in range(NA_ROWS_PER_ITER):
            r = i * NA_ROWS_PER_ITER + u
            rs = jnp.clip(r - NA_KH // 2, 0, nrows - NA_KH)
            qrows = pl.ds(pl.multiple_of(r * w, w), w)
            krows = pl.ds(pl.multiple_of(rs * w, w), nk)
            idx.append((qrows, krows, rs - r + (NA_KH - 1)))
            scores.append(_dot_t(q_ref[0, qrows, :], k_ref[0, krows, :]))
        probs = []
        for (qrows, krows, off), s in zip(idx, scores):
            s = s * (NA_DH ** -0.5) + bias_ref[0, off]
            p = jnp.exp(s - jnp.max(s, axis=-1, keepdims=True))
            probs.append((p.astype(BF16), jnp.sum(p, axis=-1, keepdims=True)))
        for (qrows, krows, off), (p, l) in zip(idx, probs):
            o = jnp.dot(p, v_ref[0, krows, :], preferred_element_type=F32) / l
            o_ref[0, qrows, :] = o.astype(o_ref.dtype)
        return carry

    lax.fori_loop(0, nrows // NA_ROWS_PER_ITER, group, 0, unroll=2)


def _na(proj3, bias_tab, col0):
    bn, seq, _ = proj3.shape
    nrows = seq // GRID_W
    qoff = col0 // NA_DH
    koff = qoff + NA_HEADS
    voff = koff + NA_HEADS
    return pl.pallas_call(
        functools.partial(_na_body, nrows=nrows),
        grid=(bn, NA_HEADS),
        in_specs=[
            pl.BlockSpec((1, seq, NA_DH), lambda b, h: (b, 0, qoff + h)),
            pl.BlockSpec((1, seq, NA_DH), lambda b, h: (b, 0, koff + h)),
            pl.BlockSpec((1, seq, NA_DH), lambda b, h: (b, 0, voff + h)),
            pl.BlockSpec((1, NA_KH, GRID_W, NA_KH * GRID_W), lambda b, h: (h, 0, 0, 0)),
        ],
        out_specs=pl.BlockSpec((1, seq, NA_DH), lambda b, h: (b, 0, h)),
        out_shape=jax.ShapeDtypeStruct((bn, seq, NA_HEADS * NA_DH), BF16),
        compiler_params=_cparams(("parallel", "parallel")),
        name="na",
    )(proj3, proj3, proj3, bias_tab)


CONV_PAD = 16


def _conv_body(x_ref, w_ref, b_ref, o_ref, xp_scr, *, seq):
    rb = min(256, seq)
    ch = x_ref.shape[-1]
    zeros = jnp.zeros((CONV_PAD, ch), xp_scr.dtype)
    xp_scr[0:CONV_PAD, :] = zeros
    xp_scr[seq + CONV_PAD:seq + 2 * CONV_PAD, :] = zeros
    xp_scr[CONV_PAD:seq + CONV_PAD, :] = x_ref[0]

    def blk(i, carry):
        r0 = pl.multiple_of(i * rb, rb)
        xw = xp_scr[pl.ds(r0, rb + 2 * CONV_PAD), :].astype(F32)
        acc = jnp.zeros((rb, ch), F32) + b_ref[...]
        for j in range(SSD_CONV):
            s0 = CONV_PAD - SSD_CONV // 2 + j
            acc = acc + xw[s0:s0 + rb] * w_ref[j:j + 1, :]
        o_ref[0, pl.ds(r0, rb), :] = _silu(acc).astype(o_ref.dtype)
        return carry

    lax.fori_loop(0, seq // rb, blk, 0)


def _conv(proj3, conv_w, conv_b, col0, tc=512):
    bn, seq, _ = proj3.shape
    cdim = conv_w.shape[1]
    c0 = col0 // tc
    return pl.pallas_call(
        functools.partial(_conv_body, seq=seq),
        grid=(bn, cdim // tc),
        in_specs=[
            pl.BlockSpec((1, seq, tc), lambda b, j: (b, 0, c0 + j)),
            pl.BlockSpec((SSD_CONV, tc), lambda b, j: (0, j)),
            pl.BlockSpec((1, tc), lambda b, j: (0, j)),
        ],
        out_specs=pl.BlockSpec((1, seq, tc), lambda b, j: (b, 0, j)),
        out_shape=jax.ShapeDtypeStruct((bn, seq, cdim), BF16),
        scratch_shapes=[pltpu.VMEM((seq + 2 * CONV_PAD, tc), BF16)],
        compiler_params=_cparams(("parallel", "parallel")),
        name="ssd_conv",
    )(proj3, conv_w, conv_b)


def _softplus(x):
    return jnp.maximum(x, 0.0) + jnp.log(1.0 + jnp.exp(-jnp.abs(x)))


def _ssd_body(xs_ref, bm_ref, cm_ref, dt_ref, pb_ref, al_ref, dsk_ref, ex_ref, o_ref,
              st_scr, run_scr, dec_scr, *, seq):
    l = SSD_CHUNK
    nc = seq // l
    nh2 = 2 * SSD_JH
    row_i = lax.broadcasted_iota(jnp.int32, (l, l), 0)
    col_i = lax.broadcasted_iota(jnp.int32, (l, l), 1)
    lane = lax.broadcasted_iota(jnp.int32, (1, LANES), 1)
    first_head = lane < SSD_HEADDIM
    bwd_row = lax.broadcasted_iota(jnp.int32, (nh2, 1), 0) >= SSD_JH
    a_col = -jnp.exp(al_ref[0])

    lower = row_i > col_i
    diag = row_i == col_i
    triu_b = (row_i <= col_i).astype(BF16)

    def decays(n):
        rows = pl.ds(pl.multiple_of(n * l, l), l)
        dtv = _softplus(dt_ref[0, rows, :].T[0:nh2] + pb_ref[0])
        da = dtv * a_col
        hi = da.astype(BF16)
        r1 = da - hi.astype(F32)
        mid = r1.astype(BF16)
        lo = (r1 - mid.astype(F32)).astype(BF16)
        p3 = jnp.dot(jnp.concatenate([hi, mid, lo], axis=0), triu_b, preferred_element_type=F32)
        pre = p3[0:nh2] + p3[nh2:2 * nh2] + p3[2 * nh2:]
        total = pre[:, l - 1:l]
        acum = jnp.where(bwd_row, total - pre + da, pre)
        dec_scr[0, n] = dtv
        dec_scr[1, n] = acum
        dec_scr[2, n] = dtv * jnp.exp(total - acum)
        dec_scr[3, n] = jnp.broadcast_to(jnp.exp(total), (nh2, l))
        dec_scr[4, n] = acum - jnp.log(dtv)

    def chunk_state(n, d):
        rows = pl.ds(pl.multiple_of(n * l, l), l)
        w_state_t = dec_scr[2, n]
        x = xs_ref[0, rows, :]
        bmt = bm_ref[0, rows, :].astype(F32).T
        edge_ch = jnp.sum(dec_scr[3, n][:, 0:1] * ex_ref[d], axis=0, keepdims=True)
        new_st = []
        ns = SSD_STATE
        for jj in range(SSD_JH // 2):
            xp = x[:, LANES * jj:LANES * (jj + 1)]
            bw2 = jnp.concatenate(
                [(bmt * w_state_t[d * SSD_JH + 2 * jj + hh:d * SSD_JH + 2 * jj + hh + 1, :]).astype(BF16)
                 for hh in range(2)], axis=0)
            s2 = jnp.dot(bw2, xp, preferred_element_type=F32)
            new_st.append(jnp.where(first_head, s2[:ns], s2[ns:]))
        st = run_scr[d]
        st_scr[d, n] = st.astype(BF16)
        run_scr[d] = st * edge_ch + jnp.concatenate(new_st, axis=1)

    def chunk_out(n):
        rows = pl.ds(pl.multiple_of(n * l, l), l)
        dtv_t = dec_scr[0, n]
        acum = dec_scr[1, n].T
        key_t = dec_scr[4, n]
        x = xs_ref[0, rows, :]
        cm = cm_ref[0, rows, :]
        cmf = cm.astype(F32)
        cb = _dot_t(cm, bm_ref[0, rows, :])
        st_f = st_scr[0, n]
        st_b = st_scr[1, n]
        rhs = [jnp.concatenate([x[:, LANES * jj:LANES * (jj + 1)],
                                st_f[:, LANES * jj:LANES * (jj + 1)],
                                st_b[:, LANES * jj:LANES * (jj + 1)]], axis=0)
               for jj in range(SSD_JH // 2)]
        slabs = [slice(r0, r0 + SSD_ROW_SLAB) for r0 in range(0, l, SSD_ROW_SLAB)]
        ys = [[] for _ in slabs]
        for jj in range(SSD_JH // 2):
            lhs = []
            for rs in slabs:
                for hh in range(2):
                    cf = 2 * jj + hh
                    cr = SSD_JH + cf
                    col_f = jnp.broadcast_to(acum[rs, cf:cf + 1], (SSD_ROW_SLAB, l))
                    col_r = jnp.broadcast_to(acum[rs, cr:cr + 1], (SSD_ROW_SLAB, l))
                    seg = jnp.where(lower[rs], col_f - key_t[cf:cf + 1, :], col_r - key_t[cr:cr + 1, :])
                    m = cb[rs] * (jnp.exp(seg) + jnp.where(diag[rs], dtv_t[cf:cf + 1, :], 0.0))
                    lhs.append(jnp.concatenate([m.astype(BF16),
                                                (cmf[rs] * jnp.exp(col_f)).astype(BF16),
                                                (cmf[rs] * jnp.exp(col_r)).astype(BF16)], axis=1))
            y4 = jnp.dot(jnp.concatenate(lhs, axis=0), rhs[jj], preferred_element_type=F32)
            for si in range(len(slabs)):
                r0 = 2 * si * SSD_ROW_SLAB
                ys[si].append(jnp.where(first_head, y4[r0:r0 + SSD_ROW_SLAB],
                                        y4[r0 + SSD_ROW_SLAB:r0 + 2 * SSD_ROW_SLAB]))
        for rs, y_parts in zip(slabs, ys):
            y = jnp.concatenate(y_parts, axis=1) + x[rs].astype(F32) * dsk_ref[...]
            o_ref[0, pl.ds(pl.multiple_of(n * l, l) + rs.start, SSD_ROW_SLAB), :] = y.astype(o_ref.dtype)

    def prep(i, carry):
        decays(i)
        return carry

    lax.fori_loop(0, nc, prep, 0, unroll=4)
    run_scr[...] = jnp.zeros_like(run_scr)

    def states(i, carry):
        chunk_state(i, 0)
        chunk_state(nc - 1 - i, 1)
        return carry

    lax.fori_loop(0, nc, states, 0, unroll=4)

    def outputs(n, carry):
        chunk_out(n)
        return carry

    lax.fori_loop(0, nc, outputs, 0, unroll=4)


def _ssd(xbc_act, dt3, pb, al, dskip, expand):
    bn, seq, _ = xbc_act.shape
    gw = SSD_JH * SSD_HEADDIM
    boff = SSD_INNER // SSD_STATE
    coff = boff + SSD_GROUPS
    return pl.pallas_call(
        functools.partial(_ssd_body, seq=seq),
        grid=(bn, SSD_GROUPS),
        in_specs=[
            pl.BlockSpec((1, seq, gw), lambda b, g: (b, 0, g)),
            pl.BlockSpec((1, seq, SSD_STATE), lambda b, g: (b, 0, boff + g)),
            pl.BlockSpec((1, seq, SSD_STATE), lambda b, g: (b, 0, coff + g)),
            pl.BlockSpec((1, seq, LANES), lambda b, g: (b, 0, g)),
            pl.BlockSpec((1, 2 * SSD_JH, 1), lambda b, g: (g, 0, 0)),
            pl.BlockSpec((1, 2 * SSD_JH, 1), lambda b, g: (g, 0, 0)),
            pl.BlockSpec((1, gw), lambda b, g: (0, g)),
            pl.BlockSpec((2, 2 * SSD_JH, gw), lambda b, g: (0, 0, 0)),
        ],
        out_specs=pl.BlockSpec((1, seq, gw), lambda b, g: (b, 0, g)),
        out_shape=jax.ShapeDtypeStruct((bn, seq, SSD_INNER), BF16),
        scratch_shapes=[pltpu.VMEM((2, seq // SSD_CHUNK, SSD_STATE, gw), BF16),
                        pltpu.VMEM((2, SSD_STATE, gw), F32),
                        pltpu.VMEM((5, seq // SSD_CHUNK, 2 * SSD_JH, SSD_CHUNK), F32)],
        compiler_params=_cparams(("parallel", "parallel")),
        name="ssd_scan",
    )(xbc_act, xbc_act, xbc_act, dt3, pb, al, dskip, expand)


def _rope_tables(seq):
    half = SWA_DH // 2
    inv = ROPE_THETA ** (-np.arange(half, dtype=np.float64) / half)
    ang = np.arange(seq, dtype=np.float64)[:, None] * inv[None, :]
    cos, sin = np.cos(ang), np.sin(ang)
    return (jnp.asarray(np.concatenate([cos, cos], axis=1), F32),
            jnp.asarray(np.concatenate([-sin, sin], axis=1), F32))


def _rope(x, cos, sin_signed):
    return x * cos + pltpu.roll(x, SWA_DH // 2, 1) * sin_signed


def _swa_body(sink_ref, q_ref, k_ref, v_ref, cos_ref, sin_ref, o_ref, kr_scr, *, seq):
    wb = SWA_BLOCK
    nb = seq // wb
    nkeys = 3 * wb
    kvh = pl.program_id(1)
    rb = min(512, seq)

    def krope(i, carry):
        rows = pl.ds(pl.multiple_of(i * rb, rb), rb)
        kr_scr[rows, :] = _rope(k_ref[0, rows, :].astype(F32), cos_ref[rows, :],
                                sin_ref[rows, :]).astype(BF16)
        return carry

    lax.fori_loop(0, seq // rb, krope, 0)

    qpos_l = lax.broadcasted_iota(jnp.int32, (wb, nkeys), 0)
    kpos_l = lax.broadcasted_iota(jnp.int32, (wb, nkeys), 1)

    def blk(n, carry):
        rows = pl.ds(pl.multiple_of(n * wb, wb), wb)
        ks = pl.multiple_of(jnp.clip(n - 1, 0, nb - 3) * wb, wb)
        krows = pl.ds(ks, nkeys)
        cos = cos_ref[rows, :]
        sin = sin_ref[rows, :]
        kk = kr_scr[krows, :]
        vv = v_ref[0, krows, :]
        valid = jnp.abs((qpos_l + n * wb) - (kpos_l + ks)) <= SWA_WINDOW
        qb = q_ref[0, rows, :].astype(F32)
        q4 = jnp.concatenate(
            [(_rope(qb[:, SWA_DH * hh:SWA_DH * (hh + 1)], cos, sin) * (SWA_DH ** -0.5)).astype(BF16)
             for hh in range(SWA_GQ)], axis=0)
        s4 = _dot_t(q4, kk)
        probs = []
        dens = []
        for hh in range(SWA_GQ):
            sink = sink_ref[kvh * SWA_GQ + hh]
            s = jnp.where(valid, s4[wb * hh:wb * (hh + 1)], NEG_BIG)
            m = jnp.maximum(jnp.max(s, axis=-1, keepdims=True), sink)
            p = jnp.exp(s - m)
            probs.append(p.astype(BF16))
            dens.append(jnp.sum(p, axis=-1, keepdims=True) + jnp.exp(sink - m))
        o4 = jnp.dot(jnp.concatenate(probs, axis=0), vv, preferred_element_type=F32)
        for hh in range(SWA_GQ):
            o = o4[wb * hh:wb * (hh + 1)] / dens[hh]
            o_ref[0, rows, SWA_DH * hh:SWA_DH * (hh + 1)] = o.astype(o_ref.dtype)
        return carry

    lax.fori_loop(0, nb, blk, 0, unroll=4)


def _swa(proj3, sink, cos, sin, col0):
    bn, seq, _ = proj3.shape
    qw = SWA_GQ * SWA_DH
    qoff = col0 // qw
    koff = (col0 + SWA_HEADS * SWA_DH) // SWA_DH
    voff = koff + SWA_KV
    return pl.pallas_call(
        functools.partial(_swa_body, seq=seq),
        grid=(bn, SWA_KV),
        in_specs=[
            pl.BlockSpec(memory_space=pltpu.SMEM),
            pl.BlockSpec((1, seq, qw), lambda b, h: (b, 0, qoff + h)),
            pl.BlockSpec((1, seq, SWA_DH), lambda b, h: (b, 0, koff + h)),
            pl.BlockSpec((1, seq, SWA_DH), lambda b, h: (b, 0, voff + h)),
            pl.BlockSpec((seq, SWA_DH), lambda b, h: (0, 0)),
            pl.BlockSpec((seq, SWA_DH), lambda b, h: (0, 0)),
        ],
        out_specs=pl.BlockSpec((1, seq, qw), lambda b, h: (b, 0, h)),
        out_shape=jax.ShapeDtypeStruct((bn, seq, SWA_HEADS * SWA_DH), BF16),
        scratch_shapes=[pltpu.VMEM((seq, SWA_DH), BF16)],
        compiler_params=_cparams(("parallel", "parallel")),
        name="swa",
    )(sink, proj3, proj3, proj3, cos, sin)


def _pad_cols(w, width):
    return jnp.pad(w, ((0, 0), (0, width - w.shape[1])))


def _even_mixers(x2, bn, seq, g0, w_in, w_decay, b_decay, gla_norm, rpb):
    d = x2.shape[1]
    n_gla = 2 * GLA_HEADS * GLA_DK + 2 * GLA_HEADS * GLA_DV
    n_lr = 2 * GLA_LOWRANK
    w_main = jnp.concatenate([w_in[:, :n_gla], w_in[:, n_gla + n_lr:]], axis=1).astype(BF16)
    w_small = _pad_cols(w_in[:, n_gla:n_gla + n_lr], LANES).astype(BF16)
    proj, small = _inproj(x2, g0, w_main, w_small)
    proj3 = proj.reshape(bn, seq, -1)
    small3 = small.reshape(bn, seq, -1)
    wdec = jnp.zeros((2, LANES, w_decay.shape[-1]), F32)
    for dd in range(2):
        wdec = wdec.at[dd, dd * GLA_LOWRANK:(dd + 1) * GLA_LOWRANK].set(w_decay[dd])
    o_a = _gla(proj3, small3, wdec, b_decay, gla_norm[None, :])
    o_b = _na(proj3, _na_bias_table(rpb), n_gla)
    return o_a.reshape(-1, o_a.shape[-1]), o_b.reshape(-1, o_b.shape[-1])


def _per_group(v):
    return v.reshape(2, SSD_GROUPS, SSD_JH).transpose(1, 0, 2).reshape(SSD_GROUPS, 2 * SSD_JH)


def _odd_mixers(x2, bn, seq, g0, w_in, conv_w, conv_b, dt_bias, a_log, d_skip, sink, cos, sin):
    conv_dim = SSD_INNER + 2 * SSD_GROUPS * SSD_STATE
    n_z = SSD_INNER
    n_dt = 2 * SSD_HEADS
    c_dt = n_z + conv_dim
    w_main = jnp.concatenate([w_in[:, :c_dt], w_in[:, c_dt + n_dt:]], axis=1).astype(BF16)
    w_dt = w_in[:, c_dt:c_dt + n_dt].reshape(-1, 2, SSD_GROUPS, SSD_JH).transpose(0, 2, 1, 3)
    w_dt = w_dt.reshape(-1, SSD_GROUPS, 2 * SSD_JH)
    w_small = jnp.pad(w_dt, ((0, 0), (0, 0), (0, LANES - 2 * SSD_JH))).reshape(-1, SSD_GROUPS * LANES)
    proj, small = _inproj(x2, g0, w_main, w_small.astype(BF16))
    proj3 = proj.reshape(bn, seq, -1)
    dt3 = small.reshape(bn, seq, SSD_GROUPS * LANES)
    xbc_act = _conv(proj3, conv_w, conv_b[None, :], n_z)
    pbg = _per_group(dt_bias)
    alg = _per_group(a_log)
    head_of_ch = np.arange(SSD_JH * SSD_HEADDIM) // SSD_HEADDIM
    expand = jnp.asarray(np.arange(2 * SSD_JH)[None, :, None]
                         == (np.arange(2)[:, None, None] * SSD_JH + head_of_ch[None, None, :]), F32)
    dskip = jnp.repeat(d_skip.astype(F32), SSD_HEADDIM)[None, :]
    y_pre = _ssd(xbc_act, dt3, pbg[:, :, None], alg[:, :, None], dskip, expand)
    o_d = _swa(proj3, sink, cos, sin, c_dt)
    return y_pre.reshape(-1, SSD_INNER), proj, o_d.reshape(-1, o_d.shape[-1])


def kernel(x, norm_gains, ffn_w_gate, ffn_w_up, ffn_w_down, even_w_in, even_w_out, gla_w_decay,
           gla_b_decay, gla_norm, na_rpb, odd_w_in, odd_w_out, ssd_conv_w, ssd_conv_b, ssd_dt_bias,
           ssd_a_log, ssd_d, ssd_norm, swa_sink):
    bn, seq, d = x.shape
    depth = norm_gains.shape[0]
    x2 = x.reshape(bn * seq, d)
    cos, sin = _rope_tables(seq)
    ffn_w = (ffn_w_gate, ffn_w_up, ffn_w_down)
    even_w_out_b = even_w_out.astype(BF16)
    odd_w_out_b = odd_w_out.astype(BF16)
    w_bf16 = None
    for layer in range(depth):
        g = norm_gains[layer][:, None, :]
        i = layer // 2
        cast_here = dict(ffn_w=ffn_w, layer=0) if layer == 0 else {}
        cast_next = dict(ffn_w=ffn_w, next_layer=layer + 1) if layer + 1 < depth else {}
        if layer % 2 == 0:
            o_a, o_b = _even_mixers(x2, bn, seq, g[0], even_w_in[i], gla_w_decay[i], gla_b_decay[i],
                                    gla_norm[i], na_rpb[i])
            x2, *cast = _outproj(x2, o_a, o_b, even_w_out_b, i, g[1], **cast_here)
        else:
            y_pre, proj, o_d = _odd_mixers(x2, bn, seq, g[0], odd_w_in[i], ssd_conv_w[i], ssd_conv_b[i],
                                           ssd_dt_bias[i], ssd_a_log[i], ssd_d[i], swa_sink[i], cos, sin)
            x2, *cast = _outproj(x2, y_pre, o_d, odd_w_out_b, i, g[1], z_src=proj,
                                 ssd_gain=ssd_norm[i][None, :], **cast_here)
        w_bf16 = cast or w_bf16
        x2, *w_bf16 = _ffn(x2, g[2], *w_bf16, g[3], **cast_next)
    return x2.reshape(bn, seq, d)
```

```python
import functools

import jax
import jax.numpy as jnp
import numpy as np
from jax import lax
from jax.experimental import pallas as pl
from jax.experimental.pallas import tpu as pltpu

F32 = jnp.float32
BF16 = jnp.bfloat16
HIGHEST = lax.Precision.HIGHEST

EPS = 1e-6
ROPE_THETA = 10000.0
GRID_W = 64
LANES = 128

GLA_DK = 64
GLA_DV = 128
GLA_HEADS = 8
GLA_LOWRANK = 16
GLA_TAU = 16.0
GLA_CHUNK = 64
GLA_SUB = 16
GLA_EXP_CLAMP = 60.0
GLA_TRI_BLOCK = 256

NA_DH = 128
NA_HEADS = 8
NA_KH = 8
NA_KW = 16
NA_ROWS_PER_ITER = 16
NEG_BIG = -1e30

SSD_INNER = 1024
SSD_HEADDIM = 64
SSD_HEADS = 16
SSD_GROUPS = 2
SSD_STATE = 128
SSD_CONV = 5
SSD_CHUNK = 128
SSD_JH = SSD_HEADS // SSD_GROUPS
SSD_ROW_SLAB = 64

SWA_DH = 128
SWA_HEADS = 8
SWA_KV = 2
SWA_GQ = SWA_HEADS // SWA_KV
SWA_WINDOW = 128
SWA_BLOCK = 128

VMEM_LIMIT = 52 * 1024 * 1024


FFN_NORM_ROWS = 256
FFN_VMEM_LIMIT = 58 * 1024 * 1024


def _cparams(sem, vmem_limit=VMEM_LIMIT):
    return pltpu.CompilerParams(dimension_semantics=sem, vmem_limit_bytes=vmem_limit)


def _rms(x, g):
    return x * lax.rsqrt(jnp.mean(x * x, axis=-1, keepdims=True) + EPS) * g


def _silu(x):
    return x * (1.0 / (1.0 + jnp.exp(-x)))


def _dot_t(a, b):
    return lax.dot_general(a, b, (((1,), (1,)), ((), ())), preferred_element_type=F32)


def _inproj_body(x_ref, g_ref, w_ref, ws_ref, o_ref, os_ref, h_scr):
    j = pl.program_id(1)
    half = x_ref.shape[0] // 2

    @pl.when(j == 0)
    def _():
        for r0 in (0, half):
            rs = pl.ds(r0, half)
            h = _rms(x_ref[rs, :], g_ref[...]).astype(BF16)
            h_scr[rs, :] = h
            os_ref[rs, :] = jnp.dot(h, ws_ref[...], preferred_element_type=F32)
            o_ref[rs, :] = jnp.dot(h, w_ref[...], preferred_element_type=F32).astype(o_ref.dtype)

    @pl.when(j > 0)
    def _():
        o_ref[...] = jnp.dot(h_scr[...], w_ref[...], preferred_element_type=F32).astype(o_ref.dtype)


def _inproj(x2, g, w_main, w_small, tm=1024, tn=2048):
    n, d = x2.shape
    nm = w_main.shape[1]
    ns = w_small.shape[1]
    return pl.pallas_call(
        _inproj_body,
        grid=(n // tm, nm // tn),
        in_specs=[
            pl.BlockSpec((tm, d), lambda i, j: (i, 0)),
            pl.BlockSpec((1, d), lambda i, j: (0, 0)),
            pl.BlockSpec((d, tn), lambda i, j: (0, j)),
            pl.BlockSpec((d, ns), lambda i, j: (0, 0)),
        ],
        out_specs=[
            pl.BlockSpec((tm, tn), lambda i, j: (i, j)),
            pl.BlockSpec((tm, ns), lambda i, j: (i, 0)),
        ],
        out_shape=[jax.ShapeDtypeStruct((n, nm), BF16), jax.ShapeDtypeStruct((n, ns), F32)],
        scratch_shapes=[pltpu.VMEM((tm, d), BF16)],
        compiler_params=_cparams(("parallel", "arbitrary")),
        name="inproj",
    )(x2, g, w_main, w_small)


def _cast_weight_slices(w_refs):
    for src, dst in zip(w_refs[:3], w_refs[3:]):
        dst[...] = src[...].astype(BF16)


def _outproj_out(rest):
    if len(rest) == 1:
        return rest[0]
    _cast_weight_slices(rest[:3] + rest[4:])
    return rest[3]


def _outproj_even_body(a_ref, b_ref, wa_ref, wb_ref, g_ref, x_ref, *rest):
    o_ref = _outproj_out(rest)
    mix = (jnp.dot(a_ref[...], wa_ref[...], preferred_element_type=F32)
           + jnp.dot(b_ref[...], wb_ref[...], preferred_element_type=F32))
    o_ref[...] = x_ref[...] + _rms(mix, g_ref[...])


def _outproj_odd_body(y_ref, z_ref, ng_ref, b_ref, wa_ref, wb_ref, g_ref, x_ref, *rest):
    o_ref = _outproj_out(rest)
    y = y_ref[...].astype(F32) * _silu(z_ref[...].astype(F32))
    a = _rms(y, ng_ref[...]).astype(BF16)
    mix = (jnp.dot(a, wa_ref[...], preferred_element_type=F32)
           + jnp.dot(b_ref[...], wb_ref[...], preferred_element_type=F32))
    o_ref[...] = x_ref[...] + _rms(mix, g_ref[...])


def _outproj(x2, a, b, w_out, li, g, ffn_w=None, layer=None, z_src=None, ssd_gain=None, tm=512):
    n, d = x2.shape
    half = a.shape[1]
    steps = n // tm
    row = lambda i: (i, 0)
    fixed = lambda i: (0, 0)
    w_specs = [pl.BlockSpec((None, half, d), lambda i: (li, 0, 0)),
               pl.BlockSpec((None, half, d), lambda i: (li, 1, 0))]
    tail_specs = [pl.BlockSpec((1, d), fixed), pl.BlockSpec((tm, d), row)]
    ffn_w = ffn_w or ()
    cast_in = [pl.BlockSpec((None, w.shape[1] // steps, w.shape[2]), lambda i: (layer, i, 0)) for w in ffn_w]
    cast_out = [pl.BlockSpec((w.shape[1] // steps, w.shape[2]), row) for w in ffn_w]
    cast_shapes = [jax.ShapeDtypeStruct(w.shape[1:], BF16) for w in ffn_w]
    if z_src is None:
        body = _outproj_even_body
        in_specs = [pl.BlockSpec((tm, half), row), pl.BlockSpec((tm, half), row)] + w_specs + tail_specs
        args = (a, b, w_out, w_out, g, x2)
    else:
        body = _outproj_odd_body
        in_specs = ([pl.BlockSpec((tm, half), row), pl.BlockSpec((tm, half), row),
                     pl.BlockSpec((1, half), fixed), pl.BlockSpec((tm, half), row)]
                    + w_specs + tail_specs)
        args = (a, z_src, ssd_gain, b, w_out, w_out, g, x2)
    return pl.pallas_call(
        body,
        grid=(steps,),
        in_specs=in_specs + cast_in,
        out_specs=[pl.BlockSpec((tm, d), row)] + cast_out,
        out_shape=[jax.ShapeDtypeStruct((n, d), F32)] + cast_shapes,
        compiler_params=_cparams(("parallel",)),
        name="outproj",
    )(*args, *ffn_w)


def _ffn_body(x_ref, g2_ref, wg_ref, wu_ref, wd_ref, g3_ref, *rest):
    if len(rest) == 2:
        o_ref, h_scr = rest
    else:
        o_ref, h_scr = rest[3], rest[7]
        _cast_weight_slices(rest[:3] + rest[4:7])
    k = pl.program_id(1)
    last = pl.num_programs(1) - 1
    half = x_ref.shape[0] // 2

    def pieces(r0):
        return [pl.ds(r0 + p, FFN_NORM_ROWS) for p in range(0, half, FFN_NORM_ROWS)]

    def partial_down(rs):
        h = h_scr[rs, :]
        gate = jnp.dot(h, wg_ref[...], preferred_element_type=F32)
        up = jnp.dot(h, wu_ref[...], preferred_element_type=F32)
        return jnp.dot((_silu(gate) * up).astype(BF16), wd_ref[...], preferred_element_type=F32)

    @pl.when(k == 0)
    def _():
        for r0 in (0, half):
            for ps in pieces(r0):
                h_scr[ps, :] = _rms(x_ref[ps, :], g2_ref[...]).astype(BF16)
            o_ref[pl.ds(r0, half), :] = partial_down(pl.ds(r0, half))

    @pl.when(jnp.logical_and(k > 0, k < last))
    def _():
        for r0 in (0, half):
            o_ref[pl.ds(r0, half), :] += partial_down(pl.ds(r0, half))

    @pl.when(k == last)
    def _():
        for r0 in (0, half):
            o_ref[pl.ds(r0, half), :] += partial_down(pl.ds(r0, half))
            for ps in pieces(r0):
                o_ref[ps, :] = x_ref[ps, :] + _rms(o_ref[ps, :], g3_ref[...])


def _ffn(x2, g2, wg, wu, wd, g3, ffn_w=None, next_layer=None, tm=1024, th=512):
    n, d = x2.shape
    hid = wd.shape[0]
    ni, nk = n // tm, hid // th
    in_specs = [
        pl.BlockSpec((tm, d), lambda i, k: (i, 0)),
        pl.BlockSpec((1, d), lambda i, k: (0, 0)),
        pl.BlockSpec((d, th), lambda i, k: (0, k)),
        pl.BlockSpec((d, th), lambda i, k: (0, k)),
        pl.BlockSpec((th, d), lambda i, k: (k, 0)),
        pl.BlockSpec((1, d), lambda i, k: (0, 0)),
    ]
    out_specs = [pl.BlockSpec((tm, d), lambda i, k: (i, 0))]
    out_shape = [jax.ShapeDtypeStruct((n, d), F32)]
    args = [x2, g2, wg, wu, wd, g3]
    if ffn_w is not None:
        up_blk, down_blk = (d // ni, hid // nk), (hid // nk, d // ni)
        in_specs += [pl.BlockSpec((None,) + up_blk, lambda i, k: (next_layer, i, k)),
                     pl.BlockSpec((None,) + up_blk, lambda i, k: (next_layer, i, k)),
                     pl.BlockSpec((None,) + down_blk, lambda i, k: (next_layer, k, i))]
        out_specs += [pl.BlockSpec(up_blk, lambda i, k: (i, k)),
                      pl.BlockSpec(up_blk, lambda i, k: (i, k)),
                      pl.BlockSpec(down_blk, lambda i, k: (k, i))]
        out_shape += [jax.ShapeDtypeStruct(w.shape[1:], BF16) for w in ffn_w]
        args += list(ffn_w)
    return pl.pallas_call(
        _ffn_body,
        grid=(ni, nk),
        in_specs=in_specs,
        out_specs=out_specs,
        out_shape=out_shape,
        scratch_shapes=[pltpu.VMEM((tm, d), BF16)],
        compiler_params=_cparams(("parallel", "arbitrary"), FFN_VMEM_LIMIT),
        name="ffn",
    )(*args)


def _gla_body(q_ref, k_ref, v_ref, g_ref, lr_ref, wd_ref, bd_ref, gn_ref, o_ref,
              bc_scr, o_scr, st_scr, *, seq):
    c = GLA_CHUNK
    nc = seq // c
    half = nc // 2
    nsub = c // GLA_SUB
    rb = min(512, seq)
    row_i = lax.broadcasted_iota(jnp.int32, (c, c), 0)
    col_i = lax.broadcasted_iota(jnp.int32, (c, c), 1)
    tri = (row_i >= col_i, row_i <= col_i)
    lane = lax.broadcasted_iota(jnp.int32, (1, LANES), 1)
    hmask = (lane < GLA_DK, lane >= GLA_DK)
    own_head = ((lax.broadcasted_iota(jnp.int32, (2 * GLA_DV, LANES), 0) < GLA_DV)
                == (lax.broadcasted_iota(jnp.int32, (2 * GLA_DV, LANES), 1) < GLA_DK))

    w2 = jnp.concatenate([wd_ref[0], wd_ref[1]], axis=1)
    w_hi = w2.astype(BF16)
    w_lo = (w2 - w_hi.astype(F32)).astype(BF16)
    w_cat = jnp.concatenate([w_hi, w_lo, w_hi], axis=0)
    tb = GLA_TRI_BLOCK
    brow = lax.broadcasted_iota(jnp.int32, (tb, tb), 0)
    bcol = lax.broadcasted_iota(jnp.int32, (tb, tb), 1)
    same_chunk = (brow // c) == (bcol // c)
    blk_tri = ((same_chunk & (brow >= bcol)).astype(BF16), (same_chunk & (brow <= bcol)).astype(BF16))

    def pre(i, carry):
        r0 = pl.multiple_of(i * rb, rb)
        lr = lr_ref[0, pl.ds(r0, rb), :]
        lr_hi = lr.astype(BF16)
        lr_lo = (lr - lr_hi.astype(F32)).astype(BF16)
        z2 = jnp.dot(jnp.concatenate([lr_hi, lr_hi, lr_lo], axis=1), w_cat, preferred_element_type=F32)
        las = []
        for d in range(2):
            z = z2[:, LANES * d:LANES * (d + 1)] + bd_ref[d:d + 1, :]
            la = (jnp.minimum(z, 0.0) - jnp.log(1.0 + jnp.exp(-jnp.abs(z)))) * (1.0 / GLA_TAU)
            la_hi = la.astype(BF16)
            las.append(jnp.concatenate([la_hi, (la - la_hi.astype(F32)).astype(BF16)], axis=1))
        for d in range(2):
            for t in range(rb // tb):
                s2 = jnp.dot(blk_tri[d], las[d][t * tb:(t + 1) * tb], preferred_element_type=F32)
                bc_scr[d, pl.ds(r0 + t * tb, tb), :] = s2[:, :LANES] + s2[:, LANES:]
        return carry

    lax.fori_loop(0, seq // rb, pre, 0, unroll=4)

    def stage1(n, d):
        rows = pl.ds(pl.multiple_of(n * c, c), c)
        bc = bc_scr[d, rows, :]
        q = q_ref[0, rows, :].astype(F32) * (GLA_DK ** -0.5)
        k = k_ref[0, rows, :].astype(F32)
        v = v_ref[0, rows, :]
        b_edge = bc[c - 1:c] if d == 0 else bc[0:1]
        qhat = q * jnp.exp(bc)
        khat = (k * jnp.exp(b_edge - bc)).astype(BF16)
        att_rows = ([], [])
        for i in range(nsub):
            lo = GLA_SUB * i
            hi = lo + GLA_SUB
            if d == 0:
                ref = bc[lo - 1:lo] if i > 0 else jnp.zeros((1, LANES), F32)
            else:
                ref = bc[hi:hi + 1] if i < nsub - 1 else jnp.zeros((1, LANES), F32)
            qi = q[lo:hi] * jnp.exp(bc[lo:hi] - ref)
            ki = (k * jnp.exp(jnp.minimum(ref - bc, GLA_EXP_CLAMP))).astype(BF16)
            q2 = jnp.concatenate([jnp.where(hmask[0], qi, 0.0), jnp.where(hmask[1], qi, 0.0)], axis=0)
            a2 = _dot_t(q2.astype(BF16), ki)
            att_rows[0].append(a2[:GLA_SUB])
            att_rows[1].append(a2[GLA_SUB:])
        kv2 = lax.dot_general(v, khat, (((0,), (0,)), ((), ())), preferred_element_type=F32)
        kv2 = jnp.where(own_head, kv2, 0.0)
        att2 = jnp.concatenate(
            [jnp.where(tri[d], jnp.concatenate(att_rows[hh], axis=0), 0.0).astype(BF16)
             for hh in range(2)], axis=1)
        zero_v = jnp.zeros((c, GLA_DV), BF16)
        v_blockdiag = jnp.concatenate(
            [jnp.concatenate([v[:, :GLA_DV], zero_v], axis=1),
             jnp.concatenate([zero_v, v[:, GLA_DV:]], axis=1)], axis=0)
        return d, att2, v_blockdiag, kv2, qhat.astype(BF16), jnp.exp(b_edge)

    def stage2(s1, states):
        d, att2, v_blockdiag, kv2, qhat, decay = s1
        st = states[d]
        o = jnp.dot(att2, v_blockdiag, preferred_element_type=F32) + _dot_t(qhat, st.astype(BF16))
        states[d] = st * decay + kv2
        return o

    def finish(n, o):
        rows = pl.ds(pl.multiple_of(n * c, c), c)
        parts = []
        for hh in range(2):
            oh = o[:, GLA_DV * hh:GLA_DV * (hh + 1)]
            parts.append(oh * lax.rsqrt(jnp.mean(oh * oh, axis=-1, keepdims=True) + EPS))
        o = jnp.concatenate(parts, axis=1) * gn_ref[...]
        o_ref[0, rows, :] = (o * _silu(g_ref[0, rows, :].astype(F32))).astype(o_ref.dtype)

    st_scr[...] = jnp.zeros_like(st_scr)

    cpi = next(u for u in (4, 2, 1) if half % u == 0)

    def walk(i, base_f, base_b):
        work = []
        for u in range(cpi):
            work.append((base_f + i * cpi + u, 0))
            work.append((base_b - i * cpi - u, 1))
        prepared = [stage1(n, d) for n, d in work]
        states = [st_scr[d] for d in range(2)]
        outs = [(n, stage2(s1, states)) for (n, _), s1 in zip(work, prepared)]
        for d in range(2):
            st_scr[d] = states[d]
        return outs

    def first(i, carry):
        for n, o in walk(i, 0, nc - 1):
            o_scr[pl.ds(pl.multiple_of(n * c, c), c), :] = o
        return carry

    lax.fori_loop(0, half // cpi, first, 0, unroll=2)

    def second(i, carry):
        for n, o in walk(i, half, half - 1):
            finish(n, o_scr[pl.ds(pl.multiple_of(n * c, c), c), :] + o)
        return carry

    lax.fori_loop(0, half // cpi, second, 0, unroll=2)


def _gla(proj3, small3, wdec_pad, bdec, gnorm):
    bn, seq, _ = proj3.shape
    npair = GLA_HEADS // 2
    qoff = 0
    koff = (GLA_HEADS * GLA_DK) // LANES
    voff = (2 * GLA_HEADS * GLA_DK) // (2 * GLA_DV)
    goff = voff + npair
    return pl.pallas_call(
        functools.partial(_gla_body, seq=seq),
        grid=(bn, npair),
        in_specs=[
            pl.BlockSpec((1, seq, LANES), lambda b, p: (b, 0, qoff + p)),
            pl.BlockSpec((1, seq, LANES), lambda b, p: (b, 0, koff + p)),
            pl.BlockSpec((1, seq, 2 * GLA_DV), lambda b, p: (b, 0, voff + p)),
            pl.BlockSpec((1, seq, 2 * GLA_DV), lambda b, p: (b, 0, goff + p)),
            pl.BlockSpec((1, seq, LANES), lambda b, p: (b, 0, 0)),
            pl.BlockSpec((2, LANES, LANES), lambda b, p: (0, 0, p)),
            pl.BlockSpec((2, LANES), lambda b, p: (0, p)),
            pl.BlockSpec((1, 2 * GLA_DV), lambda b, p: (0, p)),
        ],
        out_specs=pl.BlockSpec((1, seq, 2 * GLA_DV), lambda b, p: (b, 0, p)),
        out_shape=jax.ShapeDtypeStruct((bn, seq, GLA_HEADS * GLA_DV), BF16),
        scratch_shapes=[
            pltpu.VMEM((2, seq, LANES), F32),
            pltpu.VMEM((seq, 2 * GLA_DV), F32),
            pltpu.VMEM((2, 2 * GLA_DV, LANES), F32),
        ],
        compiler_params=_cparams(("parallel", "parallel")),
        name="gla",
    )(proj3, proj3, proj3, proj3, small3, wdec_pad, bdec, gnorm)


def _na_bias_table(rpb):
    nh, ndr, ndc = rpb.shape
    cols = np.arange(GRID_W)
    cs = np.clip(cols - NA_KW // 2, 0, GRID_W - NA_KW)
    valid = (cols[None, :] >= cs[:, None]) & (cols[None, :] < cs[:, None] + NA_KW)
    dc = cols[None, :] - cols[:, None] + NA_KW - 1
    onehot = (np.arange(ndc)[:, None, None] == dc[None]) & valid[None]
    toep = jnp.dot(rpb.reshape(nh * ndr, ndc).astype(F32),
                   jnp.asarray(onehot.reshape(ndc, -1), F32), precision=HIGHEST)
    toep = jnp.where(jnp.asarray(valid.reshape(1, -1)), toep, NEG_BIG)
    toep = toep.reshape(nh, ndr, GRID_W, GRID_W)
    t = jnp.stack([toep[:, off:off + NA_KH] for off in range(NA_KH)], axis=1)
    return t.transpose(0, 1, 3, 2, 4).reshape(nh, NA_KH, GRID_W, NA_KH * GRID_W)


def _na_body(q_ref, k_ref, v_ref, bias_ref, o_ref, *, nrows):
    w = GRID_W
    nk = NA_KH * w

    def group(i, carry):
        idx = []
        scores = []
        for u in range(NA_ROWS_PER_ITER):
            r = i * NA_ROWS_PER_ITER + u
            rs = jnp.clip(r - NA_KH // 2, 0, nrows - NA_KH)
            qrows = pl.ds(pl.multiple_of(r * w, w), w)
            krows = pl.ds(pl.multiple_of(rs * w, w), nk)
            idx.append((qrows, krows, rs - r + (NA_KH - 1)))
            scores.append(_dot_t(q_ref[0, qrows, :], k_ref[0, krows, :]))
        probs = []
        for (qrows, krows, off), s in zip(idx, scores):
            s = s * (NA_DH ** -0.5) + bias_ref[0, off]
            p = jnp.exp(s - jnp.max(s, axis=-1, keepdims=True))
            probs.append((p.astype(BF16), jnp.sum(p, axis=-1, keepdims=True)))
        for (qrows, krows, off), (p, l) in zip(idx, probs):
            o = jnp.dot(p, v_ref[0, krows, :], preferred_element_type=F32) / l
            o_ref[0, qrows, :] = o.astype(o_ref.dtype)
        return carry

    lax.fori_loop(0, nrows // NA_ROWS_PER_ITER, group, 0, unroll=2)


def _na(proj3, bias_tab, col0):
    bn, seq, _ = proj3.shape
    nrows = seq // GRID_W
    qoff = col0 // NA_DH
    koff = qoff + NA_HEADS
    voff = koff + NA_HEADS
    return pl.pallas_call(
        functools.partial(_na_body, nrows=nrows),
        grid=(bn, NA_HEADS),
        in_specs=[
            pl.BlockSpec((1, seq, NA_DH), lambda b, h: (b, 0, qoff + h)),
            pl.BlockSpec((1, seq, NA_DH), lambda b, h: (b, 0, koff + h)),
            pl.BlockSpec((1, seq, NA_DH), lambda b, h: (b, 0, voff + h)),
            pl.BlockSpec((1, NA_KH, GRID_W, NA_KH * GRID_W), lambda b, h: (h, 0, 0, 0)),
        ],
        out_specs=pl.BlockSpec((1, seq, NA_DH), lambda b, h: (b, 0, h)),
        out_shape=jax.ShapeDtypeStruct((bn, seq, NA_HEADS * NA_DH), BF16),
        compiler_params=_cparams(("parallel", "parallel")),
        name="na",
    )(proj3, proj3, proj3, bias_tab)


CONV_PAD = 16


def _conv_body(x_ref, w_ref, b_ref, o_ref, xp_scr, *, seq):
    rb = min(256, seq)
    ch = x_ref.shape[-1]
    zeros = jnp.zeros((CONV_PAD, ch), xp_scr.dtype)
    xp_scr[0:CONV_PAD, :] = zeros
    xp_scr[seq + CONV_PAD:seq + 2 * CONV_PAD, :] = zeros
    xp_scr[CONV_PAD:seq + CONV_PAD, :] = x_ref[0]

    def blk(i, carry):
        r0 = pl.multiple_of(i * rb, rb)
        xw = xp_scr[pl.ds(r0, rb + 2 * CONV_PAD), :].astype(F32)
        acc = jnp.zeros((rb, ch), F32) + b_ref[...]
        for j in range(SSD_CONV):
            s0 = CONV_PAD - SSD_CONV // 2 + j
            acc = acc + xw[s0:s0 + rb] * w_ref[j:j + 1, :]
        o_ref[0, pl.ds(r0, rb), :] = _silu(acc).astype(o_ref.dtype)
        return carry

    lax.fori_loop(0, seq // rb, blk, 0)


def _conv(proj3, conv_w, conv_b, col0, tc=512):
    bn, seq, _ = proj3.shape
    cdim = conv_w.shape[1]
    c0 = col0 // tc
    return pl.pallas_call(
        functools.partial(_conv_body, seq=seq),
        grid=(bn, cdim // tc),
        in_specs=[
            pl.BlockSpec((1, seq, tc), lambda b, j: (b, 0, c0 + j)),
            pl.BlockSpec((SSD_CONV, tc), lambda b, j: (0, j)),
            pl.BlockSpec((1, tc), lambda b, j: (0, j)),
        ],
        out_specs=pl.BlockSpec((1, seq, tc), lambda b, j: (b, 0, j)),
        out_shape=jax.ShapeDtypeStruct((bn, seq, cdim), BF16),
        scratch_shapes=[pltpu.VMEM((seq + 2 * CONV_PAD, tc), BF16)],
        compiler_params=_cparams(("parallel", "parallel")),
        name="ssd_conv",
    )(proj3, conv_w, conv_b)


def _softplus(x):
    return jnp.maximum(x, 0.0) + jnp.log(1.0 + jnp.exp(-jnp.abs(x)))


def _ssd_body(xs_ref, bm_ref, cm_ref, dt_ref, pb_ref, al_ref, dsk_ref, ex_ref, o_ref,
              st_scr, run_scr, dec_scr, *, seq):
    l = SSD_CHUNK
    nc = seq // l
    nh2 = 2 * SSD_JH
    row_i = lax.broadcasted_iota(jnp.int32, (l, l), 0)
    col_i = lax.broadcasted_iota(jnp.int32, (l, l), 1)
    lane = lax.broadcasted_iota(jnp.int32, (1, LANES), 1)
    first_head = lane < SSD_HEADDIM
    bwd_row = lax.broadcasted_iota(jnp.int32, (nh2, 1), 0) >= SSD_JH
    a_col = -jnp.exp(al_ref[0])

    lower = row_i > col_i
    diag = row_i == col_i
    triu_b = (row_i <= col_i).astype(BF16)

    def decays(n):
        rows = pl.ds(pl.multiple_of(n * l, l), l)
        dtv = _softplus(dt_ref[0, rows, :].T[0:nh2] + pb_ref[0])
        da = dtv * a_col
        hi = da.astype(BF16)
        r1 = da - hi.astype(F32)
        mid = r1.astype(BF16)
        lo = (r1 - mid.astype(F32)).astype(BF16)
        p3 = jnp.dot(jnp.concatenate([hi, mid, lo], axis=0), triu_b, preferred_element_type=F32)
        pre = p3[0:nh2] + p3[nh2:2 * nh2] + p3[2 * nh2:]
        total = pre[:, l - 1:l]
        acum = jnp.where(bwd_row, total - pre + da, pre)
        dec_scr[0, n] = dtv
        dec_scr[1, n] = acum
        dec_scr[2, n] = dtv * jnp.exp(total - acum)
        dec_scr[3, n] = jnp.broadcast_to(jnp.exp(total), (nh2, l))
        dec_scr[4, n] = acum - jnp.log(dtv)

    def chunk_state(n, d):
        rows = pl.ds(pl.multiple_of(n * l, l), l)
        w_state_t = dec_scr[2, n]
        x = xs_ref[0, rows, :]
        bmt = bm_ref[0, rows, :].astype(F32).T
        edge_ch = jnp.sum(dec_scr[3, n][:, 0:1] * ex_ref[d], axis=0, keepdims=True)
        new_st = []
        ns = SSD_STATE
        for jj in range(SSD_JH // 2):
            xp = x[:, LANES * jj:LANES * (jj + 1)]
            bw2 = jnp.concatenate(
                [(bmt * w_state_t[d * SSD_JH + 2 * jj + hh:d * SSD_JH + 2 * jj + hh + 1, :]).astype(BF16)
                 for hh in range(2)], axis=0)
            s2 = jnp.dot(bw2, xp, preferred_element_type=F32)
            new_st.append(jnp.where(first_head, s2[:ns], s2[ns:]))
        st = run_scr[d]
        st_scr[d, n] = st.astype(BF16)
        run_scr[d] = st * edge_ch + jnp.concatenate(new_st, axis=1)

    def chunk_out(n):
        rows = pl.ds(pl.multiple_of(n * l, l), l)
        dtv_t = dec_scr[0, n]
        acum = dec_scr[1, n].T
        key_t = dec_scr[4, n]
        x = xs_ref[0, rows, :]
        cm = cm_ref[0, rows, :]
        cmf = cm.astype(F32)
        cb = _dot_t(cm, bm_ref[0, rows, :])
        st_f = st_scr[0, n]
        st_b = st_scr[1, n]
        rhs = [jnp.concatenate([x[:, LANES * jj:LANES * (jj + 1)],
                                st_f[:, LANES * jj:LANES * (jj + 1)],
                                st_b[:, LANES * jj:LANES * (jj + 1)]], axis=0)
               for jj in range(SSD_JH // 2)]
        slabs = [slice(r0, r0 + SSD_ROW_SLAB) for r0 in range(0, l, SSD_ROW_SLAB)]
        ys = [[] for _ in slabs]
        for jj in range(SSD_JH // 2):
            lhs = []
            for rs in slabs:
                for hh in range(2):
                    cf = 2 * jj + hh
                    cr = SSD_JH + cf
                    col_f = jnp.broadcast_to(acum[rs, cf:cf + 1], (SSD_ROW_SLAB, l))
                    col_r = jnp.broadcast_to(acum[rs, cr:cr + 1], (SSD_ROW_SLAB, l))
                    seg = jnp.where(lower[rs], col_f - key_t[cf:cf + 1, :], col_r - key_t[cr:cr + 1, :])
                    m = cb[rs] * (jnp.exp(seg) + jnp.where(diag[rs], dtv_t[cf:cf + 1, :], 0.0))
                    lhs.append(jnp.concatenate([m.astype(BF16),
                                                (cmf[rs] * jnp.exp(col_f)).astype(BF16),
                                                (cmf[rs] * jnp.exp(col_r)).astype(BF16)], axis=1))
            y4 = jnp.dot(jnp.concatenate(lhs, axis=0), rhs[jj], preferred_element_type=F32)
            for si in range(len(slabs)):
                r0 = 2 * si * SSD_ROW_SLAB
                ys[si].append(jnp.where(first_head, y4[r0:r0 + SSD_ROW_SLAB],
                                        y4[r0 + SSD_ROW_SLAB:r0 + 2 * SSD_ROW_SLAB]))
        for rs, y_parts in zip(slabs, ys):
            y = jnp.concatenate(y_parts, axis=1) + x[rs].astype(F32) * dsk_ref[...]
            o_ref[0, pl.ds(pl.multiple_of(n * l, l) + rs.start, SSD_ROW_SLAB), :] = y.astype(o_ref.dtype)

    def prep(i, carry):
        decays(i)
        return carry

    lax.fori_loop(0, nc, prep, 0, unroll=8)
    run_scr[...] = jnp.zeros_like(run_scr)

    def states(i, carry):
        chunk_state(i, 0)
        chunk_state(nc - 1 - i, 1)
        return carry

    lax.fori_loop(0, nc, states, 0, unroll=4)

    def outputs(n, carry):
        chunk_out(n)
        return carry

    lax.fori_loop(0, nc, outputs, 0, unroll=4)


def _ssd(xbc_act, dt3, pb, al, dskip, expand):
    bn, seq, _ = xbc_act.shape
    gw = SSD_JH * SSD_HEADDIM
    boff = SSD_INNER // SSD_STATE
    coff = boff + SSD_GROUPS
    return pl.pallas_call(
        functools.partial(_ssd_body, seq=seq),
        grid=(bn, SSD_GROUPS),
        in_specs=[
            pl.BlockSpec((1, seq, gw), lambda b, g: (b, 0, g)),
            pl.BlockSpec((1, seq, SSD_STATE), lambda b, g: (b, 0, boff + g)),
            pl.BlockSpec((1, seq, SSD_STATE), lambda b, g: (b, 0, coff + g)),
            pl.BlockSpec((1, seq, LANES), lambda b, g: (b, 0, g)),
            pl.BlockSpec((1, 2 * SSD_JH, 1), lambda b, g: (g, 0, 0)),
            pl.BlockSpec((1, 2 * SSD_JH, 1), lambda b, g: (g, 0, 0)),
            pl.BlockSpec((1, gw), lambda b, g: (0, g)),
            pl.BlockSpec((2, 2 * SSD_JH, gw), lambda b, g: (0, 0, 0)),
        ],
        out_specs=pl.BlockSpec((1, seq, gw), lambda b, g: (b, 0, g)),
        out_shape=jax.ShapeDtypeStruct((bn, seq, SSD_INNER), BF16),
        scratch_shapes=[pltpu.VMEM((2, seq // SSD_CHUNK, SSD_STATE, gw), BF16),
                        pltpu.VMEM((2, SSD_STATE, gw), F32),
                        pltpu.VMEM((5, seq // SSD_CHUNK, 2 * SSD_JH, SSD_CHUNK), F32)],
        compiler_params=_cparams(("parallel", "parallel")),
        name="ssd_scan",
    )(xbc_act, xbc_act, xbc_act, dt3, pb, al, dskip, expand)


def _rope_tables(seq):
    half = SWA_DH // 2
    inv = ROPE_THETA ** (-np.arange(half, dtype=np.float64) / half)
    ang = np.arange(seq, dtype=np.float64)[:, None] * inv[None, :]
    cos, sin = np.cos(ang), np.sin(ang)
    return (jnp.asarray(np.concatenate([cos, cos], axis=1), F32),
            jnp.asarray(np.concatenate([-sin, sin], axis=1), F32))


def _rope(x, cos, sin_signed):
    return x * cos + pltpu.roll(x, SWA_DH // 2, 1) * sin_signed


def _swa_body(sink_ref, q_ref, k_ref, v_ref, cos_ref, sin_ref, o_ref, kr_scr, *, seq):
    wb = SWA_BLOCK
    nb = seq // wb
    nkeys = 3 * wb
    kvh = pl.program_id(1)
    rb = min(512, seq)

    def krope(i, carry):
        rows = pl.ds(pl.multiple_of(i * rb, rb), rb)
        kr_scr[rows, :] = _rope(k_ref[0, rows, :].astype(F32), cos_ref[rows, :],
                                sin_ref[rows, :]).astype(BF16)
        return carry

    lax.fori_loop(0, seq // rb, krope, 0, unroll=2)

    qpos_l = lax.broadcasted_iota(jnp.int32, (wb, nkeys), 0)
    kpos_l = lax.broadcasted_iota(jnp.int32, (wb, nkeys), 1)

    def blk(n, carry):
        rows = pl.ds(pl.multiple_of(n * wb, wb), wb)
        ks = pl.multiple_of(jnp.clip(n - 1, 0, nb - 3) * wb, wb)
        krows = pl.ds(ks, nkeys)
        cos = cos_ref[rows, :]
        sin = sin_ref[rows, :]
        kk = kr_scr[krows, :]
        vv = v_ref[0, krows, :]
        valid = jnp.abs((qpos_l + n * wb) - (kpos_l + ks)) <= SWA_WINDOW
        qb = q_ref[0, rows, :].astype(F32)
        q4 = jnp.concatenate(
            [(_rope(qb[:, SWA_DH * hh:SWA_DH * (hh + 1)], cos, sin) * (SWA_DH ** -0.5)).astype(BF16)
             for hh in range(SWA_GQ)], axis=0)
        s4 = _dot_t(q4, kk)
        probs = []
        dens = []
        for hh in range(SWA_GQ):
            sink = sink_ref[kvh * SWA_GQ + hh]
            s = jnp.where(valid, s4[wb * hh:wb * (hh + 1)], NEG_BIG)
            m = jnp.maximum(jnp.max(s, axis=-1, keepdims=True), sink)
            p = jnp.exp(s - m)
            probs.append(p.astype(BF16))
            dens.append(jnp.sum(p, axis=-1, keepdims=True) + jnp.exp(sink - m))
        o4 = jnp.dot(jnp.concatenate(probs, axis=0), vv, preferred_element_type=F32)
        for hh in range(SWA_GQ):
            o = o4[wb * hh:wb * (hh + 1)] / dens[hh]
            o_ref[0, rows, SWA_DH * hh:SWA_DH * (hh + 1)] = o.astype(o_ref.dtype)
        return carry

    lax.fori_loop(0, nb, blk, 0, unroll=4)


def _swa(proj3, sink, cos, sin, col0):
    bn, seq, _ = proj3.shape
    qw = SWA_GQ * SWA_DH
    qoff = col0 // qw
    koff = (col0 + SWA_HEADS * SWA_DH) // SWA_DH
    voff = koff + SWA_KV
    return pl.pallas_call(
        functools.partial(_swa_body, seq=seq),
        grid=(bn, SWA_KV),
        in_specs=[
            pl.BlockSpec(memory_space=pltpu.SMEM),
            pl.BlockSpec((1, seq, qw), lambda b, h: (b, 0, qoff + h)),
            pl.BlockSpec((1, seq, SWA_DH), lambda b, h: (b, 0, koff + h)),
            pl.BlockSpec((1, seq, SWA_DH), lambda b, h: (b, 0, voff + h)),
            pl.BlockSpec((seq, SWA_DH), lambda b, h: (0, 0)),
            pl.BlockSpec((seq, SWA_DH), lambda b, h: (0, 0)),
        ],
        out_specs=pl.BlockSpec((1, seq, qw), lambda b, h: (b, 0, h)),
        out_shape=jax.ShapeDtypeStruct((bn, seq, SWA_HEADS * SWA_DH), BF16),
        scratch_shapes=[pltpu.VMEM((seq, SWA_DH), BF16)],
        compiler_params=_cparams(("parallel", "parallel")),
        name="swa",
    )(sink, proj3, proj3, proj3, cos, sin)


def _pad_cols(w, width):
    return jnp.pad(w, ((0, 0), (0, width - w.shape[1])))


def _even_mixers(x2, bn, seq, g0, w_in, w_decay, b_decay, gla_norm, rpb):
    d = x2.shape[1]
    n_gla = 2 * GLA_HEADS * GLA_DK + 2 * GLA_HEADS * GLA_DV
    n_lr = 2 * GLA_LOWRANK
    w_main = jnp.concatenate([w_in[:, :n_gla], w_in[:, n_gla + n_lr:]], axis=1).astype(BF16)
    w_small = _pad_cols(w_in[:, n_gla:n_gla + n_lr], LANES).astype(BF16)
    proj, small = _inproj(x2, g0, w_main, w_small)
    proj3 = proj.reshape(bn, seq, -1)
    small3 = small.reshape(bn, seq, -1)
    wdec = jnp.zeros((2, LANES, w_decay.shape[-1]), F32)
    for dd in range(2):
        wdec = wdec.at[dd, dd * GLA_LOWRANK:(dd + 1) * GLA_LOWRANK].set(w_decay[dd])
    o_a = _gla(proj3, small3, wdec, b_decay, gla_norm[None, :])
    o_b = _na(proj3, _na_bias_table(rpb), n_gla)
    return o_a.reshape(-1, o_a.shape[-1]), o_b.reshape(-1, o_b.shape[-1])


def _per_group(v):
    return v.reshape(2, SSD_GROUPS, SSD_JH).transpose(1, 0, 2).reshape(SSD_GROUPS, 2 * SSD_JH)


def _odd_mixers(x2, bn, seq, g0, w_in, conv_w, conv_b, dt_bias, a_log, d_skip, sink, cos, sin):
    conv_dim = SSD_INNER + 2 * SSD_GROUPS * SSD_STATE
    n_z = SSD_INNER
    n_dt = 2 * SSD_HEADS
    c_dt = n_z + conv_dim
    w_main = jnp.concatenate([w_in[:, :c_dt], w_in[:, c_dt + n_dt:]], axis=1).astype(BF16)
    w_dt = w_in[:, c_dt:c_dt + n_dt].reshape(-1, 2, SSD_GROUPS, SSD_JH).transpose(0, 2, 1, 3)
    w_dt = w_dt.reshape(-1, SSD_GROUPS, 2 * SSD_JH)
    w_small = jnp.pad(w_dt, ((0, 0), (0, 0), (0, LANES - 2 * SSD_JH))).reshape(-1, SSD_GROUPS * LANES)
    proj, small = _inproj(x2, g0, w_main, w_small.astype(BF16))
    proj3 = proj.reshape(bn, seq, -1)
    dt3 = small.reshape(bn, seq, SSD_GROUPS * LANES)
    xbc_act = _conv(proj3, conv_w, conv_b[None, :], n_z)
    pbg = _per_group(dt_bias)
    alg = _per_group(a_log)
    head_of_ch = np.arange(SSD_JH * SSD_HEADDIM) // SSD_HEADDIM
    expand = jnp.asarray(np.arange(2 * SSD_JH)[None, :, None]
                         == (np.arange(2)[:, None, None] * SSD_JH + head_of_ch[None, None, :]), F32)
    dskip = jnp.repeat(d_skip.astype(F32), SSD_HEADDIM)[None, :]
    y_pre = _ssd(xbc_act, dt3, pbg[:, :, None], alg[:, :, None], dskip, expand)
    o_d = _swa(proj3, sink, cos, sin, c_dt)
    return y_pre.reshape(-1, SSD_INNER), proj, o_d.reshape(-1, o_d.shape[-1])


def kernel(x, norm_gains, ffn_w_gate, ffn_w_up, ffn_w_down, even_w_in, even_w_out, gla_w_decay,
           gla_b_decay, gla_norm, na_rpb, odd_w_in, odd_w_out, ssd_conv_w, ssd_conv_b, ssd_dt_bias,
           ssd_a_log, ssd_d, ssd_norm, swa_sink):
    bn, seq, d = x.shape
    depth = norm_gains.shape[0]
    x2 = x.reshape(bn * seq, d)
    cos, sin = _rope_tables(seq)
    ffn_w = (ffn_w_gate, ffn_w_up, ffn_w_down)
    even_w_out_b = even_w_out.astype(BF16)
    odd_w_out_b = odd_w_out.astype(BF16)
    w_bf16 = None
    for layer in range(depth):
        g = norm_gains[layer][:, None, :]
        i = layer // 2
        cast_here = dict(ffn_w=ffn_w, layer=0) if layer == 0 else {}
        cast_next = dict(ffn_w=ffn_w, next_layer=layer + 1) if layer + 1 < depth else {}
        if layer % 2 == 0:
            o_a, o_b = _even_mixers(x2, bn, seq, g[0], even_w_in[i], gla_w_decay[i], gla_b_decay[i],
                                    gla_norm[i], na_rpb[i])
            x2, *cast = _outproj(x2, o_a, o_b, even_w_out_b, i, g[1], **cast_here)
        else:
            y_pre, proj, o_d = _odd_mixers(x2, bn, seq, g[0], odd_w_in[i], ssd_conv_w[i], ssd_conv_b[i],
                                           ssd_dt_bias[i], ssd_a_log[i], ssd_d[i], swa_sink[i], cos, sin)
            x2, *cast = _outproj(x2, y_pre, o_d, odd_w_out_b, i, g[1], z_src=proj,
                                 ssd_gain=ssd_norm[i][None, :], **cast_here)
        w_bf16 = cast or w_bf16
        x2, *w_bf16 = _ffn(x2, g[2], *w_bf16, g[3], **cast_next)
    return x2.reshape(bn, seq, d)
```

```python
import functools

import jax
import jax.numpy as jnp
import numpy as np
from jax import lax
from jax.experimental import pallas as pl
from jax.experimental.pallas import tpu as pltpu

F32 = jnp.float32
BF16 = jnp.bfloat16
HIGHEST = lax.Precision.HIGHEST

EPS = 1e-6
ROPE_THETA = 10000.0
GRID_W = 64
LANES = 128

GLA_DK = 64
GLA_DV = 128
GLA_HEADS = 8
GLA_LOWRANK = 16
GLA_TAU = 16.0
GLA_CHUNK = 64
GLA_SUB = 16
GLA_EXP_CLAMP = 60.0
GLA_TRI_BLOCK = 256

NA_DH = 128
NA_HEADS = 8
NA_KH = 8
NA_KW = 16
NA_ROWS_PER_ITER = 16
NEG_BIG = -1e30

SSD_INNER = 1024
SSD_HEADDIM = 64
SSD_HEADS = 16
SSD_GROUPS = 2
SSD_STATE = 128
SSD_CONV = 5
SSD_CHUNK = 128
SSD_JH = SSD_HEADS // SSD_GROUPS
SSD_ROW_SLAB = 64

SWA_DH = 128
SWA_HEADS = 8
SWA_KV = 2
SWA_GQ = SWA_HEADS // SWA_KV
SWA_WINDOW = 128
SWA_BLOCK = 128

VMEM_LIMIT = 52 * 1024 * 1024


FFN_NORM_ROWS = 256
FFN_VMEM_LIMIT = 58 * 1024 * 1024


def _cparams(sem, vmem_limit=VMEM_LIMIT):
    return pltpu.CompilerParams(dimension_semantics=sem, vmem_limit_bytes=vmem_limit)


def _rms(x, g):
    return x * lax.rsqrt(jnp.mean(x * x, axis=-1, keepdims=True) + EPS) * g


def _silu(x):
    return x * (1.0 / (1.0 + jnp.exp(-x)))


def _dot_t(a, b):
    return lax.dot_general(a, b, (((1,), (1,)), ((), ())), preferred_element_type=F32)


def _inproj_body(x_ref, g_ref, w_ref, ws_ref, o_ref, os_ref, h_scr):
    j = pl.program_id(1)
    half = x_ref.shape[0] // 2

    @pl.when(j == 0)
    def _():
        for r0 in (0, half):
            rs = pl.ds(r0, half)
            h = _rms(x_ref[rs, :], g_ref[...]).astype(BF16)
            h_scr[rs, :] = h
            os_ref[rs, :] = jnp.dot(h, ws_ref[...], preferred_element_type=F32)
            o_ref[rs, :] = jnp.dot(h, w_ref[...], preferred_element_type=F32).astype(o_ref.dtype)

    @pl.when(j > 0)
    def _():
        o_ref[...] = jnp.dot(h_scr[...], w_ref[...], preferred_element_type=F32).astype(o_ref.dtype)


def _inproj(x2, g, w_main, w_small, tm=1024, tn=2048):
    n, d = x2.shape
    nm = w_main.shape[1]
    ns = w_small.shape[1]
    return pl.pallas_call(
        _inproj_body,
        grid=(n // tm, nm // tn),
        in_specs=[
            pl.BlockSpec((tm, d), lambda i, j: (i, 0)),
            pl.BlockSpec((1, d), lambda i, j: (0, 0)),
            pl.BlockSpec((d, tn), lambda i, j: (0, j)),
            pl.BlockSpec((d, ns), lambda i, j: (0, 0)),
        ],
        out_specs=[
            pl.BlockSpec((tm, tn), lambda i, j: (i, j)),
            pl.BlockSpec((tm, ns), lambda i, j: (i, 0)),
        ],
        out_shape=[jax.ShapeDtypeStruct((n, nm), BF16), jax.ShapeDtypeStruct((n, ns), F32)],
        scratch_shapes=[pltpu.VMEM((tm, d), BF16)],
        compiler_params=_cparams(("parallel", "arbitrary")),
        name="inproj",
    )(x2, g, w_main, w_small)


def _cast_weight_slices(w_refs):
    for src, dst in zip(w_refs[:3], w_refs[3:]):
        dst[...] = src[...].astype(BF16)


def _outproj_out(rest):
    if len(rest) == 1:
        return rest[0]
    _cast_weight_slices(rest[:3] + rest[4:])
    return rest[3]


def _outproj_even_body(a_ref, b_ref, wa_ref, wb_ref, g_ref, x_ref, *rest):
    o_ref = _outproj_out(rest)
    mix = (jnp.dot(a_ref[...], wa_ref[...], preferred_element_type=F32)
           + jnp.dot(b_ref[...], wb_ref[...], preferred_element_type=F32))
    o_ref[...] = x_ref[...] + _rms(mix, g_ref[...])


def _outproj_odd_body(y_ref, z_ref, ng_ref, b_ref, wa_ref, wb_ref, g_ref, x_ref, *rest):
    o_ref = _outproj_out(rest)
    y = y_ref[...].astype(F32) * _silu(z_ref[...].astype(F32))
    a = _rms(y, ng_ref[...]).astype(BF16)
    mix = (jnp.dot(a, wa_ref[...], preferred_element_type=F32)
           + jnp.dot(b_ref[...], wb_ref[...], preferred_element_type=F32))
    o_ref[...] = x_ref[...] + _rms(mix, g_ref[...])


def _outproj(x2, a, b, w_out, li, g, ffn_w=None, layer=None, z_src=None, ssd_gain=None, tm=512):
    n, d = x2.shape
    half = a.shape[1]
    steps = n // tm
    row = lambda i: (i, 0)
    fixed = lambda i: (0, 0)
    w_specs = [pl.BlockSpec((None, half, d), lambda i: (li, 0, 0)),
               pl.BlockSpec((None, half, d), lambda i: (li, 1, 0))]
    tail_specs = [pl.BlockSpec((1, d), fixed), pl.BlockSpec((tm, d), row)]
    ffn_w = ffn_w or ()
    cast_in = [pl.BlockSpec((None, w.shape[1] // steps, w.shape[2]), lambda i: (layer, i, 0)) for w in ffn_w]
    cast_out = [pl.BlockSpec((w.shape[1] // steps, w.shape[2]), row) for w in ffn_w]
    cast_shapes = [jax.ShapeDtypeStruct(w.shape[1:], BF16) for w in ffn_w]
    if z_src is None:
        body = _outproj_even_body
        in_specs = [pl.BlockSpec((tm, half), row), pl.BlockSpec((tm, half), row)] + w_specs + tail_specs
        args = (a, b, w_out, w_out, g, x2)
    else:
        body = _outproj_odd_body
        in_specs = ([pl.BlockSpec((tm, half), row), pl.BlockSpec((tm, half), row),
                     pl.BlockSpec((1, half), fixed), pl.BlockSpec((tm, half), row)]
                    + w_specs + tail_specs)
        args = (a, z_src, ssd_gain, b, w_out, w_out, g, x2)
    return pl.pallas_call(
        body,
        grid=(steps,),
        in_specs=in_specs + cast_in,
        out_specs=[pl.BlockSpec((tm, d), row)] + cast_out,
        out_shape=[jax.ShapeDtypeStruct((n, d), F32)] + cast_shapes,
        compiler_params=_cparams(("parallel",)),
        name="outproj",
    )(*args, *ffn_w)


def _ffn_body(x_ref, g2_ref, wg_ref, wu_ref, wd_ref, g3_ref, *rest):
    if len(rest) == 2:
        o_ref, h_scr = rest
    else:
        o_ref, h_scr = rest[3], rest[7]
        _cast_weight_slices(rest[:3] + rest[4:7])
    k = pl.program_id(1)
    last = pl.num_programs(1) - 1
    half = x_ref.shape[0] // 2

    def pieces(r0):
        return [pl.ds(r0 + p, FFN_NORM_ROWS) for p in range(0, half, FFN_NORM_ROWS)]

    def partial_down(rs):
        h = h_scr[rs, :]
        gate = jnp.dot(h, wg_ref[...], preferred_element_type=F32)
        up = jnp.dot(h, wu_ref[...], preferred_element_type=F32)
        return jnp.dot((_silu(gate) * up).astype(BF16), wd_ref[...], preferred_element_type=F32)

    @pl.when(k == 0)
    def _():
        for r0 in (0, half):
            for ps in pieces(r0):
                h_scr[ps, :] = _rms(x_ref[ps, :], g2_ref[...]).astype(BF16)
            o_ref[pl.ds(r0, half), :] = partial_down(pl.ds(r0, half))

    @pl.when(jnp.logical_and(k > 0, k < last))
    def _():
        for r0 in (0, half):
            o_ref[pl.ds(r0, half), :] += partial_down(pl.ds(r0, half))

    @pl.when(k == last)
    def _():
        for r0 in (0, half):
            o_ref[pl.ds(r0, half), :] += partial_down(pl.ds(r0, half))
            for ps in pieces(r0):
                o_ref[ps, :] = x_ref[ps, :] + _rms(o_ref[ps, :], g3_ref[...])


def _ffn(x2, g2, wg, wu, wd, g3, ffn_w=None, next_layer=None, tm=1024, th=512):
    n, d = x2.shape
    hid = wd.shape[0]
    ni, nk = n // tm, hid // th
    in_specs = [
        pl.BlockSpec((tm, d), lambda i, k: (i, 0)),
        pl.BlockSpec((1, d), lambda i, k: (0, 0)),
        pl.BlockSpec((d, th), lambda i, k: (0, k)),
        pl.BlockSpec((d, th), lambda i, k: (0, k)),
        pl.BlockSpec((th, d), lambda i, k: (k, 0)),
        pl.BlockSpec((1, d), lambda i, k: (0, 0)),
    ]
    out_specs = [pl.BlockSpec((tm, d), lambda i, k: (i, 0))]
    out_shape = [jax.ShapeDtypeStruct((n, d), F32)]
    args = [x2, g2, wg, wu, wd, g3]
    if ffn_w is not None:
        up_blk, down_blk = (d // ni, hid // nk), (hid // nk, d // ni)
        in_specs += [pl.BlockSpec((None,) + up_blk, lambda i, k: (next_layer, i, k)),
                     pl.BlockSpec((None,) + up_blk, lambda i, k: (next_layer, i, k)),
                     pl.BlockSpec((None,) + down_blk, lambda i, k: (next_layer, k, i))]
        out_specs += [pl.BlockSpec(up_blk, lambda i, k: (i, k)),
                      pl.BlockSpec(up_blk, lambda i, k: (i, k)),
                      pl.BlockSpec(down_blk, lambda i, k: (k, i))]
        out_shape += [jax.ShapeDtypeStruct(w.shape[1:], BF16) for w in ffn_w]
        args += list(ffn_w)
    return pl.pallas_call(
        _ffn_body,
        grid=(ni, nk),
        in_specs=in_specs,
        out_specs=out_specs,
        out_shape=out_shape,
        scratch_shapes=[pltpu.VMEM((tm, d), BF16)],
        compiler_params=_cparams(("parallel", "arbitrary"), FFN_VMEM_LIMIT),
        name="ffn",
    )(*args)


def _gla_body(q_ref, k_ref, v_ref, g_ref, lr_ref, wd_ref, bd_ref, gn_ref, o_ref,
              bc_scr, o_scr, st_scr, *, seq):
    c = GLA_CHUNK
    nc = seq // c
    half = nc // 2
    nsub = c // GLA_SUB
    rb = min(512, seq)
    row_i = lax.broadcasted_iota(jnp.int32, (c, c), 0)
    col_i = lax.broadcasted_iota(jnp.int32, (c, c), 1)
    tri = (row_i >= col_i, row_i <= col_i)
    lane = lax.broadcasted_iota(jnp.int32, (1, LANES), 1)
    hmask = (lane < GLA_DK, lane >= GLA_DK)
    own_head = ((lax.broadcasted_iota(jnp.int32, (2 * GLA_DV, LANES), 0) < GLA_DV)
                == (lax.broadcasted_iota(jnp.int32, (2 * GLA_DV, LANES), 1) < GLA_DK))

    w2 = jnp.concatenate([wd_ref[0], wd_ref[1]], axis=1)
    w_hi = w2.astype(BF16)
    w_lo = (w2 - w_hi.astype(F32)).astype(BF16)
    w_cat = jnp.concatenate([w_hi, w_lo, w_hi], axis=0)
    tb = GLA_TRI_BLOCK
    brow = lax.broadcasted_iota(jnp.int32, (tb, tb), 0)
    bcol = lax.broadcasted_iota(jnp.int32, (tb, tb), 1)
    same_chunk = (brow // c) == (bcol // c)
    blk_tri = ((same_chunk & (brow >= bcol)).astype(BF16), (same_chunk & (brow <= bcol)).astype(BF16))

    def pre(i, carry):
        r0 = pl.multiple_of(i * rb, rb)
        lr = lr_ref[0, pl.ds(r0, rb), :]
        lr_hi = lr.astype(BF16)
        lr_lo = (lr - lr_hi.astype(F32)).astype(BF16)
        z2 = jnp.dot(jnp.concatenate([lr_hi, lr_hi, lr_lo], axis=1), w_cat, preferred_element_type=F32)
        las = []
        for d in range(2):
            z = z2[:, LANES * d:LANES * (d + 1)] + bd_ref[d:d + 1, :]
            la = (jnp.minimum(z, 0.0) - jnp.log(1.0 + jnp.exp(-jnp.abs(z)))) * (1.0 / GLA_TAU)
            la_hi = la.astype(BF16)
            las.append(jnp.concatenate([la_hi, (la - la_hi.astype(F32)).astype(BF16)], axis=1))
        for d in range(2):
            for t in range(rb // tb):
                s2 = jnp.dot(blk_tri[d], las[d][t * tb:(t + 1) * tb], preferred_element_type=F32)
                bc_scr[d, pl.ds(r0 + t * tb, tb), :] = s2[:, :LANES] + s2[:, LANES:]
        return carry

    lax.fori_loop(0, seq // rb, pre, 0, unroll=4)

    def stage1(n, d):
        rows = pl.ds(pl.multiple_of(n * c, c), c)
        bc = bc_scr[d, rows, :]
        q = q_ref[0, rows, :].astype(F32) * (GLA_DK ** -0.5)
        k = k_ref[0, rows, :].astype(F32)
        v = v_ref[0, rows, :]
        b_edge = bc[c - 1:c] if d == 0 else bc[0:1]
        qhat = q * jnp.exp(bc)
        khat = (k * jnp.exp(b_edge - bc)).astype(BF16)
        att_rows = ([], [])
        for i in range(nsub):
            lo = GLA_SUB * i
            hi = lo + GLA_SUB
            if d == 0:
                ref = bc[lo - 1:lo] if i > 0 else jnp.zeros((1, LANES), F32)
            else:
                ref = bc[hi:hi + 1] if i < nsub - 1 else jnp.zeros((1, LANES), F32)
            qi = q[lo:hi] * jnp.exp(bc[lo:hi] - ref)
            ks = slice(0, hi) if d == 0 else slice(lo, c)
            ki = (k[ks] * jnp.exp(jnp.minimum(ref - bc[ks], GLA_EXP_CLAMP))).astype(BF16)
            q2 = jnp.concatenate([jnp.where(hmask[0], qi, 0.0), jnp.where(hmask[1], qi, 0.0)], axis=0)
            a2 = _dot_t(q2.astype(BF16), ki)
            n_unseen = c - (ks.stop - ks.start)
            if n_unseen:
                unseen = jnp.zeros((2 * GLA_SUB, n_unseen), F32)
                a2 = jnp.concatenate([a2, unseen] if d == 0 else [unseen, a2], axis=1)
            att_rows[0].append(a2[:GLA_SUB])
            att_rows[1].append(a2[GLA_SUB:])
        kv2 = lax.dot_general(v, khat, (((0,), (0,)), ((), ())), preferred_element_type=F32)
        kv2 = jnp.where(own_head, kv2, 0.0)
        att2 = jnp.concatenate(
            [jnp.where(tri[d], jnp.concatenate(att_rows[hh], axis=0), 0.0).astype(BF16)
             for hh in range(2)], axis=1)
        zero_v = jnp.zeros((c, GLA_DV), BF16)
        v_blockdiag = jnp.concatenate(
            [jnp.concatenate([v[:, :GLA_DV], zero_v], axis=1),
             jnp.concatenate([zero_v, v[:, GLA_DV:]], axis=1)], axis=0)
        return d, att2, v_blockdiag, kv2, qhat.astype(BF16), jnp.exp(b_edge)

    def stage2(s1, states):
        d, att2, v_blockdiag, kv2, qhat, decay = s1
        st = states[d]
        o = jnp.dot(att2, v_blockdiag, preferred_element_type=F32) + _dot_t(qhat, st.astype(BF16))
        states[d] = st * decay + kv2
        return o

    def finish(n, o):
        rows = pl.ds(pl.multiple_of(n * c, c), c)
        parts = []
        for hh in range(2):
            oh = o[:, GLA_DV * hh:GLA_DV * (hh + 1)]
            parts.append(oh * lax.rsqrt(jnp.mean(oh * oh, axis=-1, keepdims=True) + EPS))
        o = jnp.concatenate(parts, axis=1) * gn_ref[...]
        o_ref[0, rows, :] = (o * _silu(g_ref[0, rows, :].astype(F32))).astype(o_ref.dtype)

    st_scr[...] = jnp.zeros_like(st_scr)

    cpi = next(u for u in (4, 2, 1) if half % u == 0)

    def walk(i, base_f, base_b):
        work = []
        for u in range(cpi):
            work.append((base_f + i * cpi + u, 0))
            work.append((base_b - i * cpi - u, 1))
        prepared = [stage1(n, d) for n, d in work]
        states = [st_scr[d] for d in range(2)]
        outs = [(n, stage2(s1, states)) for (n, _), s1 in zip(work, prepared)]
        for d in range(2):
            st_scr[d] = states[d]
        return outs

    def first(i, carry):
        for n, o in walk(i, 0, nc - 1):
            o_scr[pl.ds(pl.multiple_of(n * c, c), c), :] = o
        return carry

    lax.fori_loop(0, half // cpi, first, 0, unroll=2)

    def second(i, carry):
        for n, o in walk(i, half, half - 1):
            finish(n, o_scr[pl.ds(pl.multiple_of(n * c, c), c), :] + o)
        return carry

    lax.fori_loop(0, half // cpi, second, 0, unroll=2)


def _gla(proj3, small3, wdec_pad, bdec, gnorm):
    bn, seq, _ = proj3.shape
    npair = GLA_HEADS // 2
    qoff = 0
    koff = (GLA_HEADS * GLA_DK) // LANES
    voff = (2 * GLA_HEADS * GLA_DK) // (2 * GLA_DV)
    goff = voff + npair
    return pl.pallas_call(
        functools.partial(_gla_body, seq=seq),
        grid=(bn, npair),
        in_specs=[
            pl.BlockSpec((1, seq, LANES), lambda b, p: (b, 0, qoff + p)),
            pl.BlockSpec((1, seq, LANES), lambda b, p: (b, 0, koff + p)),
            pl.BlockSpec((1, seq, 2 * GLA_DV), lambda b, p: (b, 0, voff + p)),
            pl.BlockSpec((1, seq, 2 * GLA_DV), lambda b, p: (b, 0, goff + p)),
            pl.BlockSpec((1, seq, LANES), lambda b, p: (b, 0, 0)),
            pl.BlockSpec((2, LANES, LANES), lambda b, p: (0, 0, p)),
            pl.BlockSpec((2, LANES), lambda b, p: (0, p)),
            pl.BlockSpec((1, 2 * GLA_DV), lambda b, p: (0, p)),
        ],
        out_specs=pl.BlockSpec((1, seq, 2 * GLA_DV), lambda b, p: (b, 0, p)),
        out_shape=jax.ShapeDtypeStruct((bn, seq, GLA_HEADS * GLA_DV), BF16),
        scratch_shapes=[
            pltpu.VMEM((2, seq, LANES), F32),
            pltpu.VMEM((seq, 2 * GLA_DV), F32),
            pltpu.VMEM((2, 2 * GLA_DV, LANES), F32),
        ],
        compiler_params=_cparams(("parallel", "parallel")),
        name="gla",
    )(proj3, proj3, proj3, proj3, small3, wdec_pad, bdec, gnorm)


def _na_bias_table(rpb):
    nh, ndr, ndc = rpb.shape
    cols = np.arange(GRID_W)
    cs = np.clip(cols - NA_KW // 2, 0, GRID_W - NA_KW)
    valid = (cols[None, :] >= cs[:, None]) & (cols[None, :] < cs[:, None] + NA_KW)
    dc = cols[None, :] - cols[:, None] + NA_KW - 1
    onehot = (np.arange(ndc)[:, None, None] == dc[None]) & valid[None]
    toep = jnp.dot(rpb.reshape(nh * ndr, ndc).astype(F32),
                   jnp.asarray(onehot.reshape(ndc, -1), F32), precision=HIGHEST)
    toep = jnp.where(jnp.asarray(valid.reshape(1, -1)), toep, NEG_BIG)
    toep = toep.reshape(nh, ndr, GRID_W, GRID_W)
    t = jnp.stack([toep[:, off:off + NA_KH] for off in range(NA_KH)], axis=1)
    return t.transpose(0, 1, 3, 2, 4).reshape(nh, NA_KH, GRID_W, NA_KH * GRID_W)


def _na_body(q_ref, k_ref, v_ref, bias_ref, o_ref, *, nrows):
    w = GRID_W
    nk = NA_KH * w

    def group(i, carry):
        idx = []
        scores = []
        for u in range(NA_ROWS_PER_ITER):
            r = i * NA_ROWS_PER_ITER + u
            rs = jnp.clip(r - NA_KH // 2, 0, nrows - NA_KH)
            qrows = pl.ds(pl.multiple_of(r * w, w), w)
            krows = pl.ds(pl.multiple_of(rs * w, w), nk)
            idx.append((qrows, krows, rs - r + (NA_KH - 1)))
            scores.append(_dot_t(q_ref[0, qrows, :], k_ref[0, krows, :]))
        probs = []
        for (qrows, krows, off), s in zip(idx, scores):
            s = s * (NA_DH ** -0.5) + bias_ref[0, off]
            p = jnp.exp(s - jnp.max(s, axis=-1, keepdims=True))
            probs.append((p.astype(BF16), jnp.sum(p, axis=-1, keepdims=True)))
        for (qrows, krows, off), (p, l) in zip(idx, probs):
            o = jnp.dot(p, v_ref[0, krows, :], preferred_element_type=F32) / l
            o_ref[0, qrows, :] = o.astype(o_ref.dtype)
        return carry

    lax.fori_loop(0, nrows // NA_ROWS_PER_ITER, group, 0, unroll=2)


def _na(proj3, bias_tab, col0):
    bn, seq, _ = proj3.shape
    nrows = seq // GRID_W
    qoff = col0 // NA_DH
    koff = qoff + NA_HEADS
    voff = koff + NA_HEADS
    return pl.pallas_call(
        functools.partial(_na_body, nrows=nrows),
        grid=(bn, NA_HEADS),
        in_specs=[
            pl.BlockSpec((1, seq, NA_DH), lambda b, h: (b, 0, qoff + h)),
            pl.BlockSpec((1, seq, NA_DH), lambda b, h: (b, 0, koff + h)),
            pl.BlockSpec((1, seq, NA_DH), lambda b, h: (b, 0, voff + h)),
            pl.BlockSpec((1, NA_KH, GRID_W, NA_KH * GRID_W), lambda b, h: (h, 0, 0, 0)),
        ],
        out_specs=pl.BlockSpec((1, seq, NA_DH), lambda b, h: (b, 0, h)),
        out_shape=jax.ShapeDtypeStruct((bn, seq, NA_HEADS * NA_DH), BF16),
        compiler_params=_cparams(("parallel", "parallel")),
        name="na",
    )(proj3, proj3, proj3, bias_tab)


CONV_PAD = 16


def _conv_body(x_ref, w_ref, b_ref, o_ref, xp_scr, *, seq):
    rb = min(256, seq)
    ch = x_ref.shape[-1]
    zeros = jnp.zeros((CONV_PAD, ch), xp_scr.dtype)
    xp_scr[0:CONV_PAD, :] = zeros
    xp_scr[seq + CONV_PAD:seq + 2 * CONV_PAD, :] = zeros
    xp_scr[CONV_PAD:seq + CONV_PAD, :] = x_ref[0]

    def blk(i, carry):
        r0 = pl.multiple_of(i * rb, rb)
        xw = xp_scr[pl.ds(r0, rb + 2 * CONV_PAD), :].astype(F32)
        acc = jnp.zeros((rb, ch), F32) + b_ref[...]
        for j in range(SSD_CONV):
            s0 = CONV_PAD - SSD_CONV // 2 + j
            acc = acc + xw[s0:s0 + rb] * w_ref[j:j + 1, :]
        o_ref[0, pl.ds(r0, rb), :] = _silu(acc).astype(o_ref.dtype)
        return carry

    lax.fori_loop(0, seq // rb, blk, 0)


def _conv(proj3, conv_w, conv_b, col0, tc=512):
    bn, seq, _ = proj3.shape
    cdim = conv_w.shape[1]
    c0 = col0 // tc
    return pl.pallas_call(
        functools.partial(_conv_body, seq=seq),
        grid=(bn, cdim // tc),
        in_specs=[
            pl.BlockSpec((1, seq, tc), lambda b, j: (b, 0, c0 + j)),
            pl.BlockSpec((SSD_CONV, tc), lambda b, j: (0, j)),
            pl.BlockSpec((1, tc), lambda b, j: (0, j)),
        ],
        out_specs=pl.BlockSpec((1, seq, tc), lambda b, j: (b, 0, j)),
        out_shape=jax.ShapeDtypeStruct((bn, seq, cdim), BF16),
        scratch_shapes=[pltpu.VMEM((seq + 2 * CONV_PAD, tc), BF16)],
        compiler_params=_cparams(("parallel", "parallel")),
        name="ssd_conv",
    )(proj3, conv_w, conv_b)


def _softplus(x):
    return jnp.maximum(x, 0.0) + jnp.log(1.0 + jnp.exp(-jnp.abs(x)))


def _ssd_body(xs_ref, bm_ref, cm_ref, dt_ref, pb_ref, al_ref, dsk_ref, ex_ref, o_ref,
              st_scr, run_scr, dec_scr, *, seq):
    l = SSD_CHUNK
    nc = seq // l
    nh2 = 2 * SSD_JH
    row_i = lax.broadcasted_iota(jnp.int32, (l, l), 0)
    col_i = lax.broadcasted_iota(jnp.int32, (l, l), 1)
    lane = lax.broadcasted_iota(jnp.int32, (1, LANES), 1)
    first_head = lane < SSD_HEADDIM
    bwd_row = lax.broadcasted_iota(jnp.int32, (nh2, 1), 0) >= SSD_JH
    a_col = -jnp.exp(al_ref[0])

    lower = row_i > col_i
    diag = row_i == col_i
    triu_b = (row_i <= col_i).astype(BF16)

    def decays(n):
        rows = pl.ds(pl.multiple_of(n * l, l), l)
        dtv = _softplus(dt_ref[0, rows, :].T[0:nh2] + pb_ref[0])
        da = dtv * a_col
        hi = da.astype(BF16)
        r1 = da - hi.astype(F32)
        mid = r1.astype(BF16)
        lo = (r1 - mid.astype(F32)).astype(BF16)
        p3 = jnp.dot(jnp.concatenate([hi, mid, lo], axis=0), triu_b, preferred_element_type=F32)
        pre = p3[0:nh2] + p3[nh2:2 * nh2] + p3[2 * nh2:]
        total = pre[:, l - 1:l]
        acum = jnp.where(bwd_row, total - pre + da, pre)
        dec_scr[0, n] = dtv
        dec_scr[1, n] = acum
        dec_scr[2, n] = dtv * jnp.exp(total - acum)
        dec_scr[3, n] = jnp.broadcast_to(jnp.exp(total), (nh2, l))
        dec_scr[4, n] = acum - jnp.log(dtv)

    def chunk_state(n, d):
        rows = pl.ds(pl.multiple_of(n * l, l), l)
        w_state_t = dec_scr[2, n]
        x = xs_ref[0, rows, :]
        bmt = bm_ref[0, rows, :].astype(F32).T
        edge_ch = jnp.sum(dec_scr[3, n][:, 0:1] * ex_ref[d], axis=0, keepdims=True)
        new_st = []
        ns = SSD_STATE
        for jj in range(SSD_JH // 2):
            xp = x[:, LANES * jj:LANES * (jj + 1)]
            bw2 = jnp.concatenate(
                [(bmt * w_state_t[d * SSD_JH + 2 * jj + hh:d * SSD_JH + 2 * jj + hh + 1, :]).astype(BF16)
                 for hh in range(2)], axis=0)
            s2 = jnp.dot(bw2, xp, preferred_element_type=F32)
            new_st.append(jnp.where(first_head, s2[:ns], s2[ns:]))
        st = run_scr[d]
        st_scr[d, n] = st.astype(BF16)
        run_scr[d] = st * edge_ch + jnp.concatenate(new_st, axis=1)

    def chunk_out(n):
        rows = pl.ds(pl.multiple_of(n * l, l), l)
        dtv_t = dec_scr[0, n]
        acum = dec_scr[1, n].T
        key_t = dec_scr[4, n]
        x = xs_ref[0, rows, :]
        cm = cm_ref[0, rows, :]
        cmf = cm.astype(F32)
        cb = _dot_t(cm, bm_ref[0, rows, :])
        st_f = st_scr[0, n]
        st_b = st_scr[1, n]
        rhs = [jnp.concatenate([x[:, LANES * jj:LANES * (jj + 1)],
                                st_f[:, LANES * jj:LANES * (jj + 1)],
                                st_b[:, LANES * jj:LANES * (jj + 1)]], axis=0)
               for jj in range(SSD_JH // 2)]
        slabs = [slice(r0, r0 + SSD_ROW_SLAB) for r0 in range(0, l, SSD_ROW_SLAB)]
        ys = [[] for _ in slabs]
        for jj in range(SSD_JH // 2):
            lhs = []
            for rs in slabs:
                for hh in range(2):
                    cf = 2 * jj + hh
                    cr = SSD_JH + cf
                    col_f = jnp.broadcast_to(acum[rs, cf:cf + 1], (SSD_ROW_SLAB, l))
                    col_r = jnp.broadcast_to(acum[rs, cr:cr + 1], (SSD_ROW_SLAB, l))
                    seg = jnp.where(lower[rs], col_f - key_t[cf:cf + 1, :], col_r - key_t[cr:cr + 1, :])
                    m = cb[rs] * (jnp.exp(seg) + jnp.where(diag[rs], dtv_t[cf:cf + 1, :], 0.0))
                    lhs.append(jnp.concatenate([m.astype(BF16),
                                                (cmf[rs] * jnp.exp(col_f)).astype(BF16),
                                                (cmf[rs] * jnp.exp(col_r)).astype(BF16)], axis=1))
            y4 = jnp.dot(jnp.concatenate(lhs, axis=0), rhs[jj], preferred_element_type=F32)
            for si in range(len(slabs)):
                r0 = 2 * si * SSD_ROW_SLAB
                ys[si].append(jnp.where(first_head, y4[r0:r0 + SSD_ROW_SLAB],
                                        y4[r0 + SSD_ROW_SLAB:r0 + 2 * SSD_ROW_SLAB]))
        for rs, y_parts in zip(slabs, ys):
            y = jnp.concatenate(y_parts, axis=1) + x[rs].astype(F32) * dsk_ref[...]
            o_ref[0, pl.ds(pl.multiple_of(n * l, l) + rs.start, SSD_ROW_SLAB), :] = y.astype(o_ref.dtype)

    def prep(i, carry):
        decays(i)
        return carry

    lax.fori_loop(0, nc, prep, 0, unroll=8)
    run_scr[...] = jnp.zeros_like(run_scr)

    def states(i, carry):
        chunk_state(i, 0)
        chunk_state(nc - 1 - i, 1)
        return carry

    lax.fori_loop(0, nc, states, 0, unroll=4)

    def outputs(n, carry):
        chunk_out(n)
        return carry

    lax.fori_loop(0, nc, outputs, 0, unroll=4)


def _ssd(xbc_act, dt3, pb, al, dskip, expand):
    bn, seq, _ = xbc_act.shape
    gw = SSD_JH * SSD_HEADDIM
    boff = SSD_INNER // SSD_STATE
    coff = boff + SSD_GROUPS
    return pl.pallas_call(
        functools.partial(_ssd_body, seq=seq),
        grid=(bn, SSD_GROUPS),
        in_specs=[
            pl.BlockSpec((1, seq, gw), lambda b, g: (b, 0, g)),
            pl.BlockSpec((1, seq, SSD_STATE), lambda b, g: (b, 0, boff + g)),
            pl.BlockSpec((1, seq, SSD_STATE), lambda b, g: (b, 0, coff + g)),
            pl.BlockSpec((1, seq, LANES), lambda b, g: (b, 0, g)),
            pl.BlockSpec((1, 2 * SSD_JH, 1), lambda b, g: (g, 0, 0)),
            pl.BlockSpec((1, 2 * SSD_JH, 1), lambda b, g: (g, 0, 0)),
            pl.BlockSpec((1, gw), lambda b, g: (0, g)),
            pl.BlockSpec((2, 2 * SSD_JH, gw), lambda b, g: (0, 0, 0)),
        ],
        out_specs=pl.BlockSpec((1, seq, gw), lambda b, g: (b, 0, g)),
        out_shape=jax.ShapeDtypeStruct((bn, seq, SSD_INNER), BF16),
        scratch_shapes=[pltpu.VMEM((2, seq // SSD_CHUNK, SSD_STATE, gw), BF16),
                        pltpu.VMEM((2, SSD_STATE, gw), F32),
                        pltpu.VMEM((5, seq // SSD_CHUNK, 2 * SSD_JH, SSD_CHUNK), F32)],
        compiler_params=_cparams(("parallel", "parallel")),
        name="ssd_scan",
    )(xbc_act, xbc_act, xbc_act, dt3, pb, al, dskip, expand)


def _rope_tables(seq):
    half = SWA_DH // 2
    inv = ROPE_THETA ** (-np.arange(half, dtype=np.float64) / half)
    ang = np.arange(seq, dtype=np.float64)[:, None] * inv[None, :]
    cos, sin = np.cos(ang), np.sin(ang)
    return (jnp.asarray(np.concatenate([cos, cos], axis=1), F32),
            jnp.asarray(np.concatenate([-sin, sin], axis=1), F32))


def _rope(x, cos, sin_signed):
    return x * cos + pltpu.roll(x, SWA_DH // 2, 1) * sin_signed


def _swa_body(sink_ref, q_ref, k_ref, v_ref, cos_ref, sin_ref, o_ref, kr_scr, *, seq):
    wb = SWA_BLOCK
    nb = seq // wb
    nkeys = 3 * wb
    kvh = pl.program_id(1)
    rb = min(512, seq)

    def krope(i, carry):
        rows = pl.ds(pl.multiple_of(i * rb, rb), rb)
        kr_scr[rows, :] = _rope(k_ref[0, rows, :].astype(F32), cos_ref[rows, :],
                                sin_ref[rows, :]).astype(BF16)
        return carry

    lax.fori_loop(0, seq // rb, krope, 0, unroll=2)

    qpos_l = lax.broadcasted_iota(jnp.int32, (wb, nkeys), 0)
    kpos_l = lax.broadcasted_iota(jnp.int32, (wb, nkeys), 1)

    def blk(n, carry):
        rows = pl.ds(pl.multiple_of(n * wb, wb), wb)
        ks = pl.multiple_of(jnp.clip(n - 1, 0, nb - 3) * wb, wb)
        krows = pl.ds(ks, nkeys)
        cos = cos_ref[rows, :]
        sin = sin_ref[rows, :]
        kk = kr_scr[krows, :]
        vv = v_ref[0, krows, :]
        valid = jnp.abs((qpos_l + n * wb) - (kpos_l + ks)) <= SWA_WINDOW
        qb = q_ref[0, rows, :].astype(F32)
        q4 = jnp.concatenate(
            [(_rope(qb[:, SWA_DH * hh:SWA_DH * (hh + 1)], cos, sin) * (SWA_DH ** -0.5)).astype(BF16)
             for hh in range(SWA_GQ)], axis=0)
        s4 = _dot_t(q4, kk)
        probs = []
        dens = []
        for hh in range(SWA_GQ):
            sink = sink_ref[kvh * SWA_GQ + hh]
            s = jnp.where(valid, s4[wb * hh:wb * (hh + 1)], NEG_BIG)
            m = jnp.maximum(jnp.max(s, axis=-1, keepdims=True), sink)
            p = jnp.exp(s - m)
            probs.append(p.astype(BF16))
            dens.append(jnp.sum(p, axis=-1, keepdims=True) + jnp.exp(sink - m))
        o4 = jnp.dot(jnp.concatenate(probs, axis=0), vv, preferred_element_type=F32)
        for hh in range(SWA_GQ):
            o = o4[wb * hh:wb * (hh + 1)] / dens[hh]
            o_ref[0, rows, SWA_DH * hh:SWA_DH * (hh + 1)] = o.astype(o_ref.dtype)
        return carry

    lax.fori_loop(0, nb, blk, 0, unroll=4)


def _swa(proj3, sink, cos, sin, col0):
    bn, seq, _ = proj3.shape
    qw = SWA_GQ * SWA_DH
    qoff = col0 // qw
    koff = (col0 + SWA_HEADS * SWA_DH) // SWA_DH
    voff = koff + SWA_KV
    return pl.pallas_call(
        functools.partial(_swa_body, seq=seq),
        grid=(bn, SWA_KV),
        in_specs=[
            pl.BlockSpec(memory_space=pltpu.SMEM),
            pl.BlockSpec((1, seq, qw), lambda b, h: (b, 0, qoff + h)),
            pl.BlockSpec((1, seq, SWA_DH), lambda b, h: (b, 0, koff + h)),
            pl.BlockSpec((1, seq, SWA_DH), lambda b, h: (b, 0, voff + h)),
            pl.BlockSpec((seq, SWA_DH), lambda b, h: (0, 0)),
            pl.BlockSpec((seq, SWA_DH), lambda b, h: (0, 0)),
        ],
        out_specs=pl.BlockSpec((1, seq, qw), lambda b, h: (b, 0, h)),
        out_shape=jax.ShapeDtypeStruct((bn, seq, SWA_HEADS * SWA_DH), BF16),
        scratch_shapes=[pltpu.VMEM((seq, SWA_DH), BF16)],
        compiler_params=_cparams(("parallel", "parallel")),
        name="swa",
    )(sink, proj3, proj3, proj3, cos, sin)


def _pad_cols(w, width):
    return jnp.pad(w, ((0, 0), (0, width - w.shape[1])))


def _even_mixers(x2, bn, seq, g0, w_in, w_decay, b_decay, gla_norm, rpb):
    d = x2.shape[1]
    n_gla = 2 * GLA_HEADS * GLA_DK + 2 * GLA_HEADS * GLA_DV
    n_lr = 2 * GLA_LOWRANK
    w_main = jnp.concatenate([w_in[:, :n_gla], w_in[:, n_gla + n_lr:]], axis=1).astype(BF16)
    w_small = _pad_cols(w_in[:, n_gla:n_gla + n_lr], LANES).astype(BF16)
    proj, small = _inproj(x2, g0, w_main, w_small)
    proj3 = proj.reshape(bn, seq, -1)
    small3 = small.reshape(bn, seq, -1)
    wdec = jnp.zeros((2, LANES, w_decay.shape[-1]), F32)
    for dd in range(2):
        wdec = wdec.at[dd, dd * GLA_LOWRANK:(dd + 1) * GLA_LOWRANK].set(w_decay[dd])
    o_a = _gla(proj3, small3, wdec, b_decay, gla_norm[None, :])
    o_b = _na(proj3, _na_bias_table(rpb), n_gla)
    return o_a.reshape(-1, o_a.shape[-1]), o_b.reshape(-1, o_b.shape[-1])


def _per_group(v):
    return v.reshape(2, SSD_GROUPS, SSD_JH).transpose(1, 0, 2).reshape(SSD_GROUPS, 2 * SSD_JH)


def _odd_mixers(x2, bn, seq, g0, w_in, conv_w, conv_b, dt_bias, a_log, d_skip, sink, cos, sin):
    conv_dim = SSD_INNER + 2 * SSD_GROUPS * SSD_STATE
    n_z = SSD_INNER
    n_dt = 2 * SSD_HEADS
    c_dt = n_z + conv_dim
    w_main = jnp.concatenate([w_in[:, :c_dt], w_in[:, c_dt + n_dt:]], axis=1).astype(BF16)
    w_dt = w_in[:, c_dt:c_dt + n_dt].reshape(-1, 2, SSD_GROUPS, SSD_JH).transpose(0, 2, 1, 3)
    w_dt = w_dt.reshape(-1, SSD_GROUPS, 2 * SSD_JH)
    w_small = jnp.pad(w_dt, ((0, 0), (0, 0), (0, LANES - 2 * SSD_JH))).reshape(-1, SSD_GROUPS * LANES)
    proj, small = _inproj(x2, g0, w_main, w_small.astype(BF16))
    proj3 = proj.reshape(bn, seq, -1)
    dt3 = small.reshape(bn, seq, SSD_GROUPS * LANES)
    xbc_act = _conv(proj3, conv_w, conv_b[None, :], n_z)
    pbg = _per_group(dt_bias)
    alg = _per_group(a_log)
    head_of_ch = np.arange(SSD_JH * SSD_HEADDIM) // SSD_HEADDIM
    expand = jnp.asarray(np.arange(2 * SSD_JH)[None, :, None]
                         == (np.arange(2)[:, None, None] * SSD_JH + head_of_ch[None, None, :]), F32)
    dskip = jnp.repeat(d_skip.astype(F32), SSD_HEADDIM)[None, :]
    y_pre = _ssd(xbc_act, dt3, pbg[:, :, None], alg[:, :, None], dskip, expand)
    o_d = _swa(proj3, sink, cos, sin, c_dt)
    return y_pre.reshape(-1, SSD_INNER), proj, o_d.reshape(-1, o_d.shape[-1])


def kernel(x, norm_gains, ffn_w_gate, ffn_w_up, ffn_w_down, even_w_in, even_w_out, gla_w_decay,
           gla_b_decay, gla_norm, na_rpb, odd_w_in, odd_w_out, ssd_conv_w, ssd_conv_b, ssd_dt_bias,
           ssd_a_log, ssd_d, ssd_norm, swa_sink):
    bn, seq, d = x.shape
    depth = norm_gains.shape[0]
    x2 = x.reshape(bn * seq, d)
    cos, sin = _rope_tables(seq)
    ffn_w = (ffn_w_gate, ffn_w_up, ffn_w_down)
    even_w_out_b = even_w_out.astype(BF16)
    odd_w_out_b = odd_w_out.astype(BF16)
    w_bf16 = None
    for layer in range(depth):
        g = norm_gains[layer][:, None, :]
        i = layer // 2
        cast_here = dict(ffn_w=ffn_w, layer=0) if layer == 0 else {}
        cast_next = dict(ffn_w=ffn_w, next_layer=layer + 1) if layer + 1 < depth else {}
        if layer % 2 == 0:
            o_a, o_b = _even_mixers(x2, bn, seq, g[0], even_w_in[i], gla_w_decay[i], gla_b_decay[i],
                                    gla_norm[i], na_rpb[i])
            x2, *cast = _outproj(x2, o_a, o_b, even_w_out_b, i, g[1], **cast_here)
        else:
            y_pre, proj, o_d = _odd_mixers(x2, bn, seq, g[0], odd_w_in[i], ssd_conv_w[i], ssd_conv_b[i],
                                           ssd_dt_bias[i], ssd_a_log[i], ssd_d[i], swa_sink[i], cos, sin)
            x2, *cast = _outproj(x2, y_pre, o_d, odd_w_out_b, i, g[1], z_src=proj,
                                 ssd_gain=ssd_norm[i][None, :], **cast_here)
        w_bf16 = cast or w_bf16
        x2, *w_bf16 = _ffn(x2, g[2], *w_bf16, g[3], **cast_next)
    return x2.reshape(bn, seq, d)
```

```python
import functools

import jax
import jax.numpy as jnp
import numpy as np
from jax import lax
from jax.experimental import pallas as pl
from jax.experimental.pallas import tpu as pltpu

F32 = jnp.float32
BF16 = jnp.bfloat16
HIGHEST = lax.Precision.HIGHEST

EPS = 1e-6
ROPE_THETA = 10000.0
GRID_W = 64
LANES = 128

GLA_DK = 64
GLA_DV = 128
GLA_HEADS = 8
GLA_LOWRANK = 16
GLA_TAU = 16.0
GLA_CHUNK = 64
GLA_SUB = 16
GLA_EXP_CLAMP = 60.0
GLA_TRI_BLOCK = 256

NA_DH = 128
NA_HEADS = 8
NA_KH = 8
NA_KW = 16
NA_ROWS_PER_ITER = 16
NEG_BIG = -1e30

SSD_INNER = 1024
SSD_HEADDIM = 64
SSD_HEADS = 16
SSD_GROUPS = 2
SSD_STATE = 128
SSD_CONV = 5
SSD_CHUNK = 128
SSD_JH = SSD_HEADS // SSD_GROUPS
SSD_ROW_SLAB = 64

SWA_DH = 128
SWA_HEADS = 8
SWA_KV = 2
SWA_GQ = SWA_HEADS // SWA_KV
SWA_WINDOW = 128
SWA_BLOCK = 128

VMEM_LIMIT = 52 * 1024 * 1024


FFN_NORM_ROWS = 256
FFN_VMEM_LIMIT = 58 * 1024 * 1024


def _cparams(sem, vmem_limit=VMEM_LIMIT):
    return pltpu.CompilerParams(dimension_semantics=sem, vmem_limit_bytes=vmem_limit)


def _rms(x, g):
    return x * lax.rsqrt(jnp.mean(x * x, axis=-1, keepdims=True) + EPS) * g


def _silu(x):
    return x * jax.nn.sigmoid(x)


def _dot_t(a, b):
    return lax.dot_general(a, b, (((1,), (1,)), ((), ())), preferred_element_type=F32)


def _inproj_body(x_ref, g_ref, w_ref, ws_ref, o_ref, os_ref, h_scr):
    j = pl.program_id(1)
    half = x_ref.shape[0] // 2

    @pl.when(j == 0)
    def _():
        for r0 in (0, half):
            rs = pl.ds(r0, half)
            h = _rms(x_ref[rs, :], g_ref[...]).astype(BF16)
            h_scr[rs, :] = h
            os_ref[rs, :] = jnp.dot(h, ws_ref[...], preferred_element_type=F32)
            o_ref[rs, :] = jnp.dot(h, w_ref[...], preferred_element_type=F32).astype(o_ref.dtype)

    @pl.when(j > 0)
    def _():
        o_ref[...] = jnp.dot(h_scr[...], w_ref[...], preferred_element_type=F32).astype(o_ref.dtype)


def _inproj(x2, g, w_main, w_small, tm=1024, tn=2048):
    n, d = x2.shape
    nm = w_main.shape[1]
    ns = w_small.shape[1]
    return pl.pallas_call(
        _inproj_body,
        grid=(n // tm, nm // tn),
        in_specs=[
            pl.BlockSpec((tm, d), lambda i, j: (i, 0)),
            pl.BlockSpec((1, d), lambda i, j: (0, 0)),
            pl.BlockSpec((d, tn), lambda i, j: (0, j)),
            pl.BlockSpec((d, ns), lambda i, j: (0, 0)),
        ],
        out_specs=[
            pl.BlockSpec((tm, tn), lambda i, j: (i, j)),
            pl.BlockSpec((tm, ns), lambda i, j: (i, 0)),
        ],
        out_shape=[jax.ShapeDtypeStruct((n, nm), BF16), jax.ShapeDtypeStruct((n, ns), F32)],
        scratch_shapes=[pltpu.VMEM((tm, d), BF16)],
        compiler_params=_cparams(("parallel", "arbitrary")),
        name="inproj",
    )(x2, g, w_main, w_small)


def _cast_weight_slices(w_refs):
    for src, dst in zip(w_refs[:3], w_refs[3:]):
        dst[...] = src[...].astype(BF16)


def _outproj_out(rest):
    if len(rest) == 1:
        return rest[0]
    _cast_weight_slices(rest[:3] + rest[4:])
    return rest[3]


def _outproj_even_body(a_ref, b_ref, wa_ref, wb_ref, g_ref, x_ref, *rest):
    o_ref = _outproj_out(rest)
    mix = (jnp.dot(a_ref[...], wa_ref[...], preferred_element_type=F32)
           + jnp.dot(b_ref[...], wb_ref[...], preferred_element_type=F32))
    o_ref[...] = x_ref[...] + _rms(mix, g_ref[...])


def _outproj_odd_body(y_ref, z_ref, ng_ref, b_ref, wa_ref, wb_ref, g_ref, x_ref, *rest):
    o_ref = _outproj_out(rest)
    y = y_ref[...].astype(F32) * _silu(z_ref[...].astype(F32))
    a = _rms(y, ng_ref[...]).astype(BF16)
    mix = (jnp.dot(a, wa_ref[...], preferred_element_type=F32)
           + jnp.dot(b_ref[...], wb_ref[...], preferred_element_type=F32))
    o_ref[...] = x_ref[...] + _rms(mix, g_ref[...])


def _outproj(x2, a, b, w_out, li, g, ffn_w=None, layer=None, z_src=None, ssd_gain=None, tm=512):
    n, d = x2.shape
    half = a.shape[1]
    steps = n // tm
    row = lambda i: (i, 0)
    fixed = lambda i: (0, 0)
    w_specs = [pl.BlockSpec((None, half, d), lambda i: (li, 0, 0)),
               pl.BlockSpec((None, half, d), lambda i: (li, 1, 0))]
    tail_specs = [pl.BlockSpec((1, d), fixed), pl.BlockSpec((tm, d), row)]
    ffn_w = ffn_w or ()
    cast_in = [pl.BlockSpec((None, w.shape[1] // steps, w.shape[2]), lambda i: (layer, i, 0)) for w in ffn_w]
    cast_out = [pl.BlockSpec((w.shape[1] // steps, w.shape[2]), row) for w in ffn_w]
    cast_shapes = [jax.ShapeDtypeStruct(w.shape[1:], BF16) for w in ffn_w]
    if z_src is None:
        body = _outproj_even_body
        in_specs = [pl.BlockSpec((tm, half), row), pl.BlockSpec((tm, half), row)] + w_specs + tail_specs
        args = (a, b, w_out, w_out, g, x2)
    else:
        body = _outproj_odd_body
        in_specs = ([pl.BlockSpec((tm, half), row), pl.BlockSpec((tm, half), row),
                     pl.BlockSpec((1, half), fixed), pl.BlockSpec((tm, half), row)]
                    + w_specs + tail_specs)
        args = (a, z_src, ssd_gain, b, w_out, w_out, g, x2)
    return pl.pallas_call(
        body,
        grid=(steps,),
        in_specs=in_specs + cast_in,
        out_specs=[pl.BlockSpec((tm, d), row)] + cast_out,
        out_shape=[jax.ShapeDtypeStruct((n, d), F32)] + cast_shapes,
        compiler_params=_cparams(("parallel",)),
        name="outproj",
    )(*args, *ffn_w)


def _ffn_body(x_ref, g2_ref, wg_ref, wu_ref, wd_ref, g3_ref, *rest):
    if len(rest) == 2:
        o_ref, h_scr = rest
    else:
        o_ref, h_scr = rest[3], rest[7]
        _cast_weight_slices(rest[:3] + rest[4:7])
    k = pl.program_id(1)
    last = pl.num_programs(1) - 1
    half = x_ref.shape[0] // 2

    def pieces(r0):
        return [pl.ds(r0 + p, FFN_NORM_ROWS) for p in range(0, half, FFN_NORM_ROWS)]

    def partial_down(rs):
        h = h_scr[rs, :]
        gate = jnp.dot(h, wg_ref[...], preferred_element_type=F32)
        up = jnp.dot(h, wu_ref[...], preferred_element_type=F32)
        return jnp.dot((_silu(gate) * up).astype(BF16), wd_ref[...], preferred_element_type=F32)

    @pl.when(k == 0)
    def _():
        for r0 in (0, half):
            for ps in pieces(r0):
                h_scr[ps, :] = _rms(x_ref[ps, :], g2_ref[...]).astype(BF16)
            o_ref[pl.ds(r0, half), :] = partial_down(pl.ds(r0, half))

    @pl.when(jnp.logical_and(k > 0, k < last))
    def _():
        for r0 in (0, half):
            o_ref[pl.ds(r0, half), :] += partial_down(pl.ds(r0, half))

    @pl.when(k == last)
    def _():
        for r0 in (0, half):
            o_ref[pl.ds(r0, half), :] += partial_down(pl.ds(r0, half))
            for ps in pieces(r0):
                o_ref[ps, :] = x_ref[ps, :] + _rms(o_ref[ps, :], g3_ref[...])


def _ffn(x2, g2, wg, wu, wd, g3, ffn_w=None, next_layer=None, tm=1024, th=512):
    n, d = x2.shape
    hid = wd.shape[0]
    ni, nk = n // tm, hid // th
    in_specs = [
        pl.BlockSpec((tm, d), lambda i, k: (i, 0)),
        pl.BlockSpec((1, d), lambda i, k: (0, 0)),
        pl.BlockSpec((d, th), lambda i, k: (0, k)),
        pl.BlockSpec((d, th), lambda i, k: (0, k)),
        pl.BlockSpec((th, d), lambda i, k: (k, 0)),
        pl.BlockSpec((1, d), lambda i, k: (0, 0)),
    ]
    out_specs = [pl.BlockSpec((tm, d), lambda i, k: (i, 0))]
    out_shape = [jax.ShapeDtypeStruct((n, d), F32)]
    args = [x2, g2, wg, wu, wd, g3]
    if ffn_w is not None:
        up_blk, down_blk = (d // ni, hid // nk), (hid // nk, d // ni)
        in_specs += [pl.BlockSpec((None,) + up_blk, lambda i, k: (next_layer, i, k)),
                     pl.BlockSpec((None,) + up_blk, lambda i, k: (next_layer, i, k)),
                     pl.BlockSpec((None,) + down_blk, lambda i, k: (next_layer, k, i))]
        out_specs += [pl.BlockSpec(up_blk, lambda i, k: (i, k)),
                      pl.BlockSpec(up_blk, lambda i, k: (i, k)),
                      pl.BlockSpec(down_blk, lambda i, k: (k, i))]
        out_shape += [jax.ShapeDtypeStruct(w.shape[1:], BF16) for w in ffn_w]
        args += list(ffn_w)
    return pl.pallas_call(
        _ffn_body,
        grid=(ni, nk),
        in_specs=in_specs,
        out_specs=out_specs,
        out_shape=out_shape,
        scratch_shapes=[pltpu.VMEM((tm, d), BF16)],
        compiler_params=_cparams(("parallel", "arbitrary"), FFN_VMEM_LIMIT),
        name="ffn",
    )(*args)


def _gla_body(q_ref, k_ref, v_ref, g_ref, lr_ref, wd_ref, bd_ref, gn_ref, o_ref,
              bc_scr, o_scr, st_scr, *, seq):
    c = GLA_CHUNK
    nc = seq // c
    half = nc // 2
    nsub = c // GLA_SUB
    rb = min(512, seq)
    row_i = lax.broadcasted_iota(jnp.int32, (c, c), 0)
    col_i = lax.broadcasted_iota(jnp.int32, (c, c), 1)
    tri = (row_i >= col_i, row_i <= col_i)
    lane = lax.broadcasted_iota(jnp.int32, (1, LANES), 1)
    hmask = (lane < GLA_DK, lane >= GLA_DK)
    own_head = ((lax.broadcasted_iota(jnp.int32, (2 * GLA_DV, LANES), 0) < GLA_DV)
                == (lax.broadcasted_iota(jnp.int32, (2 * GLA_DV, LANES), 1) < GLA_DK))

    w2 = jnp.concatenate([wd_ref[0], wd_ref[1]], axis=1)
    w_hi = w2.astype(BF16)
    w_lo = (w2 - w_hi.astype(F32)).astype(BF16)
    w_cat = jnp.concatenate([w_hi, w_lo, w_hi], axis=0)
    tb = GLA_TRI_BLOCK
    brow = lax.broadcasted_iota(jnp.int32, (tb, tb), 0)
    bcol = lax.broadcasted_iota(jnp.int32, (tb, tb), 1)
    same_chunk = (brow // c) == (bcol // c)
    blk_tri = ((same_chunk & (brow >= bcol)).astype(BF16), (same_chunk & (brow <= bcol)).astype(BF16))

    def pre(i, carry):
        r0 = pl.multiple_of(i * rb, rb)
        lr = lr_ref[0, pl.ds(r0, rb), :]
        lr_hi = lr.astype(BF16)
        lr_lo = (lr - lr_hi.astype(F32)).astype(BF16)
        z2 = jnp.dot(jnp.concatenate([lr_hi, lr_hi, lr_lo], axis=1), w_cat, preferred_element_type=F32)
        las = []
        for d in range(2):
            z = z2[:, LANES * d:LANES * (d + 1)] + bd_ref[d:d + 1, :]
            la = (jnp.minimum(z, 0.0) - jnp.log(1.0 + jnp.exp(-jnp.abs(z)))) * (1.0 / GLA_TAU)
            la_hi = la.astype(BF16)
            las.append(jnp.concatenate([la_hi, (la - la_hi.astype(F32)).astype(BF16)], axis=1))
        for d in range(2):
            for t in range(rb // tb):
                s2 = jnp.dot(blk_tri[d], las[d][t * tb:(t + 1) * tb], preferred_element_type=F32)
                bc_scr[d, pl.ds(r0 + t * tb, tb), :] = s2[:, :LANES] + s2[:, LANES:]
        return carry

    lax.fori_loop(0, seq // rb, pre, 0, unroll=8)

    def stage1(n, d):
        rows = pl.ds(pl.multiple_of(n * c, c), c)
        bc = bc_scr[d, rows, :]
        q = q_ref[0, rows, :].astype(F32) * (GLA_DK ** -0.5)
        k = k_ref[0, rows, :].astype(F32)
        v = v_ref[0, rows, :]
        b_edge = bc[c - 1:c] if d == 0 else bc[0:1]
        qhat = q * jnp.exp(bc)
        khat = (k * jnp.exp(b_edge - bc)).astype(BF16)
        att_rows = ([], [])
        for i in range(nsub):
            lo = GLA_SUB * i
            hi = lo + GLA_SUB
            if d == 0:
                ref = bc[lo - 1:lo] if i > 0 else jnp.zeros((1, LANES), F32)
            else:
                ref = bc[hi:hi + 1] if i < nsub - 1 else jnp.zeros((1, LANES), F32)
            qi = q[lo:hi] * jnp.exp(bc[lo:hi] - ref)
            ks = slice(0, hi) if d == 0 else slice(lo, c)
            ki = (k[ks] * jnp.exp(jnp.minimum(ref - bc[ks], GLA_EXP_CLAMP))).astype(BF16)
            q2 = jnp.concatenate([jnp.where(hmask[0], qi, 0.0), jnp.where(hmask[1], qi, 0.0)], axis=0)
            a2 = _dot_t(q2.astype(BF16), ki)
            n_unseen = c - (ks.stop - ks.start)
            if n_unseen:
                unseen = jnp.zeros((2 * GLA_SUB, n_unseen), F32)
                a2 = jnp.concatenate([a2, unseen] if d == 0 else [unseen, a2], axis=1)
            att_rows[0].append(a2[:GLA_SUB])
            att_rows[1].append(a2[GLA_SUB:])
        kv2 = lax.dot_general(v, khat, (((0,), (0,)), ((), ())), preferred_element_type=F32)
        kv2 = jnp.where(own_head, kv2, 0.0)
        att2 = jnp.concatenate(
            [jnp.where(tri[d], jnp.concatenate(att_rows[hh], axis=0), 0.0).astype(BF16)
             for hh in range(2)], axis=1)
        zero_v = jnp.zeros((c, GLA_DV), BF16)
        v_blockdiag = jnp.concatenate(
            [jnp.concatenate([v[:, :GLA_DV], zero_v], axis=1),
             jnp.concatenate([zero_v, v[:, GLA_DV:]], axis=1)], axis=0)
        return d, att2, v_blockdiag, kv2, qhat.astype(BF16), jnp.exp(b_edge)

    def stage2(s1, states):
        d, att2, v_blockdiag, kv2, qhat, decay = s1
        st = states[d]
        o = jnp.dot(att2, v_blockdiag, preferred_element_type=F32) + _dot_t(qhat, st.astype(BF16))
        states[d] = st * decay + kv2
        return o

    def finish(n, o):
        rows = pl.ds(pl.multiple_of(n * c, c), c)
        parts = []
        for hh in range(2):
            oh = o[:, GLA_DV * hh:GLA_DV * (hh + 1)]
            parts.append(oh * lax.rsqrt(jnp.mean(oh * oh, axis=-1, keepdims=True) + EPS))
        o = jnp.concatenate(parts, axis=1) * gn_ref[...]
        o_ref[0, rows, :] = (o * _silu(g_ref[0, rows, :].astype(F32))).astype(o_ref.dtype)

    st_scr[...] = jnp.zeros_like(st_scr)

    cpi = next(u for u in (4, 2, 1) if half % u == 0)

    def walk(i, base_f, base_b):
        work = []
        for u in range(cpi):
            work.append((base_f + i * cpi + u, 0))
            work.append((base_b - i * cpi - u, 1))
        prepared = [stage1(n, d) for n, d in work]
        states = [st_scr[d] for d in range(2)]
        outs = [(n, stage2(s1, states)) for (n, _), s1 in zip(work, prepared)]
        for d in range(2):
            st_scr[d] = states[d]
        return outs

    def first(i, carry):
        for n, o in walk(i, 0, nc - 1):
            o_scr[pl.ds(pl.multiple_of(n * c, c), c), :] = o
        return carry

    lax.fori_loop(0, half // cpi, first, 0, unroll=2)

    def second(i, carry):
        for n, o in walk(i, half, half - 1):
            finish(n, o_scr[pl.ds(pl.multiple_of(n * c, c), c), :] + o)
        return carry

    lax.fori_loop(0, half // cpi, second, 0, unroll=2)


def _gla(proj3, small3, wdec_pad, bdec, gnorm):
    bn, seq, _ = proj3.shape
    npair = GLA_HEADS // 2
    qoff = 0
    koff = (GLA_HEADS * GLA_DK) // LANES
    voff = (2 * GLA_HEADS * GLA_DK) // (2 * GLA_DV)
    goff = voff + npair
    return pl.pallas_call(
        functools.partial(_gla_body, seq=seq),
        grid=(bn, npair),
        in_specs=[
            pl.BlockSpec((1, seq, LANES), lambda b, p: (b, 0, qoff + p)),
            pl.BlockSpec((1, seq, LANES), lambda b, p: (b, 0, koff + p)),
            pl.BlockSpec((1, seq, 2 * GLA_DV), lambda b, p: (b, 0, voff + p)),
            pl.BlockSpec((1, seq, 2 * GLA_DV), lambda b, p: (b, 0, goff + p)),
            pl.BlockSpec((1, seq, LANES), lambda b, p: (b, 0, 0)),
            pl.BlockSpec((2, LANES, LANES), lambda b, p: (0, 0, p)),
            pl.BlockSpec((2, LANES), lambda b, p: (0, p)),
            pl.BlockSpec((1, 2 * GLA_DV), lambda b, p: (0, p)),
        ],
        out_specs=pl.BlockSpec((1, seq, 2 * GLA_DV), lambda b, p: (b, 0, p)),
        out_shape=jax.ShapeDtypeStruct((bn, seq, GLA_HEADS * GLA_DV), BF16),
        scratch_shapes=[
            pltpu.VMEM((2, seq, LANES), F32),
            pltpu.VMEM((seq, 2 * GLA_DV), F32),
            pltpu.VMEM((2, 2 * GLA_DV, LANES), F32),
        ],
        compiler_params=_cparams(("parallel", "parallel")),
        name="gla",
    )(proj3, proj3, proj3, proj3, small3, wdec_pad, bdec, gnorm)


def _na_bias_table(rpb):
    nh, ndr, ndc = rpb.shape
    cols = np.arange(GRID_W)
    cs = np.clip(cols - NA_KW // 2, 0, GRID_W - NA_KW)
    valid = (cols[None, :] >= cs[:, None]) & (cols[None, :] < cs[:, None] + NA_KW)
    dc = cols[None, :] - cols[:, None] + NA_KW - 1
    onehot = (np.arange(ndc)[:, None, None] == dc[None]) & valid[None]
    toep = jnp.dot(rpb.reshape(nh * ndr, ndc).astype(F32),
                   jnp.asarray(onehot.reshape(ndc, -1), F32), precision=HIGHEST)
    toep = jnp.where(jnp.asarray(valid.reshape(1, -1)), toep, NEG_BIG)
    toep = toep.reshape(nh, ndr, GRID_W, GRID_W)
    t = jnp.stack([toep[:, off:off + NA_KH] for off in range(NA_KH)], axis=1)
    return t.transpose(0, 1, 3, 2, 4).reshape(nh, NA_KH, GRID_W, NA_KH * GRID_W)


def _na_body(q_ref, k_ref, v_ref, bias_ref, o_ref, *, nrows):
    w = GRID_W
    nk = NA_KH * w

    def group(i, carry):
        idx = []
        scores = []
        for u in range(NA_ROWS_PER_ITER):
            r = i * NA_ROWS_PER_ITER + u
            rs = jnp.clip(r - NA_KH // 2, 0, nrows - NA_KH)
            qrows = pl.ds(pl.multiple_of(r * w, w), w)
            krows = pl.ds(pl.multiple_of(rs * w, w), nk)
            idx.append((qrows, krows, rs - r + (NA_KH - 1)))
            scores.append(_dot_t(q_ref[0, qrows, :], k_ref[0, krows, :]))
        probs = []
        for (qrows, krows, off), s in zip(idx, scores):
            s = s * (NA_DH ** -0.5) + bias_ref[0, off]
            p = jnp.exp(s - jnp.max(s, axis=-1, keepdims=True))
            probs.append((p.astype(BF16), jnp.sum(p, axis=-1, keepdims=True)))
        for (qrows, krows, off), (p, l) in zip(idx, probs):
            o = jnp.dot(p, v_ref[0, krows, :], preferred_element_type=F32) / l
            o_ref[0, qrows, :] = o.astype(o_ref.dtype)
        return carry

    lax.fori_loop(0, nrows // NA_ROWS_PER_ITER, group, 0, unroll=2)


def _na(proj3, bias_tab, col0):
    bn, seq, _ = proj3.shape
    nrows = seq // GRID_W
    qoff = col0 // NA_DH
    koff = qoff + NA_HEADS
    voff = koff + NA_HEADS
    return pl.pallas_call(
        functools.partial(_na_body, nrows=nrows),
        grid=(bn, NA_HEADS),
        in_specs=[
            pl.BlockSpec((1, seq, NA_DH), lambda b, h: (b, 0, qoff + h)),
            pl.BlockSpec((1, seq, NA_DH), lambda b, h: (b, 0, koff + h)),
            pl.BlockSpec((1, seq, NA_DH), lambda b, h: (b, 0, voff + h)),
            pl.BlockSpec((1, NA_KH, GRID_W, NA_KH * GRID_W), lambda b, h: (h, 0, 0, 0)),
        ],
        out_specs=pl.BlockSpec((1, seq, NA_DH), lambda b, h: (b, 0, h)),
        out_shape=jax.ShapeDtypeStruct((bn, seq, NA_HEADS * NA_DH), BF16),
        compiler_params=_cparams(("parallel", "parallel")),
        name="na",
    )(proj3, proj3, proj3, bias_tab)


CONV_PAD = 16


def _conv_body(x_ref, w_ref, b_ref, o_ref, xp_scr, *, seq):
    rb = min(256, seq)
    ch = x_ref.shape[-1]
    zeros = jnp.zeros((CONV_PAD, ch), xp_scr.dtype)
    xp_scr[0:CONV_PAD, :] = zeros
    xp_scr[seq + CONV_PAD:seq + 2 * CONV_PAD, :] = zeros
    xp_scr[CONV_PAD:seq + CONV_PAD, :] = x_ref[0]

    def blk(i, carry):
        r0 = pl.multiple_of(i * rb, rb)
        xw = xp_scr[pl.ds(r0, rb + 2 * CONV_PAD), :].astype(F32)
        acc = jnp.zeros((rb, ch), F32) + b_ref[...]
        for j in range(SSD_CONV):
            s0 = CONV_PAD - SSD_CONV // 2 + j
            acc = acc + xw[s0:s0 + rb] * w_ref[j:j + 1, :]
        o_ref[0, pl.ds(r0, rb), :] = _silu(acc).astype(o_ref.dtype)
        return carry

    lax.fori_loop(0, seq // rb, blk, 0)


def _conv(proj3, conv_w, conv_b, col0, tc=512):
    bn, seq, _ = proj3.shape
    cdim = conv_w.shape[1]
    c0 = col0 // tc
    return pl.pallas_call(
        functools.partial(_conv_body, seq=seq),
        grid=(bn, cdim // tc),
        in_specs=[
            pl.BlockSpec((1, seq, tc), lambda b, j: (b, 0, c0 + j)),
            pl.BlockSpec((SSD_CONV, tc), lambda b, j: (0, j)),
            pl.BlockSpec((1, tc), lambda b, j: (0, j)),
        ],
        out_specs=pl.BlockSpec((1, seq, tc), lambda b, j: (b, 0, j)),
        out_shape=jax.ShapeDtypeStruct((bn, seq, cdim), BF16),
        scratch_shapes=[pltpu.VMEM((seq + 2 * CONV_PAD, tc), BF16)],
        compiler_params=_cparams(("parallel", "parallel")),
        name="ssd_conv",
    )(proj3, conv_w, conv_b)


def _softplus(x):
    return jnp.maximum(x, 0.0) + jnp.log(1.0 + jnp.exp(-jnp.abs(x)))


def _ssd_body(xs_ref, bm_ref, cm_ref, dt_ref, pb_ref, al_ref, dsk_ref, ex_ref, o_ref,
              st_scr, run_scr, dec_scr, *, seq):
    l = SSD_CHUNK
    nc = seq // l
    nh2 = 2 * SSD_JH
    row_i = lax.broadcasted_iota(jnp.int32, (l, l), 0)
    col_i = lax.broadcasted_iota(jnp.int32, (l, l), 1)
    lane = lax.broadcasted_iota(jnp.int32, (1, LANES), 1)
    first_head = lane < SSD_HEADDIM
    bwd_row = lax.broadcasted_iota(jnp.int32, (nh2, 1), 0) >= SSD_JH
    a_col = -jnp.exp(al_ref[0])

    lower = row_i > col_i
    diag = row_i == col_i
    triu_b = (row_i <= col_i).astype(BF16)

    def decays(n):
        rows = pl.ds(pl.multiple_of(n * l, l), l)
        dtv = _softplus(dt_ref[0, rows, :].T[0:nh2] + pb_ref[0])
        da = dtv * a_col
        hi = da.astype(BF16)
        r1 = da - hi.astype(F32)
        mid = r1.astype(BF16)
        lo = (r1 - mid.astype(F32)).astype(BF16)
        p3 = jnp.dot(jnp.concatenate([hi, mid, lo], axis=0), triu_b, preferred_element_type=F32)
        pre = p3[0:nh2] + p3[nh2:2 * nh2] + p3[2 * nh2:]
        total = pre[:, l - 1:l]
        acum = jnp.where(bwd_row, total - pre + da, pre)
        dec_scr[0, n] = dtv
        dec_scr[1, n] = acum
        dec_scr[2, n] = dtv * jnp.exp(total - acum)
        dec_scr[3, n] = jnp.broadcast_to(jnp.exp(total), (nh2, l))
        dec_scr[4, n] = acum - jnp.log(dtv)

    def chunk_state(n, d):
        rows = pl.ds(pl.multiple_of(n * l, l), l)
        w_state_t = dec_scr[2, n]
        x = xs_ref[0, rows, :]
        bmt = bm_ref[0, rows, :].astype(F32).T
        edge_ch = jnp.sum(dec_scr[3, n][:, 0:1] * ex_ref[d], axis=0, keepdims=True)
        new_st = []
        ns = SSD_STATE
        for jj in range(SSD_JH // 2):
            xp = x[:, LANES * jj:LANES * (jj + 1)]
            bw2 = jnp.concatenate(
                [(bmt * w_state_t[d * SSD_JH + 2 * jj + hh:d * SSD_JH + 2 * jj + hh + 1, :]).astype(BF16)
                 for hh in range(2)], axis=0)
            s2 = jnp.dot(bw2, xp, preferred_element_type=F32)
            new_st.append(jnp.where(first_head, s2[:ns], s2[ns:]))
        st = run_scr[d]
        st_scr[d, n] = st.astype(BF16)
        run_scr[d] = st * edge_ch + jnp.concatenate(new_st, axis=1)

    def chunk_out(n):
        rows = pl.ds(pl.multiple_of(n * l, l), l)
        dtv_t = dec_scr[0, n]
        acum = dec_scr[1, n].T
        key_t = dec_scr[4, n]
        x = xs_ref[0, rows, :]
        cm = cm_ref[0, rows, :]
        cmf = cm.astype(F32)
        cb = _dot_t(cm, bm_ref[0, rows, :])
        st_f = st_scr[0, n]
        st_b = st_scr[1, n]
        rhs = [jnp.concatenate([x[:, LANES * jj:LANES * (jj + 1)],
                                st_f[:, LANES * jj:LANES * (jj + 1)],
                                st_b[:, LANES * jj:LANES * (jj + 1)]], axis=0)
               for jj in range(SSD_JH // 2)]
        slabs = [slice(r0, r0 + SSD_ROW_SLAB) for r0 in range(0, l, SSD_ROW_SLAB)]
        ys = [[] for _ in slabs]
        for jj in range(SSD_JH // 2):
            lhs = []
            for rs in slabs:
                for hh in range(2):
                    cf = 2 * jj + hh
                    cr = SSD_JH + cf
                    col_f = jnp.broadcast_to(acum[rs, cf:cf + 1], (SSD_ROW_SLAB, l))
                    col_r = jnp.broadcast_to(acum[rs, cr:cr + 1], (SSD_ROW_SLAB, l))
                    seg = jnp.where(lower[rs], col_f - key_t[cf:cf + 1, :], col_r - key_t[cr:cr + 1, :])
                    m = cb[rs] * (jnp.exp(seg) + jnp.where(diag[rs], dtv_t[cf:cf + 1, :], 0.0))
                    lhs.append(jnp.concatenate([m.astype(BF16),
                                                (cmf[rs] * jnp.exp(col_f)).astype(BF16),
                                                (cmf[rs] * jnp.exp(col_r)).astype(BF16)], axis=1))
            y4 = jnp.dot(jnp.concatenate(lhs, axis=0), rhs[jj], preferred_element_type=F32)
            for si in range(len(slabs)):
                r0 = 2 * si * SSD_ROW_SLAB
                ys[si].append(jnp.where(first_head, y4[r0:r0 + SSD_ROW_SLAB],
                                        y4[r0 + SSD_ROW_SLAB:r0 + 2 * SSD_ROW_SLAB]))
        for rs, y_parts in zip(slabs, ys):
            y = jnp.concatenate(y_parts, axis=1) + x[rs].astype(F32) * dsk_ref[...]
            o_ref[0, pl.ds(pl.multiple_of(n * l, l) + rs.start, SSD_ROW_SLAB), :] = y.astype(o_ref.dtype)

    def prep(i, carry):
        decays(i)
        return carry

    lax.fori_loop(0, nc, prep, 0, unroll=8)
    run_scr[...] = jnp.zeros_like(run_scr)

    def states(i, carry):
        chunk_state(i, 0)
        chunk_state(nc - 1 - i, 1)
        return carry

    lax.fori_loop(0, nc, states, 0, unroll=8)

    def outputs(n, carry):
        chunk_out(n)
        return carry

    lax.fori_loop(0, nc, outputs, 0, unroll=4)


def _ssd(xbc_act, dt3, pb, al, dskip, expand):
    bn, seq, _ = xbc_act.shape
    gw = SSD_JH * SSD_HEADDIM
    boff = SSD_INNER // SSD_STATE
    coff = boff + SSD_GROUPS
    return pl.pallas_call(
        functools.partial(_ssd_body, seq=seq),
        grid=(bn, SSD_GROUPS),
        in_specs=[
            pl.BlockSpec((1, seq, gw), lambda b, g: (b, 0, g)),
            pl.BlockSpec((1, seq, SSD_STATE), lambda b, g: (b, 0, boff + g)),
            pl.BlockSpec((1, seq, SSD_STATE), lambda b, g: (b, 0, coff + g)),
            pl.BlockSpec((1, seq, LANES), lambda b, g: (b, 0, g)),
            pl.BlockSpec((1, 2 * SSD_JH, 1), lambda b, g: (g, 0, 0)),
            pl.BlockSpec((1, 2 * SSD_JH, 1), lambda b, g: (g, 0, 0)),
            pl.BlockSpec((1, gw), lambda b, g: (0, g)),
            pl.BlockSpec((2, 2 * SSD_JH, gw), lambda b, g: (0, 0, 0)),
        ],
        out_specs=pl.BlockSpec((1, seq, gw), lambda b, g: (b, 0, g)),
        out_shape=jax.ShapeDtypeStruct((bn, seq, SSD_INNER), BF16),
        scratch_shapes=[pltpu.VMEM((2, seq // SSD_CHUNK, SSD_STATE, gw), BF16),
                        pltpu.VMEM((2, SSD_STATE, gw), F32),
                        pltpu.VMEM((5, seq // SSD_CHUNK, 2 * SSD_JH, SSD_CHUNK), F32)],
        compiler_params=_cparams(("parallel", "parallel")),
        name="ssd_scan",
    )(xbc_act, xbc_act, xbc_act, dt3, pb, al, dskip, expand)


def _rope_tables(seq):
    half = SWA_DH // 2
    inv = ROPE_THETA ** (-np.arange(half, dtype=np.float64) / half)
    ang = np.arange(seq, dtype=np.float64)[:, None] * inv[None, :]
    cos, sin = np.cos(ang), np.sin(ang)
    return (jnp.asarray(np.concatenate([cos, cos], axis=1), F32),
            jnp.asarray(np.concatenate([-sin, sin], axis=1), F32))


def _rope(x, cos, sin_signed):
    return x * cos + pltpu.roll(x, SWA_DH // 2, 1) * sin_signed


def _swa_body(sink_ref, q_ref, k_ref, v_ref, cos_ref, sin_ref, o_ref, kr_scr, *, seq):
    wb = SWA_BLOCK
    nb = seq // wb
    nkeys = 3 * wb
    kvh = pl.program_id(1)
    rb = min(512, seq)

    def krope(i, carry):
        rows = pl.ds(pl.multiple_of(i * rb, rb), rb)
        kr_scr[rows, :] = _rope(k_ref[0, rows, :].astype(F32), cos_ref[rows, :],
                                sin_ref[rows, :]).astype(BF16)
        return carry

    lax.fori_loop(0, seq // rb, krope, 0, unroll=2)

    qpos_l = lax.broadcasted_iota(jnp.int32, (wb, nkeys), 0)
    kpos_l = lax.broadcasted_iota(jnp.int32, (wb, nkeys), 1)

    def blk(n, carry):
        rows = pl.ds(pl.multiple_of(n * wb, wb), wb)
        ks = pl.multiple_of(jnp.clip(n - 1, 0, nb - 3) * wb, wb)
        krows = pl.ds(ks, nkeys)
        cos = cos_ref[rows, :]
        sin = sin_ref[rows, :]
        kk = kr_scr[krows, :]
        vv = v_ref[0, krows, :]
        valid = jnp.abs((qpos_l + n * wb) - (kpos_l + ks)) <= SWA_WINDOW
        qb = q_ref[0, rows, :].astype(F32)
        q4 = jnp.concatenate(
            [(_rope(qb[:, SWA_DH * hh:SWA_DH * (hh + 1)], cos, sin) * (SWA_DH ** -0.5)).astype(BF16)
             for hh in range(SWA_GQ)], axis=0)
        s4 = _dot_t(q4, kk)
        probs = []
        dens = []
        for hh in range(SWA_GQ):
            sink = sink_ref[kvh * SWA_GQ + hh]
            s = jnp.where(valid, s4[wb * hh:wb * (hh + 1)], NEG_BIG)
            m = jnp.maximum(jnp.max(s, axis=-1, keepdims=True), sink)
            p = jnp.exp(s - m)
            probs.append(p.astype(BF16))
            dens.append(jnp.sum(p, axis=-1, keepdims=True) + jnp.exp(sink - m))
        o4 = jnp.dot(jnp.concatenate(probs, axis=0), vv, preferred_element_type=F32)
        for hh in range(SWA_GQ):
            o = o4[wb * hh:wb * (hh + 1)] / dens[hh]
            o_ref[0, rows, SWA_DH * hh:SWA_DH * (hh + 1)] = o.astype(o_ref.dtype)
        return carry

    lax.fori_loop(0, nb, blk, 0, unroll=4)


def _swa(proj3, sink, cos, sin, col0):
    bn, seq, _ = proj3.shape
    qw = SWA_GQ * SWA_DH
    qoff = col0 // qw
    koff = (col0 + SWA_HEADS * SWA_DH) // SWA_DH
    voff = koff + SWA_KV
    return pl.pallas_call(
        functools.partial(_swa_body, seq=seq),
        grid=(bn, SWA_KV),
        in_specs=[
            pl.BlockSpec(memory_space=pltpu.SMEM),
            pl.BlockSpec((1, seq, qw), lambda b, h: (b, 0, qoff + h)),
            pl.BlockSpec((1, seq, SWA_DH), lambda b, h: (b, 0, koff + h)),
            pl.BlockSpec((1, seq, SWA_DH), lambda b, h: (b, 0, voff + h)),
            pl.BlockSpec((seq, SWA_DH), lambda b, h: (0, 0)),
            pl.BlockSpec((seq, SWA_DH), lambda b, h: (0, 0)),
        ],
        out_specs=pl.BlockSpec((1, seq, qw), lambda b, h: (b, 0, h)),
        out_shape=jax.ShapeDtypeStruct((bn, seq, SWA_HEADS * SWA_DH), BF16),
        scratch_shapes=[pltpu.VMEM((seq, SWA_DH), BF16)],
        compiler_params=_cparams(("parallel", "parallel")),
        name="swa",
    )(sink, proj3, proj3, proj3, cos, sin)


def _pad_cols(w, width):
    return jnp.pad(w, ((0, 0), (0, width - w.shape[1])))


def _even_mixers(x2, bn, seq, g0, w_in, w_decay, b_decay, gla_norm, rpb):
    d = x2.shape[1]
    n_gla = 2 * GLA_HEADS * GLA_DK + 2 * GLA_HEADS * GLA_DV
    n_lr = 2 * GLA_LOWRANK
    w_main = jnp.concatenate([w_in[:, :n_gla], w_in[:, n_gla + n_lr:]], axis=1).astype(BF16)
    w_small = _pad_cols(w_in[:, n_gla:n_gla + n_lr], LANES).astype(BF16)
    proj, small = _inproj(x2, g0, w_main, w_small)
    proj3 = proj.reshape(bn, seq, -1)
    small3 = small.reshape(bn, seq, -1)
    wdec = jnp.zeros((2, LANES, w_decay.shape[-1]), F32)
    for dd in range(2):
        wdec = wdec.at[dd, dd * GLA_LOWRANK:(dd + 1) * GLA_LOWRANK].set(w_decay[dd])
    o_a = _gla(proj3, small3, wdec, b_decay, gla_norm[None, :])
    o_b = _na(proj3, _na_bias_table(rpb), n_gla)
    return o_a.reshape(-1, o_a.shape[-1]), o_b.reshape(-1, o_b.shape[-1])


def _per_group(v):
    return v.reshape(2, SSD_GROUPS, SSD_JH).transpose(1, 0, 2).reshape(SSD_GROUPS, 2 * SSD_JH)


def _odd_mixers(x2, bn, seq, g0, w_in, conv_w, conv_b, dt_bias, a_log, d_skip, sink, cos, sin):
    conv_dim = SSD_INNER + 2 * SSD_GROUPS * SSD_STATE
    n_z = SSD_INNER
    n_dt = 2 * SSD_HEADS
    c_dt = n_z + conv_dim
    w_main = jnp.concatenate([w_in[:, :c_dt], w_in[:, c_dt + n_dt:]], axis=1).astype(BF16)
    w_dt = w_in[:, c_dt:c_dt + n_dt].reshape(-1, 2, SSD_GROUPS, SSD_JH).transpose(0, 2, 1, 3)
    w_dt = w_dt.reshape(-1, SSD_GROUPS, 2 * SSD_JH)
    w_small = jnp.pad(w_dt, ((0, 0), (0, 0), (0, LANES - 2 * SSD_JH))).reshape(-1, SSD_GROUPS * LANES)
    proj, small = _inproj(x2, g0, w_main, w_small.astype(BF16))
    proj3 = proj.reshape(bn, seq, -1)
    dt3 = small.reshape(bn, seq, SSD_GROUPS * LANES)
    xbc_act = _conv(proj3, conv_w, conv_b[None, :], n_z)
    pbg = _per_group(dt_bias)
    alg = _per_group(a_log)
    head_of_ch = np.arange(SSD_JH * SSD_HEADDIM) // SSD_HEADDIM
    expand = jnp.asarray(np.arange(2 * SSD_JH)[None, :, None]
                         == (np.arange(2)[:, None, None] * SSD_JH + head_of_ch[None, None, :]), F32)
    dskip = jnp.repeat(d_skip.astype(F32), SSD_HEADDIM)[None, :]
    y_pre = _ssd(xbc_act, dt3, pbg[:, :, None], alg[:, :, None], dskip, expand)
    o_d = _swa(proj3, sink, cos, sin, c_dt)
    return y_pre.reshape(-1, SSD_INNER), proj, o_d.reshape(-1, o_d.shape[-1])


def kernel(x, norm_gains, ffn_w_gate, ffn_w_up, ffn_w_down, even_w_in, even_w_out, gla_w_decay,
           gla_b_decay, gla_norm, na_rpb, odd_w_in, odd_w_out, ssd_conv_w, ssd_conv_b, ssd_dt_bias,
           ssd_a_log, ssd_d, ssd_norm, swa_sink):
    bn, seq, d = x.shape
    depth = norm_gains.shape[0]
    x2 = x.reshape(bn * seq, d)
    cos, sin = _rope_tables(seq)
    ffn_w = (ffn_w_gate, ffn_w_up, ffn_w_down)
    even_w_out_b = even_w_out.astype(BF16)
    odd_w_out_b = odd_w_out.astype(BF16)
    w_bf16 = None
    for layer in range(depth):
        g = norm_gains[layer][:, None, :]
        i = layer // 2
        cast_here = dict(ffn_w=ffn_w, layer=0) if layer == 0 else {}
        cast_next = dict(ffn_w=ffn_w, next_layer=layer + 1) if layer + 1 < depth else {}
        if layer % 2 == 0:
            o_a, o_b = _even_mixers(x2, bn, seq, g[0], even_w_in[i], gla_w_decay[i], gla_b_decay[i],
                                    gla_norm[i], na_rpb[i])
            x2, *cast = _outproj(x2, o_a, o_b, even_w_out_b, i, g[1], **cast_here)
        else:
            y_pre, proj, o_d = _odd_mixers(x2, bn, seq, g[0], odd_w_in[i], ssd_conv_w[i], ssd_conv_b[i],
                                           ssd_dt_bias[i], ssd_a_log[i], ssd_d[i], swa_sink[i], cos, sin)
            x2, *cast = _outproj(x2, y_pre, o_d, odd_w_out_b, i, g[1], z_src=proj,
                                 ssd_gain=ssd_norm[i][None, :], **cast_here)
        w_bf16 = cast or w_bf16
        x2, *w_bf16 = _ffn(x2, g[2], *w_bf16, g[3], **cast_next)
    return x2.reshape(bn, seq, d)
```

```python
import functools

import jax
import jax.numpy as jnp
import numpy as np
from jax import lax
from jax.experimental import pallas as pl
from jax.experimental.pallas import tpu as pltpu

F32 = jnp.float32
BF16 = jnp.bfloat16
HIGHEST = lax.Precision.HIGHEST

EPS = 1e-6
ROPE_THETA = 10000.0
GRID_W = 64
LANES = 128

GLA_DK = 64
GLA_DV = 128
GLA_HEADS = 8
GLA_LOWRANK = 16
GLA_TAU = 16.0
GLA_CHUNK = 64
GLA_SUB = 16
GLA_EXP_CLAMP = 60.0
GLA_TRI_BLOCK = 256

NA_DH = 128
NA_HEADS = 8
NA_KH = 8
NA_KW = 16
NA_ROWS_PER_ITER = 16
NEG_BIG = -1e30

SSD_INNER = 1024
SSD_HEADDIM = 64
SSD_HEADS = 16
SSD_GROUPS = 2
SSD_STATE = 128
SSD_CONV = 5
SSD_CHUNK = 128
SSD_JH = SSD_HEADS // SSD_GROUPS
SSD_ROW_SLAB = 64

SWA_DH = 128
SWA_HEADS = 8
SWA_KV = 2
SWA_GQ = SWA_HEADS // SWA_KV
SWA_WINDOW = 128
SWA_BLOCK = 128

VMEM_LIMIT = 52 * 1024 * 1024


FFN_NORM_ROWS = 256
FFN_VMEM_LIMIT = 58 * 1024 * 1024


def _cparams(sem, vmem_limit=VMEM_LIMIT):
    return pltpu.CompilerParams(dimension_semantics=sem, vmem_limit_bytes=vmem_limit)


def _rms(x, g):
    return x * lax.rsqrt(jnp.mean(x * x, axis=-1, keepdims=True) + EPS) * g


def _silu(x):
    return x * jax.nn.sigmoid(x)


def _dot_t(a, b):
    return lax.dot_general(a, b, (((1,), (1,)), ((), ())), preferred_element_type=F32)


def _inproj_body(x_ref, g_ref, w_ref, ws_ref, o_ref, os_ref, h_scr):
    j = pl.program_id(1)
    half = x_ref.shape[0] // 2

    @pl.when(j == 0)
    def _():
        for r0 in (0, half):
            rs = pl.ds(r0, half)
            h = _rms(x_ref[rs, :], g_ref[...]).astype(BF16)
            h_scr[rs, :] = h
            os_ref[rs, :] = jnp.dot(h, ws_ref[...], preferred_element_type=F32)
            o_ref[rs, :] = jnp.dot(h, w_ref[...], preferred_element_type=F32).astype(o_ref.dtype)

    @pl.when(j > 0)
    def _():
        o_ref[...] = jnp.dot(h_scr[...], w_ref[...], preferred_element_type=F32).astype(o_ref.dtype)


def _inproj(x2, g, w_main, w_small, tm=1024, tn=2048):
    n, d = x2.shape
    nm = w_main.shape[1]
    ns = w_small.shape[1]
    return pl.pallas_call(
        _inproj_body,
        grid=(n // tm, nm // tn),
        in_specs=[
            pl.BlockSpec((tm, d), lambda i, j: (i, 0)),
            pl.BlockSpec((1, d), lambda i, j: (0, 0)),
            pl.BlockSpec((d, tn), lambda i, j: (0, j)),
            pl.BlockSpec((d, ns), lambda i, j: (0, 0)),
        ],
        out_specs=[
            pl.BlockSpec((tm, tn), lambda i, j: (i, j)),
            pl.BlockSpec((tm, ns), lambda i, j: (i, 0)),
        ],
        out_shape=[jax.ShapeDtypeStruct((n, nm), BF16), jax.ShapeDtypeStruct((n, ns), F32)],
        scratch_shapes=[pltpu.VMEM((tm, d), BF16)],
        compiler_params=_cparams(("parallel", "arbitrary")),
        name="inproj",
    )(x2, g, w_main, w_small)


def _cast_weight_slices(w_refs):
    for src, dst in zip(w_refs[:3], w_refs[3:]):
        dst[...] = src[...].astype(BF16)


def _outproj_out(rest):
    if len(rest) == 1:
        return rest[0]
    _cast_weight_slices(rest[:3] + rest[4:])
    return rest[3]


def _outproj_even_body(a_ref, b_ref, wa_ref, wb_ref, g_ref, x_ref, *rest):
    o_ref = _outproj_out(rest)
    mix = (jnp.dot(a_ref[...], wa_ref[...], preferred_element_type=F32)
           + jnp.dot(b_ref[...], wb_ref[...], preferred_element_type=F32))
    o_ref[...] = x_ref[...] + _rms(mix, g_ref[...])


def _outproj_odd_body(y_ref, z_ref, ng_ref, b_ref, wa_ref, wb_ref, g_ref, x_ref, *rest):
    o_ref = _outproj_out(rest)
    y = y_ref[...].astype(F32) * _silu(z_ref[...].astype(F32))
    a = _rms(y, ng_ref[...]).astype(BF16)
    mix = (jnp.dot(a, wa_ref[...], preferred_element_type=F32)
           + jnp.dot(b_ref[...], wb_ref[...], preferred_element_type=F32))
    o_ref[...] = x_ref[...] + _rms(mix, g_ref[...])


def _outproj(x2, a, b, w_out, li, g, ffn_w=None, layer=None, z_src=None, ssd_gain=None, tm=512):
    n, d = x2.shape
    half = a.shape[1]
    steps = n // tm
    row = lambda i: (i, 0)
    fixed = lambda i: (0, 0)
    w_specs = [pl.BlockSpec((None, half, d), lambda i: (li, 0, 0)),
               pl.BlockSpec((None, half, d), lambda i: (li, 1, 0))]
    tail_specs = [pl.BlockSpec((1, d), fixed), pl.BlockSpec((tm, d), row)]
    ffn_w = ffn_w or ()
    cast_in = [pl.BlockSpec((None, w.shape[1] // steps, w.shape[2]), lambda i: (layer, i, 0)) for w in ffn_w]
    cast_out = [pl.BlockSpec((w.shape[1] // steps, w.shape[2]), row) for w in ffn_w]
    cast_shapes = [jax.ShapeDtypeStruct(w.shape[1:], BF16) for w in ffn_w]
    if z_src is None:
        body = _outproj_even_body
        in_specs = [pl.BlockSpec((tm, half), row), pl.BlockSpec((tm, half), row)] + w_specs + tail_specs
        args = (a, b, w_out, w_out, g, x2)
    else:
        body = _outproj_odd_body
        in_specs = ([pl.BlockSpec((tm, half), row), pl.BlockSpec((tm, half), row),
                     pl.BlockSpec((1, half), fixed), pl.BlockSpec((tm, half), row)]
                    + w_specs + tail_specs)
        args = (a, z_src, ssd_gain, b, w_out, w_out, g, x2)
    return pl.pallas_call(
        body,
        grid=(steps,),
        in_specs=in_specs + cast_in,
        out_specs=[pl.BlockSpec((tm, d), row)] + cast_out,
        out_shape=[jax.ShapeDtypeStruct((n, d), F32)] + cast_shapes,
        compiler_params=_cparams(("parallel",)),
        name="outproj",
    )(*args, *ffn_w)


def _ffn_body(x_ref, g2_ref, wg_ref, wu_ref, wd_ref, g3_ref, *rest):
    if len(rest) == 2:
        o_ref, h_scr = rest
    else:
        o_ref, h_scr = rest[3], rest[7]
        _cast_weight_slices(rest[:3] + rest[4:7])
    k = pl.program_id(1)
    last = pl.num_programs(1) - 1
    half = x_ref.shape[0] // 2

    def pieces(r0):
        return [pl.ds(r0 + p, FFN_NORM_ROWS) for p in range(0, half, FFN_NORM_ROWS)]

    def partial_down(rs):
        h = h_scr[rs, :]
        gate = jnp.dot(h, wg_ref[...], preferred_element_type=F32)
        up = jnp.dot(h, wu_ref[...], preferred_element_type=F32)
        act = gate * (1.0 / (1.0 + jnp.exp(-gate))) * up
        return jnp.dot(act.astype(BF16), wd_ref[...], preferred_element_type=F32)

    @pl.when(k == 0)
    def _():
        for r0 in (0, half):
            for ps in pieces(r0):
                h_scr[ps, :] = _rms(x_ref[ps, :], g2_ref[...]).astype(BF16)
            o_ref[pl.ds(r0, half), :] = partial_down(pl.ds(r0, half))

    @pl.when(jnp.logical_and(k > 0, k < last))
    def _():
        for r0 in (0, half):
            o_ref[pl.ds(r0, half), :] += partial_down(pl.ds(r0, half))

    @pl.when(k == last)
    def _():
        for r0 in (0, half):
            o_ref[pl.ds(r0, half), :] += partial_down(pl.ds(r0, half))
            for ps in pieces(r0):
                o_ref[ps, :] = x_ref[ps, :] + _rms(o_ref[ps, :], g3_ref[...])


def _ffn(x2, g2, wg, wu, wd, g3, ffn_w=None, next_layer=None, tm=1024, th=512):
    n, d = x2.shape
    hid = wd.shape[0]
    ni, nk = n // tm, hid // th
    in_specs = [
        pl.BlockSpec((tm, d), lambda i, k: (i, 0)),
        pl.BlockSpec((1, d), lambda i, k: (0, 0)),
        pl.BlockSpec((d, th), lambda i, k: (0, k)),
        pl.BlockSpec((d, th), lambda i, k: (0, k)),
        pl.BlockSpec((th, d), lambda i, k: (k, 0)),
        pl.BlockSpec((1, d), lambda i, k: (0, 0)),
    ]
    out_specs = [pl.BlockSpec((tm, d), lambda i, k: (i, 0))]
    out_shape = [jax.ShapeDtypeStruct((n, d), F32)]
    args = [x2, g2, wg, wu, wd, g3]
    if ffn_w is not None:
        up_blk, down_blk = (d // ni, hid // nk), (hid // nk, d // ni)
        in_specs += [pl.BlockSpec((None,) + up_blk, lambda i, k: (next_layer, i, k)),
                     pl.BlockSpec((None,) + up_blk, lambda i, k: (next_layer, i, k)),
                     pl.BlockSpec((None,) + down_blk, lambda i, k: (next_layer, k, i))]
        out_specs += [pl.BlockSpec(up_blk, lambda i, k: (i, k)),
                      pl.BlockSpec(up_blk, lambda i, k: (i, k)),
                      pl.BlockSpec(down_blk, lambda i, k: (k, i))]
        out_shape += [jax.ShapeDtypeStruct(w.shape[1:], BF16) for w in ffn_w]
        args += list(ffn_w)
    return pl.pallas_call(
        _ffn_body,
        grid=(ni, nk),
        in_specs=in_specs,
        out_specs=out_specs,
        out_shape=out_shape,
        scratch_shapes=[pltpu.VMEM((tm, d), BF16)],
        compiler_params=_cparams(("parallel", "arbitrary"), FFN_VMEM_LIMIT),
        name="ffn",
    )(*args)


def _gla_body(q_ref, k_ref, v_ref, g_ref, lr_ref, wd_ref, bd_ref, gn_ref, o_ref,
              bc_scr, o_scr, st_scr, *, seq):
    c = GLA_CHUNK
    nc = seq // c
    half = nc // 2
    nsub = c // GLA_SUB
    rb = min(512, seq)
    row_i = lax.broadcasted_iota(jnp.int32, (c, c), 0)
    col_i = lax.broadcasted_iota(jnp.int32, (c, c), 1)
    tri = (row_i >= col_i, row_i <= col_i)
    lane = lax.broadcasted_iota(jnp.int32, (1, LANES), 1)
    hmask = (lane < GLA_DK, lane >= GLA_DK)
    own_head = ((lax.broadcasted_iota(jnp.int32, (2 * GLA_DV, LANES), 0) < GLA_DV)
                == (lax.broadcasted_iota(jnp.int32, (2 * GLA_DV, LANES), 1) < GLA_DK))

    w2 = jnp.concatenate([wd_ref[0], wd_ref[1]], axis=1)
    w_hi = w2.astype(BF16)
    w_lo = (w2 - w_hi.astype(F32)).astype(BF16)
    w_cat = jnp.concatenate([w_hi, w_lo, w_hi], axis=0)
    tb = GLA_TRI_BLOCK
    brow = lax.broadcasted_iota(jnp.int32, (tb, tb), 0)
    bcol = lax.broadcasted_iota(jnp.int32, (tb, tb), 1)
    same_chunk = (brow // c) == (bcol // c)
    blk_tri = ((same_chunk & (brow >= bcol)).astype(BF16), (same_chunk & (brow <= bcol)).astype(BF16))

    def pre(i, carry):
        r0 = pl.multiple_of(i * rb, rb)
        lr = lr_ref[0, pl.ds(r0, rb), :]
        lr_hi = lr.astype(BF16)
        lr_lo = (lr - lr_hi.astype(F32)).astype(BF16)
        z2 = jnp.dot(jnp.concatenate([lr_hi, lr_hi, lr_lo], axis=1), w_cat, preferred_element_type=F32)
        las = []
        for d in range(2):
            z = z2[:, LANES * d:LANES * (d + 1)] + bd_ref[d:d + 1, :]
            la = (jnp.minimum(z, 0.0) - jnp.log(1.0 + jnp.exp(-jnp.abs(z)))) * (1.0 / GLA_TAU)
            la_hi = la.astype(BF16)
            las.append(jnp.concatenate([la_hi, (la - la_hi.astype(F32)).astype(BF16)], axis=1))
        for d in range(2):
            for t in range(rb // tb):
                s2 = jnp.dot(blk_tri[d], las[d][t * tb:(t + 1) * tb], preferred_element_type=F32)
                bc_scr[d, pl.ds(r0 + t * tb, tb), :] = s2[:, :LANES] + s2[:, LANES:]
        return carry

    lax.fori_loop(0, seq // rb, pre, 0, unroll=8)

    def stage1(n, d):
        rows = pl.ds(pl.multiple_of(n * c, c), c)
        bc = bc_scr[d, rows, :]
        q = q_ref[0, rows, :].astype(F32) * (GLA_DK ** -0.5)
        k = k_ref[0, rows, :].astype(F32)
        v = v_ref[0, rows, :]
        b_edge = bc[c - 1:c] if d == 0 else bc[0:1]
        qhat = q * jnp.exp(bc)
        khat = (k * jnp.exp(b_edge - bc)).astype(BF16)
        att_rows = ([], [])
        for i in range(nsub):
            lo = GLA_SUB * i
            hi = lo + GLA_SUB
            if d == 0:
                ref = bc[lo - 1:lo] if i > 0 else jnp.zeros((1, LANES), F32)
            else:
                ref = bc[hi:hi + 1] if i < nsub - 1 else jnp.zeros((1, LANES), F32)
            qi = q[lo:hi] * jnp.exp(bc[lo:hi] - ref)
            ks = slice(0, hi) if d == 0 else slice(lo, c)
            ki = (k[ks] * jnp.exp(jnp.minimum(ref - bc[ks], GLA_EXP_CLAMP))).astype(BF16)
            q2 = jnp.concatenate([jnp.where(hmask[0], qi, 0.0), jnp.where(hmask[1], qi, 0.0)], axis=0)
            a2 = _dot_t(q2.astype(BF16), ki)
            n_unseen = c - (ks.stop - ks.start)
            if n_unseen:
                unseen = jnp.zeros((2 * GLA_SUB, n_unseen), F32)
                a2 = jnp.concatenate([a2, unseen] if d == 0 else [unseen, a2], axis=1)
            att_rows[0].append(a2[:GLA_SUB])
            att_rows[1].append(a2[GLA_SUB:])
        kv2 = lax.dot_general(v, khat, (((0,), (0,)), ((), ())), preferred_element_type=F32)
        kv2 = jnp.where(own_head, kv2, 0.0)
        att2 = jnp.concatenate(
            [jnp.where(tri[d], jnp.concatenate(att_rows[hh], axis=0), 0.0).astype(BF16)
             for hh in range(2)], axis=1)
        zero_v = jnp.zeros((c, GLA_DV), BF16)
        v_blockdiag = jnp.concatenate(
            [jnp.concatenate([v[:, :GLA_DV], zero_v], axis=1),
             jnp.concatenate([zero_v, v[:, GLA_DV:]], axis=1)], axis=0)
        return d, att2, v_blockdiag, kv2, qhat.astype(BF16), jnp.exp(b_edge)

    def stage2(s1, states):
        d, att2, v_blockdiag, kv2, qhat, decay = s1
        st = states[d]
        o = jnp.dot(att2, v_blockdiag, preferred_element_type=F32) + _dot_t(qhat, st.astype(BF16))
        states[d] = st * decay + kv2
        return o

    def finish(n, o):
        rows = pl.ds(pl.multiple_of(n * c, c), c)
        parts = []
        for hh in range(2):
            oh = o[:, GLA_DV * hh:GLA_DV * (hh + 1)]
            parts.append(oh * lax.rsqrt(jnp.mean(oh * oh, axis=-1, keepdims=True) + EPS))
        o = jnp.concatenate(parts, axis=1) * gn_ref[...]
        o_ref[0, rows, :] = (o * _silu(g_ref[0, rows, :].astype(F32))).astype(o_ref.dtype)

    st_scr[...] = jnp.zeros_like(st_scr)

    cpi = next(u for u in (4, 2, 1) if half % u == 0)

    def walk(i, base_f, base_b):
        work = []
        for u in range(cpi):
            work.append((base_f + i * cpi + u, 0))
            work.append((base_b - i * cpi - u, 1))
        prepared = [stage1(n, d) for n, d in work]
        states = [st_scr[d] for d in range(2)]
        outs = [(n, stage2(s1, states)) for (n, _), s1 in zip(work, prepared)]
        for d in range(2):
            st_scr[d] = states[d]
        return outs

    def first(i, carry):
        for n, o in walk(i, 0, nc - 1):
            o_scr[pl.ds(pl.multiple_of(n * c, c), c), :] = o
        return carry

    lax.fori_loop(0, half // cpi, first, 0, unroll=2)

    def second(i, carry):
        for n, o in walk(i, half, half - 1):
            finish(n, o_scr[pl.ds(pl.multiple_of(n * c, c), c), :] + o)
        return carry

    lax.fori_loop(0, half // cpi, second, 0, unroll=2)


def _gla(proj3, small3, wdec_pad, bdec, gnorm):
    bn, seq, _ = proj3.shape
    npair = GLA_HEADS // 2
    qoff = 0
    koff = (GLA_HEADS * GLA_DK) // LANES
    voff = (2 * GLA_HEADS * GLA_DK) // (2 * GLA_DV)
    goff = voff + npair
    return pl.pallas_call(
        functools.partial(_gla_body, seq=seq),
        grid=(bn, npair),
        in_specs=[
            pl.BlockSpec((1, seq, LANES), lambda b, p: (b, 0, qoff + p)),
            pl.BlockSpec((1, seq, LANES), lambda b, p: (b, 0, koff + p)),
            pl.BlockSpec((1, seq, 2 * GLA_DV), lambda b, p: (b, 0, voff + p)),
            pl.BlockSpec((1, seq, 2 * GLA_DV), lambda b, p: (b, 0, goff + p)),
            pl.BlockSpec((1, seq, LANES), lambda b, p: (b, 0, 0)),
            pl.BlockSpec((2, LANES, LANES), lambda b, p: (0, 0, p)),
            pl.BlockSpec((2, LANES), lambda b, p: (0, p)),
            pl.BlockSpec((1, 2 * GLA_DV), lambda b, p: (0, p)),
        ],
        out_specs=pl.BlockSpec((1, seq, 2 * GLA_DV), lambda b, p: (b, 0, p)),
        out_shape=jax.ShapeDtypeStruct((bn, seq, GLA_HEADS * GLA_DV), BF16),
        scratch_shapes=[
            pltpu.VMEM((2, seq, LANES), F32),
            pltpu.VMEM((seq, 2 * GLA_DV), F32),
            pltpu.VMEM((2, 2 * GLA_DV, LANES), F32),
        ],
        compiler_params=_cparams(("parallel", "parallel")),
        name="gla",
    )(proj3, proj3, proj3, proj3, small3, wdec_pad, bdec, gnorm)


def _na_bias_table(rpb):
    nh, ndr, ndc = rpb.shape
    cols = np.arange(GRID_W)
    cs = np.clip(cols - NA_KW // 2, 0, GRID_W - NA_KW)
    valid = (cols[None, :] >= cs[:, None]) & (cols[None, :] < cs[:, None] + NA_KW)
    dc = cols[None, :] - cols[:, None] + NA_KW - 1
    onehot = (np.arange(ndc)[:, None, None] == dc[None]) & valid[None]
    toep = jnp.dot(rpb.reshape(nh * ndr, ndc).astype(F32),
                   jnp.asarray(onehot.reshape(ndc, -1), F32), precision=HIGHEST)
    toep = jnp.where(jnp.asarray(valid.reshape(1, -1)), toep, NEG_BIG)
    toep = toep.reshape(nh, ndr, GRID_W, GRID_W)
    t = jnp.stack([toep[:, off:off + NA_KH] for off in range(NA_KH)], axis=1)
    return t.transpose(0, 1, 3, 2, 4).reshape(nh, NA_KH, GRID_W, NA_KH * GRID_W)


def _na_body(q_ref, k_ref, v_ref, bias_ref, o_ref, *, nrows):
    w = GRID_W
    nk = NA_KH * w

    def group(i, carry):
        idx = []
        scores = []
        for u in range(NA_ROWS_PER_ITER):
            r = i * NA_ROWS_PER_ITER + u
            rs = jnp.clip(r - NA_KH // 2, 0, nrows - NA_KH)
            qrows = pl.ds(pl.multiple_of(r * w, w), w)
            krows = pl.ds(pl.multiple_of(rs * w, w), nk)
            idx.append((qrows, krows, rs - r + (NA_KH - 1)))
            scores.append(_dot_t(q_ref[0, qrows, :], k_ref[0, krows, :]))
        probs = []
        for (qrows, krows, off), s in zip(idx, scores):
            s = s * (NA_DH ** -0.5) + bias_ref[0, off]
            p = jnp.exp(s - jnp.max(s, axis=-1, keepdims=True))
            probs.append((p.astype(BF16), jnp.sum(p, axis=-1, keepdims=True)))
        for (qrows, krows, off), (p, l) in zip(idx, probs):
            o = jnp.dot(p, v_ref[0, krows, :], preferred_element_type=F32) / l
            o_ref[0, qrows, :] = o.astype(o_ref.dtype)
        return carry

    lax.fori_loop(0, nrows // NA_ROWS_PER_ITER, group, 0, unroll=2)


def _na(proj3, bias_tab, col0):
    bn, seq, _ = proj3.shape
    nrows = seq // GRID_W
    qoff = col0 // NA_DH
    koff = qoff + NA_HEADS
    voff = koff + NA_HEADS
    return pl.pallas_call(
        functools.partial(_na_body, nrows=nrows),
        grid=(bn, NA_HEADS),
        in_specs=[
            pl.BlockSpec((1, seq, NA_DH), lambda b, h: (b, 0, qoff + h)),
            pl.BlockSpec((1, seq, NA_DH), lambda b, h: (b, 0, koff + h)),
            pl.BlockSpec((1, seq, NA_DH), lambda b, h: (b, 0, voff + h)),
            pl.BlockSpec((1, NA_KH, GRID_W, NA_KH * GRID_W), lambda b, h: (h, 0, 0, 0)),
        ],
        out_specs=pl.BlockSpec((1, seq, NA_DH), lambda b, h: (b, 0, h)),
        out_shape=jax.ShapeDtypeStruct((bn, seq, NA_HEADS * NA_DH), BF16),
        compiler_params=_cparams(("parallel", "parallel")),
        name="na",
    )(proj3, proj3, proj3, bias_tab)


CONV_PAD = 16


def _conv_body(x_ref, w_ref, b_ref, o_ref, xp_scr, *, seq):
    rb = min(256, seq)
    ch = x_ref.shape[-1]
    zeros = jnp.zeros((CONV_PAD, ch), xp_scr.dtype)
    xp_scr[0:CONV_PAD, :] = zeros
    xp_scr[seq + CONV_PAD:seq + 2 * CONV_PAD, :] = zeros
    xp_scr[CONV_PAD:seq + CONV_PAD, :] = x_ref[0]

    def blk(i, carry):
        r0 = pl.multiple_of(i * rb, rb)
        xw = xp_scr[pl.ds(r0, rb + 2 * CONV_PAD), :].astype(F32)
        acc = jnp.zeros((rb, ch), F32) + b_ref[...]
        for j in range(SSD_CONV):
            s0 = CONV_PAD - SSD_CONV // 2 + j
            acc = acc + xw[s0:s0 + rb] * w_ref[j:j + 1, :]
        o_ref[0, pl.ds(r0, rb), :] = _silu(acc).astype(o_ref.dtype)
        return carry

    lax.fori_loop(0, seq // rb, blk, 0)


def _conv(proj3, conv_w, conv_b, col0, tc=512):
    bn, seq, _ = proj3.shape
    cdim = conv_w.shape[1]
    c0 = col0 // tc
    return pl.pallas_call(
        functools.partial(_conv_body, seq=seq),
        grid=(bn, cdim // tc),
        in_specs=[
            pl.BlockSpec((1, seq, tc), lambda b, j: (b, 0, c0 + j)),
            pl.BlockSpec((SSD_CONV, tc), lambda b, j: (0, j)),
            pl.BlockSpec((1, tc), lambda b, j: (0, j)),
        ],
        out_specs=pl.BlockSpec((1, seq, tc), lambda b, j: (b, 0, j)),
        out_shape=jax.ShapeDtypeStruct((bn, seq, cdim), BF16),
        scratch_shapes=[pltpu.VMEM((seq + 2 * CONV_PAD, tc), BF16)],
        compiler_params=_cparams(("parallel", "parallel")),
        name="ssd_conv",
    )(proj3, conv_w, conv_b)


def _softplus(x):
    return jnp.maximum(x, 0.0) + jnp.log(1.0 + jnp.exp(-jnp.abs(x)))


def _ssd_body(xs_ref, bm_ref, cm_ref, dt_ref, pb_ref, al_ref, dsk_ref, ex_ref, o_ref,
              st_scr, run_scr, dec_scr, *, seq):
    l = SSD_CHUNK
    nc = seq // l
    nh2 = 2 * SSD_JH
    row_i = lax.broadcasted_iota(jnp.int32, (l, l), 0)
    col_i = lax.broadcasted_iota(jnp.int32, (l, l), 1)
    lane = lax.broadcasted_iota(jnp.int32, (1, LANES), 1)
    first_head = lane < SSD_HEADDIM
    bwd_row = lax.broadcasted_iota(jnp.int32, (nh2, 1), 0) >= SSD_JH
    a_col = -jnp.exp(al_ref[0])

    lower = row_i > col_i
    diag = row_i == col_i
    triu_b = (row_i <= col_i).astype(BF16)

    def decays(n):
        rows = pl.ds(pl.multiple_of(n * l, l), l)
        dtv = _softplus(dt_ref[0, rows, :].T[0:nh2] + pb_ref[0])
        da = dtv * a_col
        hi = da.astype(BF16)
        r1 = da - hi.astype(F32)
        mid = r1.astype(BF16)
        lo = (r1 - mid.astype(F32)).astype(BF16)
        p3 = jnp.dot(jnp.concatenate([hi, mid, lo], axis=0), triu_b, preferred_element_type=F32)
        pre = p3[0:nh2] + p3[nh2:2 * nh2] + p3[2 * nh2:]
        total = pre[:, l - 1:l]
        acum = jnp.where(bwd_row, total - pre + da, pre)
        dec_scr[0, n] = dtv
        dec_scr[1, n] = acum
        dec_scr[2, n] = dtv * jnp.exp(total - acum)
        dec_scr[3, n] = jnp.broadcast_to(jnp.exp(total), (nh2, l))
        dec_scr[4, n] = acum - jnp.log(dtv)

    def chunk_state(n, d):
        rows = pl.ds(pl.multiple_of(n * l, l), l)
        w_state_t = dec_scr[2, n]
        x = xs_ref[0, rows, :]
        bmt = bm_ref[0, rows, :].astype(F32).T
        edge_ch = jnp.sum(dec_scr[3, n][:, 0:1] * ex_ref[d], axis=0, keepdims=True)
        new_st = []
        ns = SSD_STATE
        for jj in range(SSD_JH // 2):
            xp = x[:, LANES * jj:LANES * (jj + 1)]
            bw2 = jnp.concatenate(
                [(bmt * w_state_t[d * SSD_JH + 2 * jj + hh:d * SSD_JH + 2 * jj + hh + 1, :]).astype(BF16)
                 for hh in range(2)], axis=0)
            s2 = jnp.dot(bw2, xp, preferred_element_type=F32)
            new_st.append(jnp.where(first_head, s2[:ns], s2[ns:]))
        st = run_scr[d]
        st_scr[d, n] = st.astype(BF16)
        run_scr[d] = st * edge_ch + jnp.concatenate(new_st, axis=1)

    def chunk_out(n):
        rows = pl.ds(pl.multiple_of(n * l, l), l)
        dtv_t = dec_scr[0, n]
        acum = dec_scr[1, n].T
        key_t = dec_scr[4, n]
        x = xs_ref[0, rows, :]
        cm = cm_ref[0, rows, :]
        cmf = cm.astype(F32)
        cb = _dot_t(cm, bm_ref[0, rows, :])
        st_f = st_scr[0, n]
        st_b = st_scr[1, n]
        rhs = [jnp.concatenate([x[:, LANES * jj:LANES * (jj + 1)],
                                st_f[:, LANES * jj:LANES * (jj + 1)],
                                st_b[:, LANES * jj:LANES * (jj + 1)]], axis=0)
               for jj in range(SSD_JH // 2)]
        slabs = [slice(r0, r0 + SSD_ROW_SLAB) for r0 in range(0, l, SSD_ROW_SLAB)]
        ys = [[] for _ in slabs]
        for jj in range(SSD_JH // 2):
            lhs = []
            for rs in slabs:
                for hh in range(2):
                    cf = 2 * jj + hh
                    cr = SSD_JH + cf
                    col_f = jnp.broadcast_to(acum[rs, cf:cf + 1], (SSD_ROW_SLAB, l))
                    col_r = jnp.broadcast_to(acum[rs, cr:cr + 1], (SSD_ROW_SLAB, l))
                    seg = jnp.where(lower[rs], col_f - key_t[cf:cf + 1, :], col_r - key_t[cr:cr + 1, :])
                    m = cb[rs] * (jnp.exp(seg) + jnp.where(diag[rs], dtv_t[cf:cf + 1, :], 0.0))
                    lhs.append(jnp.concatenate([m.astype(BF16),
                                                (cmf[rs] * jnp.exp(col_f)).astype(BF16),
                                                (cmf[rs] * jnp.exp(col_r)).astype(BF16)], axis=1))
            y4 = jnp.dot(jnp.concatenate(lhs, axis=0), rhs[jj], preferred_element_type=F32)
            for si in range(len(slabs)):
                r0 = 2 * si * SSD_ROW_SLAB
                ys[si].append(jnp.where(first_head, y4[r0:r0 + SSD_ROW_SLAB],
                                        y4[r0 + SSD_ROW_SLAB:r0 + 2 * SSD_ROW_SLAB]))
        for rs, y_parts in zip(slabs, ys):
            y = jnp.concatenate(y_parts, axis=1) + x[rs].astype(F32) * dsk_ref[...]
            o_ref[0, pl.ds(pl.multiple_of(n * l, l) + rs.start, SSD_ROW_SLAB), :] = y.astype(o_ref.dtype)

    def prep(i, carry):
        decays(i)
        return carry

    lax.fori_loop(0, nc, prep, 0, unroll=8)
    run_scr[...] = jnp.zeros_like(run_scr)

    def states(i, carry):
        chunk_state(i, 0)
        chunk_state(nc - 1 - i, 1)
        return carry

    lax.fori_loop(0, nc, states, 0, unroll=8)

    def outputs(n, carry):
        chunk_out(n)
        return carry

    lax.fori_loop(0, nc, outputs, 0, unroll=4)


def _ssd(xbc_act, dt3, pb, al, dskip, expand):
    bn, seq, _ = xbc_act.shape
    gw = SSD_JH * SSD_HEADDIM
    boff = SSD_INNER // SSD_STATE
    coff = boff + SSD_GROUPS
    return pl.pallas_call(
        functools.partial(_ssd_body, seq=seq),
        grid=(bn, SSD_GROUPS),
        in_specs=[
            pl.BlockSpec((1, seq, gw), lambda b, g: (b, 0, g)),
            pl.BlockSpec((1, seq, SSD_STATE), lambda b, g: (b, 0, boff + g)),
            pl.BlockSpec((1, seq, SSD_STATE), lambda b, g: (b, 0, coff + g)),
            pl.BlockSpec((1, seq, LANES), lambda b, g: (b, 0, g)),
            pl.BlockSpec((1, 2 * SSD_JH, 1), lambda b, g: (g, 0, 0)),
            pl.BlockSpec((1, 2 * SSD_JH, 1), lambda b, g: (g, 0, 0)),
            pl.BlockSpec((1, gw), lambda b, g: (0, g)),
            pl.BlockSpec((2, 2 * SSD_JH, gw), lambda b, g: (0, 0, 0)),
        ],
        out_specs=pl.BlockSpec((1, seq, gw), lambda b, g: (b, 0, g)),
        out_shape=jax.ShapeDtypeStruct((bn, seq, SSD_INNER), BF16),
        scratch_shapes=[pltpu.VMEM((2, seq // SSD_CHUNK, SSD_STATE, gw), BF16),
                        pltpu.VMEM((2, SSD_STATE, gw), F32),
                        pltpu.VMEM((5, seq // SSD_CHUNK, 2 * SSD_JH, SSD_CHUNK), F32)],
        compiler_params=_cparams(("parallel", "parallel")),
        name="ssd_scan",
    )(xbc_act, xbc_act, xbc_act, dt3, pb, al, dskip, expand)


def _rope_tables(seq):
    half = SWA_DH // 2
    inv = ROPE_THETA ** (-np.arange(half, dtype=np.float64) / half)
    ang = np.arange(seq, dtype=np.float64)[:, None] * inv[None, :]
    cos, sin = np.cos(ang), np.sin(ang)
    return (jnp.asarray(np.concatenate([cos, cos], axis=1), F32),
            jnp.asarray(np.concatenate([-sin, sin], axis=1), F32))


def _rope(x, cos, sin_signed):
    return x * cos + pltpu.roll(x, SWA_DH // 2, 1) * sin_signed


def _swa_body(sink_ref, q_ref, k_ref, v_ref, cos_ref, sin_ref, o_ref, kr_scr, *, seq):
    wb = SWA_BLOCK
    nb = seq // wb
    nkeys = 3 * wb
    kvh = pl.program_id(1)
    rb = min(512, seq)

    def krope(i, carry):
        rows = pl.ds(pl.multiple_of(i * rb, rb), rb)
        kr_scr[rows, :] = _rope(k_ref[0, rows, :].astype(F32), cos_ref[rows, :],
                                sin_ref[rows, :]).astype(BF16)
        return carry

    lax.fori_loop(0, seq // rb, krope, 0, unroll=2)

    qpos_l = lax.broadcasted_iota(jnp.int32, (wb, nkeys), 0)
    kpos_l = lax.broadcasted_iota(jnp.int32, (wb, nkeys), 1)

    def blk(n, carry):
        rows = pl.ds(pl.multiple_of(n * wb, wb), wb)
        ks = pl.multiple_of(jnp.clip(n - 1, 0, nb - 3) * wb, wb)
        krows = pl.ds(ks, nkeys)
        cos = cos_ref[rows, :]
        sin = sin_ref[rows, :]
        kk = kr_scr[krows, :]
        vv = v_ref[0, krows, :]
        valid = jnp.abs((qpos_l + n * wb) - (kpos_l + ks)) <= SWA_WINDOW
        qb = q_ref[0, rows, :].astype(F32)
        q4 = jnp.concatenate(
            [(_rope(qb[:, SWA_DH * hh:SWA_DH * (hh + 1)], cos, sin) * (SWA_DH ** -0.5)).astype(BF16)
             for hh in range(SWA_GQ)], axis=0)
        s4 = _dot_t(q4, kk)
        probs = []
        dens = []
        for hh in range(SWA_GQ):
            sink = sink_ref[kvh * SWA_GQ + hh]
            s = jnp.where(valid, s4[wb * hh:wb * (hh + 1)], NEG_BIG)
            m = jnp.maximum(jnp.max(s, axis=-1, keepdims=True), sink)
            p = jnp.exp(s - m)
            probs.append(p.astype(BF16))
            dens.append(jnp.sum(p, axis=-1, keepdims=True) + jnp.exp(sink - m))
        o4 = jnp.dot(jnp.concatenate(probs, axis=0), vv, preferred_element_type=F32)
        for hh in range(SWA_GQ):
            o = o4[wb * hh:wb * (hh + 1)] / dens[hh]
            o_ref[0, rows, SWA_DH * hh:SWA_DH * (hh + 1)] = o.astype(o_ref.dtype)
        return carry

    lax.fori_loop(0, nb, blk, 0, unroll=4)


def _swa(proj3, sink, cos, sin, col0):
    bn, seq, _ = proj3.shape
    qw = SWA_GQ * SWA_DH
    qoff = col0 // qw
    koff = (col0 + SWA_HEADS * SWA_DH) // SWA_DH
    voff = koff + SWA_KV
    return pl.pallas_call(
        functools.partial(_swa_body, seq=seq),
        grid=(bn, SWA_KV),
        in_specs=[
            pl.BlockSpec(memory_space=pltpu.SMEM),
            pl.BlockSpec((1, seq, qw), lambda b, h: (b, 0, qoff + h)),
            pl.BlockSpec((1, seq, SWA_DH), lambda b, h: (b, 0, koff + h)),
            pl.BlockSpec((1, seq, SWA_DH), lambda b, h: (b, 0, voff + h)),
            pl.BlockSpec((seq, SWA_DH), lambda b, h: (0, 0)),
            pl.BlockSpec((seq, SWA_DH), lambda b, h: (0, 0)),
        ],
        out_specs=pl.BlockSpec((1, seq, qw), lambda b, h: (b, 0, h)),
        out_shape=jax.ShapeDtypeStruct((bn, seq, SWA_HEADS * SWA_DH), BF16),
        scratch_shapes=[pltpu.VMEM((seq, SWA_DH), BF16)],
        compiler_params=_cparams(("parallel", "parallel")),
        name="swa",
    )(sink, proj3, proj3, proj3, cos, sin)


def _pad_cols(w, width):
    return jnp.pad(w, ((0, 0), (0, width - w.shape[1])))


def _even_mixers(x2, bn, seq, g0, w_in, w_decay, b_decay, gla_norm, rpb):
    d = x2.shape[1]
    n_gla = 2 * GLA_HEADS * GLA_DK + 2 * GLA_HEADS * GLA_DV
    n_lr = 2 * GLA_LOWRANK
    w_main = jnp.concatenate([w_in[:, :n_gla], w_in[:, n_gla + n_lr:]], axis=1).astype(BF16)
    w_small = _pad_cols(w_in[:, n_gla:n_gla + n_lr], LANES).astype(BF16)
    proj, small = _inproj(x2, g0, w_main, w_small)
    proj3 = proj.reshape(bn, seq, -1)
    small3 = small.reshape(bn, seq, -1)
    wdec = jnp.zeros((2, LANES, w_decay.shape[-1]), F32)
    for dd in range(2):
        wdec = wdec.at[dd, dd * GLA_LOWRANK:(dd + 1) * GLA_LOWRANK].set(w_decay[dd])
    o_a = _gla(proj3, small3, wdec, b_decay, gla_norm[None, :])
    o_b = _na(proj3, _na_bias_table(rpb), n_gla)
    return o_a.reshape(-1, o_a.shape[-1]), o_b.reshape(-1, o_b.shape[-1])


def _per_group(v):
    return v.reshape(2, SSD_GROUPS, SSD_JH).transpose(1, 0, 2).reshape(SSD_GROUPS, 2 * SSD_JH)


def _odd_mixers(x2, bn, seq, g0, w_in, conv_w, conv_b, dt_bias, a_log, d_skip, sink, cos, sin):
    conv_dim = SSD_INNER + 2 * SSD_GROUPS * SSD_STATE
    n_z = SSD_INNER
    n_dt = 2 * SSD_HEADS
    c_dt = n_z + conv_dim
    w_main = jnp.concatenate([w_in[:, :c_dt], w_in[:, c_dt + n_dt:]], axis=1).astype(BF16)
    w_dt = w_in[:, c_dt:c_dt + n_dt].reshape(-1, 2, SSD_GROUPS, SSD_JH).transpose(0, 2, 1, 3)
    w_dt = w_dt.reshape(-1, SSD_GROUPS, 2 * SSD_JH)
    w_small = jnp.pad(w_dt, ((0, 0), (0, 0), (0, LANES - 2 * SSD_JH))).reshape(-1, SSD_GROUPS * LANES)
    proj, small = _inproj(x2, g0, w_main, w_small.astype(BF16))
    proj3 = proj.reshape(bn, seq, -1)
    dt3 = small.reshape(bn, seq, SSD_GROUPS * LANES)
    xbc_act = _conv(proj3, conv_w, conv_b[None, :], n_z)
    pbg = _per_group(dt_bias)
    alg = _per_group(a_log)
    head_of_ch = np.arange(SSD_JH * SSD_HEADDIM) // SSD_HEADDIM
    expand = jnp.asarray(np.arange(2 * SSD_JH)[None, :, None]
                         == (np.arange(2)[:, None, None] * SSD_JH + head_of_ch[None, None, :]), F32)
    dskip = jnp.repeat(d_skip.astype(F32), SSD_HEADDIM)[None, :]
    y_pre = _ssd(xbc_act, dt3, pbg[:, :, None], alg[:, :, None], dskip, expand)
    o_d = _swa(proj3, sink, cos, sin, c_dt)
    return y_pre.reshape(-1, SSD_INNER), proj, o_d.reshape(-1, o_d.shape[-1])


def kernel(x, norm_gains, ffn_w_gate, ffn_w_up, ffn_w_down, even_w_in, even_w_out, gla_w_decay,
           gla_b_decay, gla_norm, na_rpb, odd_w_in, odd_w_out, ssd_conv_w, ssd_conv_b, ssd_dt_bias,
           ssd_a_log, ssd_d, ssd_norm, swa_sink):
    bn, seq, d = x.shape
    depth = norm_gains.shape[0]
    x2 = x.reshape(bn * seq, d)
    cos, sin = _rope_tables(seq)
    ffn_w = (ffn_w_gate, ffn_w_up, ffn_w_down)
    even_w_out_b = even_w_out.astype(BF16)
    odd_w_out_b = odd_w_out.astype(BF16)
    w_bf16 = None
    for layer in range(depth):
        g = norm_gains[layer][:, None, :]
        i = layer // 2
        cast_here = dict(ffn_w=ffn_w, layer=0) if layer == 0 else {}
        cast_next = dict(ffn_w=ffn_w, next_layer=layer + 1) if layer + 1 < depth else {}
        if layer % 2 == 0:
            o_a, o_b = _even_mixers(x2, bn, seq, g[0], even_w_in[i], gla_w_decay[i], gla_b_decay[i],
                                    gla_norm[i], na_rpb[i])
            x2, *cast = _outproj(x2, o_a, o_b, even_w_out_b, i, g[1], **cast_here)
        else:
            y_pre, proj, o_d = _odd_mixers(x2, bn, seq, g[0], odd_w_in[i], ssd_conv_w[i], ssd_conv_b[i],
                                           ssd_dt_bias[i], ssd_a_log[i], ssd_d[i], swa_sink[i], cos, sin)
            x2, *cast = _outproj(x2, y_pre, o_d, odd_w_out_b, i, g[1], z_src=proj,
                                 ssd_gain=ssd_norm[i][None, :], **cast_here)
        w_bf16 = cast or w_bf16
        x2, *w_bf16 = _ffn(x2, g[2], *w_bf16, g[3], **cast_next)
    return x2.reshape(bn, seq, d)
```
